```python
import math
import jax, jax.numpy as jnp
from jax import lax
import numpy as np

D_MODEL = 2048
BATCH = 2
SEQ = 4096
DEPTH = 1
DEC_BATCH = 4
DEC_SEQ = 2048
PAST_LEN = 128

MIX_WIDTH = D_MODEL
ATTN_WIDTH = MIX_WIDTH // 2
FOURIER_WIDTH = MIX_WIDTH - ATTN_WIDTH
HEAD_DIM = 128
N_HEADS = ATTN_WIDTH // HEAD_DIM
N_FOURIER_GROUPS = 4
FOURIER_GROUP_DIM = FOURIER_WIDTH // N_FOURIER_GROUPS
ROPE_THETA = 500000.0
ROPE_DIM = HEAD_DIM // 4
DILATED_PATTERNS = ((128, 1), (512, 4), (2048, 16))
QUERY_BLOCK = 128
RMS_EPS = 1e-6
IN_WIDTH = 4 * ATTN_WIDTH + 2 * FOURIER_WIDTH

kernel_name = 'hybrid_fourier_dilated_encoder'


def _rmsnorm(x, gain):
    xf = x.astype(jnp.float32)
    y = xf * lax.rsqrt(jnp.mean(xf * xf, axis=-1, keepdims=True) + RMS_EPS) * gain.astype(jnp.float32)
    return y.astype(x.dtype)


def _partial_rope(x):
    S = x.shape[1]
    half = ROPE_DIM // 2
    inv_freq = 1.0 / (ROPE_THETA ** (jnp.arange(half, dtype=jnp.float32) / half))
    ang = jnp.arange(S, dtype=jnp.float32)[:, None] * inv_freq[None, :]
    cos = jnp.cos(ang)[None, :, None, :]
    sin = jnp.sin(ang)[None, :, None, :]
    xf = x.astype(jnp.float32)
    x1, x2, rest = xf[..., :half], xf[..., half:ROPE_DIM], xf[..., ROPE_DIM:]
    rot = jnp.concatenate([x1 * cos - x2 * sin, x1 * sin + x2 * cos, rest], axis=-1)
    return rot.astype(x.dtype)


def _dilated_window_branch(q, k, v, window, dilation):
    B, S, H, Dh = q.shape
    half = window // (2 * dilation)
    L = S // dilation

    def to_sub(t):
        return t.reshape(B, L, dilation, H, Dh).transpose(0, 2, 1, 3, 4)

    qs, ks, vs = to_sub(q), to_sub(k), to_sub(v)
    qb = math.gcd(L, QUERY_BLOCK)
    nb = L // qb
    span = qb + 2 * half
    pad = ((0, 0), (0, 0), (half, half), (0, 0), (0, 0))
    kp, vp = jnp.pad(ks, pad), jnp.pad(vs, pad)
    key_idx = jnp.arange(nb)[:, None] * qb + jnp.arange(span)[None, :]
    kb = kp[:, :, key_idx]
    vb = vp[:, :, key_idx]
    qblk = qs.reshape(B, dilation, nb, qb, H, Dh)
    scale = HEAD_DIM ** -0.5
    scores = jnp.einsum('brnqhd,brnkhd->brnhqk', qblk, kb).astype(jnp.float32) * scale
    rel = jnp.arange(span)[None, :] - half - jnp.arange(qb)[:, None]
    key_pos = key_idx - half
    valid = (jnp.abs(rel) <= half)[None] & ((key_pos >= 0) & (key_pos < L))[:, None, :]
    scores = jnp.where(valid[None, None, :, None, :, :], scores, -jnp.inf)
    m = jnp.max(scores, axis=-1, keepdims=True)
    p = jnp.exp(scores - m)
    s = jnp.sum(p, axis=-1, keepdims=True)
    o = jnp.einsum('brnhqk,brnkhd->brnqhd', p / s, vb.astype(jnp.float32))
    lse = (m + jnp.log(s))[..., 0].transpose(0, 1, 2, 4, 3)
    o = o.reshape(B, dilation, L, H, Dh).transpose(0, 2, 1, 3, 4).reshape(B, S, H, Dh)
    lse = lse.reshape(B, dilation, L, H).transpose(0, 2, 1, 3).reshape(B, S, H)
    return o, lse


def _dilated_attention(q, k, v):
    outs, lses = [], []
    for window, dilation in DILATED_PATTERNS:
        o, lse = _dilated_window_branch(q, k, v, window, dilation)
        outs.append(o)
        lses.append(lse)
    w = jax.nn.softmax(jnp.stack(lses, axis=0), axis=0)
    return jnp.einsum('pbsh,pbshd->bshd', w, jnp.stack(outs, axis=0))


def _fourier_mix(u, w_fourier):
    B, S, _ = u.shape
    ug = u.reshape(B, S, N_FOURIER_GROUPS, FOURIER_GROUP_DIM).astype(jnp.float32)
    f = jnp.fft.fftn(ug, axes=(1, 3), norm='ortho').real
    f = jnp.einsum('bsgc,gce->bsge', f, w_fourier.astype(jnp.float32))
    return f.reshape(B, S, FOURIER_WIDTH)


def _layer(x, rms_gain, w_in, q_norm_gain, k_norm_gain, w_fourier, w_out):
    B, S, _ = x.shape
    h = _rmsnorm(x, rms_gain)
    proj = jnp.einsum('bsd,de->bse', h, w_in)
    A, F = ATTN_WIDTH, FOURIER_WIDTH
    q, k, v, g_a, u_f, g_f = jnp.split(proj, [A, 2 * A, 3 * A, 4 * A, 4 * A + F], axis=-1)
    q = _partial_rope(_rmsnorm(q.reshape(B, S, N_HEADS, HEAD_DIM), q_norm_gain))
    k = _partial_rope(_rmsnorm(k.reshape(B, S, N_HEADS, HEAD_DIM), k_norm_gain))
    v = v.reshape(B, S, N_HEADS, HEAD_DIM)
    attn = _dilated_attention(q, k, v).reshape(B, S, A)
    attn = attn * jax.nn.silu(g_a.astype(jnp.float32))
    four = _fourier_mix(u_f, w_fourier) * jax.nn.silu(g_f.astype(jnp.float32))
    mix = jnp.concatenate([attn, four], axis=-1).astype(x.dtype)
    return x + jnp.einsum('bse,ed->bsd', mix, w_out)


def _trunk(x, rms_gain, w_in, q_norm_gain, k_norm_gain, w_fourier, w_out):
    for l in range(DEPTH):
        x = _layer(x, rms_gain[l], w_in[l], q_norm_gain[l], k_norm_gain[l], w_fourier[l], w_out[l])
    return x


def setup_inputs(seed: int = 0) -> dict:
    key = jax.random.key(seed)
    ks = jax.random.split(key, 8)
    nrm = jax.random.normal
    f32 = jnp.float32
    return {
        'x_prompt': nrm(ks[0], (BATCH, SEQ, D_MODEL), f32),
        'x_sample': nrm(ks[1], (DEC_BATCH, DEC_SEQ, D_MODEL), f32),
        'rms_gain': 1.0 + 0.02 * nrm(ks[2], (DEPTH, D_MODEL), f32),
        'w_in': nrm(ks[3], (DEPTH, D_MODEL, IN_WIDTH), f32) * (D_MODEL ** -0.5),
        'q_norm_gain': 1.0 + 0.02 * nrm(ks[4], (DEPTH, HEAD_DIM), f32),
        'k_norm_gain': 1.0 + 0.02 * nrm(ks[5], (DEPTH, HEAD_DIM), f32),
        'w_fourier': nrm(ks[6], (DEPTH, N_FOURIER_GROUPS, FOURIER_GROUP_DIM, FOURIER_GROUP_DIM), f32) * (FOURIER_GROUP_DIM ** -0.5),
        'w_out': nrm(ks[7], (DEPTH, MIX_WIDTH, D_MODEL), f32) * (MIX_WIDTH ** -0.5),
    }


def reference(x_prompt, x_sample, rms_gain, w_in, q_norm_gain, k_norm_gain, w_fourier, w_out):
    y_prompt = _trunk(x_prompt, rms_gain, w_in, q_norm_gain, k_norm_gain, w_fourier, w_out)
    y_sample = _trunk(x_sample, rms_gain, w_in, q_norm_gain, k_norm_gain, w_fourier, w_out)
    return (y_prompt, y_sample)
```

```python
import functools
import math

import jax
import jax.numpy as jnp
import numpy as np
from jax import lax
from jax.experimental import pallas as pl
from jax.experimental.pallas import tpu as pltpu

D_MODEL = 2048
ATTN_WIDTH = 1024
FOURIER_WIDTH = 1024
HEAD_DIM = 128
N_HEADS = ATTN_WIDTH // HEAD_DIM
N_GROUPS = 4
GROUP_DIM = FOURIER_WIDTH // N_GROUPS
ROPE_THETA = 500000.0
ROPE_DIM = HEAD_DIM // 4
ROPE_HALF = ROPE_DIM // 2
DILATIONS = (1, 4, 16)
HALF_KEYS = 64
RMS_EPS = 1e-6
N_IN_TILES = 6
SEQ_RADIX = 16
MASK_VALUE = -1e30

VMEM_LIMIT_BYTES = 48 * 1024 * 1024

F32 = jnp.float32
BF16 = jnp.bfloat16


def _params(*semantics):
    return pltpu.CompilerParams(dimension_semantics=semantics, vmem_limit_bytes=VMEM_LIMIT_BYTES)


def _rope_tables(seq, scale):
    expo = np.arange(ROPE_HALF, dtype=np.float32) / np.float32(ROPE_HALF)
    inv_freq = (np.float32(1.0) / np.power(np.float32(ROPE_THETA), expo)).astype(np.float32)
    ang = (np.arange(seq, dtype=np.float32)[:, None] * inv_freq[None, :]).astype(np.float64)
    cos, sin = np.cos(ang), np.sin(ang)
    a = np.ones((seq, HEAD_DIM))
    b = np.zeros((seq, HEAD_DIM))
    c = np.zeros((seq, HEAD_DIM))
    a[:, :ROPE_HALF] = cos
    a[:, ROPE_HALF:ROPE_DIM] = cos
    b[:, ROPE_HALF:ROPE_DIM] = sin
    c[:, :ROPE_HALF] = -sin
    return tuple(jnp.asarray((t * scale).astype(np.float32)) for t in (a, b, c))


def _dft_cos_sin(n, rows, cols, scale=1.0):
    m = (np.asarray(rows, dtype=np.int64)[:, None] * np.asarray(cols, dtype=np.int64)[None, :]) % n
    ang = 2.0 * np.pi * m.astype(np.float64) / n
    return np.cos(ang) * scale, np.sin(ang) * scale


def _silu(a):
    return a * (1.0 / (1.0 + jnp.exp(-a)))


def _inproj_kernel(x_ref, gain_ref, w_ref, qg_ref, kg_ref,
                   qa_ref, qb_ref, qc_ref, ka_ref, kb_ref, kc_ref,
                   qkvg_ref, u_ref, gf_ref, h_ref):
    j = pl.program_id(1)

    @pl.when(j == 0)
    def _():
        x = x_ref[...]
        ms = jnp.mean(x * x, axis=-1, keepdims=True)
        h_ref[...] = (x * lax.rsqrt(ms + RMS_EPS) * gain_ref[...]).astype(BF16)

    acc = jnp.dot(h_ref[...], w_ref[...], preferred_element_type=F32)

    def norm_rope(g_ref, a_ref, b_ref, c_ref):
        for h in range(N_HEADS):
            hs = slice(h * HEAD_DIM, (h + 1) * HEAD_DIM)
            a = acc[:, hs]
            ms = jnp.mean(a * a, axis=-1, keepdims=True)
            n = a * lax.rsqrt(ms + RMS_EPS) * g_ref[...]
            r = (n * a_ref[...]
                 + pltpu.roll(n, ROPE_HALF, 1) * b_ref[...]
                 + pltpu.roll(n, HEAD_DIM - ROPE_HALF, 1) * c_ref[...])
            qkvg_ref[:, hs] = r.astype(BF16)

    @pl.when(j == 0)
    def _():
        norm_rope(qg_ref, qa_ref, qb_ref, qc_ref)

    @pl.when(j == 1)
    def _():
        norm_rope(kg_ref, ka_ref, kb_ref, kc_ref)

    @pl.when(j == 2)
    def _():
        qkvg_ref[...] = acc.astype(BF16)

    @pl.when(j == 3)
    def _():
        qkvg_ref[...] = _silu(acc).astype(BF16)

    @pl.when(j == 4)
    def _():
        u_ref[...] = acc.astype(BF16)

    @pl.when(j == 5)
    def _():
        gf_ref[...] = _silu(acc).astype(BF16)


def _inproj(x2d, seq, gain, w_in, qg, kg, tm=512):
    t = x2d.shape[0]
    n_seq_tiles = seq // tm
    qa, qb, qc = _rope_tables(seq, HEAD_DIM ** -0.5)
    ka, kb, kc = _rope_tables(seq, 1.0)
    rope_spec = pl.BlockSpec((tm, HEAD_DIM), lambda i, j: (i % n_seq_tiles, 0))
    vec_spec = pl.BlockSpec((1, HEAD_DIM), lambda i, j: (0, 0))
    return pl.pallas_call(
        _inproj_kernel,
        grid=(t // tm, N_IN_TILES),
        in_specs=[
            pl.BlockSpec((tm, D_MODEL), lambda i, j: (i, 0)),
            pl.BlockSpec((1, D_MODEL), lambda i, j: (0, 0)),
            pl.BlockSpec((D_MODEL, ATTN_WIDTH), lambda i, j: (0, j)),
            vec_spec, vec_spec,
            rope_spec, rope_spec, rope_spec, rope_spec, rope_spec, rope_spec,
        ],
        out_specs=[
            pl.BlockSpec((tm, ATTN_WIDTH), lambda i, j: (i, jnp.minimum(j, 3))),
            pl.BlockSpec((tm, FOURIER_WIDTH), lambda i, j: (i, 0)),
            pl.BlockSpec((tm, FOURIER_WIDTH), lambda i, j: (i, 0)),
        ],
        out_shape=[
            jax.ShapeDtypeStruct((t, 4 * ATTN_WIDTH), BF16),
            jax.ShapeDtypeStruct((t, FOURIER_WIDTH), BF16),
            jax.ShapeDtypeStruct((t, FOURIER_WIDTH), BF16),
        ],
        scratch_shapes=[pltpu.VMEM((tm, D_MODEL), BF16)],
        compiler_params=_params("parallel", "arbitrary"),
        name="inproj",
    )(x2d, gain, w_in, qg, kg, qa, qb, qc, ka, kb, kc)


def _attn_kernel(q_ref, k_ref, v_ref, o_ref, lse_ref, *, sub_len, window, tl):
    l0 = pl.program_id(2) * tl
    if window == sub_len:
        ws = 0
    else:
        ws = pl.multiple_of(jnp.clip(l0 - HALF_KEYS, 0, sub_len - window), HALF_KEYS)
    qi = lax.broadcasted_iota(jnp.int32, (tl, window), 0)
    kc = lax.broadcasted_iota(jnp.int32, (tl, window), 1)
    valid = jnp.abs(qi - kc + (l0 - ws)) <= HALF_KEYS
    lane = lax.broadcasted_iota(jnp.int32, (tl, HEAD_DIM), 1)
    lse_tile = jnp.zeros((tl, HEAD_DIM), F32)
    for h in range(N_HEADS):
        hs = slice(h * HEAD_DIM, (h + 1) * HEAD_DIM)
        q = q_ref[:, hs]
        k = k_ref[pl.ds(ws, window), hs]
        v = v_ref[pl.ds(ws, window), hs]
        s = lax.dot_general(q, k, (((1,), (1,)), ((), ())), preferred_element_type=F32)
        s = jnp.where(valid, s, MASK_VALUE)
        m = jnp.max(s, axis=-1, keepdims=True)
        p = jnp.exp(s - m)
        l = jnp.sum(p, axis=-1, keepdims=True)
        o = jnp.dot(p.astype(BF16), v, preferred_element_type=F32) * (1.0 / l)
        o_ref[:, hs] = o.astype(BF16)
        lse_tile = jnp.where(lane == h, m + jnp.log(l), lse_tile)
    lse_ref[...] = lse_tile


def _attention_pattern(qkvg, batch, seq, dilation, tl=128):
    sub_len = seq // dilation
    window = min(tl + 2 * HALF_KEYS, sub_len)
    row_w = 4 * ATTN_WIDTH
    view = qkvg.reshape(batch, sub_len, dilation * row_w)
    kernel = functools.partial(_attn_kernel, sub_len=sub_len, window=window, tl=tl)
    o, lse = pl.pallas_call(
        kernel,
        grid=(batch, dilation, sub_len // tl),
        in_specs=[
            pl.BlockSpec((None, tl, ATTN_WIDTH), lambda b, r, l: (b, l, 4 * r)),
            pl.BlockSpec((None, sub_len, ATTN_WIDTH), lambda b, r, l: (b, 0, 4 * r + 1)),
            pl.BlockSpec((None, sub_len, ATTN_WIDTH), lambda b, r, l: (b, 0, 4 * r + 2)),
        ],
        out_specs=[
            pl.BlockSpec((None, tl, ATTN_WIDTH), lambda b, r, l: (b, l, r)),
            pl.BlockSpec((None, tl, HEAD_DIM), lambda b, r, l: (b, l, r)),
        ],
        out_shape=[
            jax.ShapeDtypeStruct((batch, sub_len, dilation * ATTN_WIDTH), BF16),
            jax.ShapeDtypeStruct((batch, sub_len, dilation * HEAD_DIM), F32),
        ],
        compiler_params=_params("parallel", "parallel", "arbitrary"),
        name=f"attn_d{dilation}",
    )(view, view, view)
    return o.reshape(batch * seq, ATTN_WIDTH), lse.reshape(batch * seq, HEAD_DIM)


def _chanmat_kernel(c_ref, s_ref, w_ref, m_ref):
    for g in range(N_GROUPS):
        w = w_ref[g]
        mc = jnp.dot(c_ref[...], w, preferred_element_type=F32, precision=lax.Precision.HIGHEST)
        ms = jnp.dot(s_ref[...], w, preferred_element_type=F32, precision=lax.Precision.HIGHEST)
        m_ref[g, :, :GROUP_DIM] = mc.astype(BF16)
        m_ref[g, :, GROUP_DIM:] = ms.astype(BF16)


def _channel_matrices(w_fourier):
    idx = np.arange(GROUP_DIM)
    cc, sc = _dft_cos_sin(GROUP_DIM, idx, idx, GROUP_DIM ** -0.5)
    return pl.pallas_call(
        _chanmat_kernel,
        out_shape=jax.ShapeDtypeStruct((N_GROUPS, GROUP_DIM, 2 * GROUP_DIM), BF16),
        name="chanmat",
    )(jnp.asarray(cc, F32), jnp.asarray(sc, F32), w_fourier)


def _chan_kernel(u_ref, m_ref, a_ref, b_ref):
    for g in range(N_GROUPS):
        gs = slice(g * GROUP_DIM, (g + 1) * GROUP_DIM)
        ab = jnp.dot(u_ref[:, gs], m_ref[g], preferred_element_type=F32)
        a_ref[:, gs] = ab[:, :GROUP_DIM].astype(BF16)
        b_ref[:, gs] = ab[:, GROUP_DIM:].astype(BF16)


def _channel_stage(u, chan_mats, tm=1024):
    t = u.shape[0]
    tok_spec = pl.BlockSpec((tm, FOURIER_WIDTH), lambda i: (i, 0))
    return pl.pallas_call(
        _chan_kernel,
        grid=(t // tm,),
        in_specs=[tok_spec,
                  pl.BlockSpec((N_GROUPS, GROUP_DIM, 2 * GROUP_DIM), lambda i: (0, 0, 0))],
        out_specs=[tok_spec, tok_spec],
        out_shape=[jax.ShapeDtypeStruct((t, FOURIER_WIDTH), BF16)] * 2,
        compiler_params=_params("parallel"),
        name="fourier_chan",
    )(u, chan_mats)


def _dft1_kernel(a_ref, b_ref, c_ref, s_ref, yr_ref, yi_ref):
    a, b, c, s = a_ref[...], b_ref[...], c_ref[...], s_ref[...]
    dot = functools.partial(jnp.dot, preferred_element_type=F32)
    yr_ref[...] = (dot(c, a) - dot(s, b)).astype(BF16)
    yi_ref[...] = (dot(s, a) + dot(c, b)).astype(BF16)


def _seq_stage1(a, b, batch, seq, tn=2048):
    s1 = seq // SEQ_RADIX
    cols = SEQ_RADIX * FOURIER_WIDTH
    idx = np.arange(s1)
    c1, sn1 = _dft_cos_sin(s1, idx, idx)
    tok_spec = pl.BlockSpec((None, s1, tn), lambda bb, j: (bb, 0, j))
    mat_spec = pl.BlockSpec((s1, s1), lambda bb, j: (0, 0))
    yr, yi = pl.pallas_call(
        _dft1_kernel,
        grid=(batch, cols // tn),
        in_specs=[tok_spec, tok_spec, mat_spec, mat_spec],
        out_specs=[tok_spec, tok_spec],
        out_shape=[jax.ShapeDtypeStruct((batch, s1, cols), BF16)] * 2,
        compiler_params=_params("parallel", "parallel"),
        name="fourier_seq1",
    )(a.reshape(batch, s1, cols), b.reshape(batch, s1, cols),
      jnp.asarray(c1, F32).astype(BF16), jnp.asarray(sn1, F32).astype(BF16))
    return yr, yi


def _dft2_kernel(yr_ref, yi_ref, ec_ref, es_ref, gf_ref, o_ref, *, group):
    for t in range(group):
        rows = slice(t * SEQ_RADIX, (t + 1) * SEQ_RADIX)
        cols = slice(t * FOURIER_WIDTH, (t + 1) * FOURIER_WIDTH)
        x = (jnp.dot(ec_ref[t], yr_ref[rows, :], preferred_element_type=F32)
             + jnp.dot(es_ref[t], yi_ref[rows, :], preferred_element_type=F32))
        o_ref[:, cols] = (x * gf_ref[:, cols].astype(F32)).astype(BF16)


def _seq_stage2(yr, yi, gf, batch, seq, group=16):
    s1 = seq // SEQ_RADIX
    k = (np.arange(s1)[:, None] + s1 * np.arange(SEQ_RADIX)[None, :]).reshape(-1)
    ec, es = _dft_cos_sin(seq, k, np.arange(SEQ_RADIX), seq ** -0.5)
    ec = jnp.asarray(ec.reshape(s1, SEQ_RADIX, SEQ_RADIX), F32).astype(BF16)
    es = jnp.asarray(-es.reshape(s1, SEQ_RADIX, SEQ_RADIX), F32).astype(BF16)
    y_spec = pl.BlockSpec((None, group * SEQ_RADIX, FOURIER_WIDTH), lambda bb, kb: (bb, kb, 0))
    e_spec = pl.BlockSpec((group, SEQ_RADIX, SEQ_RADIX), lambda bb, kb: (kb, 0, 0))
    g_spec = pl.BlockSpec((None, SEQ_RADIX, group * FOURIER_WIDTH), lambda bb, kb: (bb, 0, kb))
    out = pl.pallas_call(
        functools.partial(_dft2_kernel, group=group),
        grid=(batch, s1 // group),
        in_specs=[y_spec, y_spec, e_spec, e_spec, g_spec],
        out_specs=g_spec,
        out_shape=jax.ShapeDtypeStruct((batch, SEQ_RADIX, s1 * FOURIER_WIDTH), BF16),
        compiler_params=_params("parallel", "parallel"),
        name="fourier_seq2",
    )(yr.reshape(batch, seq, FOURIER_WIDTH), yi.reshape(batch, seq, FOURIER_WIDTH),
      ec, es, gf.reshape(batch, SEQ_RADIX, s1 * FOURIER_WIDTH))
    return out.reshape(batch * seq, FOURIER_WIDTH)


def _outproj_kernel(x_ref, o1_ref, o2_ref, o3_ref, l1_ref, l2_ref, l3_ref,
                    ga_ref, mf_ref, w_ref, y_ref):
    l1, l2, l3 = l1_ref[...], l2_ref[...], l3_ref[...]
    m = jnp.maximum(jnp.maximum(l1, l2), l3)
    e1, e2, e3 = jnp.exp(l1 - m), jnp.exp(l2 - m), jnp.exp(l3 - m)
    inv = 1.0 / (e1 + e2 + e3)
    w1, w2, w3 = e1 * inv, e2 * inv, e3 * inv
    parts = []
    for h in range(N_HEADS):
        hs = slice(h * HEAD_DIM, (h + 1) * HEAD_DIM)
        a = (w1[:, h:h + 1] * o1_ref[:, hs].astype(F32)
             + w2[:, h:h + 1] * o2_ref[:, hs].astype(F32)
             + w3[:, h:h + 1] * o3_ref[:, hs].astype(F32))
        parts.append((a * ga_ref[:, hs].astype(F32)).astype(BF16))
    mix_a = jnp.concatenate(parts, axis=1)
    y = x_ref[...]
    y = y + jnp.dot(mix_a, w_ref[:ATTN_WIDTH, :], preferred_element_type=F32)
    y = y + jnp.dot(mf_ref[...], w_ref[ATTN_WIDTH:, :], preferred_element_type=F32)
    y_ref[...] = y


def _outproj(x2d, outs, lses, qkvg, mix_f, w_out, tm=512):
    t = x2d.shape[0]
    a_spec = pl.BlockSpec((tm, ATTN_WIDTH), lambda i: (i, 0))
    l_spec = pl.BlockSpec((tm, HEAD_DIM), lambda i: (i, 0))
    x_spec = pl.BlockSpec((tm, D_MODEL), lambda i: (i, 0))
    return pl.pallas_call(
        _outproj_kernel,
        grid=(t // tm,),
        in_specs=[x_spec, a_spec, a_spec, a_spec, l_spec, l_spec, l_spec,
                  pl.BlockSpec((tm, ATTN_WIDTH), lambda i: (i, 3)),
                  a_spec,
                  pl.BlockSpec((D_MODEL, D_MODEL), lambda i: (0, 0))],
        out_specs=x_spec,
        out_shape=jax.ShapeDtypeStruct((t, D_MODEL), F32),
        compiler_params=_params("parallel"),
        name="outproj",
    )(x2d, *outs, *lses, qkvg, mix_f, w_out)


def _layer(x, gain, w_in, qg, kg, chan_mats, w_out):
    batch, seq, _ = x.shape
    x2d = x.reshape(batch * seq, D_MODEL)
    qkvg, u, gf = _inproj(x2d, seq, gain, w_in, qg, kg)
    outs, lses = [], []
    for d in DILATIONS:
        o, lse = _attention_pattern(qkvg, batch, seq, d)
        outs.append(o)
        lses.append(lse)
    a, b = _channel_stage(u, chan_mats)
    yr, yi = _seq_stage1(a, b, batch, seq)
    mix_f = _seq_stage2(yr, yi, gf, batch, seq)
    y = _outproj(x2d, outs, lses, qkvg, mix_f, w_out)
    return y.reshape(batch, seq, D_MODEL)


def kernel(x_prompt, x_sample, rms_gain, w_in, q_norm_gain, k_norm_gain, w_fourier, w_out):
    depth = rms_gain.shape[0]
    for l in range(depth):
        gain = rms_gain[l].reshape(1, D_MODEL)
        w_in_l = w_in[l].astype(BF16)
        w_out_l = w_out[l].astype(BF16)
        qg = q_norm_gain[l].reshape(1, HEAD_DIM)
        kg = k_norm_gain[l].reshape(1, HEAD_DIM)
        chan_mats = _channel_matrices(w_fourier[l])
        x_prompt = _layer(x_prompt, gain, w_in_l, qg, kg, chan_mats, w_out_l)
        x_sample = _layer(x_sample, gain, w_in_l, qg, kg, chan_mats, w_out_l)
    return (x_prompt, x_sample)
```

```python
import functools

import jax
import jax.numpy as jnp
import numpy as np
from jax import lax
from jax.experimental import pallas as pl
from jax.experimental.pallas import tpu as pltpu

D_MODEL = 2048
ATTN_WIDTH = 1024
FOURIER_WIDTH = 1024
HEAD_DIM = 128
N_HEADS = ATTN_WIDTH // HEAD_DIM
N_GROUPS = 4
GROUP_DIM = FOURIER_WIDTH // N_GROUPS
ROPE_THETA = 500000.0
ROPE_DIM = HEAD_DIM // 4
ROPE_HALF = ROPE_DIM // 2
HALF_KEYS = 64
RMS_EPS = 1e-6
RADIX = 16
ROW_GROUP = RADIX * RADIX
Q_BLOCK = 128
MASK_VALUE = -1e30

VMEM_LIMIT_BYTES = 56 * 1024 * 1024

F32 = jnp.float32
BF16 = jnp.bfloat16


def _params(*semantics):
    return pltpu.CompilerParams(dimension_semantics=semantics, vmem_limit_bytes=VMEM_LIMIT_BYTES)


def _resident(shape):
    return pl.BlockSpec(shape, lambda *_: (0,) * len(shape), pipeline_mode=pl.Buffered(1))


def _rope_tables(seq, scale):
    expo = np.arange(ROPE_HALF, dtype=np.float32) / np.float32(ROPE_HALF)
    inv_freq = (np.float32(1.0) / np.power(np.float32(ROPE_THETA), expo)).astype(np.float32)
    ang = (np.arange(seq, dtype=np.float32)[:, None] * inv_freq[None, :]).astype(np.float64)
    cos, sin = np.cos(ang), np.sin(ang)
    a = np.ones((seq, HEAD_DIM))
    b = np.zeros((seq, HEAD_DIM))
    c = np.zeros((seq, HEAD_DIM))
    a[:, :ROPE_HALF] = cos
    a[:, ROPE_HALF:ROPE_DIM] = cos
    b[:, ROPE_HALF:ROPE_DIM] = sin
    c[:, :ROPE_HALF] = -sin
    return tuple(jnp.asarray((t * scale).astype(np.float32)) for t in (a, b, c))


def _dft_cos_sin(n, rows, cols, scale=1.0):
    m = (np.asarray(rows, dtype=np.int64)[:, None] * np.asarray(cols, dtype=np.int64)[None, :]) % n
    ang = 2.0 * np.pi * m.astype(np.float64) / n
    return np.cos(ang) * scale, np.sin(ang) * scale


def _group_permutation():
    p = np.zeros((ROW_GROUP, ROW_GROUP), np.float32)
    j, r = np.meshgrid(np.arange(RADIX), np.arange(RADIX), indexing="ij")
    p[(RADIX * r + j).ravel(), (RADIX * j + r).ravel()] = 1.0
    return p


def _bf16_const(a):
    return jnp.asarray(np.asarray(a, np.float32)).astype(BF16)


def _silu(a):
    return a * (1.0 / (1.0 + jnp.exp(-a)))


def _inproj_kernel(x_ref, gain_ref, w_ref, qg_ref, kg_ref,
                   qa_ref, qb_ref, qc_ref, ka_ref, kb_ref, kc_ref, perm_ref,
                   q16_ref, k1_ref, k16_ref, v1_ref, v16_ref, ga_ref, u16_ref, gf_ref):
    x = x_ref[...]
    ms = jnp.mean(x * x, axis=-1, keepdims=True)
    h = (x * lax.rsqrt(ms + RMS_EPS) * gain_ref[...]).astype(BF16)
    perm = perm_ref[...]

    def column_tile(j):
        cols = slice(j * ATTN_WIDTH, (j + 1) * ATTN_WIDTH)
        return jnp.dot(h, w_ref[:, cols], preferred_element_type=F32)

    def norm_rope(acc, g_ref, a_ref, b_ref, c_ref):
        heads = []
        for hd in range(N_HEADS):
            a = acc[:, hd * HEAD_DIM:(hd + 1) * HEAD_DIM]
            ms_h = jnp.mean(a * a, axis=-1, keepdims=True)
            n = a * lax.rsqrt(ms_h + RMS_EPS) * g_ref[...]
            r = (n * a_ref[...]
                 + pltpu.roll(n, ROPE_HALF, 1) * b_ref[...]
                 + pltpu.roll(n, HEAD_DIM - ROPE_HALF, 1) * c_ref[...])
            heads.append(r.astype(BF16))
        return jnp.concatenate(heads, axis=1)

    def to_mod16(t):
        return jnp.dot(perm, t, preferred_element_type=F32)

    def store_heads(ref, t):
        for hd in range(N_HEADS):
            ref[hd] = t[:, hd * HEAD_DIM:(hd + 1) * HEAD_DIM]

    def store_head_pieces(ref, t):
        for hd in range(N_HEADS):
            for r in range(RADIX):
                ref[hd, r] = t[r * RADIX:(r + 1) * RADIX, hd * HEAD_DIM:(hd + 1) * HEAD_DIM]

    q = norm_rope(column_tile(0), qg_ref, qa_ref, qb_ref, qc_ref)
    store_head_pieces(q16_ref, to_mod16(q))

    k = norm_rope(column_tile(1), kg_ref, ka_ref, kb_ref, kc_ref)
    store_heads(k1_ref, k)
    store_head_pieces(k16_ref, to_mod16(k).astype(BF16))

    v = column_tile(2).astype(BF16)
    store_heads(v1_ref, v)
    store_head_pieces(v16_ref, to_mod16(v).astype(BF16))

    ga_ref[...] = _silu(column_tile(3)).astype(BF16)

    u16 = to_mod16(column_tile(4).astype(BF16)).astype(BF16)
    for r in range(RADIX):
        u16_ref[r] = u16[r * RADIX:(r + 1) * RADIX, :]

    gf_ref[...] = _silu(column_tile(5)).astype(BF16)


def _inproj(x2d, batch, seq, gain, w_in, qg, kg, perm):
    t = x2d.shape[0]
    groups = seq // ROW_GROUP
    sub = seq // RADIX
    qa, qb, qc = _rope_tables(seq, HEAD_DIM ** -0.5)
    ka, kb, kc = _rope_tables(seq, 1.0)
    rope_spec = pl.BlockSpec((ROW_GROUP, HEAD_DIM), lambda i: (i % groups, 0))
    tok_spec = pl.BlockSpec((ROW_GROUP, ATTN_WIDTH), lambda i: (i, 0))
    head_nat_spec = pl.BlockSpec((N_HEADS, ROW_GROUP, HEAD_DIM), lambda i: (0, i, 0))
    head_m16_spec = pl.BlockSpec((N_HEADS, None, RADIX, RADIX, HEAD_DIM),
                                 lambda i: (0, i // groups, 0, i % groups, 0))
    head_nat = jax.ShapeDtypeStruct((N_HEADS, t, HEAD_DIM), BF16)
    head_m16 = lambda dt: jax.ShapeDtypeStruct((N_HEADS, batch, RADIX, sub, HEAD_DIM), dt)
    return pl.pallas_call(
        _inproj_kernel,
        grid=(t // ROW_GROUP,),
        in_specs=[
            pl.BlockSpec((ROW_GROUP, D_MODEL), lambda i: (i, 0)),
            _resident((1, D_MODEL)),
            _resident(w_in.shape),
            _resident((1, HEAD_DIM)), _resident((1, HEAD_DIM)),
            rope_spec, rope_spec, rope_spec, rope_spec, rope_spec, rope_spec,
            _resident((ROW_GROUP, ROW_GROUP)),
        ],
        out_specs=[
            head_m16_spec, head_nat_spec, head_m16_spec, head_nat_spec, head_m16_spec,
            tok_spec,
            pl.BlockSpec((None, RADIX, RADIX, FOURIER_WIDTH), lambda i: (i // groups, 0, i % groups, 0)),
            tok_spec,
        ],
        out_shape=[
            head_m16(F32), head_nat, head_m16(BF16), head_nat, head_m16(BF16),
            jax.ShapeDtypeStruct((t, ATTN_WIDTH), BF16),
            jax.ShapeDtypeStruct((batch, RADIX, sub, FOURIER_WIDTH), BF16),
            jax.ShapeDtypeStruct((t, FOURIER_WIDTH), BF16),
        ],
        compiler_params=_params("arbitrary"),
        name="inproj",
    )(x2d, gain, w_in, qg, kg, qa, qb, qc, ka, kb, kc, perm)


def _softmax_block(q, k, v, valid):
    s = lax.dot_general(q, k, (((1,), (1,)), ((), ())), preferred_element_type=F32)
    s = jnp.where(valid, s, MASK_VALUE)
    m = jnp.max(s, axis=-1, keepdims=True)
    p = jnp.exp(s - m)
    l = jnp.sum(p, axis=-1, keepdims=True)
    acc = jnp.dot(p.astype(BF16), v, preferred_element_type=F32)
    return m, l, acc


def _attn_kernel(q16_ref, k1_ref, v1_ref, k16_ref, v16_ref, o_ref, acc_s, m_s, l_s, *, seq):
    sub = seq // RADIX
    lanes = (Q_BLOCK, HEAD_DIM)

    def merge_into(pieces, m, l, acc):
        m_old = jnp.concatenate([m_s[r, pl.ds(s0, n), :] for r, s0, n in pieces], axis=0)
        l_old = jnp.concatenate([l_s[r, pl.ds(s0, n), :] for r, s0, n in pieces], axis=0)
        a_old = jnp.concatenate([acc_s[r, pl.ds(s0, n), :] for r, s0, n in pieces], axis=0)
        m_b = jnp.broadcast_to(m, lanes)
        m_new = jnp.maximum(m_old, m_b)
        wa = jnp.exp(m_old - m_new)
        wb = jnp.exp(m_b - m_new)
        a_new = wa * a_old + wb * acc
        l_new = wa * l_old + wb * jnp.broadcast_to(l, lanes)
        off = 0
        for r, s0, n in pieces:
            m_s[r, pl.ds(s0, n), :] = m_new[off:off + n]
            l_s[r, pl.ds(s0, n), :] = l_new[off:off + n]
            acc_s[r, pl.ds(s0, n), :] = a_new[off:off + n]
            off += n

    win16 = min(Q_BLOCK + 2 * HALF_KEYS, sub)
    qi = lax.broadcasted_iota(jnp.int32, (Q_BLOCK, win16), 0)
    kc = lax.broadcasted_iota(jnp.int32, (Q_BLOCK, win16), 1)
    diff16 = qi - kc

    def d16_body(r, carry):
        for blk in range(sub // Q_BLOCK):
            l0 = blk * Q_BLOCK
            ws = min(max(l0 - HALF_KEYS, 0), sub - win16)
            valid = jnp.abs(diff16 + (l0 - ws)) <= HALF_KEYS
            q = q16_ref[r, l0:l0 + Q_BLOCK, :].astype(BF16)
            m, l, acc = _softmax_block(q, k16_ref[r, ws:ws + win16, :], v16_ref[r, ws:ws + win16, :], valid)
            m_s[r, l0:l0 + Q_BLOCK, :] = jnp.broadcast_to(m, lanes)
            l_s[r, l0:l0 + Q_BLOCK, :] = jnp.broadcast_to(l, lanes)
            acc_s[r, l0:l0 + Q_BLOCK, :] = acc
        return carry

    lax.fori_loop(0, RADIX, d16_body, 0)

    q_rows4 = Q_BLOCK // 4
    k_rows4 = q_rows4 + 2 * HALF_KEYS // 4
    qi = lax.broadcasted_iota(jnp.int32, (Q_BLOCK, 4 * k_rows4), 0)
    kc = lax.broadcasted_iota(jnp.int32, (Q_BLOCK, 4 * k_rows4), 1)
    diff4 = 4 * (qi % q_rows4 - kc % k_rows4) + (qi // q_rows4 - kc // k_rows4)

    def d4_body(r4, carry):
        for blk in range(sub // q_rows4):
            l0 = blk * q_rows4
            ws = min(max(l0 - HALF_KEYS // 4, 0), sub - k_rows4)
            valid = jnp.abs(diff4 + 4 * (l0 - ws)) <= HALF_KEYS
            q = jnp.concatenate([q16_ref[r4 + 4 * a, l0:l0 + q_rows4, :] for a in range(4)], axis=0)
            k = jnp.concatenate([k16_ref[r4 + 4 * a, ws:ws + k_rows4, :] for a in range(4)], axis=0)
            v = jnp.concatenate([v16_ref[r4 + 4 * a, ws:ws + k_rows4, :] for a in range(4)], axis=0)
            m, l, acc = _softmax_block(q.astype(BF16), k, v, valid)
            merge_into([(r4 + 4 * a, l0, q_rows4) for a in range(4)], m, l, acc)
        return carry

    lax.fori_loop(0, 4, d4_body, 0)

    q_rows1 = Q_BLOCK // RADIX
    win1 = Q_BLOCK + 2 * HALF_KEYS
    qi = lax.broadcasted_iota(jnp.int32, (Q_BLOCK, win1), 0)
    kc = lax.broadcasted_iota(jnp.int32, (Q_BLOCK, win1), 1)
    diff1 = RADIX * (qi % q_rows1) + qi // q_rows1 - kc

    def d1_body(i, carry):
        n0 = i * Q_BLOCK
        l0 = pl.multiple_of(i * q_rows1, q_rows1)
        ws = pl.multiple_of(jnp.clip(n0 - HALF_KEYS, 0, seq - win1), HALF_KEYS)
        valid = jnp.abs(diff1 + (n0 - ws)) <= HALF_KEYS
        q = jnp.concatenate([q16_ref[r, pl.ds(l0, q_rows1), :] for r in range(RADIX)], axis=0)
        m, l, acc = _softmax_block(q.astype(BF16), k1_ref[pl.ds(ws, win1), :], v1_ref[pl.ds(ws, win1), :], valid)
        merge_into([(r, l0, q_rows1) for r in range(RADIX)], m, l, acc)
        return carry

    lax.fori_loop(0, seq // Q_BLOCK, d1_body, 0)

    def finish(r, carry):
        o_ref[r] = (acc_s[r] * (1.0 / l_s[r])).astype(BF16)
        return carry

    lax.fori_loop(0, RADIX, finish, 0)


def _attention(q16, k1, v1, k16, v16, batch, seq):
    sub = seq // RADIX
    m16_spec = pl.BlockSpec((None, None, RADIX, sub, HEAD_DIM), lambda b, h: (h, b, 0, 0, 0))
    nat_spec = pl.BlockSpec((None, seq, HEAD_DIM), lambda b, h: (h, b, 0))
    state = pltpu.VMEM((RADIX, sub, HEAD_DIM), F32)
    return pl.pallas_call(
        functools.partial(_attn_kernel, seq=seq),
        grid=(batch, N_HEADS),
        in_specs=[m16_spec, nat_spec, nat_spec, m16_spec, m16_spec],
        out_specs=m16_spec,
        out_shape=jax.ShapeDtypeStruct((N_HEADS, batch, RADIX, sub, HEAD_DIM), BF16),
        scratch_shapes=[state, state, state],
        compiler_params=_params("parallel", "parallel"),
        name="attention",
    )(q16, k1, v1, k16, v16)


def _chanmat_kernel(c_ref, s_ref, w_ref, m_ref):
    for g in range(N_GROUPS):
        w = w_ref[g]
        mc = jnp.dot(c_ref[...], w, preferred_element_type=F32, precision=lax.Precision.HIGHEST)
        ms = jnp.dot(s_ref[...], w, preferred_element_type=F32, precision=lax.Precision.HIGHEST)
        m_ref[g, :, :GROUP_DIM] = mc.astype(BF16)
        m_ref[g, :, GROUP_DIM:] = ms.astype(BF16)


def _channel_matrices(w_fourier):
    idx = np.arange(GROUP_DIM)
    cc, sc = _dft_cos_sin(GROUP_DIM, idx, idx, GROUP_DIM ** -0.5)
    return pl.pallas_call(
        _chanmat_kernel,
        out_shape=jax.ShapeDtypeStruct((N_GROUPS, GROUP_DIM, 2 * GROUP_DIM), BF16),
        name="chanmat",
    )(jnp.asarray(cc, F32), jnp.asarray(sc, F32), w_fourier)


def _fourier1_kernel(u_ref, m_ref, c_ref, s_ref, yr_ref, yi_ref):
    dot = functools.partial(jnp.dot, preferred_element_type=F32)
    a_parts, b_parts = [], []
    for g in range(N_GROUPS):
        ab = dot(u_ref[:, g * GROUP_DIM:(g + 1) * GROUP_DIM], m_ref[g])
        a_parts.append(ab[:, :GROUP_DIM].astype(BF16))
        b_parts.append(ab[:, GROUP_DIM:].astype(BF16))
    a = jnp.concatenate(a_parts, axis=1)
    b = jnp.concatenate(b_parts, axis=1)
    c, s = c_ref[...], s_ref[...]
    yr_ref[...] = (dot(c, a) - dot(s, b)).astype(BF16)
    yi_ref[...] = (dot(s, a) + dot(c, b)).astype(BF16)


def _fourier_stage1(u16, chan_mats, batch, seq):
    s1 = seq // RADIX
    idx = np.arange(s1)
    c1, sn1 = _dft_cos_sin(s1, idx, idx)
    row_spec = pl.BlockSpec((None, None, s1, FOURIER_WIDTH), lambda b, n2: (b, n2, 0, 0))
    y = jax.ShapeDtypeStruct((batch, RADIX, s1, FOURIER_WIDTH), BF16)
    return pl.pallas_call(
        _fourier1_kernel,
        grid=(batch, RADIX),
        in_specs=[row_spec, _resident(chan_mats.shape), _resident((s1, s1)), _resident((s1, s1))],
        out_specs=[row_spec, row_spec],
        out_shape=[y, y],
        compiler_params=_params("parallel", "parallel"),
        name="fourier_seq1",
    )(u16, chan_mats, _bf16_const(c1), _bf16_const(sn1))


def _fourier2_kernel(yr_ref, yi_ref, mc_ref, ms_ref, gf_ref, o_ref, *, tiles):
    dot = functools.partial(jnp.dot, preferred_element_type=F32)
    for t in range(tiles):
        rows = slice(t * RADIX, (t + 1) * RADIX)
        yr = jnp.concatenate([yr_ref[n2, rows, :] for n2 in range(RADIX)], axis=0)
        yi = jnp.concatenate([yi_ref[n2, rows, :] for n2 in range(RADIX)], axis=0)
        x = dot(mc_ref[t], yr) + dot(ms_ref[t], yi)
        for k2 in range(RADIX):
            piece = x[k2 * RADIX:(k2 + 1) * RADIX, :] * gf_ref[k2, rows, :].astype(F32)
            o_ref[k2, rows, :] = piece.astype(BF16)


def _stage2_matrices(seq):
    s1 = seq // RADIX
    tiles = s1 // RADIX
    c, k2, k1, n2 = np.meshgrid(np.arange(tiles), np.arange(RADIX), np.arange(RADIX), np.arange(RADIX),
                                indexing="ij")
    k = RADIX * c + k1 + s1 * k2
    ang = 2.0 * np.pi * ((n2 * k) % seq).astype(np.float64) / seq
    mc = np.zeros((tiles, ROW_GROUP, ROW_GROUP), np.float32)
    ms = np.zeros((tiles, ROW_GROUP, ROW_GROUP), np.float32)
    mc[c, RADIX * k2 + k1, RADIX * n2 + k1] = np.cos(ang) * seq ** -0.5
    ms[c, RADIX * k2 + k1, RADIX * n2 + k1] = -np.sin(ang) * seq ** -0.5
    return _bf16_const(mc), _bf16_const(ms)


def _fourier_stage2(yr, yi, gf, batch, seq, tiles=4):
    s1 = seq // RADIX
    mc, ms = _stage2_matrices(seq)
    rows = tiles * RADIX
    y_spec = pl.BlockSpec((None, RADIX, rows, FOURIER_WIDTH), lambda b, c: (b, 0, c, 0))
    m_spec = pl.BlockSpec((tiles, ROW_GROUP, ROW_GROUP), lambda b, c: (c, 0, 0))
    out = pl.pallas_call(
        functools.partial(_fourier2_kernel, tiles=tiles),
        grid=(batch, s1 // rows),
        in_specs=[y_spec, y_spec, m_spec, m_spec, y_spec],
        out_specs=y_spec,
        out_shape=jax.ShapeDtypeStruct((batch, RADIX, s1, FOURIER_WIDTH), BF16),
        compiler_params=_params("parallel", "parallel"),
        name="fourier_seq2",
    )(yr, yi, mc, ms, gf.reshape(batch, RADIX, s1, FOURIER_WIDTH))
    return out.reshape(batch * seq, FOURIER_WIDTH)


def _outproj_kernel(x_ref, o16_ref, ga_ref, mf_ref, unperm_ref, w_ref, y_ref):
    heads = [jnp.concatenate([o16_ref[hd, r] for r in range(RADIX)], axis=0) for hd in range(N_HEADS)]
    attn16 = jnp.concatenate(heads, axis=1)
    attn = jnp.dot(unperm_ref[...], attn16, preferred_element_type=F32)
    mix_a = (attn * ga_ref[...].astype(F32)).astype(BF16)
    y = x_ref[...]
    y = y + jnp.dot(mix_a, w_ref[:ATTN_WIDTH, :], preferred_element_type=F32)
    y = y + jnp.dot(mf_ref[...], w_ref[ATTN_WIDTH:, :], preferred_element_type=F32)
    y_ref[...] = y


def _outproj(x2d, o16, ga, mix_f, unperm, w_out, seq):
    t = x2d.shape[0]
    groups = seq // ROW_GROUP
    tok_spec = pl.BlockSpec((ROW_GROUP, ATTN_WIDTH), lambda i: (i, 0))
    x_spec = pl.BlockSpec((ROW_GROUP, D_MODEL), lambda i: (i, 0))
    return pl.pallas_call(
        _outproj_kernel,
        grid=(t // ROW_GROUP,),
        in_specs=[x_spec,
                  pl.BlockSpec((N_HEADS, None, RADIX, RADIX, HEAD_DIM),
                               lambda i: (0, i // groups, 0, i % groups, 0)),
                  tok_spec, tok_spec,
                  _resident((ROW_GROUP, ROW_GROUP)),
                  _resident((D_MODEL, D_MODEL))],
        out_specs=x_spec,
        out_shape=jax.ShapeDtypeStruct((t, D_MODEL), F32),
        compiler_params=_params("arbitrary"),
        name="outproj",
    )(x2d, o16, ga, mix_f, unperm, w_out)


def _layer(x, gain, w_in, qg, kg, chan_mats, w_out, perm, unperm):
    batch, seq, _ = x.shape
    x2d = x.reshape(batch * seq, D_MODEL)
    q16, k1, k16, v1, v16, ga, u16, gf = _inproj(x2d, batch, seq, gain, w_in, qg, kg, perm)
    o16 = _attention(q16, k1, v1, k16, v16, batch, seq)
    yr, yi = _fourier_stage1(u16, chan_mats, batch, seq)
    mix_f = _fourier_stage2(yr, yi, gf, batch, seq)
    y = _outproj(x2d, o16, ga, mix_f, unperm, w_out, seq)
    return y.reshape(batch, seq, D_MODEL)


def kernel(x_prompt, x_sample, rms_gain, w_in, q_norm_gain, k_norm_gain, w_fourier, w_out):
    depth = rms_gain.shape[0]
    p = _group_permutation()
    perm, unperm = _bf16_const(p), _bf16_const(p.T)
    for l in range(depth):
        gain = rms_gain[l].reshape(1, D_MODEL)
        w_in_l = w_in[l].astype(BF16)
        w_out_l = w_out[l].astype(BF16)
        qg = q_norm_gain[l].reshape(1, HEAD_DIM)
        kg = k_norm_gain[l].reshape(1, HEAD_DIM)
        chan_mats = _channel_matrices(w_fourier[l])
        x_prompt = _layer(x_prompt, gain, w_in_l, qg, kg, chan_mats, w_out_l, perm, unperm)
        x_sample = _layer(x_sample, gain, w_in_l, qg, kg, chan_mats, w_out_l, perm, unperm)
    return (x_prompt, x_sample)
```

```python
import functools
import math

import jax
import jax.numpy as jnp
import numpy as np
from jax import lax
from jax.experimental import pallas as pl
from jax.experimental.pallas import tpu as pltpu

D_MODEL = 2048
ATTN_WIDTH = 1024
FOURIER_WIDTH = 1024
HEAD_DIM = 128
N_HEADS = ATTN_WIDTH // HEAD_DIM
N_GROUPS = 4
GROUP_DIM = FOURIER_WIDTH // N_GROUPS
ROPE_THETA = 500000.0
ROPE_DIM = HEAD_DIM // 4
ROPE_HALF = ROPE_DIM // 2
HALF_KEYS = 64
RMS_EPS = 1e-6
RADIX = 16
ROW_GROUP = RADIX * RADIX
Q_BLOCK = 128
BLOCKS_PER_TRIP = 8
MASK_VALUE = -1e30

VMEM_LIMIT_BYTES = 56 * 1024 * 1024

F32 = jnp.float32
BF16 = jnp.bfloat16


def _params(*semantics):
    return pltpu.CompilerParams(dimension_semantics=semantics, vmem_limit_bytes=VMEM_LIMIT_BYTES)


def _resident(shape):
    return pl.BlockSpec(shape, lambda *_: (0,) * len(shape), pipeline_mode=pl.Buffered(1))


def _rope_tables(seq, scale):
    expo = np.arange(ROPE_HALF, dtype=np.float32) / np.float32(ROPE_HALF)
    inv_freq = (np.float32(1.0) / np.power(np.float32(ROPE_THETA), expo)).astype(np.float32)
    ang = (np.arange(seq, dtype=np.float32)[:, None] * inv_freq[None, :]).astype(np.float64)
    cos, sin = np.cos(ang), np.sin(ang)
    a = np.ones((seq, HEAD_DIM))
    b = np.zeros((seq, HEAD_DIM))
    c = np.zeros((seq, HEAD_DIM))
    a[:, :ROPE_HALF] = cos
    a[:, ROPE_HALF:ROPE_DIM] = cos
    b[:, ROPE_HALF:ROPE_DIM] = sin
    c[:, :ROPE_HALF] = -sin
    return tuple(jnp.asarray((t * scale).astype(np.float32)) for t in (a, b, c))


def _dft_cos_sin(n, rows, cols, scale=1.0):
    m = (np.asarray(rows, dtype=np.int64)[:, None] * np.asarray(cols, dtype=np.int64)[None, :]) % n
    ang = 2.0 * np.pi * m.astype(np.float64) / n
    return np.cos(ang) * scale, np.sin(ang) * scale


def _group_permutation():
    p = np.zeros((ROW_GROUP, ROW_GROUP), np.float32)
    j, r = np.meshgrid(np.arange(RADIX), np.arange(RADIX), indexing="ij")
    p[(RADIX * r + j).ravel(), (RADIX * j + r).ravel()] = 1.0
    return p


def _bf16_const(a):
    return jnp.asarray(np.asarray(a, np.float32)).astype(BF16)


def _silu(a):
    return a * (1.0 / (1.0 + jnp.exp(-a)))


def _inproj_kernel(x_ref, gain_ref, w_ref, qg_ref, kg_ref,
                   qa_ref, qb_ref, qc_ref, ka_ref, kb_ref, kc_ref, perm_ref,
                   q16_ref, k1_ref, k16_ref, v1_ref, v16_ref, ga_ref, u16_ref, gf_ref):
    x = x_ref[...]
    ms = jnp.mean(x * x, axis=-1, keepdims=True)
    h = (x * lax.rsqrt(ms + RMS_EPS) * gain_ref[...]).astype(BF16)
    perm = perm_ref[...]

    def column_tile(j):
        cols = slice(j * ATTN_WIDTH, (j + 1) * ATTN_WIDTH)
        return jnp.dot(h, w_ref[:, cols], preferred_element_type=F32)

    def norm_rope(acc, g_ref, a_ref, b_ref, c_ref):
        heads = []
        for hd in range(N_HEADS):
            a = acc[:, hd * HEAD_DIM:(hd + 1) * HEAD_DIM]
            ms_h = jnp.mean(a * a, axis=-1, keepdims=True)
            n = a * lax.rsqrt(ms_h + RMS_EPS) * g_ref[...]
            r = (n * a_ref[...]
                 + pltpu.roll(n, ROPE_HALF, 1) * b_ref[...]
                 + pltpu.roll(n, HEAD_DIM - ROPE_HALF, 1) * c_ref[...])
            heads.append(r.astype(BF16))
        return jnp.concatenate(heads, axis=1)

    def to_mod16(t):
        return jnp.dot(perm, t, preferred_element_type=F32)

    def store_heads(ref, t):
        for hd in range(N_HEADS):
            ref[hd] = t[:, hd * HEAD_DIM:(hd + 1) * HEAD_DIM]

    def store_head_pieces(ref, t):
        for hd in range(N_HEADS):
            for r in range(RADIX):
                ref[hd, r] = t[r * RADIX:(r + 1) * RADIX, hd * HEAD_DIM:(hd + 1) * HEAD_DIM]

    q = norm_rope(column_tile(0), qg_ref, qa_ref, qb_ref, qc_ref)
    store_head_pieces(q16_ref, to_mod16(q))

    k = norm_rope(column_tile(1), kg_ref, ka_ref, kb_ref, kc_ref)
    store_heads(k1_ref, k)
    store_head_pieces(k16_ref, to_mod16(k).astype(BF16))

    v = column_tile(2).astype(BF16)
    store_heads(v1_ref, v)
    store_head_pieces(v16_ref, to_mod16(v).astype(BF16))

    ga_ref[...] = _silu(column_tile(3)).astype(BF16)

    u16 = to_mod16(column_tile(4).astype(BF16)).astype(BF16)
    for r in range(RADIX):
        u16_ref[r] = u16[r * RADIX:(r + 1) * RADIX, :]

    gf_ref[...] = _silu(column_tile(5)).astype(BF16)


def _inproj(x2d, batch, seq, gain, w_in, qg, kg, perm):
    t = x2d.shape[0]
    groups = seq // ROW_GROUP
    sub = seq // RADIX
    qa, qb, qc = _rope_tables(seq, HEAD_DIM ** -0.5 * math.log2(math.e))
    ka, kb, kc = _rope_tables(seq, 1.0)
    rope_spec = pl.BlockSpec((ROW_GROUP, HEAD_DIM), lambda i: (i % groups, 0))
    tok_spec = pl.BlockSpec((ROW_GROUP, ATTN_WIDTH), lambda i: (i, 0))
    head_nat_spec = pl.BlockSpec((N_HEADS, ROW_GROUP, HEAD_DIM), lambda i: (0, i, 0))
    head_m16_spec = pl.BlockSpec((N_HEADS, None, RADIX, RADIX, HEAD_DIM),
                                 lambda i: (0, i // groups, 0, i % groups, 0))
    head_nat = jax.ShapeDtypeStruct((N_HEADS, t, HEAD_DIM), BF16)
    head_m16 = lambda dt: jax.ShapeDtypeStruct((N_HEADS, batch, RADIX, sub, HEAD_DIM), dt)
    return pl.pallas_call(
        _inproj_kernel,
        grid=(t // ROW_GROUP,),
        in_specs=[
            pl.BlockSpec((ROW_GROUP, D_MODEL), lambda i: (i, 0)),
            _resident((1, D_MODEL)),
            _resident(w_in.shape),
            _resident((1, HEAD_DIM)), _resident((1, HEAD_DIM)),
            rope_spec, rope_spec, rope_spec, rope_spec, rope_spec, rope_spec,
            _resident((ROW_GROUP, ROW_GROUP)),
        ],
        out_specs=[
            head_m16_spec, head_nat_spec, head_m16_spec, head_nat_spec, head_m16_spec,
            tok_spec,
            pl.BlockSpec((None, RADIX, RADIX, FOURIER_WIDTH), lambda i: (i // groups, 0, i % groups, 0)),
            tok_spec,
        ],
        out_shape=[
            head_m16(F32), head_nat, head_m16(BF16), head_nat, head_m16(BF16),
            jax.ShapeDtypeStruct((t, ATTN_WIDTH), BF16),
            jax.ShapeDtypeStruct((batch, RADIX, sub, FOURIER_WIDTH), BF16),
            jax.ShapeDtypeStruct((t, FOURIER_WIDTH), BF16),
        ],
        compiler_params=_params("arbitrary"),
        name="inproj",
    )(x2d, gain, w_in, qg, kg, qa, qb, qc, ka, kb, kc, perm)


def _window_start(first, half, total, window):
    return min(max(first - half, 0), total - window)


class _Pattern:
    def __init__(self, q_first, k_first, diff0, scale):
        deltas = [scale * (q0 - k0) for q0, k0 in zip(q_first, k_first)]
        uniq = sorted(set(deltas))
        self.q_first, self.k_first = q_first, k_first
        self.table_of = [uniq.index(d) for d in deltas]
        self.bias = jnp.asarray(np.stack(
            [np.where(np.abs(diff0 + d) <= HALF_KEYS, 0.0, MASK_VALUE) for d in uniq]).astype(np.float32))


def _attention_patterns(seq):
    sub = seq // RADIX
    qi = np.arange(Q_BLOCK)[:, None]
    win16 = min(Q_BLOCK + 2 * HALF_KEYS, sub)
    q16 = list(range(0, sub, Q_BLOCK))
    p16 = _Pattern(q16, [_window_start(q0, HALF_KEYS, sub, win16) for q0 in q16],
                   qi - np.arange(win16)[None, :], 1)
    q_rows, k_rows = Q_BLOCK // 4, Q_BLOCK // 4 + 2 * HALF_KEYS // 4
    kc = np.arange(4 * k_rows)[None, :]
    q4 = list(range(0, sub, q_rows))
    p4 = _Pattern(q4, [_window_start(q0, HALF_KEYS // 4, sub, k_rows) for q0 in q4],
                  4 * (qi % q_rows - kc % k_rows) + (qi // q_rows - kc // k_rows), 4)
    win1 = Q_BLOCK + 2 * HALF_KEYS
    q1 = list(range(0, seq, Q_BLOCK))
    p1 = _Pattern(q1, [_window_start(q0, HALF_KEYS, seq, win1) for q0 in q1],
                  RADIX * (qi % (Q_BLOCK // RADIX)) + qi // (Q_BLOCK // RADIX) - np.arange(win1)[None, :], 1)
    return p16, p4, p1


def _attn_kernel(q16_ref, k1_ref, v1_ref, k16_ref, v16_ref, b16_ref, b4_ref, b1_ref,
                 o_ref, acc_s, m_s, l_s, *, seq, pat16, pat4, pat1):
    sub = seq // RADIX

    def gather(ref, pieces):
        return jnp.concatenate([ref[r, pl.ds(s0, n), :] for r, s0, n in pieces], axis=0)

    def run_blocks(blocks, first):
        scores = []
        for q, k, _, bias, _ in blocks:
            s = lax.dot_general(q(), k(), (((1,), (1,)), ((), ())), preferred_element_type=F32) + bias()
            m = jnp.broadcast_to(jnp.max(s, axis=-1, keepdims=True), (Q_BLOCK, HEAD_DIM))
            scores.append((s, m))
        partial = []
        for (s, m), (_, _, v, _, pieces) in zip(scores, blocks):
            m_old = None if first else gather(m_s, pieces)
            m_new = m if first else jnp.maximum(m_old, m)
            p = jnp.exp2(s - jnp.concatenate([m_new] * (s.shape[1] // HEAD_DIM), axis=1))
            vw = v()
            v_aug = jnp.concatenate([vw, jnp.ones(vw.shape, BF16)], axis=1)
            partial.append((m_old, m_new, jnp.dot(p.astype(BF16), v_aug, preferred_element_type=F32)))
        for (m_old, m_new, pv), (_, _, _, _, pieces) in zip(partial, blocks):
            acc, l = pv[:, :HEAD_DIM], pv[:, HEAD_DIM:]
            if not first:
                w_old = jnp.exp2(m_old - m_new)
                acc = w_old * gather(acc_s, pieces) + acc
                l = w_old * gather(l_s, pieces) + l
            off = 0
            for r, s0, n in pieces:
                m_s[r, pl.ds(s0, n), :] = m_new[off:off + n]
                l_s[r, pl.ds(s0, n), :] = l[off:off + n]
                acc_s[r, pl.ds(s0, n), :] = acc[off:off + n]
                off += n

    win16 = b16_ref.shape[2]
    n16 = len(pat16.q_first)
    res_per_trip = max(1, BLOCKS_PER_TRIP // n16)

    def d16_body(t, carry):
        blocks = []
        for j in range(res_per_trip):
            r = t * res_per_trip + j
            for blk in range(n16):
                l0, ws = pat16.q_first[blk], pat16.k_first[blk]
                blocks.append((
                    lambda r=r, l0=l0: q16_ref[r, l0:l0 + Q_BLOCK, :].astype(BF16),
                    lambda r=r, ws=ws: k16_ref[r, ws:ws + win16, :],
                    lambda r=r, ws=ws: v16_ref[r, ws:ws + win16, :],
                    lambda blk=blk: b16_ref[pat16.table_of[blk]],
                    [(r, l0, Q_BLOCK)]))
        run_blocks(blocks, first=True)
        return carry

    lax.fori_loop(0, RADIX // res_per_trip, d16_body, 0)

    q_rows4 = Q_BLOCK // 4
    k_rows4 = b4_ref.shape[2] // 4

    n4 = len(pat4.q_first)
    r4_per_trip = max(1, BLOCKS_PER_TRIP // n4)

    def d4_body(t, carry):
        blocks = []
        for j in range(r4_per_trip):
            r4 = t * r4_per_trip + j
            for blk in range(n4):
                l0, ws = pat4.q_first[blk], pat4.k_first[blk]
                res = [r4 + 4 * a for a in range(4)]
                q_pieces = [(r, l0, q_rows4) for r in res]
                k_pieces = [(r, ws, k_rows4) for r in res]
                blocks.append((
                    lambda p=q_pieces: gather(q16_ref, p).astype(BF16),
                    lambda p=k_pieces: gather(k16_ref, p),
                    lambda p=k_pieces: gather(v16_ref, p),
                    lambda blk=blk: b4_ref[pat4.table_of[blk]],
                    q_pieces))
        run_blocks(blocks, first=False)
        return carry

    lax.fori_loop(0, 4 // r4_per_trip, d4_body, 0)

    q_rows1 = Q_BLOCK // RADIX
    win1 = b1_ref.shape[2]
    n1 = len(pat1.q_first)
    per_trip = BLOCKS_PER_TRIP
    assert pat1.table_of[0] == 0 and pat1.table_of[-1] == 2 and set(pat1.table_of[1:-1]) == {1}

    def d1_body(t, carry):
        blocks = []
        for j in range(per_trip):
            i = t * per_trip + j
            n0 = i * Q_BLOCK
            l0 = pl.multiple_of(i * q_rows1, q_rows1)
            ws = pl.multiple_of(jnp.clip(n0 - HALF_KEYS, 0, seq - win1), HALF_KEYS)
            table = jnp.where(i == 0, 0, jnp.where(i == n1 - 1, 2, 1))
            q_pieces = [(r, l0, q_rows1) for r in range(RADIX)]
            blocks.append((
                lambda p=q_pieces: gather(q16_ref, p).astype(BF16),
                lambda ws=ws: k1_ref[pl.ds(ws, win1), :],
                lambda ws=ws: v1_ref[pl.ds(ws, win1), :],
                lambda table=table: b1_ref[table],
                q_pieces))
        run_blocks(blocks, first=False)
        return carry

    lax.fori_loop(0, n1 // per_trip, d1_body, 0)

    def finish(r, carry):
        o_ref[r] = (acc_s[r] * (1.0 / l_s[r])).astype(BF16)
        return carry

    lax.fori_loop(0, RADIX, finish, 0)


def _attention(q16, k1, v1, k16, v16, batch, seq):
    sub = seq // RADIX
    pat16, pat4, pat1 = _attention_patterns(seq)
    m16_spec = pl.BlockSpec((None, None, RADIX, sub, HEAD_DIM), lambda b, h: (h, b, 0, 0, 0))
    nat_spec = pl.BlockSpec((None, seq, HEAD_DIM), lambda b, h: (h, b, 0))
    state = pltpu.VMEM((RADIX, sub, HEAD_DIM), F32)
    return pl.pallas_call(
        functools.partial(_attn_kernel, seq=seq, pat16=pat16, pat4=pat4, pat1=pat1),
        grid=(batch, N_HEADS),
        in_specs=[m16_spec, nat_spec, nat_spec, m16_spec, m16_spec,
                  _resident(pat16.bias.shape), _resident(pat4.bias.shape), _resident(pat1.bias.shape)],
        out_specs=m16_spec,
        out_shape=jax.ShapeDtypeStruct((N_HEADS, batch, RADIX, sub, HEAD_DIM), BF16),
        scratch_shapes=[state, state, state],
        compiler_params=_params("parallel", "parallel"),
        name="attention",
    )(q16, k1, v1, k16, v16, pat16.bias, pat4.bias, pat1.bias)


def _chanmat_kernel(c_ref, s_ref, w_ref, m_ref):
    for g in range(N_GROUPS):
        w = w_ref[g]
        mc = jnp.dot(c_ref[...], w, preferred_element_type=F32, precision=lax.Precision.HIGHEST)
        ms = jnp.dot(s_ref[...], w, preferred_element_type=F32, precision=lax.Precision.HIGHEST)
        m_ref[g, :, :GROUP_DIM] = mc.astype(BF16)
        m_ref[g, :, GROUP_DIM:] = ms.astype(BF16)


def _channel_matrices(w_fourier):
    idx = np.arange(GROUP_DIM)
    cc, sc = _dft_cos_sin(GROUP_DIM, idx, idx, GROUP_DIM ** -0.5)
    return pl.pallas_call(
        _chanmat_kernel,
        out_shape=jax.ShapeDtypeStruct((N_GROUPS, GROUP_DIM, 2 * GROUP_DIM), BF16),
        name="chanmat",
    )(jnp.asarray(cc, F32), jnp.asarray(sc, F32), w_fourier)


def _fourier1_kernel(u_ref, m_ref, c_ref, s_ref, yr_ref, yi_ref):
    dot = functools.partial(jnp.dot, preferred_element_type=F32)
    a_parts, b_parts = [], []
    for g in range(N_GROUPS):
        ab = dot(u_ref[:, g * GROUP_DIM:(g + 1) * GROUP_DIM], m_ref[g])
        a_parts.append(ab[:, :GROUP_DIM].astype(BF16))
        b_parts.append(ab[:, GROUP_DIM:].astype(BF16))
    a = jnp.concatenate(a_parts, axis=1)
    b = jnp.concatenate(b_parts, axis=1)
    c, s = c_ref[...], s_ref[...]
    yr_ref[...] = (dot(c, a) - dot(s, b)).astype(BF16)
    yi_ref[...] = (dot(s, a) + dot(c, b)).astype(BF16)


def _fourier_stage1(u16, chan_mats, batch, seq):
    s1 = seq // RADIX
    idx = np.arange(s1)
    c1, sn1 = _dft_cos_sin(s1, idx, idx)
    row_spec = pl.BlockSpec((None, None, s1, FOURIER_WIDTH), lambda b, n2: (b, n2, 0, 0))
    y = jax.ShapeDtypeStruct((batch, RADIX, s1, FOURIER_WIDTH), BF16)
    return pl.pallas_call(
        _fourier1_kernel,
        grid=(batch, RADIX),
        in_specs=[row_spec, _resident(chan_mats.shape), _resident((s1, s1)), _resident((s1, s1))],
        out_specs=[row_spec, row_spec],
        out_shape=[y, y],
        compiler_params=_params("parallel", "parallel"),
        name="fourier_seq1",
    )(u16, chan_mats, _bf16_const(c1), _bf16_const(sn1))


def _fourier2_kernel(yr_ref, yi_ref, mc_ref, ms_ref, gf_ref, o_ref, *, tiles):
    dot = functools.partial(jnp.dot, preferred_element_type=F32)
    for t in range(tiles):
        rows = slice(t * RADIX, (t + 1) * RADIX)
        yr = jnp.concatenate([yr_ref[n2, rows, :] for n2 in range(RADIX)], axis=0)
        yi = jnp.concatenate([yi_ref[n2, rows, :] for n2 in range(RADIX)], axis=0)
        x = dot(mc_ref[t], yr) + dot(ms_ref[t], yi)
        for k2 in range(RADIX):
            piece = x[k2 * RADIX:(k2 + 1) * RADIX, :] * gf_ref[k2, rows, :].astype(F32)
            o_ref[k2, rows, :] = piece.astype(BF16)


def _stage2_matrices(seq):
    s1 = seq // RADIX
    tiles = s1 // RADIX
    c, k2, k1, n2 = np.meshgrid(np.arange(tiles), np.arange(RADIX), np.arange(RADIX), np.arange(RADIX),
                                indexing="ij")
    k = RADIX * c + k1 + s1 * k2
    ang = 2.0 * np.pi * ((n2 * k) % seq).astype(np.float64) / seq
    mc = np.zeros((tiles, ROW_GROUP, ROW_GROUP), np.float32)
    ms = np.zeros((tiles, ROW_GROUP, ROW_GROUP), np.float32)
    mc[c, RADIX * k2 + k1, RADIX * n2 + k1] = np.cos(ang) * seq ** -0.5
    ms[c, RADIX * k2 + k1, RADIX * n2 + k1] = -np.sin(ang) * seq ** -0.5
    return _bf16_const(mc), _bf16_const(ms)


def _fourier_stage2(yr, yi, gf, batch, seq, tiles=4):
    s1 = seq // RADIX
    mc, ms = _stage2_matrices(seq)
    rows = tiles * RADIX
    y_spec = pl.BlockSpec((None, RADIX, rows, FOURIER_WIDTH), lambda b, c: (b, 0, c, 0))
    m_spec = pl.BlockSpec((tiles, ROW_GROUP, ROW_GROUP), lambda b, c: (c, 0, 0))
    out = pl.pallas_call(
        functools.partial(_fourier2_kernel, tiles=tiles),
        grid=(batch, s1 // rows),
        in_specs=[y_spec, y_spec, m_spec, m_spec, y_spec],
        out_specs=y_spec,
        out_shape=jax.ShapeDtypeStruct((batch, RADIX, s1, FOURIER_WIDTH), BF16),
        compiler_params=_params("parallel", "parallel"),
        name="fourier_seq2",
    )(yr, yi, mc, ms, gf.reshape(batch, RADIX, s1, FOURIER_WIDTH))
    return out.reshape(batch * seq, FOURIER_WIDTH)


def _outproj_kernel(x_ref, o16_ref, ga_ref, mf_ref, unperm_ref, w_ref, y_ref):
    heads = [jnp.concatenate([o16_ref[hd, r] for r in range(RADIX)], axis=0) for hd in range(N_HEADS)]
    attn16 = jnp.concatenate(heads, axis=1)
    attn = jnp.dot(unperm_ref[...], attn16, preferred_element_type=F32)
    mix_a = (attn * ga_ref[...].astype(F32)).astype(BF16)
    y = x_ref[...]
    y = y + jnp.dot(mix_a, w_ref[:ATTN_WIDTH, :], preferred_element_type=F32)
    y = y + jnp.dot(mf_ref[...], w_ref[ATTN_WIDTH:, :], preferred_element_type=F32)
    y_ref[...] = y


def _outproj(x2d, o16, ga, mix_f, unperm, w_out, seq):
    t = x2d.shape[0]
    groups = seq // ROW_GROUP
    tok_spec = pl.BlockSpec((ROW_GROUP, ATTN_WIDTH), lambda i: (i, 0))
    x_spec = pl.BlockSpec((ROW_GROUP, D_MODEL), lambda i: (i, 0))
    return pl.pallas_call(
        _outproj_kernel,
        grid=(t // ROW_GROUP,),
        in_specs=[x_spec,
                  pl.BlockSpec((N_HEADS, None, RADIX, RADIX, HEAD_DIM),
                               lambda i: (0, i // groups, 0, i % groups, 0)),
                  tok_spec, tok_spec,
                  _resident((ROW_GROUP, ROW_GROUP)),
                  _resident((D_MODEL, D_MODEL))],
        out_specs=x_spec,
        out_shape=jax.ShapeDtypeStruct((t, D_MODEL), F32),
        compiler_params=_params("arbitrary"),
        name="outproj",
    )(x2d, o16, ga, mix_f, unperm, w_out)


def _layer(x, gain, w_in, qg, kg, chan_mats, w_out, perm, unperm):
    batch, seq, _ = x.shape
    x2d = x.reshape(batch * seq, D_MODEL)
    q16, k1, k16, v1, v16, ga, u16, gf = _inproj(x2d, batch, seq, gain, w_in, qg, kg, perm)
    o16 = _attention(q16, k1, v1, k16, v16, batch, seq)
    yr, yi = _fourier_stage1(u16, chan_mats, batch, seq)
    mix_f = _fourier_stage2(yr, yi, gf, batch, seq)
    y = _outproj(x2d, o16, ga, mix_f, unperm, w_out, seq)
    return y.reshape(batch, seq, D_MODEL)


def kernel(x_prompt, x_sample, rms_gain, w_in, q_norm_gain, k_norm_gain, w_fourier, w_out):
    depth = rms_gain.shape[0]
    p = _group_permutation()
    perm, unperm = _bf16_const(p), _bf16_const(p.T)
    for l in range(depth):
        gain = rms_gain[l].reshape(1, D_MODEL)
        w_in_l = w_in[l].astype(BF16)
        w_out_l = w_out[l].astype(BF16)
        qg = q_norm_gain[l].reshape(1, HEAD_DIM)
        kg = k_norm_gain[l].reshape(1, HEAD_DIM)
        chan_mats = _channel_matrices(w_fourier[l])
        x_prompt = _layer(x_prompt, gain, w_in_l, qg, kg, chan_mats, w_out_l, perm, unperm)
        x_sample = _layer(x_sample, gain, w_in_l, qg, kg, chan_mats, w_out_l, perm, unperm)
    return (x_prompt, x_sample)
```

```python
import functools
import math

import jax
import jax.numpy as jnp
import numpy as np
from jax import lax
from jax.experimental import pallas as pl
from jax.experimental.pallas import tpu as pltpu

D_MODEL = 2048
ATTN_WIDTH = 1024
FOURIER_WIDTH = 1024
HEAD_DIM = 128
N_HEADS = ATTN_WIDTH // HEAD_DIM
N_GROUPS = 4
GROUP_DIM = FOURIER_WIDTH // N_GROUPS
ROPE_THETA = 500000.0
ROPE_DIM = HEAD_DIM // 4
ROPE_HALF = ROPE_DIM // 2
HALF_KEYS = 64
RMS_EPS = 1e-6
RADIX = 16
ROW_GROUP = RADIX * RADIX
Q_BLOCK = 128
FOURIER1_ROWS = 512
BLOCKS_PER_TRIP = 8
MASK_VALUE = -1e30

VMEM_LIMIT_BYTES = 56 * 1024 * 1024

F32 = jnp.float32
BF16 = jnp.bfloat16


def _params(*semantics):
    return pltpu.CompilerParams(dimension_semantics=semantics, vmem_limit_bytes=VMEM_LIMIT_BYTES)


def _resident(shape):
    return pl.BlockSpec(shape, lambda *_: (0,) * len(shape), pipeline_mode=pl.Buffered(1))


def _rope_tables(seq, scale):
    expo = np.arange(ROPE_HALF, dtype=np.float32) / np.float32(ROPE_HALF)
    inv_freq = (np.float32(1.0) / np.power(np.float32(ROPE_THETA), expo)).astype(np.float32)
    ang = (np.arange(seq, dtype=np.float32)[:, None] * inv_freq[None, :]).astype(np.float64)
    cos, sin = np.cos(ang), np.sin(ang)
    a = np.ones((seq, HEAD_DIM))
    b = np.zeros((seq, HEAD_DIM))
    c = np.zeros((seq, HEAD_DIM))
    a[:, :ROPE_HALF] = cos
    a[:, ROPE_HALF:ROPE_DIM] = cos
    b[:, ROPE_HALF:ROPE_DIM] = sin
    c[:, :ROPE_HALF] = -sin
    return tuple(jnp.asarray((t * scale).astype(np.float32)) for t in (a, b, c))


def _dft_cos_sin(n, rows, cols, scale=1.0):
    m = (np.asarray(rows, dtype=np.int64)[:, None] * np.asarray(cols, dtype=np.int64)[None, :]) % n
    ang = 2.0 * np.pi * m.astype(np.float64) / n
    return np.cos(ang) * scale, np.sin(ang) * scale


def _group_permutation():
    p = np.zeros((ROW_GROUP, ROW_GROUP), np.float32)
    j, r = np.meshgrid(np.arange(RADIX), np.arange(RADIX), indexing="ij")
    p[(RADIX * r + j).ravel(), (RADIX * j + r).ravel()] = 1.0
    return p


def _bf16_const(a):
    return jnp.asarray(np.asarray(a, np.float32)).astype(BF16)


def _silu(a):
    return a * (1.0 / (1.0 + jnp.exp(-a)))


def _inproj_kernel(x_ref, gain_ref, w_ref, qg_ref, kg_ref,
                   qa_ref, qb_ref, qc_ref, ka_ref, kb_ref, kc_ref, perm_ref,
                   q16_ref, k1_ref, k16_ref, v1_ref, v16_ref, ga_ref, u16_ref, gf_ref):
    x = x_ref[...]
    ms = jnp.mean(x * x, axis=-1, keepdims=True)
    h = (x * lax.rsqrt(ms + RMS_EPS) * gain_ref[...]).astype(BF16)
    perm = perm_ref[...]

    def column_tile(j):
        cols = slice(j * ATTN_WIDTH, (j + 1) * ATTN_WIDTH)
        return jnp.dot(h, w_ref[:, cols], preferred_element_type=F32)

    def norm_rope(acc, g_ref, a_ref, b_ref, c_ref):
        heads = []
        for hd in range(N_HEADS):
            a = acc[:, hd * HEAD_DIM:(hd + 1) * HEAD_DIM]
            ms_h = jnp.mean(a * a, axis=-1, keepdims=True)
            n = a * lax.rsqrt(ms_h + RMS_EPS) * g_ref[...]
            r = (n * a_ref[...]
                 + pltpu.roll(n, ROPE_HALF, 1) * b_ref[...]
                 + pltpu.roll(n, HEAD_DIM - ROPE_HALF, 1) * c_ref[...])
            heads.append(r.astype(BF16))
        return jnp.concatenate(heads, axis=1)

    def to_mod16(t):
        return jnp.dot(perm, t, preferred_element_type=F32)

    def store_heads(ref, t):
        for hd in range(N_HEADS):
            ref[hd] = t[:, hd * HEAD_DIM:(hd + 1) * HEAD_DIM]

    def store_head_pieces(ref, t):
        for hd in range(N_HEADS):
            for r in range(RADIX):
                ref[hd, r] = t[r * RADIX:(r + 1) * RADIX, hd * HEAD_DIM:(hd + 1) * HEAD_DIM]

    acc_q = column_tile(0)
    acc_k = column_tile(1)
    q = norm_rope(acc_q, qg_ref, qa_ref, qb_ref, qc_ref)
    acc_v = column_tile(2)
    store_head_pieces(q16_ref, to_mod16(q))

    k = norm_rope(acc_k, kg_ref, ka_ref, kb_ref, kc_ref)
    acc_ga = column_tile(3)
    store_heads(k1_ref, k)
    store_head_pieces(k16_ref, to_mod16(k).astype(BF16))

    v = acc_v.astype(BF16)
    acc_u = column_tile(4)
    store_heads(v1_ref, v)
    store_head_pieces(v16_ref, to_mod16(v).astype(BF16))

    ga_ref[...] = _silu(acc_ga).astype(BF16)
    acc_gf = column_tile(5)

    u16 = to_mod16(acc_u.astype(BF16)).astype(BF16)
    for r in range(RADIX):
        u16_ref[r] = u16[r * RADIX:(r + 1) * RADIX, :]

    gf_ref[...] = _silu(acc_gf).astype(BF16)


def _inproj(x2d, batch, seq, gain, w_in, qg, kg, perm):
    t = x2d.shape[0]
    groups = seq // ROW_GROUP
    sub = seq // RADIX
    qa, qb, qc = _rope_tables(seq, HEAD_DIM ** -0.5 * math.log2(math.e))
    ka, kb, kc = _rope_tables(seq, 1.0)
    rope_spec = pl.BlockSpec((ROW_GROUP, HEAD_DIM), lambda i: (i % groups, 0))
    tok_spec = pl.BlockSpec((ROW_GROUP, ATTN_WIDTH), lambda i: (i, 0))
    head_nat_spec = pl.BlockSpec((N_HEADS, ROW_GROUP, HEAD_DIM), lambda i: (0, i, 0))
    head_m16_spec = pl.BlockSpec((N_HEADS, None, RADIX, RADIX, HEAD_DIM),
                                 lambda i: (0, i // groups, 0, i % groups, 0))
    head_nat = jax.ShapeDtypeStruct((N_HEADS, t, HEAD_DIM), BF16)
    head_m16 = lambda dt: jax.ShapeDtypeStruct((N_HEADS, batch, RADIX, sub, HEAD_DIM), dt)
    return pl.pallas_call(
        _inproj_kernel,
        grid=(t // ROW_GROUP,),
        in_specs=[
            pl.BlockSpec((ROW_GROUP, D_MODEL), lambda i: (i, 0)),
            _resident((1, D_MODEL)),
            _resident(w_in.shape),
            _resident((1, HEAD_DIM)), _resident((1, HEAD_DIM)),
            rope_spec, rope_spec, rope_spec, rope_spec, rope_spec, rope_spec,
            _resident((ROW_GROUP, ROW_GROUP)),
        ],
        out_specs=[
            head_m16_spec, head_nat_spec, head_m16_spec, head_nat_spec, head_m16_spec,
            tok_spec,
            pl.BlockSpec((None, RADIX, RADIX, FOURIER_WIDTH), lambda i: (i // groups, 0, i % groups, 0)),
            tok_spec,
        ],
        out_shape=[
            head_m16(F32), head_nat, head_m16(BF16), head_nat, head_m16(BF16),
            jax.ShapeDtypeStruct((t, ATTN_WIDTH), BF16),
            jax.ShapeDtypeStruct((batch, RADIX, sub, FOURIER_WIDTH), BF16),
            jax.ShapeDtypeStruct((t, FOURIER_WIDTH), BF16),
        ],
        compiler_params=_params("arbitrary"),
        name="inproj",
    )(x2d, gain, w_in, qg, kg, qa, qb, qc, ka, kb, kc, perm)


def _window_start(first, half, total, window):
    return min(max(first - half, 0), total - window)


class _Pattern:
    def __init__(self, q_first, k_first, diff0, scale):
        deltas = [scale * (q0 - k0) for q0, k0 in zip(q_first, k_first)]
        uniq = sorted(set(deltas))
        self.q_first, self.k_first = q_first, k_first
        self.table_of = [uniq.index(d) for d in deltas]
        self.bias = jnp.asarray(np.stack(
            [np.where(np.abs(diff0 + d) <= HALF_KEYS, 0.0, MASK_VALUE) for d in uniq]).astype(np.float32))


def _attention_patterns(seq):
    sub = seq // RADIX
    qi = np.arange(Q_BLOCK)[:, None]
    win16 = min(Q_BLOCK + 2 * HALF_KEYS, sub)
    q16 = list(range(0, sub, Q_BLOCK))
    p16 = _Pattern(q16, [_window_start(q0, HALF_KEYS, sub, win16) for q0 in q16],
                   qi - np.arange(win16)[None, :], 1)
    q_rows, k_rows = Q_BLOCK // 4, Q_BLOCK // 4 + 2 * HALF_KEYS // 4
    kc = np.arange(4 * k_rows)[None, :]
    q4 = list(range(0, sub, q_rows))
    p4 = _Pattern(q4, [_window_start(q0, HALF_KEYS // 4, sub, k_rows) for q0 in q4],
                  4 * (qi % q_rows - kc % k_rows) + (qi // q_rows - kc // k_rows), 4)
    win1 = Q_BLOCK + 2 * HALF_KEYS
    q1 = list(range(0, seq, Q_BLOCK))
    p1 = _Pattern(q1, [_window_start(q0, HALF_KEYS, seq, win1) for q0 in q1],
                  RADIX * (qi % (Q_BLOCK // RADIX)) + qi // (Q_BLOCK // RADIX) - np.arange(win1)[None, :], 1)
    return p16, p4, p1


def _attn_kernel(q16_ref, k1_ref, v1_ref, k16_ref, v16_ref, b16_ref, b4_ref, b1_ref,
                 o_ref, acc_s, m_s, l_s, *, seq, pat16, pat4, pat1):
    sub = seq // RADIX

    def gather(ref, pieces):
        return jnp.concatenate([ref[r, pl.ds(s0, n), :] for r, s0, n in pieces], axis=0)

    def run_blocks(blocks, first):
        scores = []
        for q, k, _, bias, _ in blocks:
            s = lax.dot_general(q(), k(), (((1,), (1,)), ((), ())), preferred_element_type=F32) + bias()
            m = jnp.broadcast_to(jnp.max(s, axis=-1, keepdims=True), (Q_BLOCK, HEAD_DIM))
            scores.append((s, m))
        partial = []
        for (s, m), (_, _, v, _, pieces) in zip(scores, blocks):
            m_old = None if first else gather(m_s, pieces)
            m_new = m if first else jnp.maximum(m_old, m)
            p = jnp.exp2(s - jnp.concatenate([m_new] * (s.shape[1] // HEAD_DIM), axis=1))
            vw = v()
            v_aug = jnp.concatenate([vw, jnp.ones(vw.shape, BF16)], axis=1)
            partial.append((m_old, m_new, jnp.dot(p.astype(BF16), v_aug, preferred_element_type=F32)))
        for (m_old, m_new, pv), (_, _, _, _, pieces) in zip(partial, blocks):
            acc, l = pv[:, :HEAD_DIM], pv[:, HEAD_DIM:]
            if not first:
                w_old = jnp.exp2(m_old - m_new)
                acc = w_old * gather(acc_s, pieces) + acc
                l = w_old * gather(l_s, pieces) + l
            off = 0
            for r, s0, n in pieces:
                m_s[r, pl.ds(s0, n), :] = m_new[off:off + n]
                l_s[r, pl.ds(s0, n), :] = l[off:off + n]
                acc_s[r, pl.ds(s0, n), :] = acc[off:off + n]
                off += n

    win16 = b16_ref.shape[2]
    n16 = len(pat16.q_first)
    res_per_trip = max(1, BLOCKS_PER_TRIP // n16)

    def d16_body(t, carry):
        blocks = []
        for j in range(res_per_trip):
            r = t * res_per_trip + j
            for blk in range(n16):
                l0, ws = pat16.q_first[blk], pat16.k_first[blk]
                blocks.append((
                    lambda r=r, l0=l0: q16_ref[r, l0:l0 + Q_BLOCK, :].astype(BF16),
                    lambda r=r, ws=ws: k16_ref[r, ws:ws + win16, :],
                    lambda r=r, ws=ws: v16_ref[r, ws:ws + win16, :],
                    lambda blk=blk: b16_ref[pat16.table_of[blk]],
                    [(r, l0, Q_BLOCK)]))
        run_blocks(blocks, first=True)
        return carry

    lax.fori_loop(0, RADIX // res_per_trip, d16_body, 0)

    q_rows4 = Q_BLOCK // 4
    k_rows4 = b4_ref.shape[2] // 4

    n4 = len(pat4.q_first)
    r4_per_trip = max(1, BLOCKS_PER_TRIP // n4)

    def d4_body(t, carry):
        blocks = []
        for j in range(r4_per_trip):
            r4 = t * r4_per_trip + j
            for blk in range(n4):
                l0, ws = pat4.q_first[blk], pat4.k_first[blk]
                res = [r4 + 4 * a for a in range(4)]
                q_pieces = [(r, l0, q_rows4) for r in res]
                k_pieces = [(r, ws, k_rows4) for r in res]
                blocks.append((
                    lambda p=q_pieces: gather(q16_ref, p).astype(BF16),
                    lambda p=k_pieces: gather(k16_ref, p),
                    lambda p=k_pieces: gather(v16_ref, p),
                    lambda blk=blk: b4_ref[pat4.table_of[blk]],
                    q_pieces))
        run_blocks(blocks, first=False)
        return carry

    lax.fori_loop(0, 4 // r4_per_trip, d4_body, 0)

    q_rows1 = Q_BLOCK // RADIX
    win1 = b1_ref.shape[2]
    n1 = len(pat1.q_first)
    per_trip = BLOCKS_PER_TRIP
    assert pat1.table_of[0] == 0 and pat1.table_of[-1] == 2 and set(pat1.table_of[1:-1]) == {1}

    def d1_body(t, carry):
        blocks = []
        for j in range(per_trip):
            i = t * per_trip + j
            n0 = i * Q_BLOCK
            l0 = pl.multiple_of(i * q_rows1, q_rows1)
            ws = pl.multiple_of(jnp.clip(n0 - HALF_KEYS, 0, seq - win1), HALF_KEYS)
            table = jnp.where(i == 0, 0, jnp.where(i == n1 - 1, 2, 1))
            q_pieces = [(r, l0, q_rows1) for r in range(RADIX)]
            blocks.append((
                lambda p=q_pieces: gather(q16_ref, p).astype(BF16),
                lambda ws=ws: k1_ref[pl.ds(ws, win1), :],
                lambda ws=ws: v1_ref[pl.ds(ws, win1), :],
                lambda table=table: b1_ref[table],
                q_pieces))
        run_blocks(blocks, first=False)
        return carry

    lax.fori_loop(0, n1 // per_trip, d1_body, 0)

    def finish(r, carry):
        o_ref[r] = (acc_s[r] * (1.0 / l_s[r])).astype(BF16)
        return carry

    lax.fori_loop(0, RADIX, finish, 0)


def _attention(q16, k1, v1, k16, v16, batch, seq):
    sub = seq // RADIX
    pat16, pat4, pat1 = _attention_patterns(seq)
    m16_spec = pl.BlockSpec((None, None, RADIX, sub, HEAD_DIM), lambda b, h: (h, b, 0, 0, 0))
    nat_spec = pl.BlockSpec((None, seq, HEAD_DIM), lambda b, h: (h, b, 0))
    state = pltpu.VMEM((RADIX, sub, HEAD_DIM), F32)
    return pl.pallas_call(
        functools.partial(_attn_kernel, seq=seq, pat16=pat16, pat4=pat4, pat1=pat1),
        grid=(batch, N_HEADS),
        in_specs=[m16_spec, nat_spec, nat_spec, m16_spec, m16_spec,
                  _resident(pat16.bias.shape), _resident(pat4.bias.shape), _resident(pat1.bias.shape)],
        out_specs=m16_spec,
        out_shape=jax.ShapeDtypeStruct((N_HEADS, batch, RADIX, sub, HEAD_DIM), BF16),
        scratch_shapes=[state, state, state],
        compiler_params=_params("parallel", "parallel"),
        name="attention",
    )(q16, k1, v1, k16, v16, pat16.bias, pat4.bias, pat1.bias)


def _chanmat_kernel(c_ref, s_ref, w_ref, m_ref):
    for g in range(N_GROUPS):
        w = w_ref[g]
        mc = jnp.dot(c_ref[...], w, preferred_element_type=F32, precision=lax.Precision.HIGHEST)
        ms = jnp.dot(s_ref[...], w, preferred_element_type=F32, precision=lax.Precision.HIGHEST)
        m_ref[g, :, :GROUP_DIM] = mc.astype(BF16)
        m_ref[g, :, GROUP_DIM:] = ms.astype(BF16)


def _channel_matrices(w_fourier):
    idx = np.arange(GROUP_DIM)
    cc, sc = _dft_cos_sin(GROUP_DIM, idx, idx, GROUP_DIM ** -0.5)
    return pl.pallas_call(
        _chanmat_kernel,
        out_shape=jax.ShapeDtypeStruct((N_GROUPS, GROUP_DIM, 2 * GROUP_DIM), BF16),
        name="chanmat",
    )(jnp.asarray(cc, F32), jnp.asarray(sc, F32), w_fourier)


def _fourier1_kernel(u_ref, m_ref, c_ref, s_ref, yr_ref, yi_ref):
    dot = functools.partial(jnp.dot, preferred_element_type=F32)
    z = []
    for n2 in range(u_ref.shape[0]):
        a_parts, b_parts = [], []
        for g in range(N_GROUPS):
            ab = dot(u_ref[n2, :, g * GROUP_DIM:(g + 1) * GROUP_DIM], m_ref[g])
            a_parts.append(ab[:, :GROUP_DIM].astype(BF16))
            b_parts.append(ab[:, GROUP_DIM:].astype(BF16))
        z.append((jnp.concatenate(a_parts, axis=1), jnp.concatenate(b_parts, axis=1)))
    c, s = c_ref[...], s_ref[...]
    for n2, (a, b) in enumerate(z):
        yr_ref[n2] = (dot(c, a) - dot(s, b)).astype(BF16)
        yi_ref[n2] = (dot(s, a) + dot(c, b)).astype(BF16)


def _fourier_stage1(u16, chan_mats, batch, seq):
    s1 = seq // RADIX
    idx = np.arange(s1)
    c1, sn1 = _dft_cos_sin(s1, idx, idx)
    per_step = FOURIER1_ROWS // s1
    row_spec = pl.BlockSpec((None, per_step, s1, FOURIER_WIDTH), lambda b, n2: (b, n2, 0, 0))
    y = jax.ShapeDtypeStruct((batch, RADIX, s1, FOURIER_WIDTH), BF16)
    return pl.pallas_call(
        _fourier1_kernel,
        grid=(batch, RADIX // per_step),
        in_specs=[row_spec, _resident(chan_mats.shape), _resident((s1, s1)), _resident((s1, s1))],
        out_specs=[row_spec, row_spec],
        out_shape=[y, y],
        compiler_params=_params("parallel", "parallel"),
        name="fourier_seq1",
    )(u16, chan_mats, _bf16_const(c1), _bf16_const(sn1))


def _fourier2_kernel(yr_ref, yi_ref, mc_ref, ms_ref, gf_ref, o_ref, *, tiles):
    dot = functools.partial(jnp.dot, preferred_element_type=F32)
    for t in range(tiles):
        rows = slice(t * RADIX, (t + 1) * RADIX)
        yr = jnp.concatenate([yr_ref[n2, rows, :] for n2 in range(RADIX)], axis=0)
        yi = jnp.concatenate([yi_ref[n2, rows, :] for n2 in range(RADIX)], axis=0)
        x = dot(mc_ref[t], yr) + dot(ms_ref[t], yi)
        for k2 in range(RADIX):
            piece = x[k2 * RADIX:(k2 + 1) * RADIX, :] * gf_ref[k2, rows, :].astype(F32)
            o_ref[k2, rows, :] = piece.astype(BF16)


def _stage2_matrices(seq):
    s1 = seq // RADIX
    tiles = s1 // RADIX
    c, k2, k1, n2 = np.meshgrid(np.arange(tiles), np.arange(RADIX), np.arange(RADIX), np.arange(RADIX),
                                indexing="ij")
    k = RADIX * c + k1 + s1 * k2
    ang = 2.0 * np.pi * ((n2 * k) % seq).astype(np.float64) / seq
    mc = np.zeros((tiles, ROW_GROUP, ROW_GROUP), np.float32)
    ms = np.zeros((tiles, ROW_GROUP, ROW_GROUP), np.float32)
    mc[c, RADIX * k2 + k1, RADIX * n2 + k1] = np.cos(ang) * seq ** -0.5
    ms[c, RADIX * k2 + k1, RADIX * n2 + k1] = -np.sin(ang) * seq ** -0.5
    return _bf16_const(mc), _bf16_const(ms)


def _fourier_stage2(yr, yi, gf, batch, seq, tiles=4):
    s1 = seq // RADIX
    mc, ms = _stage2_matrices(seq)
    rows = tiles * RADIX
    y_spec = pl.BlockSpec((None, RADIX, rows, FOURIER_WIDTH), lambda b, c: (b, 0, c, 0))
    m_spec = pl.BlockSpec((tiles, ROW_GROUP, ROW_GROUP), lambda b, c: (c, 0, 0))
    out = pl.pallas_call(
        functools.partial(_fourier2_kernel, tiles=tiles),
        grid=(batch, s1 // rows),
        in_specs=[y_spec, y_spec, m_spec, m_spec, y_spec],
        out_specs=y_spec,
        out_shape=jax.ShapeDtypeStruct((batch, RADIX, s1, FOURIER_WIDTH), BF16),
        compiler_params=_params("parallel", "parallel"),
        name="fourier_seq2",
    )(yr, yi, mc, ms, gf.reshape(batch, RADIX, s1, FOURIER_WIDTH))
    return out.reshape(batch * seq, FOURIER_WIDTH)


def _outproj_kernel(x_ref, o16_ref, ga_ref, mf_ref, unperm_ref, w_ref, y_ref):
    heads = [jnp.concatenate([o16_ref[hd, r] for r in range(RADIX)], axis=0) for hd in range(N_HEADS)]
    attn16 = jnp.concatenate(heads, axis=1)
    attn = jnp.dot(unperm_ref[...], attn16, preferred_element_type=F32)
    y = x_ref[...] + jnp.dot(mf_ref[...], w_ref[ATTN_WIDTH:, :], preferred_element_type=F32)
    mix_a = (attn * ga_ref[...].astype(F32)).astype(BF16)
    y_ref[...] = y + jnp.dot(mix_a, w_ref[:ATTN_WIDTH, :], preferred_element_type=F32)


def _outproj(x2d, o16, ga, mix_f, unperm, w_out, seq):
    t = x2d.shape[0]
    groups = seq // ROW_GROUP
    tok_spec = pl.BlockSpec((ROW_GROUP, ATTN_WIDTH), lambda i: (i, 0))
    x_spec = pl.BlockSpec((ROW_GROUP, D_MODEL), lambda i: (i, 0))
    return pl.pallas_call(
        _outproj_kernel,
        grid=(t // ROW_GROUP,),
        in_specs=[x_spec,
                  pl.BlockSpec((N_HEADS, None, RADIX, RADIX, HEAD_DIM),
                               lambda i: (0, i // groups, 0, i % groups, 0)),
                  tok_spec, tok_spec,
                  _resident((ROW_GROUP, ROW_GROUP)),
                  _resident((D_MODEL, D_MODEL))],
        out_specs=x_spec,
        out_shape=jax.ShapeDtypeStruct((t, D_MODEL), F32),
        compiler_params=_params("arbitrary"),
        name="outproj",
    )(x2d, o16, ga, mix_f, unperm, w_out)


def _layer(x, gain, w_in, qg, kg, chan_mats, w_out, perm, unperm):
    batch, seq, _ = x.shape
    x2d = x.reshape(batch * seq, D_MODEL)
    q16, k1, k16, v1, v16, ga, u16, gf = _inproj(x2d, batch, seq, gain, w_in, qg, kg, perm)
    o16 = _attention(q16, k1, v1, k16, v16, batch, seq)
    yr, yi = _fourier_stage1(u16, chan_mats, batch, seq)
    mix_f = _fourier_stage2(yr, yi, gf, batch, seq)
    y = _outproj(x2d, o16, ga, mix_f, unperm, w_out, seq)
    return y.reshape(batch, seq, D_MODEL)


def kernel(x_prompt, x_sample, rms_gain, w_in, q_norm_gain, k_norm_gain, w_fourier, w_out):
    depth = rms_gain.shape[0]
    p = _group_permutation()
    perm, unperm = _bf16_const(p), _bf16_const(p.T)
    for l in range(depth):
        gain = rms_gain[l].reshape(1, D_MODEL)
        w_in_l = w_in[l].astype(BF16)
        w_out_l = w_out[l].astype(BF16)
        qg = q_norm_gain[l].reshape(1, HEAD_DIM)
        kg = k_norm_gain[l].reshape(1, HEAD_DIM)
        chan_mats = _channel_matrices(w_fourier[l])
        x_prompt = _layer(x_prompt, gain, w_in_l, qg, kg, chan_mats, w_out_l, perm, unperm)
        x_sample = _layer(x_sample, gain, w_in_l, qg, kg, chan_mats, w_out_l, perm, unperm)
    return (x_prompt, x_sample)
```

```python
import functools
import math

import jax
import jax.numpy as jnp
import numpy as np
from jax import lax
from jax.experimental import pallas as pl
from jax.experimental.pallas import tpu as pltpu

D_MODEL = 2048
ATTN_WIDTH = 1024
FOURIER_WIDTH = 1024
HEAD_DIM = 128
N_HEADS = ATTN_WIDTH // HEAD_DIM
N_GROUPS = 4
GROUP_DIM = FOURIER_WIDTH // N_GROUPS
ROPE_THETA = 500000.0
ROPE_DIM = HEAD_DIM // 4
ROPE_HALF = ROPE_DIM // 2
HALF_KEYS = 64
RMS_EPS = 1e-6
RADIX = 16
ROW_GROUP = RADIX * RADIX
Q_BLOCK = 128
FOURIER1_ROWS = 512
BLOCKS_PER_GROUP = 8
MASK_VALUE = -1e30

VMEM_LIMIT_BYTES = 56 * 1024 * 1024

F32 = jnp.float32
BF16 = jnp.bfloat16


def _params(*semantics):
    return pltpu.CompilerParams(dimension_semantics=semantics, vmem_limit_bytes=VMEM_LIMIT_BYTES)


def _resident(shape):
    return pl.BlockSpec(shape, lambda *_: (0,) * len(shape), pipeline_mode=pl.Buffered(1))


def _rope_tables(seq, scale):
    expo = np.arange(ROPE_HALF, dtype=np.float32) / np.float32(ROPE_HALF)
    inv_freq = (np.float32(1.0) / np.power(np.float32(ROPE_THETA), expo)).astype(np.float32)
    ang = (np.arange(seq, dtype=np.float32)[:, None] * inv_freq[None, :]).astype(np.float64)
    cos, sin = np.cos(ang), np.sin(ang)
    a = np.ones((seq, HEAD_DIM))
    b = np.zeros((seq, HEAD_DIM))
    c = np.zeros((seq, HEAD_DIM))
    a[:, :ROPE_HALF] = cos
    a[:, ROPE_HALF:ROPE_DIM] = cos
    b[:, ROPE_HALF:ROPE_DIM] = sin
    c[:, :ROPE_HALF] = -sin
    return tuple(jnp.asarray((t * scale).astype(np.float32)) for t in (a, b, c))


def _dft_cos_sin(n, rows, cols, scale=1.0):
    m = (np.asarray(rows, dtype=np.int64)[:, None] * np.asarray(cols, dtype=np.int64)[None, :]) % n
    ang = 2.0 * np.pi * m.astype(np.float64) / n
    return np.cos(ang) * scale, np.sin(ang) * scale


def _group_permutation():
    p = np.zeros((ROW_GROUP, ROW_GROUP), np.float32)
    j, r = np.meshgrid(np.arange(RADIX), np.arange(RADIX), indexing="ij")
    p[(RADIX * r + j).ravel(), (RADIX * j + r).ravel()] = 1.0
    return p


def _bf16_const(a):
    return jnp.asarray(np.asarray(a, np.float32)).astype(BF16)


def _silu(a):
    return a * (1.0 / (1.0 + jnp.exp(-a)))


def _inproj_kernel(x_ref, gain_ref, w_ref, qg_ref, kg_ref,
                   qa_ref, qb_ref, qc_ref, ka_ref, kb_ref, kc_ref, perm_ref,
                   q16_ref, k1_ref, k16_ref, v1_ref, v16_ref, ga_ref, u16_ref, gf_ref):
    x = x_ref[...]
    ms = jnp.mean(x * x, axis=-1, keepdims=True)
    h = (x * lax.rsqrt(ms + RMS_EPS) * gain_ref[...]).astype(BF16)
    perm = perm_ref[...]

    def column_tile(j):
        cols = slice(j * ATTN_WIDTH, (j + 1) * ATTN_WIDTH)
        return jnp.dot(h, w_ref[:, cols], preferred_element_type=F32)

    def norm_rope(acc, g_ref, a_ref, b_ref, c_ref):
        heads = []
        for hd in range(N_HEADS):
            a = acc[:, hd * HEAD_DIM:(hd + 1) * HEAD_DIM]
            ms_h = jnp.mean(a * a, axis=-1, keepdims=True)
            n = a * lax.rsqrt(ms_h + RMS_EPS) * g_ref[...]
            r = (n * a_ref[...]
                 + pltpu.roll(n, ROPE_HALF, 1) * b_ref[...]
                 + pltpu.roll(n, HEAD_DIM - ROPE_HALF, 1) * c_ref[...])
            heads.append(r.astype(BF16))
        return jnp.concatenate(heads, axis=1)

    def to_mod16(t):
        return jnp.dot(perm, t, preferred_element_type=F32)

    def store_heads(ref, t):
        for hd in range(N_HEADS):
            ref[hd] = t[:, hd * HEAD_DIM:(hd + 1) * HEAD_DIM]

    def store_head_pieces(ref, t):
        for hd in range(N_HEADS):
            for r in range(RADIX):
                ref[hd, r] = t[r * RADIX:(r + 1) * RADIX, hd * HEAD_DIM:(hd + 1) * HEAD_DIM]

    acc_q = column_tile(0)
    acc_k = column_tile(1)
    q = norm_rope(acc_q, qg_ref, qa_ref, qb_ref, qc_ref)
    acc_v = column_tile(2)
    store_head_pieces(q16_ref, to_mod16(q))

    k = norm_rope(acc_k, kg_ref, ka_ref, kb_ref, kc_ref)
    acc_ga = column_tile(3)
    store_heads(k1_ref, k)
    store_head_pieces(k16_ref, to_mod16(k).astype(BF16))

    v = acc_v.astype(BF16)
    acc_u = column_tile(4)
    store_heads(v1_ref, v)
    store_head_pieces(v16_ref, to_mod16(v).astype(BF16))

    ga_ref[...] = _silu(acc_ga).astype(BF16)
    acc_gf = column_tile(5)

    u16 = to_mod16(acc_u.astype(BF16)).astype(BF16)
    for r in range(RADIX):
        u16_ref[r] = u16[r * RADIX:(r + 1) * RADIX, :]

    gf_ref[...] = _silu(acc_gf).astype(BF16)


def _inproj(x2d, batch, seq, gain, w_in, qg, kg, perm):
    t = x2d.shape[0]
    groups = seq // ROW_GROUP
    sub = seq // RADIX
    qa, qb, qc = _rope_tables(seq, HEAD_DIM ** -0.5 * math.log2(math.e))
    ka, kb, kc = _rope_tables(seq, 1.0)
    rope_spec = pl.BlockSpec((ROW_GROUP, HEAD_DIM), lambda i: (i % groups, 0))
    tok_spec = pl.BlockSpec((ROW_GROUP, ATTN_WIDTH), lambda i: (i, 0))
    head_nat_spec = pl.BlockSpec((N_HEADS, ROW_GROUP, HEAD_DIM), lambda i: (0, i, 0))
    head_m16_spec = pl.BlockSpec((N_HEADS, None, RADIX, RADIX, HEAD_DIM),
                                 lambda i: (0, i // groups, 0, i % groups, 0))
    head_nat = jax.ShapeDtypeStruct((N_HEADS, t, HEAD_DIM), BF16)
    head_m16 = lambda dt: jax.ShapeDtypeStruct((N_HEADS, batch, RADIX, sub, HEAD_DIM), dt)
    return pl.pallas_call(
        _inproj_kernel,
        grid=(t // ROW_GROUP,),
        in_specs=[
            pl.BlockSpec((ROW_GROUP, D_MODEL), lambda i: (i, 0)),
            _resident((1, D_MODEL)),
            _resident(w_in.shape),
            _resident((1, HEAD_DIM)), _resident((1, HEAD_DIM)),
            rope_spec, rope_spec, rope_spec, rope_spec, rope_spec, rope_spec,
            _resident((ROW_GROUP, ROW_GROUP)),
        ],
        out_specs=[
            head_m16_spec, head_nat_spec, head_m16_spec, head_nat_spec, head_m16_spec,
            tok_spec,
            pl.BlockSpec((None, RADIX, RADIX, FOURIER_WIDTH), lambda i: (i // groups, 0, i % groups, 0)),
            tok_spec,
        ],
        out_shape=[
            head_m16(F32), head_nat, head_m16(BF16), head_nat, head_m16(BF16),
            jax.ShapeDtypeStruct((t, ATTN_WIDTH), BF16),
            jax.ShapeDtypeStruct((batch, RADIX, sub, FOURIER_WIDTH), BF16),
            jax.ShapeDtypeStruct((t, FOURIER_WIDTH), BF16),
        ],
        compiler_params=_params("arbitrary"),
        name="inproj",
    )(x2d, gain, w_in, qg, kg, qa, qb, qc, ka, kb, kc, perm)


def _window_start(first, half, total, window):
    return min(max(first - half, 0), total - window)


class _Pattern:
    def __init__(self, q_first, k_first, diff0, scale):
        deltas = [scale * (q0 - k0) for q0, k0 in zip(q_first, k_first)]
        uniq = sorted(set(deltas))
        self.q_first, self.k_first = q_first, k_first
        self.table_of = [uniq.index(d) for d in deltas]
        self.bias = jnp.asarray(np.stack(
            [np.where(np.abs(diff0 + d) <= HALF_KEYS, 0.0, MASK_VALUE) for d in uniq]).astype(np.float32))


def _attention_patterns(seq):
    sub = seq // RADIX
    qi = np.arange(Q_BLOCK)[:, None]
    win16 = min(Q_BLOCK + 2 * HALF_KEYS, sub)
    q16 = list(range(0, sub, Q_BLOCK))
    p16 = _Pattern(q16, [_window_start(q0, HALF_KEYS, sub, win16) for q0 in q16],
                   qi - np.arange(win16)[None, :], 1)
    q_rows, k_rows = Q_BLOCK // 4, Q_BLOCK // 4 + 2 * HALF_KEYS // 4
    kc = np.arange(4 * k_rows)[None, :]
    q4 = list(range(0, sub, q_rows))
    p4 = _Pattern(q4, [_window_start(q0, HALF_KEYS // 4, sub, k_rows) for q0 in q4],
                  4 * (qi % q_rows - kc % k_rows) + (qi // q_rows - kc // k_rows), 4)
    win1 = Q_BLOCK + 2 * HALF_KEYS
    q1 = list(range(0, seq, Q_BLOCK))
    p1 = _Pattern(q1, [_window_start(q0, HALF_KEYS, seq, win1) for q0 in q1],
                  RADIX * (qi % (Q_BLOCK // RADIX)) + qi // (Q_BLOCK // RADIX) - np.arange(win1)[None, :], 1)
    return p16, p4, p1


def _attn_kernel(q16_ref, k1_ref, v1_ref, k16_ref, v16_ref, b16_ref, b4_ref, b1_ref,
                 o_ref, acc_s, m_s, l_s, p0_s, w0_s, p1_s, w1_s, *, pat16, pat4, pat1):
    def gather(ref, pieces):
        return jnp.concatenate([ref[r, pl.ds(s0, n), :] for r, s0, n in pieces], axis=0)

    def scatter(ref, pieces, value):
        off = 0
        for r, s0, n in pieces:
            ref[r, pl.ds(s0, n), :] = value[off:off + n]
            off += n

    def scores_stage(blocks, first, p_scr, w_scr):
        for i, (q, k, _, bias, pieces) in enumerate(blocks):
            s = lax.dot_general(q(), k(), (((1,), (1,)), ((), ())), preferred_element_type=F32) + bias()
            m_new = jnp.broadcast_to(jnp.max(s, axis=-1, keepdims=True), (Q_BLOCK, HEAD_DIM))
            if not first:
                m_old = gather(m_s, pieces)
                m_new = jnp.maximum(m_old, m_new)
                w_scr[i] = jnp.exp2(m_old - m_new)
            width = s.shape[1]
            p = jnp.exp2(s - jnp.concatenate([m_new] * (width // HEAD_DIM), axis=1))
            p_scr[i, :, :width] = p.astype(BF16)
            scatter(m_s, pieces, m_new)

    def values_stage(blocks, first, p_scr, w_scr):
        for i, (_, _, v, _, pieces) in enumerate(blocks):
            vw = v()
            v_aug = jnp.concatenate([vw, jnp.ones(vw.shape, BF16)], axis=1)
            pv = jnp.dot(p_scr[i, :, :vw.shape[0]], v_aug, preferred_element_type=F32)
            acc, l = pv[:, :HEAD_DIM], pv[:, HEAD_DIM:]
            if not first:
                w_old = w_scr[i]
                acc = w_old * gather(acc_s, pieces) + acc
                l = w_old * gather(l_s, pieces) + l
            scatter(l_s, pieces, l)
            scatter(acc_s, pieces, acc)

    groups = []

    def add_groups(blocks, first):
        groups.extend((blocks[i:i + BLOCKS_PER_GROUP], first) for i in range(0, len(blocks), BLOCKS_PER_GROUP))

    win16 = b16_ref.shape[2]
    blocks = []
    for r in range(RADIX):
        for blk, (l0, ws) in enumerate(zip(pat16.q_first, pat16.k_first)):
            blocks.append((
                lambda r=r, l0=l0: q16_ref[r, l0:l0 + Q_BLOCK, :].astype(BF16),
                lambda r=r, ws=ws: k16_ref[r, ws:ws + win16, :],
                lambda r=r, ws=ws: v16_ref[r, ws:ws + win16, :],
                lambda blk=blk: b16_ref[pat16.table_of[blk]],
                [(r, l0, Q_BLOCK)]))
    add_groups(blocks, True)

    q_rows4 = Q_BLOCK // 4
    k_rows4 = b4_ref.shape[2] // 4
    blocks = []
    for r4 in range(4):
        res = [r4 + 4 * a for a in range(4)]
        for blk, (l0, ws) in enumerate(zip(pat4.q_first, pat4.k_first)):
            q_pieces = [(r, l0, q_rows4) for r in res]
            k_pieces = [(r, ws, k_rows4) for r in res]
            blocks.append((
                lambda p=q_pieces: gather(q16_ref, p).astype(BF16),
                lambda p=k_pieces: gather(k16_ref, p),
                lambda p=k_pieces: gather(v16_ref, p),
                lambda blk=blk: b4_ref[pat4.table_of[blk]],
                q_pieces))
    add_groups(blocks, False)

    q_rows1 = Q_BLOCK // RADIX
    win1 = b1_ref.shape[2]
    blocks = []
    for blk, (n0, ws) in enumerate(zip(pat1.q_first, pat1.k_first)):
        q_pieces = [(r, n0 // RADIX, q_rows1) for r in range(RADIX)]
        blocks.append((
            lambda p=q_pieces: gather(q16_ref, p).astype(BF16),
            lambda ws=ws: k1_ref[ws:ws + win1, :],
            lambda ws=ws: v1_ref[ws:ws + win1, :],
            lambda blk=blk: b1_ref[pat1.table_of[blk]],
            q_pieces))
    add_groups(blocks, False)

    slots = ((p0_s, w0_s), (p1_s, w1_s))
    scores_stage(*groups[0], *slots[0])
    for g in range(1, len(groups)):
        scores_stage(*groups[g], *slots[g % 2])
        values_stage(*groups[g - 1], *slots[(g - 1) % 2])
    values_stage(*groups[-1], *slots[(len(groups) - 1) % 2])

    def finish(r, carry):
        o_ref[r] = (acc_s[r] * (1.0 / l_s[r])).astype(BF16)
        return carry

    lax.fori_loop(0, RADIX, finish, 0)


def _attention(q16, k1, v1, k16, v16, batch, seq):
    sub = seq // RADIX
    pat16, pat4, pat1 = _attention_patterns(seq)
    m16_spec = pl.BlockSpec((None, None, RADIX, sub, HEAD_DIM), lambda b, h: (h, b, 0, 0, 0))
    nat_spec = pl.BlockSpec((None, seq, HEAD_DIM), lambda b, h: (h, b, 0))
    state = pltpu.VMEM((RADIX, sub, HEAD_DIM), F32)
    probs = pltpu.VMEM((BLOCKS_PER_GROUP, Q_BLOCK, Q_BLOCK + 2 * HALF_KEYS), BF16)
    rescale = pltpu.VMEM((BLOCKS_PER_GROUP, Q_BLOCK, HEAD_DIM), F32)
    return pl.pallas_call(
        functools.partial(_attn_kernel, pat16=pat16, pat4=pat4, pat1=pat1),
        grid=(batch, N_HEADS),
        in_specs=[m16_spec, nat_spec, nat_spec, m16_spec, m16_spec,
                  _resident(pat16.bias.shape), _resident(pat4.bias.shape), _resident(pat1.bias.shape)],
        out_specs=m16_spec,
        out_shape=jax.ShapeDtypeStruct((N_HEADS, batch, RADIX, sub, HEAD_DIM), BF16),
        scratch_shapes=[state, state, state, probs, rescale, probs, rescale],
        compiler_params=_params("parallel", "parallel"),
        name="attention",
    )(q16, k1, v1, k16, v16, pat16.bias, pat4.bias, pat1.bias)


def _chanmat_kernel(c_ref, s_ref, w_ref, m_ref):
    for g in range(N_GROUPS):
        w = w_ref[g]
        mc = jnp.dot(c_ref[...], w, preferred_element_type=F32, precision=lax.Precision.HIGHEST)
        ms = jnp.dot(s_ref[...], w, preferred_element_type=F32, precision=lax.Precision.HIGHEST)
        m_ref[g, :, :GROUP_DIM] = mc.astype(BF16)
        m_ref[g, :, GROUP_DIM:] = ms.astype(BF16)


def _channel_matrices(w_fourier):
    idx = np.arange(GROUP_DIM)
    cc, sc = _dft_cos_sin(GROUP_DIM, idx, idx, GROUP_DIM ** -0.5)
    return pl.pallas_call(
        _chanmat_kernel,
        out_shape=jax.ShapeDtypeStruct((N_GROUPS, GROUP_DIM, 2 * GROUP_DIM), BF16),
        name="chanmat",
    )(jnp.asarray(cc, F32), jnp.asarray(sc, F32), w_fourier)


def _fourier1_kernel(u_ref, m_ref, c_ref, s_ref, yr_ref, yi_ref):
    dot = functools.partial(jnp.dot, preferred_element_type=F32)
    z = []
    for n2 in range(u_ref.shape[0]):
        a_parts, b_parts = [], []
        for g in range(N_GROUPS):
            ab = dot(u_ref[n2, :, g * GROUP_DIM:(g + 1) * GROUP_DIM], m_ref[g])
            a_parts.append(ab[:, :GROUP_DIM].astype(BF16))
            b_parts.append(ab[:, GROUP_DIM:].astype(BF16))
        z.append((jnp.concatenate(a_parts, axis=1), jnp.concatenate(b_parts, axis=1)))
    c, s = c_ref[...], s_ref[...]
    for n2, (a, b) in enumerate(z):
        yr_ref[n2] = (dot(c, a) - dot(s, b)).astype(BF16)
        yi_ref[n2] = (dot(s, a) + dot(c, b)).astype(BF16)


def _fourier_stage1(u16, chan_mats, batch, seq):
    s1 = seq // RADIX
    idx = np.arange(s1)
    c1, sn1 = _dft_cos_sin(s1, idx, idx)
    per_step = FOURIER1_ROWS // s1
    row_spec = pl.BlockSpec((None, per_step, s1, FOURIER_WIDTH), lambda b, n2: (b, n2, 0, 0))
    y = jax.ShapeDtypeStruct((batch, RADIX, s1, FOURIER_WIDTH), BF16)
    return pl.pallas_call(
        _fourier1_kernel,
        grid=(batch, RADIX // per_step),
        in_specs=[row_spec, _resident(chan_mats.shape), _resident((s1, s1)), _resident((s1, s1))],
        out_specs=[row_spec, row_spec],
        out_shape=[y, y],
        compiler_params=_params("parallel", "parallel"),
        name="fourier_seq1",
    )(u16, chan_mats, _bf16_const(c1), _bf16_const(sn1))


def _fourier2_kernel(yr_ref, yi_ref, mc_ref, ms_ref, gf_ref, o_ref, *, tiles):
    dot = functools.partial(jnp.dot, preferred_element_type=F32)
    for t in range(tiles):
        rows = slice(t * RADIX, (t + 1) * RADIX)
        yr = jnp.concatenate([yr_ref[n2, rows, :] for n2 in range(RADIX)], axis=0)
        yi = jnp.concatenate([yi_ref[n2, rows, :] for n2 in range(RADIX)], axis=0)
        x = dot(mc_ref[t], yr) + dot(ms_ref[t], yi)
        for k2 in range(RADIX):
            piece = x[k2 * RADIX:(k2 + 1) * RADIX, :] * gf_ref[k2, rows, :].astype(F32)
            o_ref[k2, rows, :] = piece.astype(BF16)


def _stage2_matrices(seq):
    s1 = seq // RADIX
    tiles = s1 // RADIX
    c, k2, k1, n2 = np.meshgrid(np.arange(tiles), np.arange(RADIX), np.arange(RADIX), np.arange(RADIX),
                                indexing="ij")
    k = RADIX * c + k1 + s1 * k2
    ang = 2.0 * np.pi * ((n2 * k) % seq).astype(np.float64) / seq
    mc = np.zeros((tiles, ROW_GROUP, ROW_GROUP), np.float32)
    ms = np.zeros((tiles, ROW_GROUP, ROW_GROUP), np.float32)
    mc[c, RADIX * k2 + k1, RADIX * n2 + k1] = np.cos(ang) * seq ** -0.5
    ms[c, RADIX * k2 + k1, RADIX * n2 + k1] = -np.sin(ang) * seq ** -0.5
    return _bf16_const(mc), _bf16_const(ms)


def _fourier_stage2(yr, yi, gf, batch, seq, tiles=4):
    s1 = seq // RADIX
    mc, ms = _stage2_matrices(seq)
    rows = tiles * RADIX
    y_spec = pl.BlockSpec((None, RADIX, rows, FOURIER_WIDTH), lambda b, c: (b, 0, c, 0))
    m_spec = pl.BlockSpec((tiles, ROW_GROUP, ROW_GROUP), lambda b, c: (c, 0, 0))
    out = pl.pallas_call(
        functools.partial(_fourier2_kernel, tiles=tiles),
        grid=(batch, s1 // rows),
        in_specs=[y_spec, y_spec, m_spec, m_spec, y_spec],
        out_specs=y_spec,
        out_shape=jax.ShapeDtypeStruct((batch, RADIX, s1, FOURIER_WIDTH), BF16),
        compiler_params=_params("parallel", "parallel"),
        name="fourier_seq2",
    )(yr, yi, mc, ms, gf.reshape(batch, RADIX, s1, FOURIER_WIDTH))
    return out.reshape(batch * seq, FOURIER_WIDTH)


def _outproj_kernel(x_ref, o16_ref, ga_ref, mf_ref, unperm_ref, w_ref, y_ref):
    heads = [jnp.concatenate([o16_ref[hd, r] for r in range(RADIX)], axis=0) for hd in range(N_HEADS)]
    attn16 = jnp.concatenate(heads, axis=1)
    attn = jnp.dot(unperm_ref[...], attn16, preferred_element_type=F32)
    y = x_ref[...] + jnp.dot(mf_ref[...], w_ref[ATTN_WIDTH:, :], preferred_element_type=F32)
    mix_a = (attn * ga_ref[...].astype(F32)).astype(BF16)
    y_ref[...] = y + jnp.dot(mix_a, w_ref[:ATTN_WIDTH, :], preferred_element_type=F32)


def _outproj(x2d, o16, ga, mix_f, unperm, w_out, seq):
    t = x2d.shape[0]
    groups = seq // ROW_GROUP
    tok_spec = pl.BlockSpec((ROW_GROUP, ATTN_WIDTH), lambda i: (i, 0))
    x_spec = pl.BlockSpec((ROW_GROUP, D_MODEL), lambda i: (i, 0))
    return pl.pallas_call(
        _outproj_kernel,
        grid=(t // ROW_GROUP,),
        in_specs=[x_spec,
                  pl.BlockSpec((N_HEADS, None, RADIX, RADIX, HEAD_DIM),
                               lambda i: (0, i // groups, 0, i % groups, 0)),
                  tok_spec, tok_spec,
                  _resident((ROW_GROUP, ROW_GROUP)),
                  _resident((D_MODEL, D_MODEL))],
        out_specs=x_spec,
        out_shape=jax.ShapeDtypeStruct((t, D_MODEL), F32),
        compiler_params=_params("arbitrary"),
        name="outproj",
    )(x2d, o16, ga, mix_f, unperm, w_out)


def _layer(x, gain, w_in, qg, kg, chan_mats, w_out, perm, unperm):
    batch, seq, _ = x.shape
    x2d = x.reshape(batch * seq, D_MODEL)
    q16, k1, k16, v1, v16, ga, u16, gf = _inproj(x2d, batch, seq, gain, w_in, qg, kg, perm)
    o16 = _attention(q16, k1, v1, k16, v16, batch, seq)
    yr, yi = _fourier_stage1(u16, chan_mats, batch, seq)
    mix_f = _fourier_stage2(yr, yi, gf, batch, seq)
    y = _outproj(x2d, o16, ga, mix_f, unperm, w_out, seq)
    return y.reshape(batch, seq, D_MODEL)


def kernel(x_prompt, x_sample, rms_gain, w_in, q_norm_gain, k_norm_gain, w_fourier, w_out):
    depth = rms_gain.shape[0]
    p = _group_permutation()
    perm, unperm = _bf16_const(p), _bf16_const(p.T)
    for l in range(depth):
        gain = rms_gain[l].reshape(1, D_MODEL)
        w_in_l = w_in[l].astype(BF16)
        w_out_l = w_out[l].astype(BF16)
        qg = q_norm_gain[l].reshape(1, HEAD_DIM)
        kg = k_norm_gain[l].reshape(1, HEAD_DIM)
        chan_mats = _channel_matrices(w_fourier[l])
        x_prompt = _layer(x_prompt, gain, w_in_l, qg, kg, chan_mats, w_out_l, perm, unperm)
        x_sample = _layer(x_sample, gain, w_in_l, qg, kg, chan_mats, w_out_l, perm, unperm)
    return (x_prompt, x_sample)
```

```python
import functools
import math

import jax
import jax.numpy as jnp
import numpy as np
from jax import lax
from jax.experimental import pallas as pl
from jax.experimental.pallas import tpu as pltpu

D_MODEL = 2048
ATTN_WIDTH = 1024
FOURIER_WIDTH = 1024
HEAD_DIM = 128
N_HEADS = ATTN_WIDTH // HEAD_DIM
N_GROUPS = 4
GROUP_DIM = FOURIER_WIDTH // N_GROUPS
ROPE_THETA = 500000.0
ROPE_DIM = HEAD_DIM // 4
ROPE_HALF = ROPE_DIM // 2
HALF_KEYS = 64
RMS_EPS = 1e-6
RADIX = 16
ROW_GROUP = RADIX * RADIX
Q_BLOCK = 128
FOURIER1_ROWS = 512
BLOCKS_PER_GROUP = 4
MASK_VALUE = -1e30

VMEM_LIMIT_BYTES = 56 * 1024 * 1024

F32 = jnp.float32
BF16 = jnp.bfloat16


def _params(*semantics):
    return pltpu.CompilerParams(dimension_semantics=semantics, vmem_limit_bytes=VMEM_LIMIT_BYTES)


def _resident(shape):
    return pl.BlockSpec(shape, lambda *_: (0,) * len(shape), pipeline_mode=pl.Buffered(1))


def _rope_tables(seq, scale):
    expo = np.arange(ROPE_HALF, dtype=np.float32) / np.float32(ROPE_HALF)
    inv_freq = (np.float32(1.0) / np.power(np.float32(ROPE_THETA), expo)).astype(np.float32)
    ang = (np.arange(seq, dtype=np.float32)[:, None] * inv_freq[None, :]).astype(np.float64)
    cos, sin = np.cos(ang), np.sin(ang)
    a = np.ones((seq, HEAD_DIM))
    b = np.zeros((seq, HEAD_DIM))
    c = np.zeros((seq, HEAD_DIM))
    a[:, :ROPE_HALF] = cos
    a[:, ROPE_HALF:ROPE_DIM] = cos
    b[:, ROPE_HALF:ROPE_DIM] = sin
    c[:, :ROPE_HALF] = -sin
    return tuple(jnp.asarray((t * scale).astype(np.float32)) for t in (a, b, c))


def _dft_cos_sin(n, rows, cols, scale=1.0):
    m = (np.asarray(rows, dtype=np.int64)[:, None] * np.asarray(cols, dtype=np.int64)[None, :]) % n
    ang = 2.0 * np.pi * m.astype(np.float64) / n
    return np.cos(ang) * scale, np.sin(ang) * scale


def _group_permutation():
    p = np.zeros((ROW_GROUP, ROW_GROUP), np.float32)
    j, r = np.meshgrid(np.arange(RADIX), np.arange(RADIX), indexing="ij")
    p[(RADIX * r + j).ravel(), (RADIX * j + r).ravel()] = 1.0
    return p


def _bf16_const(a):
    return jnp.asarray(np.asarray(a, np.float32)).astype(BF16)


def _silu(a):
    return a * (1.0 / (1.0 + jnp.exp(-a)))


def _inproj_kernel(x_ref, gain_ref, w_ref, qg_ref, kg_ref,
                   qa_ref, qb_ref, qc_ref, ka_ref, kb_ref, kc_ref, perm_ref,
                   q16_ref, k1_ref, k16_ref, v1_ref, v16_ref, ga_ref, u16_ref, gf_ref):
    x = x_ref[...]
    ms = jnp.mean(x * x, axis=-1, keepdims=True)
    h = (x * lax.rsqrt(ms + RMS_EPS) * gain_ref[...]).astype(BF16)
    perm = perm_ref[...]

    def column_tile(j):
        cols = slice(j * ATTN_WIDTH, (j + 1) * ATTN_WIDTH)
        return jnp.dot(h, w_ref[:, cols], preferred_element_type=F32)

    def norm_rope(acc, g_ref, a_ref, b_ref, c_ref):
        heads = []
        for hd in range(N_HEADS):
            a = acc[:, hd * HEAD_DIM:(hd + 1) * HEAD_DIM]
            ms_h = jnp.mean(a * a, axis=-1, keepdims=True)
            n = a * lax.rsqrt(ms_h + RMS_EPS) * g_ref[...]
            r = (n * a_ref[...]
                 + pltpu.roll(n, ROPE_HALF, 1) * b_ref[...]
                 + pltpu.roll(n, HEAD_DIM - ROPE_HALF, 1) * c_ref[...])
            heads.append(r.astype(BF16))
        return jnp.concatenate(heads, axis=1)

    def to_mod16(t):
        return jnp.dot(perm, t, preferred_element_type=F32)

    def store_heads(ref, t):
        for hd in range(N_HEADS):
            ref[hd] = t[:, hd * HEAD_DIM:(hd + 1) * HEAD_DIM]

    def store_head_pieces(ref, t):
        for hd in range(N_HEADS):
            for r in range(RADIX):
                ref[hd, r] = t[r * RADIX:(r + 1) * RADIX, hd * HEAD_DIM:(hd + 1) * HEAD_DIM]

    acc_q = column_tile(0)
    acc_k = column_tile(1)
    q = norm_rope(acc_q, qg_ref, qa_ref, qb_ref, qc_ref)
    acc_v = column_tile(2)
    store_head_pieces(q16_ref, to_mod16(q))

    k = norm_rope(acc_k, kg_ref, ka_ref, kb_ref, kc_ref)
    acc_ga = column_tile(3)
    store_heads(k1_ref, k)
    store_head_pieces(k16_ref, to_mod16(k).astype(BF16))

    v = acc_v.astype(BF16)
    acc_u = column_tile(4)
    store_heads(v1_ref, v)
    store_head_pieces(v16_ref, to_mod16(v).astype(BF16))

    ga_ref[...] = _silu(acc_ga).astype(BF16)
    acc_gf = column_tile(5)

    u16 = to_mod16(acc_u.astype(BF16)).astype(BF16)
    for r in range(RADIX):
        u16_ref[r] = u16[r * RADIX:(r + 1) * RADIX, :]

    gf_ref[...] = _silu(acc_gf).astype(BF16)


def _inproj(x2d, batch, seq, gain, w_in, qg, kg, perm):
    t = x2d.shape[0]
    groups = seq // ROW_GROUP
    sub = seq // RADIX
    qa, qb, qc = _rope_tables(seq, HEAD_DIM ** -0.5 * math.log2(math.e))
    ka, kb, kc = _rope_tables(seq, 1.0)
    rope_spec = pl.BlockSpec((ROW_GROUP, HEAD_DIM), lambda i: (i % groups, 0))
    tok_spec = pl.BlockSpec((ROW_GROUP, ATTN_WIDTH), lambda i: (i, 0))
    head_nat_spec = pl.BlockSpec((N_HEADS, ROW_GROUP, HEAD_DIM), lambda i: (0, i, 0))
    head_m16_spec = pl.BlockSpec((N_HEADS, None, RADIX, RADIX, HEAD_DIM),
                                 lambda i: (0, i // groups, 0, i % groups, 0))
    head_nat = jax.ShapeDtypeStruct((N_HEADS, t, HEAD_DIM), BF16)
    head_m16 = lambda dt: jax.ShapeDtypeStruct((N_HEADS, batch, RADIX, sub, HEAD_DIM), dt)
    return pl.pallas_call(
        _inproj_kernel,
        grid=(t // ROW_GROUP,),
        in_specs=[
            pl.BlockSpec((ROW_GROUP, D_MODEL), lambda i: (i, 0)),
            _resident((1, D_MODEL)),
            _resident(w_in.shape),
            _resident((1, HEAD_DIM)), _resident((1, HEAD_DIM)),
            rope_spec, rope_spec, rope_spec, rope_spec, rope_spec, rope_spec,
            _resident((ROW_GROUP, ROW_GROUP)),
        ],
        out_specs=[
            head_m16_spec, head_nat_spec, head_m16_spec, head_nat_spec, head_m16_spec,
            tok_spec,
            pl.BlockSpec((None, RADIX, RADIX, FOURIER_WIDTH), lambda i: (i // groups, 0, i % groups, 0)),
            tok_spec,
        ],
        out_shape=[
            head_m16(F32), head_nat, head_m16(BF16), head_nat, head_m16(BF16),
            jax.ShapeDtypeStruct((t, ATTN_WIDTH), BF16),
            jax.ShapeDtypeStruct((batch, RADIX, sub, FOURIER_WIDTH), BF16),
            jax.ShapeDtypeStruct((t, FOURIER_WIDTH), BF16),
        ],
        compiler_params=_params("arbitrary"),
        name="inproj",
    )(x2d, gain, w_in, qg, kg, qa, qb, qc, ka, kb, kc, perm)


def _window_start(first, half, total, window):
    return min(max(first - half, 0), total - window)


class _Pattern:
    def __init__(self, q_first, k_first, diff0, scale):
        deltas = [scale * (q0 - k0) for q0, k0 in zip(q_first, k_first)]
        uniq = sorted(set(deltas))
        self.q_first, self.k_first = q_first, k_first
        self.table_of = [uniq.index(d) for d in deltas]
        self.bias = jnp.asarray(np.stack(
            [np.where(np.abs(diff0 + d) <= HALF_KEYS, 0.0, MASK_VALUE) for d in uniq]).astype(np.float32))


def _attention_patterns(seq):
    sub = seq // RADIX
    qi = np.arange(Q_BLOCK)[:, None]
    win16 = min(Q_BLOCK + 2 * HALF_KEYS, sub)
    q16 = list(range(0, sub, Q_BLOCK))
    p16 = _Pattern(q16, [_window_start(q0, HALF_KEYS, sub, win16) for q0 in q16],
                   qi - np.arange(win16)[None, :], 1)
    q_rows, k_rows = Q_BLOCK // 4, Q_BLOCK // 4 + 2 * HALF_KEYS // 4
    kc = np.arange(4 * k_rows)[None, :]
    q4 = list(range(0, sub, q_rows))
    p4 = _Pattern(q4, [_window_start(q0, HALF_KEYS // 4, sub, k_rows) for q0 in q4],
                  4 * (qi % q_rows - kc % k_rows) + (qi // q_rows - kc // k_rows), 4)
    win1 = Q_BLOCK + 2 * HALF_KEYS
    q1 = list(range(0, seq, Q_BLOCK))
    p1 = _Pattern(q1, [_window_start(q0, HALF_KEYS, seq, win1) for q0 in q1],
                  RADIX * (qi % (Q_BLOCK // RADIX)) + qi // (Q_BLOCK // RADIX) - np.arange(win1)[None, :], 1)
    return p16, p4, p1


def _attn_kernel(q16_ref, k1_ref, v1_ref, k16_ref, v16_ref, b16_ref, b4_ref, b1_ref,
                 o_ref, acc_s, m_s, l_s, p0_s, w0_s, p1_s, w1_s, *, pat16, pat4, pat1):
    def gather(ref, pieces):
        return jnp.concatenate([ref[r, pl.ds(s0, n), :] for r, s0, n in pieces], axis=0)

    def scatter(ref, pieces, value):
        off = 0
        for r, s0, n in pieces:
            ref[r, pl.ds(s0, n), :] = value[off:off + n]
            off += n

    def scores_stage(blocks, first, p_scr, w_scr):
        for i, (q, k, _, bias, pieces) in enumerate(blocks):
            s = lax.dot_general(q(), k(), (((1,), (1,)), ((), ())), preferred_element_type=F32) + bias()
            m_new = jnp.broadcast_to(jnp.max(s, axis=-1, keepdims=True), (Q_BLOCK, HEAD_DIM))
            if not first:
                m_old = gather(m_s, pieces)
                m_new = jnp.maximum(m_old, m_new)
                w_scr[i] = jnp.exp2(m_old - m_new)
            width = s.shape[1]
            p = jnp.exp2(s - jnp.concatenate([m_new] * (width // HEAD_DIM), axis=1))
            p_scr[i, :, :width] = p.astype(BF16)
            scatter(m_s, pieces, m_new)

    def values_stage(blocks, first, p_scr, w_scr):
        for i, (_, _, v, _, pieces) in enumerate(blocks):
            vw = v()
            v_aug = jnp.concatenate([vw, jnp.ones(vw.shape, BF16)], axis=1)
            pv = jnp.dot(p_scr[i, :, :vw.shape[0]], v_aug, preferred_element_type=F32)
            acc, l = pv[:, :HEAD_DIM], pv[:, HEAD_DIM:]
            if not first:
                w_old = w_scr[i]
                acc = w_old * gather(acc_s, pieces) + acc
                l = w_old * gather(l_s, pieces) + l
            scatter(l_s, pieces, l)
            scatter(acc_s, pieces, acc)

    groups = []

    def add_groups(blocks, first):
        groups.extend((blocks[i:i + BLOCKS_PER_GROUP], first) for i in range(0, len(blocks), BLOCKS_PER_GROUP))

    win16 = b16_ref.shape[2]
    blocks = []
    for r in range(RADIX):
        for blk, (l0, ws) in enumerate(zip(pat16.q_first, pat16.k_first)):
            blocks.append((
                lambda r=r, l0=l0: q16_ref[r, l0:l0 + Q_BLOCK, :].astype(BF16),
                lambda r=r, ws=ws: k16_ref[r, ws:ws + win16, :],
                lambda r=r, ws=ws: v16_ref[r, ws:ws + win16, :],
                lambda blk=blk: b16_ref[pat16.table_of[blk]],
                [(r, l0, Q_BLOCK)]))
    add_groups(blocks, True)

    q_rows4 = Q_BLOCK // 4
    k_rows4 = b4_ref.shape[2] // 4
    blocks = []
    for r4 in range(4):
        res = [r4 + 4 * a for a in range(4)]
        for blk, (l0, ws) in enumerate(zip(pat4.q_first, pat4.k_first)):
            q_pieces = [(r, l0, q_rows4) for r in res]
            k_pieces = [(r, ws, k_rows4) for r in res]
            blocks.append((
                lambda p=q_pieces: gather(q16_ref, p).astype(BF16),
                lambda p=k_pieces: gather(k16_ref, p),
                lambda p=k_pieces: gather(v16_ref, p),
                lambda blk=blk: b4_ref[pat4.table_of[blk]],
                q_pieces))
    add_groups(blocks, False)

    q_rows1 = Q_BLOCK // RADIX
    win1 = b1_ref.shape[2]
    blocks = []
    for blk, (n0, ws) in enumerate(zip(pat1.q_first, pat1.k_first)):
        q_pieces = [(r, n0 // RADIX, q_rows1) for r in range(RADIX)]
        blocks.append((
            lambda p=q_pieces: gather(q16_ref, p).astype(BF16),
            lambda ws=ws: k1_ref[ws:ws + win1, :],
            lambda ws=ws: v1_ref[ws:ws + win1, :],
            lambda blk=blk: b1_ref[pat1.table_of[blk]],
            q_pieces))
    add_groups(blocks, False)

    slots = ((p0_s, w0_s), (p1_s, w1_s))
    scores_stage(*groups[0], *slots[0])
    for g in range(1, len(groups)):
        scores_stage(*groups[g], *slots[g % 2])
        values_stage(*groups[g - 1], *slots[(g - 1) % 2])
    values_stage(*groups[-1], *slots[(len(groups) - 1) % 2])

    def finish(r, carry):
        o_ref[r] = (acc_s[r] * (1.0 / l_s[r])).astype(BF16)
        return carry

    lax.fori_loop(0, RADIX, finish, 0)


def _attention(q16, k1, v1, k16, v16, batch, seq):
    sub = seq // RADIX
    pat16, pat4, pat1 = _attention_patterns(seq)
    m16_spec = pl.BlockSpec((None, None, RADIX, sub, HEAD_DIM), lambda b, h: (h, b, 0, 0, 0))
    nat_spec = pl.BlockSpec((None, seq, HEAD_DIM), lambda b, h: (h, b, 0))
    state = pltpu.VMEM((RADIX, sub, HEAD_DIM), F32)
    probs = pltpu.VMEM((BLOCKS_PER_GROUP, Q_BLOCK, Q_BLOCK + 2 * HALF_KEYS), BF16)
    rescale = pltpu.VMEM((BLOCKS_PER_GROUP, Q_BLOCK, HEAD_DIM), F32)
    return pl.pallas_call(
        functools.partial(_attn_kernel, pat16=pat16, pat4=pat4, pat1=pat1),
        grid=(batch, N_HEADS),
        in_specs=[m16_spec, nat_spec, nat_spec, m16_spec, m16_spec,
                  _resident(pat16.bias.shape), _resident(pat4.bias.shape), _resident(pat1.bias.shape)],
        out_specs=m16_spec,
        out_shape=jax.ShapeDtypeStruct((N_HEADS, batch, RADIX, sub, HEAD_DIM), BF16),
        scratch_shapes=[state, state, state, probs, rescale, probs, rescale],
        compiler_params=_params("parallel", "parallel"),
        name="attention",
    )(q16, k1, v1, k16, v16, pat16.bias, pat4.bias, pat1.bias)


def _chanmat_kernel(c_ref, s_ref, w_ref, m_ref):
    for g in range(N_GROUPS):
        w = w_ref[g]
        mc = jnp.dot(c_ref[...], w, preferred_element_type=F32, precision=lax.Precision.HIGHEST)
        ms = jnp.dot(s_ref[...], w, preferred_element_type=F32, precision=lax.Precision.HIGHEST)
        m_ref[g, :, :GROUP_DIM] = mc.astype(BF16)
        m_ref[g, :, GROUP_DIM:] = ms.astype(BF16)


def _channel_matrices(w_fourier):
    idx = np.arange(GROUP_DIM)
    cc, sc = _dft_cos_sin(GROUP_DIM, idx, idx, GROUP_DIM ** -0.5)
    return pl.pallas_call(
        _chanmat_kernel,
        out_shape=jax.ShapeDtypeStruct((N_GROUPS, GROUP_DIM, 2 * GROUP_DIM), BF16),
        name="chanmat",
    )(jnp.asarray(cc, F32), jnp.asarray(sc, F32), w_fourier)


def _fourier_kernel(u_ref, m_ref, c_ref, s_ref, mc_ref, ms_ref, gf_ref, o_ref, yr_s, yi_s,
                    *, stage1_steps, tiles):
    t = pl.program_id(1)
    dot = functools.partial(jnp.dot, preferred_element_type=F32)
    per_step = u_ref.shape[0]

    @pl.when(t < stage1_steps)
    def _():
        z = []
        for j in range(per_step):
            a_parts, b_parts = [], []
            for g in range(N_GROUPS):
                ab = dot(u_ref[j, :, g * GROUP_DIM:(g + 1) * GROUP_DIM], m_ref[g])
                a_parts.append(ab[:, :GROUP_DIM].astype(BF16))
                b_parts.append(ab[:, GROUP_DIM:].astype(BF16))
            z.append((jnp.concatenate(a_parts, axis=1), jnp.concatenate(b_parts, axis=1)))
        c, s = c_ref[...], s_ref[...]
        for j, (a, b) in enumerate(z):
            n2 = t * per_step + j
            yr_s[n2] = (dot(c, a) - dot(s, b)).astype(BF16)
            yi_s[n2] = (dot(s, a) + dot(c, b)).astype(BF16)

    @pl.when(t >= stage1_steps)
    def _():
        first_row = (t - stage1_steps) * (tiles * RADIX)
        for i in range(tiles):
            rows = pl.ds(pl.multiple_of(first_row + i * RADIX, RADIX), RADIX)
            yr = jnp.concatenate([yr_s[n2, rows, :] for n2 in range(RADIX)], axis=0)
            yi = jnp.concatenate([yi_s[n2, rows, :] for n2 in range(RADIX)], axis=0)
            x = dot(mc_ref[i], yr) + dot(ms_ref[i], yi)
            out_rows = slice(i * RADIX, (i + 1) * RADIX)
            for k2 in range(RADIX):
                piece = x[k2 * RADIX:(k2 + 1) * RADIX, :] * gf_ref[k2, out_rows, :].astype(F32)
                o_ref[k2, out_rows, :] = piece.astype(BF16)


def _stage2_matrices(seq):
    s1 = seq // RADIX
    tiles = s1 // RADIX
    c, k2, k1, n2 = np.meshgrid(np.arange(tiles), np.arange(RADIX), np.arange(RADIX), np.arange(RADIX),
                                indexing="ij")
    k = RADIX * c + k1 + s1 * k2
    ang = 2.0 * np.pi * ((n2 * k) % seq).astype(np.float64) / seq
    mc = np.zeros((tiles, ROW_GROUP, ROW_GROUP), np.float32)
    ms = np.zeros((tiles, ROW_GROUP, ROW_GROUP), np.float32)
    mc[c, RADIX * k2 + k1, RADIX * n2 + k1] = np.cos(ang) * seq ** -0.5
    ms[c, RADIX * k2 + k1, RADIX * n2 + k1] = -np.sin(ang) * seq ** -0.5
    return _bf16_const(mc), _bf16_const(ms)


def _fourier(u16, chan_mats, gf, batch, seq, tiles=4):
    s1 = seq // RADIX
    idx = np.arange(s1)
    c1, sn1 = _dft_cos_sin(s1, idx, idx)
    mc, ms = _stage2_matrices(seq)
    per_step = FOURIER1_ROWS // s1
    stage1_steps = RADIX // per_step
    rows = tiles * RADIX
    stage2 = lambda t: jnp.maximum(t - stage1_steps, 0)
    nat_spec = pl.BlockSpec((None, RADIX, rows, FOURIER_WIDTH), lambda b, t: (b, 0, stage2(t), 0))
    m_spec = pl.BlockSpec((tiles, ROW_GROUP, ROW_GROUP), lambda b, t: (stage2(t), 0, 0))
    y_scratch = pltpu.VMEM((RADIX, s1, FOURIER_WIDTH), BF16)
    out = pl.pallas_call(
        functools.partial(_fourier_kernel, stage1_steps=stage1_steps, tiles=tiles),
        grid=(batch, stage1_steps + s1 // rows),
        in_specs=[pl.BlockSpec((None, per_step, s1, FOURIER_WIDTH),
                               lambda b, t: (b, jnp.minimum(t, stage1_steps - 1), 0, 0)),
                  _resident(chan_mats.shape), _resident((s1, s1)), _resident((s1, s1)),
                  m_spec, m_spec, nat_spec],
        out_specs=nat_spec,
        out_shape=jax.ShapeDtypeStruct((batch, RADIX, s1, FOURIER_WIDTH), BF16),
        scratch_shapes=[y_scratch, y_scratch],
        compiler_params=_params("parallel", "arbitrary"),
        name="fourier",
    )(u16, chan_mats, _bf16_const(c1), _bf16_const(sn1), mc, ms,
      gf.reshape(batch, RADIX, s1, FOURIER_WIDTH))
    return out.reshape(batch * seq, FOURIER_WIDTH)


def _outproj_kernel(x_ref, o16_ref, ga_ref, mf_ref, unperm_ref, w_ref, y_ref):
    heads = [jnp.concatenate([o16_ref[hd, r] for r in range(RADIX)], axis=0) for hd in range(N_HEADS)]
    attn16 = jnp.concatenate(heads, axis=1)
    attn = jnp.dot(unperm_ref[...], attn16, preferred_element_type=F32)
    y = x_ref[...] + jnp.dot(mf_ref[...], w_ref[ATTN_WIDTH:, :], preferred_element_type=F32)
    mix_a = (attn * ga_ref[...].astype(F32)).astype(BF16)
    y_ref[...] = y + jnp.dot(mix_a, w_ref[:ATTN_WIDTH, :], preferred_element_type=F32)


def _outproj(x2d, o16, ga, mix_f, unperm, w_out, seq):
    t = x2d.shape[0]
    groups = seq // ROW_GROUP
    tok_spec = pl.BlockSpec((ROW_GROUP, ATTN_WIDTH), lambda i: (i, 0))
    x_spec = pl.BlockSpec((ROW_GROUP, D_MODEL), lambda i: (i, 0))
    return pl.pallas_call(
        _outproj_kernel,
        grid=(t // ROW_GROUP,),
        in_specs=[x_spec,
                  pl.BlockSpec((N_HEADS, None, RADIX, RADIX, HEAD_DIM),
                               lambda i: (0, i // groups, 0, i % groups, 0)),
                  tok_spec, tok_spec,
                  _resident((ROW_GROUP, ROW_GROUP)),
                  _resident((D_MODEL, D_MODEL))],
        out_specs=x_spec,
        out_shape=jax.ShapeDtypeStruct((t, D_MODEL), F32),
        compiler_params=_params("arbitrary"),
        name="outproj",
    )(x2d, o16, ga, mix_f, unperm, w_out)


def _layer(x, gain, w_in, qg, kg, chan_mats, w_out, perm, unperm):
    batch, seq, _ = x.shape
    x2d = x.reshape(batch * seq, D_MODEL)
    q16, k1, k16, v1, v16, ga, u16, gf = _inproj(x2d, batch, seq, gain, w_in, qg, kg, perm)
    o16 = _attention(q16, k1, v1, k16, v16, batch, seq)
    mix_f = _fourier(u16, chan_mats, gf, batch, seq)
    y = _outproj(x2d, o16, ga, mix_f, unperm, w_out, seq)
    return y.reshape(batch, seq, D_MODEL)


def kernel(x_prompt, x_sample, rms_gain, w_in, q_norm_gain, k_norm_gain, w_fourier, w_out):
    depth = rms_gain.shape[0]
    p = _group_permutation()
    perm, unperm = _bf16_const(p), _bf16_const(p.T)
    for l in range(depth):
        gain = rms_gain[l].reshape(1, D_MODEL)
        w_in_l = w_in[l].astype(BF16)
        w_out_l = w_out[l].astype(BF16)
        qg = q_norm_gain[l].reshape(1, HEAD_DIM)
        kg = k_norm_gain[l].reshape(1, HEAD_DIM)
        chan_mats = _channel_matrices(w_fourier[l])
        x_prompt = _layer(x_prompt, gain, w_in_l, qg, kg, chan_mats, w_out_l, perm, unperm)
        x_sample = _layer(x_sample, gain, w_in_l, qg, kg, chan_mats, w_out_l, perm, unperm)
    return (x_prompt, x_sample)
```

```python
import functools
import math

import jax
import jax.numpy as jnp
import numpy as np
from jax import lax
from jax.experimental import pallas as pl
from jax.experimental.pallas import tpu as pltpu

D_MODEL = 2048
ATTN_WIDTH = 1024
FOURIER_WIDTH = 1024
HEAD_DIM = 128
N_HEADS = ATTN_WIDTH // HEAD_DIM
N_GROUPS = 4
GROUP_DIM = FOURIER_WIDTH // N_GROUPS
ROPE_THETA = 500000.0
ROPE_DIM = HEAD_DIM // 4
ROPE_HALF = ROPE_DIM // 2
HALF_KEYS = 64
RMS_EPS = 1e-6
RADIX = 16
ROW_GROUP = RADIX * RADIX
Q_BLOCK = 128
FOURIER1_ROWS = 512
BLOCKS_PER_GROUP = 4
MASK_VALUE = -1e30

VMEM_LIMIT_BYTES = 56 * 1024 * 1024

F32 = jnp.float32
BF16 = jnp.bfloat16


def _params(*semantics):
    return pltpu.CompilerParams(dimension_semantics=semantics, vmem_limit_bytes=VMEM_LIMIT_BYTES)


def _resident(shape):
    return pl.BlockSpec(shape, lambda *_: (0,) * len(shape), pipeline_mode=pl.Buffered(1))


def _rope_tables(seq, scale):
    expo = np.arange(ROPE_HALF, dtype=np.float32) / np.float32(ROPE_HALF)
    inv_freq = (np.float32(1.0) / np.power(np.float32(ROPE_THETA), expo)).astype(np.float32)
    ang = (np.arange(seq, dtype=np.float32)[:, None] * inv_freq[None, :]).astype(np.float64)
    cos, sin = np.cos(ang), np.sin(ang)
    a = np.ones((seq, HEAD_DIM))
    b = np.zeros((seq, HEAD_DIM))
    c = np.zeros((seq, HEAD_DIM))
    a[:, :ROPE_HALF] = cos
    a[:, ROPE_HALF:ROPE_DIM] = cos
    b[:, ROPE_HALF:ROPE_DIM] = sin
    c[:, :ROPE_HALF] = -sin
    return tuple(jnp.asarray((t * scale).astype(np.float32)) for t in (a, b, c))


def _dft_cos_sin(n, rows, cols, scale=1.0):
    m = (np.asarray(rows, dtype=np.int64)[:, None] * np.asarray(cols, dtype=np.int64)[None, :]) % n
    ang = 2.0 * np.pi * m.astype(np.float64) / n
    return np.cos(ang) * scale, np.sin(ang) * scale


def _permute_scratch():
    return pltpu.VMEM((N_HEADS * ROW_GROUP, HEAD_DIM), F32)


def _bf16_const(a):
    return jnp.asarray(np.asarray(a, np.float32)).astype(BF16)


def _silu(a):
    return a * (1.0 / (1.0 + jnp.exp(-a)))


def _inproj_kernel(x_ref, gain_ref, w_ref, qg_ref, kg_ref,
                   qa_ref, qb_ref, qc_ref, ka_ref, kb_ref, kc_ref,
                   q16_ref, k1_ref, k16_ref, v1_ref, v16_ref, ga_ref, u16_ref, gf_ref,
                   *perm_scratch):
    q_scr, k_scr, v_scr, u_scr = (perm_scratch[i:i + 2] for i in range(0, 8, 2))
    x = x_ref[...]
    ms = jnp.mean(x * x, axis=-1, keepdims=True)
    h = (x * lax.rsqrt(ms + RMS_EPS) * gain_ref[...]).astype(BF16)

    def column_tile(j):
        cols = slice(j * ATTN_WIDTH, (j + 1) * ATTN_WIDTH)
        return jnp.dot(h, w_ref[:, cols], preferred_element_type=F32)

    def lane_tiles(t):
        return [t[:, i * HEAD_DIM:(i + 1) * HEAD_DIM] for i in range(N_HEADS)]

    def norm_rope(acc, g_ref, a_ref, b_ref, c_ref):
        heads = []
        for a in lane_tiles(acc):
            ms_h = jnp.mean(a * a, axis=-1, keepdims=True)
            n = a * lax.rsqrt(ms_h + RMS_EPS) * g_ref[...]
            heads.append(n * a_ref[...]
                         + pltpu.roll(n, ROPE_HALF, 1) * b_ref[...]
                         + pltpu.roll(n, HEAD_DIM - ROPE_HALF, 1) * c_ref[...])
        return heads

    def to_mod16(tiles, scr):
        a_scr, b_scr = scr
        quarter = ROW_GROUP // 4
        for i, t in enumerate(tiles):
            a_scr[i * ROW_GROUP:(i + 1) * ROW_GROUP, :] = t
        for i in range(N_HEADS):
            for a in range(4):
                b_scr[pl.ds(i * ROW_GROUP + a * quarter, quarter), :] = (
                    a_scr[pl.ds(i * ROW_GROUP + a, quarter, stride=4), :])
        return [[b_scr[pl.ds(i * ROW_GROUP + r // 4 + 4 * RADIX * (r % 4), RADIX, stride=4), :]
                 for r in range(RADIX)] for i in range(N_HEADS)]

    def store_heads(ref, heads):
        for hd, t in enumerate(heads):
            ref[hd] = t.astype(ref.dtype)

    def store_head_pieces(ref, pieces):
        for hd in range(N_HEADS):
            for r in range(RADIX):
                ref[hd, r] = pieces[hd][r].astype(ref.dtype)

    acc_q = column_tile(0)
    acc_k = column_tile(1)
    q = norm_rope(acc_q, qg_ref, qa_ref, qb_ref, qc_ref)
    acc_v = column_tile(2)
    store_head_pieces(q16_ref, to_mod16(q, q_scr))

    k = norm_rope(acc_k, kg_ref, ka_ref, kb_ref, kc_ref)
    acc_ga = column_tile(3)
    store_heads(k1_ref, k)
    store_head_pieces(k16_ref, to_mod16(k, k_scr))

    v = lane_tiles(acc_v)
    acc_u = column_tile(4)
    store_heads(v1_ref, v)
    store_head_pieces(v16_ref, to_mod16(v, v_scr))

    ga_ref[...] = _silu(acc_ga).astype(BF16)
    acc_gf = column_tile(5)

    u16 = to_mod16(lane_tiles(acc_u), u_scr)
    for i in range(N_HEADS):
        for r in range(RADIX):
            u16_ref[r, :, i * HEAD_DIM:(i + 1) * HEAD_DIM] = u16[i][r].astype(BF16)

    gf_ref[...] = _silu(acc_gf).astype(BF16)


def _inproj(x2d, batch, seq, gain, w_in, qg, kg):
    t = x2d.shape[0]
    groups = seq // ROW_GROUP
    sub = seq // RADIX
    qa, qb, qc = _rope_tables(seq, HEAD_DIM ** -0.5 * math.log2(math.e))
    ka, kb, kc = _rope_tables(seq, 1.0)
    rope_spec = pl.BlockSpec((ROW_GROUP, HEAD_DIM), lambda i: (i % groups, 0))
    tok_spec = pl.BlockSpec((ROW_GROUP, ATTN_WIDTH), lambda i: (i, 0))
    head_nat_spec = pl.BlockSpec((N_HEADS, ROW_GROUP, HEAD_DIM), lambda i: (0, i, 0))
    head_m16_spec = pl.BlockSpec((N_HEADS, None, RADIX, RADIX, HEAD_DIM),
                                 lambda i: (0, i // groups, 0, i % groups, 0))
    head_nat = jax.ShapeDtypeStruct((N_HEADS, t, HEAD_DIM), BF16)
    head_m16 = lambda dt: jax.ShapeDtypeStruct((N_HEADS, batch, RADIX, sub, HEAD_DIM), dt)
    return pl.pallas_call(
        _inproj_kernel,
        grid=(t // ROW_GROUP,),
        in_specs=[
            pl.BlockSpec((ROW_GROUP, D_MODEL), lambda i: (i, 0)),
            _resident((1, D_MODEL)),
            _resident(w_in.shape),
            _resident((1, HEAD_DIM)), _resident((1, HEAD_DIM)),
            rope_spec, rope_spec, rope_spec, rope_spec, rope_spec, rope_spec,
        ],
        out_specs=[
            head_m16_spec, head_nat_spec, head_m16_spec, head_nat_spec, head_m16_spec,
            tok_spec,
            pl.BlockSpec((None, RADIX, RADIX, FOURIER_WIDTH), lambda i: (i // groups, 0, i % groups, 0)),
            tok_spec,
        ],
        out_shape=[
            head_m16(F32), head_nat, head_m16(BF16), head_nat, head_m16(BF16),
            jax.ShapeDtypeStruct((t, ATTN_WIDTH), BF16),
            jax.ShapeDtypeStruct((batch, RADIX, sub, FOURIER_WIDTH), BF16),
            jax.ShapeDtypeStruct((t, FOURIER_WIDTH), BF16),
        ],
        scratch_shapes=[_permute_scratch()] * 8,
        compiler_params=_params("arbitrary"),
        name="inproj",
    )(x2d, gain, w_in, qg, kg, qa, qb, qc, ka, kb, kc)


def _window_start(first, half, total, window):
    return min(max(first - half, 0), total - window)


class _Pattern:
    def __init__(self, q_first, k_first, diff0, scale):
        deltas = [scale * (q0 - k0) for q0, k0 in zip(q_first, k_first)]
        uniq = sorted(set(deltas))
        self.q_first, self.k_first = q_first, k_first
        self.table_of = [uniq.index(d) for d in deltas]
        self.bias = jnp.asarray(np.stack(
            [np.where(np.abs(diff0 + d) <= HALF_KEYS, 0.0, MASK_VALUE) for d in uniq]).astype(np.float32))


def _attention_patterns(seq):
    sub = seq // RADIX
    qi = np.arange(Q_BLOCK)[:, None]
    win16 = min(Q_BLOCK + 2 * HALF_KEYS, sub)
    q16 = list(range(0, sub, Q_BLOCK))
    p16 = _Pattern(q16, [_window_start(q0, HALF_KEYS, sub, win16) for q0 in q16],
                   qi - np.arange(win16)[None, :], 1)
    q_rows, k_rows = Q_BLOCK // 4, Q_BLOCK // 4 + 2 * HALF_KEYS // 4
    kc = np.arange(4 * k_rows)[None, :]
    q4 = list(range(0, sub, q_rows))
    p4 = _Pattern(q4, [_window_start(q0, HALF_KEYS // 4, sub, k_rows) for q0 in q4],
                  4 * (qi % q_rows - kc % k_rows) + (qi // q_rows - kc // k_rows), 4)
    win1 = Q_BLOCK + 2 * HALF_KEYS
    q1 = list(range(0, seq, Q_BLOCK))
    p1 = _Pattern(q1, [_window_start(q0, HALF_KEYS, seq, win1) for q0 in q1],
                  RADIX * (qi % (Q_BLOCK // RADIX)) + qi // (Q_BLOCK // RADIX) - np.arange(win1)[None, :], 1)
    return p16, p4, p1


def _attn_kernel(q16_ref, k1_ref, v1_ref, k16_ref, v16_ref, b16_ref, b4_ref, b1_ref,
                 o_ref, acc_s, m_s, l_s, p0_s, w0_s, p1_s, w1_s, *, pat16, pat4, pat1):
    def gather(ref, pieces):
        return jnp.concatenate([ref[r, pl.ds(s0, n), :] for r, s0, n in pieces], axis=0)

    def scatter(ref, pieces, value):
        off = 0
        for r, s0, n in pieces:
            ref[r, pl.ds(s0, n), :] = value[off:off + n]
            off += n

    def scores_stage(blocks, first, p_scr, w_scr):
        for i, (q, k, _, bias, pieces) in enumerate(blocks):
            s = lax.dot_general(q(), k(), (((1,), (1,)), ((), ())), preferred_element_type=F32) + bias()
            m_new = jnp.broadcast_to(jnp.max(s, axis=-1, keepdims=True), (Q_BLOCK, HEAD_DIM))
            if not first:
                m_old = gather(m_s, pieces)
                m_new = jnp.maximum(m_old, m_new)
                w_scr[i] = jnp.exp2(m_old - m_new)
            width = s.shape[1]
            p = jnp.exp2(s - jnp.concatenate([m_new] * (width // HEAD_DIM), axis=1))
            p_scr[i, :, :width] = p.astype(BF16)
            scatter(m_s, pieces, m_new)

    def values_stage(blocks, first, p_scr, w_scr):
        for i, (_, _, v, _, pieces) in enumerate(blocks):
            vw = v()
            v_aug = jnp.concatenate([vw, jnp.ones(vw.shape, BF16)], axis=1)
            pv = jnp.dot(p_scr[i, :, :vw.shape[0]], v_aug, preferred_element_type=F32)
            acc, l = pv[:, :HEAD_DIM], pv[:, HEAD_DIM:]
            if not first:
                w_old = w_scr[i]
                acc = w_old * gather(acc_s, pieces) + acc
                l = w_old * gather(l_s, pieces) + l
            scatter(l_s, pieces, l)
            scatter(acc_s, pieces, acc)

    groups = []

    def add_groups(blocks, first):
        groups.extend((blocks[i:i + BLOCKS_PER_GROUP], first) for i in range(0, len(blocks), BLOCKS_PER_GROUP))

    win16 = b16_ref.shape[2]
    blocks = []
    for r in range(RADIX):
        for blk, (l0, ws) in enumerate(zip(pat16.q_first, pat16.k_first)):
            blocks.append((
                lambda r=r, l0=l0: q16_ref[r, l0:l0 + Q_BLOCK, :].astype(BF16),
                lambda r=r, ws=ws: k16_ref[r, ws:ws + win16, :],
                lambda r=r, ws=ws: v16_ref[r, ws:ws + win16, :],
                lambda blk=blk: b16_ref[pat16.table_of[blk]],
                [(r, l0, Q_BLOCK)]))
    add_groups(blocks, True)

    q_rows4 = Q_BLOCK // 4
    k_rows4 = b4_ref.shape[2] // 4
    blocks = []
    for r4 in range(4):
        res = [r4 + 4 * a for a in range(4)]
        for blk, (l0, ws) in enumerate(zip(pat4.q_first, pat4.k_first)):
            q_pieces = [(r, l0, q_rows4) for r in res]
            k_pieces = [(r, ws, k_rows4) for r in res]
            blocks.append((
                lambda p=q_pieces: gather(q16_ref, p).astype(BF16),
                lambda p=k_pieces: gather(k16_ref, p),
                lambda p=k_pieces: gather(v16_ref, p),
                lambda blk=blk: b4_ref[pat4.table_of[blk]],
                q_pieces))
    add_groups(blocks, False)

    q_rows1 = Q_BLOCK // RADIX
    win1 = b1_ref.shape[2]
    blocks = []
    for blk, (n0, ws) in enumerate(zip(pat1.q_first, pat1.k_first)):
        q_pieces = [(r, n0 // RADIX, q_rows1) for r in range(RADIX)]
        blocks.append((
            lambda p=q_pieces: gather(q16_ref, p).astype(BF16),
            lambda ws=ws: k1_ref[ws:ws + win1, :],
            lambda ws=ws: v1_ref[ws:ws + win1, :],
            lambda blk=blk: b1_ref[pat1.table_of[blk]],
            q_pieces))
    add_groups(blocks, False)

    slots = ((p0_s, w0_s), (p1_s, w1_s))
    scores_stage(*groups[0], *slots[0])
    for g in range(1, len(groups)):
        scores_stage(*groups[g], *slots[g % 2])
        values_stage(*groups[g - 1], *slots[(g - 1) % 2])
    values_stage(*groups[-1], *slots[(len(groups) - 1) % 2])

    def finish(r, carry):
        o_ref[r] = (acc_s[r] * (1.0 / l_s[r])).astype(BF16)
        return carry

    lax.fori_loop(0, RADIX, finish, 0)


def _attention(q16, k1, v1, k16, v16, batch, seq):
    sub = seq // RADIX
    pat16, pat4, pat1 = _attention_patterns(seq)
    m16_spec = pl.BlockSpec((None, None, RADIX, sub, HEAD_DIM), lambda b, h: (h, b, 0, 0, 0))
    nat_spec = pl.BlockSpec((None, seq, HEAD_DIM), lambda b, h: (h, b, 0))
    state = pltpu.VMEM((RADIX, sub, HEAD_DIM), F32)
    probs = pltpu.VMEM((BLOCKS_PER_GROUP, Q_BLOCK, Q_BLOCK + 2 * HALF_KEYS), BF16)
    rescale = pltpu.VMEM((BLOCKS_PER_GROUP, Q_BLOCK, HEAD_DIM), F32)
    return pl.pallas_call(
        functools.partial(_attn_kernel, pat16=pat16, pat4=pat4, pat1=pat1),
        grid=(batch, N_HEADS),
        in_specs=[m16_spec, nat_spec, nat_spec, m16_spec, m16_spec,
                  _resident(pat16.bias.shape), _resident(pat4.bias.shape), _resident(pat1.bias.shape)],
        out_specs=m16_spec,
        out_shape=jax.ShapeDtypeStruct((N_HEADS, batch, RADIX, sub, HEAD_DIM), BF16),
        scratch_shapes=[state, state, state, probs, rescale, probs, rescale],
        compiler_params=_params("parallel", "parallel"),
        name="attention",
    )(q16, k1, v1, k16, v16, pat16.bias, pat4.bias, pat1.bias)


def _chanmat_kernel(c_ref, s_ref, w_ref, m_ref):
    for g in range(N_GROUPS):
        w = w_ref[g]
        mc = jnp.dot(c_ref[...], w, preferred_element_type=F32, precision=lax.Precision.HIGHEST)
        ms = jnp.dot(s_ref[...], w, preferred_element_type=F32, precision=lax.Precision.HIGHEST)
        m_ref[g, :, :GROUP_DIM] = mc.astype(BF16)
        m_ref[g, :, GROUP_DIM:] = ms.astype(BF16)


def _channel_matrices(w_fourier):
    idx = np.arange(GROUP_DIM)
    cc, sc = _dft_cos_sin(GROUP_DIM, idx, idx, GROUP_DIM ** -0.5)
    return pl.pallas_call(
        _chanmat_kernel,
        out_shape=jax.ShapeDtypeStruct((N_GROUPS, GROUP_DIM, 2 * GROUP_DIM), BF16),
        name="chanmat",
    )(jnp.asarray(cc, F32), jnp.asarray(sc, F32), w_fourier)


def _fourier_kernel(u_ref, m_ref, c_ref, s_ref, mc_ref, ms_ref, gf_ref, o_ref, yr_s, yi_s,
                    *, stage1_steps, tiles):
    t = pl.program_id(1)
    dot = functools.partial(jnp.dot, preferred_element_type=F32)
    per_step = u_ref.shape[0]

    @pl.when(t < stage1_steps)
    def _():
        z = []
        for j in range(per_step):
            a_parts, b_parts = [], []
            for g in range(N_GROUPS):
                ab = dot(u_ref[j, :, g * GROUP_DIM:(g + 1) * GROUP_DIM], m_ref[g])
                a_parts.append(ab[:, :GROUP_DIM].astype(BF16))
                b_parts.append(ab[:, GROUP_DIM:].astype(BF16))
            z.append((jnp.concatenate(a_parts, axis=1), jnp.concatenate(b_parts, axis=1)))
        c, s = c_ref[...], s_ref[...]
        for j, (a, b) in enumerate(z):
            n2 = t * per_step + j
            yr_s[n2] = (dot(c, a) - dot(s, b)).astype(BF16)
            yi_s[n2] = (dot(s, a) + dot(c, b)).astype(BF16)

    @pl.when(t >= stage1_steps)
    def _():
        first_row = (t - stage1_steps) * (tiles * RADIX)
        for i in range(tiles):
            rows = pl.ds(pl.multiple_of(first_row + i * RADIX, RADIX), RADIX)
            yr = jnp.concatenate([yr_s[n2, rows, :] for n2 in range(RADIX)], axis=0)
            yi = jnp.concatenate([yi_s[n2, rows, :] for n2 in range(RADIX)], axis=0)
            x = dot(mc_ref[i], yr) + dot(ms_ref[i], yi)
            out_rows = slice(i * RADIX, (i + 1) * RADIX)
            for k2 in range(RADIX):
                piece = x[k2 * RADIX:(k2 + 1) * RADIX, :] * gf_ref[k2, out_rows, :].astype(F32)
                o_ref[k2, out_rows, :] = piece.astype(BF16)


def _stage2_matrices(seq):
    s1 = seq // RADIX
    tiles = s1 // RADIX
    c, k2, k1, n2 = np.meshgrid(np.arange(tiles), np.arange(RADIX), np.arange(RADIX), np.arange(RADIX),
                                indexing="ij")
    k = RADIX * c + k1 + s1 * k2
    ang = 2.0 * np.pi * ((n2 * k) % seq).astype(np.float64) / seq
    mc = np.zeros((tiles, ROW_GROUP, ROW_GROUP), np.float32)
    ms = np.zeros((tiles, ROW_GROUP, ROW_GROUP), np.float32)
    mc[c, RADIX * k2 + k1, RADIX * n2 + k1] = np.cos(ang) * seq ** -0.5
    ms[c, RADIX * k2 + k1, RADIX * n2 + k1] = -np.sin(ang) * seq ** -0.5
    return _bf16_const(mc), _bf16_const(ms)


def _fourier(u16, chan_mats, gf, batch, seq, tiles=4):
    s1 = seq // RADIX
    idx = np.arange(s1)
    c1, sn1 = _dft_cos_sin(s1, idx, idx)
    mc, ms = _stage2_matrices(seq)
    per_step = FOURIER1_ROWS // s1
    stage1_steps = RADIX // per_step
    rows = tiles * RADIX
    stage2 = lambda t: jnp.maximum(t - stage1_steps, 0)
    nat_spec = pl.BlockSpec((None, RADIX, rows, FOURIER_WIDTH), lambda b, t: (b, 0, stage2(t), 0))
    m_spec = pl.BlockSpec((tiles, ROW_GROUP, ROW_GROUP), lambda b, t: (stage2(t), 0, 0))
    y_scratch = pltpu.VMEM((RADIX, s1, FOURIER_WIDTH), BF16)
    out = pl.pallas_call(
        functools.partial(_fourier_kernel, stage1_steps=stage1_steps, tiles=tiles),
        grid=(batch, stage1_steps + s1 // rows),
        in_specs=[pl.BlockSpec((None, per_step, s1, FOURIER_WIDTH),
                               lambda b, t: (b, jnp.minimum(t, stage1_steps - 1), 0, 0)),
                  _resident(chan_mats.shape), _resident((s1, s1)), _resident((s1, s1)),
                  m_spec, m_spec, nat_spec],
        out_specs=nat_spec,
        out_shape=jax.ShapeDtypeStruct((batch, RADIX, s1, FOURIER_WIDTH), BF16),
        scratch_shapes=[y_scratch, y_scratch],
        compiler_params=_params("parallel", "arbitrary"),
        name="fourier",
    )(u16, chan_mats, _bf16_const(c1), _bf16_const(sn1), mc, ms,
      gf.reshape(batch, RADIX, s1, FOURIER_WIDTH))
    return out.reshape(batch * seq, FOURIER_WIDTH)


def _outproj_kernel(x_ref, o16_ref, ga_ref, mf_ref, w_ref, y_ref, scr):
    y = x_ref[...] + jnp.dot(mf_ref[...], w_ref[ATTN_WIDTH:, :], preferred_element_type=F32)
    for hd in range(N_HEADS):
        for r in range(RADIX):
            scr[pl.ds(hd * ROW_GROUP + r * RADIX, RADIX), :] = o16_ref[hd, r].astype(F32)
    heads = []
    for hd in range(N_HEADS):
        attn = jnp.concatenate([scr[pl.ds(hd * ROW_GROUP + j, RADIX, stride=RADIX), :]
                                for j in range(RADIX)], axis=0)
        gate = ga_ref[:, hd * HEAD_DIM:(hd + 1) * HEAD_DIM].astype(F32)
        heads.append((attn * gate).astype(BF16))
    mix_a = jnp.concatenate(heads, axis=1)
    y_ref[...] = y + jnp.dot(mix_a, w_ref[:ATTN_WIDTH, :], preferred_element_type=F32)


def _outproj(x2d, o16, ga, mix_f, w_out, seq):
    t = x2d.shape[0]
    groups = seq // ROW_GROUP
    tok_spec = pl.BlockSpec((ROW_GROUP, ATTN_WIDTH), lambda i: (i, 0))
    x_spec = pl.BlockSpec((ROW_GROUP, D_MODEL), lambda i: (i, 0))
    return pl.pallas_call(
        _outproj_kernel,
        grid=(t // ROW_GROUP,),
        in_specs=[x_spec,
                  pl.BlockSpec((N_HEADS, None, RADIX, RADIX, HEAD_DIM),
                               lambda i: (0, i // groups, 0, i % groups, 0)),
                  tok_spec, tok_spec,
                  _resident((D_MODEL, D_MODEL))],
        out_specs=x_spec,
        out_shape=jax.ShapeDtypeStruct((t, D_MODEL), F32),
        scratch_shapes=[_permute_scratch()],
        compiler_params=_params("arbitrary"),
        name="outproj",
    )(x2d, o16, ga, mix_f, w_out)


def _layer(x, gain, w_in, qg, kg, chan_mats, w_out):
    batch, seq, _ = x.shape
    x2d = x.reshape(batch * seq, D_MODEL)
    q16, k1, k16, v1, v16, ga, u16, gf = _inproj(x2d, batch, seq, gain, w_in, qg, kg)
    o16 = _attention(q16, k1, v1, k16, v16, batch, seq)
    mix_f = _fourier(u16, chan_mats, gf, batch, seq)
    y = _outproj(x2d, o16, ga, mix_f, w_out, seq)
    return y.reshape(batch, seq, D_MODEL)


def kernel(x_prompt, x_sample, rms_gain, w_in, q_norm_gain, k_norm_gain, w_fourier, w_out):
    depth = rms_gain.shape[0]
    for l in range(depth):
        gain = rms_gain[l].reshape(1, D_MODEL)
        w_in_l = w_in[l].astype(BF16)
        w_out_l = w_out[l].astype(BF16)
        qg = q_norm_gain[l].reshape(1, HEAD_DIM)
        kg = k_norm_gain[l].reshape(1, HEAD_DIM)
        chan_mats = _channel_matrices(w_fourier[l])
        x_prompt = _layer(x_prompt, gain, w_in_l, qg, kg, chan_mats, w_out_l)
        x_sample = _layer(x_sample, gain, w_in_l, qg, kg, chan_mats, w_out_l)
    return (x_prompt, x_sample)
```

```python
import functools
import math

import jax
import jax.numpy as jnp
import numpy as np
from jax import lax
from jax.experimental import pallas as pl
from jax.experimental.pallas import tpu as pltpu

D_MODEL = 2048
ATTN_WIDTH = 1024
FOURIER_WIDTH = 1024
HEAD_DIM = 128
N_HEADS = ATTN_WIDTH // HEAD_DIM
N_GROUPS = 4
GROUP_DIM = FOURIER_WIDTH // N_GROUPS
ROPE_THETA = 500000.0
ROPE_DIM = HEAD_DIM // 4
ROPE_HALF = ROPE_DIM // 2
HALF_KEYS = 64
RMS_EPS = 1e-6
RADIX = 16
ROW_GROUP = RADIX * RADIX
Q_BLOCK = 128
FOURIER1_ROWS = 512
BLOCKS_PER_GROUP = 4
MASK_VALUE = -1e30

VMEM_LIMIT_BYTES = 56 * 1024 * 1024

F32 = jnp.float32
BF16 = jnp.bfloat16


def _params(*semantics):
    return pltpu.CompilerParams(dimension_semantics=semantics, vmem_limit_bytes=VMEM_LIMIT_BYTES)


def _resident(shape):
    return pl.BlockSpec(shape, lambda *_: (0,) * len(shape), pipeline_mode=pl.Buffered(1))


def _rope_tables(seq, scale):
    expo = np.arange(ROPE_HALF, dtype=np.float32) / np.float32(ROPE_HALF)
    inv_freq = (np.float32(1.0) / np.power(np.float32(ROPE_THETA), expo)).astype(np.float32)
    ang = (np.arange(seq, dtype=np.float32)[:, None] * inv_freq[None, :]).astype(np.float64)
    cos, sin = np.cos(ang), np.sin(ang)
    a = np.ones((seq, HEAD_DIM))
    b = np.zeros((seq, HEAD_DIM))
    c = np.zeros((seq, HEAD_DIM))
    a[:, :ROPE_HALF] = cos
    a[:, ROPE_HALF:ROPE_DIM] = cos
    b[:, ROPE_HALF:ROPE_DIM] = sin
    c[:, :ROPE_HALF] = -sin
    return tuple(jnp.asarray((t * scale).astype(np.float32)) for t in (a, b, c))


def _dft_cos_sin(n, rows, cols, scale=1.0):
    m = (np.asarray(rows, dtype=np.int64)[:, None] * np.asarray(cols, dtype=np.int64)[None, :]) % n
    ang = 2.0 * np.pi * m.astype(np.float64) / n
    return np.cos(ang) * scale, np.sin(ang) * scale


def _group_permutation():
    p = np.zeros((ROW_GROUP, ROW_GROUP), np.float32)
    j, r = np.meshgrid(np.arange(RADIX), np.arange(RADIX), indexing="ij")
    p[(RADIX * r + j).ravel(), (RADIX * j + r).ravel()] = 1.0
    return p


def _bf16_const(a):
    return jnp.asarray(np.asarray(a, np.float32)).astype(BF16)


def _silu(a):
    return a * (1.0 / (1.0 + jnp.exp(-a)))


def _inproj_kernel(x_ref, gain_ref, w_ref, qg_ref, kg_ref,
                   qa_ref, qb_ref, qc_ref, ka_ref, kb_ref, kc_ref, perm_ref,
                   q16_ref, k1_ref, k16_ref, v1_ref, v16_ref, ga_ref, u16_ref, gf_ref):
    x = x_ref[...]
    ms = jnp.mean(x * x, axis=-1, keepdims=True)
    h = (x * lax.rsqrt(ms + RMS_EPS) * gain_ref[...]).astype(BF16)
    perm = perm_ref[...]

    def column_tile(j):
        cols = slice(j * ATTN_WIDTH, (j + 1) * ATTN_WIDTH)
        return jnp.dot(h, w_ref[:, cols], preferred_element_type=F32)

    def norm_rope(acc, g_ref, a_ref, b_ref, c_ref):
        heads = []
        for hd in range(N_HEADS):
            a = acc[:, hd * HEAD_DIM:(hd + 1) * HEAD_DIM]
            ms_h = jnp.mean(a * a, axis=-1, keepdims=True)
            n = a * lax.rsqrt(ms_h + RMS_EPS) * g_ref[...]
            r = (n * a_ref[...]
                 + pltpu.roll(n, ROPE_HALF, 1) * b_ref[...]
                 + pltpu.roll(n, HEAD_DIM - ROPE_HALF, 1) * c_ref[...])
            heads.append(r.astype(BF16))
        return jnp.concatenate(heads, axis=1)

    def to_mod16(t):
        return jnp.dot(perm, t, preferred_element_type=F32)

    def store_heads(ref, t):
        for hd in range(N_HEADS):
            ref[hd] = t[:, hd * HEAD_DIM:(hd + 1) * HEAD_DIM]

    def store_head_pieces(ref, t):
        for hd in range(N_HEADS):
            for r in range(RADIX):
                ref[hd, r] = t[r * RADIX:(r + 1) * RADIX, hd * HEAD_DIM:(hd + 1) * HEAD_DIM]

    acc_q = column_tile(0)
    acc_k = column_tile(1)
    q = norm_rope(acc_q, qg_ref, qa_ref, qb_ref, qc_ref)
    acc_v = column_tile(2)
    store_head_pieces(q16_ref, to_mod16(q))

    k = norm_rope(acc_k, kg_ref, ka_ref, kb_ref, kc_ref)
    acc_ga = column_tile(3)
    store_heads(k1_ref, k)
    store_head_pieces(k16_ref, to_mod16(k).astype(BF16))

    v = acc_v.astype(BF16)
    acc_u = column_tile(4)
    store_heads(v1_ref, v)
    store_head_pieces(v16_ref, to_mod16(v).astype(BF16))

    ga_ref[...] = _silu(acc_ga).astype(BF16)
    acc_gf = column_tile(5)

    u16 = to_mod16(acc_u.astype(BF16)).astype(BF16)
    for r in range(RADIX):
        u16_ref[r] = u16[r * RADIX:(r + 1) * RADIX, :]

    gf_ref[...] = _silu(acc_gf).astype(BF16)


def _inproj(x2d, batch, seq, gain, w_in, qg, kg, perm):
    t = x2d.shape[0]
    groups = seq // ROW_GROUP
    sub = seq // RADIX
    qa, qb, qc = _rope_tables(seq, HEAD_DIM ** -0.5 * math.log2(math.e))
    ka, kb, kc = _rope_tables(seq, 1.0)
    rope_spec = pl.BlockSpec((ROW_GROUP, HEAD_DIM), lambda i: (i % groups, 0))
    tok_spec = pl.BlockSpec((ROW_GROUP, ATTN_WIDTH), lambda i: (i, 0))
    head_nat_spec = pl.BlockSpec((N_HEADS, ROW_GROUP, HEAD_DIM), lambda i: (0, i, 0))
    head_m16_spec = pl.BlockSpec((N_HEADS, None, RADIX, RADIX, HEAD_DIM),
                                 lambda i: (0, i // groups, 0, i % groups, 0))
    head_nat = jax.ShapeDtypeStruct((N_HEADS, t, HEAD_DIM), BF16)
    head_m16 = lambda dt: jax.ShapeDtypeStruct((N_HEADS, batch, RADIX, sub, HEAD_DIM), dt)
    return pl.pallas_call(
        _inproj_kernel,
        grid=(t // ROW_GROUP,),
        in_specs=[
            pl.BlockSpec((ROW_GROUP, D_MODEL), lambda i: (i, 0)),
            _resident((1, D_MODEL)),
            _resident(w_in.shape),
            _resident((1, HEAD_DIM)), _resident((1, HEAD_DIM)),
            rope_spec, rope_spec, rope_spec, rope_spec, rope_spec, rope_spec,
            _resident((ROW_GROUP, ROW_GROUP)),
        ],
        out_specs=[
            head_m16_spec, head_nat_spec, head_m16_spec, head_nat_spec, head_m16_spec,
            tok_spec,
            pl.BlockSpec((None, RADIX, RADIX, FOURIER_WIDTH), lambda i: (i // groups, 0, i % groups, 0)),
            tok_spec,
        ],
        out_shape=[
            head_m16(F32), head_nat, head_m16(BF16), head_nat, head_m16(BF16),
            jax.ShapeDtypeStruct((t, ATTN_WIDTH), BF16),
            jax.ShapeDtypeStruct((batch, RADIX, sub, FOURIER_WIDTH), BF16),
            jax.ShapeDtypeStruct((t, FOURIER_WIDTH), BF16),
        ],
        compiler_params=_params("arbitrary"),
        name="inproj",
    )(x2d, gain, w_in, qg, kg, qa, qb, qc, ka, kb, kc, perm)


def _window_start(first, half, total, window):
    return min(max(first - half, 0), total - window)


class _Pattern:
    def __init__(self, q_first, k_first, diff0, scale):
        deltas = [scale * (q0 - k0) for q0, k0 in zip(q_first, k_first)]
        uniq = sorted(set(deltas))
        self.q_first, self.k_first = q_first, k_first
        self.table_of = [uniq.index(d) for d in deltas]
        self.bias = jnp.asarray(np.stack(
            [np.where(np.abs(diff0 + d) <= HALF_KEYS, 0.0, MASK_VALUE) for d in uniq]).astype(np.float32))


def _attention_patterns(seq):
    sub = seq // RADIX
    qi = np.arange(Q_BLOCK)[:, None]
    win16 = min(Q_BLOCK + 2 * HALF_KEYS, sub)
    q16 = list(range(0, sub, Q_BLOCK))
    p16 = _Pattern(q16, [_window_start(q0, HALF_KEYS, sub, win16) for q0 in q16],
                   qi - np.arange(win16)[None, :], 1)
    q_rows, k_rows = Q_BLOCK // 4, Q_BLOCK // 4 + 2 * HALF_KEYS // 4
    kc = np.arange(4 * k_rows)[None, :]
    q4 = list(range(0, sub, q_rows))
    p4 = _Pattern(q4, [_window_start(q0, HALF_KEYS // 4, sub, k_rows) for q0 in q4],
                  4 * (qi % q_rows - kc % k_rows) + (qi // q_rows - kc // k_rows), 4)
    win1 = Q_BLOCK + 2 * HALF_KEYS
    q1 = list(range(0, seq, Q_BLOCK))
    p1 = _Pattern(q1, [_window_start(q0, HALF_KEYS, seq, win1) for q0 in q1],
                  RADIX * (qi % (Q_BLOCK // RADIX)) + qi // (Q_BLOCK // RADIX) - np.arange(win1)[None, :], 1)
    return p16, p4, p1


def _attn_kernel(q16_ref, k1_ref, v1_ref, k16_ref, v16_ref, b16_ref, b4_ref, b1_ref,
                 o_ref, acc_s, m_s, l_s, p0_s, w0_s, p1_s, w1_s, *, pat16, pat4, pat1):
    def gather(ref, pieces):
        return jnp.concatenate([ref[r, pl.ds(s0, n), :] for r, s0, n in pieces], axis=0)

    def scatter(ref, pieces, value):
        off = 0
        for r, s0, n in pieces:
            ref[r, pl.ds(s0, n), :] = value[off:off + n]
            off += n

    def scores_stage(blocks, first, p_scr, w_scr):
        for i, (q, k, _, bias, pieces) in enumerate(blocks):
            s = lax.dot_general(q(), k(), (((1,), (1,)), ((), ())), preferred_element_type=F32) + bias()
            m_new = jnp.broadcast_to(jnp.max(s, axis=-1, keepdims=True), (Q_BLOCK, HEAD_DIM))
            if not first:
                m_old = gather(m_s, pieces)
                m_new = jnp.maximum(m_old, m_new)
                w_scr[i] = jnp.exp2(m_old - m_new)
            width = s.shape[1]
            p = jnp.exp2(s - jnp.concatenate([m_new] * (width // HEAD_DIM), axis=1))
            p_scr[i, :, :width] = p.astype(BF16)
            scatter(m_s, pieces, m_new)

    def values_stage(blocks, first, p_scr, w_scr):
        for i, (_, _, v, _, pieces) in enumerate(blocks):
            vw = v()
            v_aug = jnp.concatenate([vw, jnp.ones(vw.shape, BF16)], axis=1)
            pv = jnp.dot(p_scr[i, :, :vw.shape[0]], v_aug, preferred_element_type=F32)
            acc, l = pv[:, :HEAD_DIM], pv[:, HEAD_DIM:]
            if not first:
                w_old = w_scr[i]
                acc = w_old * gather(acc_s, pieces) + acc
                l = w_old * gather(l_s, pieces) + l
            scatter(l_s, pieces, l)
            scatter(acc_s, pieces, acc)

    groups = []

    def add_groups(blocks, first):
        groups.extend((blocks[i:i + BLOCKS_PER_GROUP], first) for i in range(0, len(blocks), BLOCKS_PER_GROUP))

    win16 = b16_ref.shape[2]
    blocks = []
    for r in range(RADIX):
        for blk, (l0, ws) in enumerate(zip(pat16.q_first, pat16.k_first)):
            blocks.append((
                lambda r=r, l0=l0: q16_ref[r, l0:l0 + Q_BLOCK, :].astype(BF16),
                lambda r=r, ws=ws: k16_ref[r, ws:ws + win16, :],
                lambda r=r, ws=ws: v16_ref[r, ws:ws + win16, :],
                lambda blk=blk: b16_ref[pat16.table_of[blk]],
                [(r, l0, Q_BLOCK)]))
    add_groups(blocks, True)

    q_rows4 = Q_BLOCK // 4
    k_rows4 = b4_ref.shape[2] // 4
    blocks = []
    for r4 in range(4):
        res = [r4 + 4 * a for a in range(4)]
        for blk, (l0, ws) in enumerate(zip(pat4.q_first, pat4.k_first)):
            q_pieces = [(r, l0, q_rows4) for r in res]
            k_pieces = [(r, ws, k_rows4) for r in res]
            blocks.append((
                lambda p=q_pieces: gather(q16_ref, p).astype(BF16),
                lambda p=k_pieces: gather(k16_ref, p),
                lambda p=k_pieces: gather(v16_ref, p),
                lambda blk=blk: b4_ref[pat4.table_of[blk]],
                q_pieces))
    add_groups(blocks, False)

    q_rows1 = Q_BLOCK // RADIX
    win1 = b1_ref.shape[2]
    blocks = []
    for blk, (n0, ws) in enumerate(zip(pat1.q_first, pat1.k_first)):
        q_pieces = [(r, n0 // RADIX, q_rows1) for r in range(RADIX)]
        blocks.append((
            lambda p=q_pieces: gather(q16_ref, p).astype(BF16),
            lambda ws=ws: k1_ref[ws:ws + win1, :],
            lambda ws=ws: v1_ref[ws:ws + win1, :],
            lambda blk=blk: b1_ref[pat1.table_of[blk]],
            q_pieces))
    add_groups(blocks, False)

    slots = ((p0_s, w0_s), (p1_s, w1_s))
    scores_stage(*groups[0], *slots[0])
    for g in range(1, len(groups)):
        scores_stage(*groups[g], *slots[g % 2])
        values_stage(*groups[g - 1], *slots[(g - 1) % 2])
    values_stage(*groups[-1], *slots[(len(groups) - 1) % 2])

    def finish(r, carry):
        o_ref[r] = (acc_s[r] * (1.0 / l_s[r])).astype(BF16)
        return carry

    lax.fori_loop(0, RADIX, finish, 0)


def _attention(q16, k1, v1, k16, v16, batch, seq):
    sub = seq // RADIX
    pat16, pat4, pat1 = _attention_patterns(seq)
    m16_spec = pl.BlockSpec((None, None, RADIX, sub, HEAD_DIM), lambda b, h: (h, b, 0, 0, 0))
    nat_spec = pl.BlockSpec((None, seq, HEAD_DIM), lambda b, h: (h, b, 0))
    state = pltpu.VMEM((RADIX, sub, HEAD_DIM), F32)
    probs = pltpu.VMEM((BLOCKS_PER_GROUP, Q_BLOCK, Q_BLOCK + 2 * HALF_KEYS), BF16)
    rescale = pltpu.VMEM((BLOCKS_PER_GROUP, Q_BLOCK, HEAD_DIM), F32)
    return pl.pallas_call(
        functools.partial(_attn_kernel, pat16=pat16, pat4=pat4, pat1=pat1),
        grid=(batch, N_HEADS),
        in_specs=[m16_spec, nat_spec, nat_spec, m16_spec, m16_spec,
                  _resident(pat16.bias.shape), _resident(pat4.bias.shape), _resident(pat1.bias.shape)],
        out_specs=m16_spec,
        out_shape=jax.ShapeDtypeStruct((N_HEADS, batch, RADIX, sub, HEAD_DIM), BF16),
        scratch_shapes=[state, state, state, probs, rescale, probs, rescale],
        compiler_params=_params("parallel", "parallel"),
        name="attention",
    )(q16, k1, v1, k16, v16, pat16.bias, pat4.bias, pat1.bias)


def _chanmat_kernel(c_ref, s_ref, w_ref, m_ref):
    for g in range(N_GROUPS):
        w = w_ref[g]
        mc = jnp.dot(c_ref[...], w, preferred_element_type=F32, precision=lax.Precision.HIGHEST)
        ms = jnp.dot(s_ref[...], w, preferred_element_type=F32, precision=lax.Precision.HIGHEST)
        m_ref[g, :, :GROUP_DIM] = mc.astype(BF16)
        m_ref[g, :, GROUP_DIM:] = ms.astype(BF16)


def _channel_matrices(w_fourier):
    idx = np.arange(GROUP_DIM)
    cc, sc = _dft_cos_sin(GROUP_DIM, idx, idx, GROUP_DIM ** -0.5)
    return pl.pallas_call(
        _chanmat_kernel,
        out_shape=jax.ShapeDtypeStruct((N_GROUPS, GROUP_DIM, 2 * GROUP_DIM), BF16),
        name="chanmat",
    )(jnp.asarray(cc, F32), jnp.asarray(sc, F32), w_fourier)


def _fourier_kernel(u_ref, m_ref, c_ref, smc_ref, cps_ref, mc_ref, ms_ref, gf_ref, o_ref, yr_s, yi_s,
                    *, stage1_steps, tiles):
    t = pl.program_id(1)
    dot = functools.partial(jnp.dot, preferred_element_type=F32)
    per_step = u_ref.shape[0]

    @pl.when(t < stage1_steps)
    def _():
        z = []
        for j in range(per_step):
            a_parts, b_parts, ab_parts = [], [], []
            for g in range(N_GROUPS):
                ab = dot(u_ref[j, :, g * GROUP_DIM:(g + 1) * GROUP_DIM], m_ref[g])
                a, b = ab[:, :GROUP_DIM], ab[:, GROUP_DIM:]
                a_parts.append(a.astype(BF16))
                b_parts.append(b.astype(BF16))
                ab_parts.append((a + b).astype(BF16))
            z.append(tuple(jnp.concatenate(p, axis=1) for p in (a_parts, b_parts, ab_parts)))
        c, s_minus_c, c_plus_s = c_ref[...], smc_ref[...], cps_ref[...]
        for j, (a, b, a_plus_b) in enumerate(z):
            n2 = t * per_step + j
            k1 = dot(c, a_plus_b)
            yr_s[n2] = (k1 - dot(c_plus_s, b)).astype(BF16)
            yi_s[n2] = (k1 + dot(s_minus_c, a)).astype(BF16)

    @pl.when(t >= stage1_steps)
    def _():
        first_row = (t - stage1_steps) * (tiles * RADIX)
        for i in range(tiles):
            rows = pl.ds(pl.multiple_of(first_row + i * RADIX, RADIX), RADIX)
            yr = jnp.concatenate([yr_s[n2, rows, :] for n2 in range(RADIX)], axis=0)
            yi = jnp.concatenate([yi_s[n2, rows, :] for n2 in range(RADIX)], axis=0)
            x = dot(mc_ref[i], yr) + dot(ms_ref[i], yi)
            out_rows = slice(i * RADIX, (i + 1) * RADIX)
            for k2 in range(RADIX):
                piece = x[k2 * RADIX:(k2 + 1) * RADIX, :] * gf_ref[k2, out_rows, :].astype(F32)
                o_ref[k2, out_rows, :] = piece.astype(BF16)


def _stage2_matrices(seq):
    s1 = seq // RADIX
    tiles = s1 // RADIX
    c, k2, k1, n2 = np.meshgrid(np.arange(tiles), np.arange(RADIX), np.arange(RADIX), np.arange(RADIX),
                                indexing="ij")
    k = RADIX * c + k1 + s1 * k2
    ang = 2.0 * np.pi * ((n2 * k) % seq).astype(np.float64) / seq
    mc = np.zeros((tiles, ROW_GROUP, ROW_GROUP), np.float32)
    ms = np.zeros((tiles, ROW_GROUP, ROW_GROUP), np.float32)
    mc[c, RADIX * k2 + k1, RADIX * n2 + k1] = np.cos(ang) * seq ** -0.5
    ms[c, RADIX * k2 + k1, RADIX * n2 + k1] = -np.sin(ang) * seq ** -0.5
    return _bf16_const(mc), _bf16_const(ms)


def _fourier(u16, chan_mats, gf, batch, seq, tiles=4):
    s1 = seq // RADIX
    idx = np.arange(s1)
    c1, sn1 = _dft_cos_sin(s1, idx, idx)
    mc, ms = _stage2_matrices(seq)
    per_step = FOURIER1_ROWS // s1
    stage1_steps = RADIX // per_step
    rows = tiles * RADIX
    stage2 = lambda t: jnp.maximum(t - stage1_steps, 0)
    nat_spec = pl.BlockSpec((None, RADIX, rows, FOURIER_WIDTH), lambda b, t: (b, 0, stage2(t), 0))
    m_spec = pl.BlockSpec((tiles, ROW_GROUP, ROW_GROUP), lambda b, t: (stage2(t), 0, 0))
    y_scratch = pltpu.VMEM((RADIX, s1, FOURIER_WIDTH), BF16)
    out = pl.pallas_call(
        functools.partial(_fourier_kernel, stage1_steps=stage1_steps, tiles=tiles),
        grid=(batch, stage1_steps + s1 // rows),
        in_specs=[pl.BlockSpec((None, per_step, s1, FOURIER_WIDTH),
                               lambda b, t: (b, jnp.minimum(t, stage1_steps - 1), 0, 0)),
                  _resident(chan_mats.shape), _resident((s1, s1)), _resident((s1, s1)), _resident((s1, s1)),
                  m_spec, m_spec, nat_spec],
        out_specs=nat_spec,
        out_shape=jax.ShapeDtypeStruct((batch, RADIX, s1, FOURIER_WIDTH), BF16),
        scratch_shapes=[y_scratch, y_scratch],
        compiler_params=_params("parallel", "arbitrary"),
        name="fourier",
    )(u16, chan_mats, _bf16_const(c1), _bf16_const(sn1 - c1), _bf16_const(c1 + sn1), mc, ms,
      gf.reshape(batch, RADIX, s1, FOURIER_WIDTH))
    return out.reshape(batch * seq, FOURIER_WIDTH)


def _outproj_kernel(x_ref, o16_ref, ga_ref, mf_ref, unperm_ref, w_ref, y_ref):
    heads = [jnp.concatenate([o16_ref[hd, r] for r in range(RADIX)], axis=0) for hd in range(N_HEADS)]
    attn16 = jnp.concatenate(heads, axis=1)
    attn = jnp.dot(unperm_ref[...], attn16, preferred_element_type=F32)
    y = x_ref[...] + jnp.dot(mf_ref[...], w_ref[ATTN_WIDTH:, :], preferred_element_type=F32)
    mix_a = (attn * ga_ref[...].astype(F32)).astype(BF16)
    y_ref[...] = y + jnp.dot(mix_a, w_ref[:ATTN_WIDTH, :], preferred_element_type=F32)


def _outproj(x2d, o16, ga, mix_f, unperm, w_out, seq):
    t = x2d.shape[0]
    groups = seq // ROW_GROUP
    tok_spec = pl.BlockSpec((ROW_GROUP, ATTN_WIDTH), lambda i: (i, 0))
    x_spec = pl.BlockSpec((ROW_GROUP, D_MODEL), lambda i: (i, 0))
    return pl.pallas_call(
        _outproj_kernel,
        grid=(t // ROW_GROUP,),
        in_specs=[x_spec,
                  pl.BlockSpec((N_HEADS, None, RADIX, RADIX, HEAD_DIM),
                               lambda i: (0, i // groups, 0, i % groups, 0)),
                  tok_spec, tok_spec,
                  _resident((ROW_GROUP, ROW_GROUP)),
                  _resident((D_MODEL, D_MODEL))],
        out_specs=x_spec,
        out_shape=jax.ShapeDtypeStruct((t, D_MODEL), F32),
        compiler_params=_params("arbitrary"),
        name="outproj",
    )(x2d, o16, ga, mix_f, unperm, w_out)


def _layer(x, gain, w_in, qg, kg, chan_mats, w_out, perm, unperm):
    batch, seq, _ = x.shape
    x2d = x.reshape(batch * seq, D_MODEL)
    q16, k1, k16, v1, v16, ga, u16, gf = _inproj(x2d, batch, seq, gain, w_in, qg, kg, perm)
    o16 = _attention(q16, k1, v1, k16, v16, batch, seq)
    mix_f = _fourier(u16, chan_mats, gf, batch, seq)
    y = _outproj(x2d, o16, ga, mix_f, unperm, w_out, seq)
    return y.reshape(batch, seq, D_MODEL)


def kernel(x_prompt, x_sample, rms_gain, w_in, q_norm_gain, k_norm_gain, w_fourier, w_out):
    depth = rms_gain.shape[0]
    p = _group_permutation()
    perm, unperm = _bf16_const(p), _bf16_const(p.T)
    for l in range(depth):
        gain = rms_gain[l].reshape(1, D_MODEL)
        w_in_l = w_in[l].astype(BF16)
        w_out_l = w_out[l].astype(BF16)
        qg = q_norm_gain[l].reshape(1, HEAD_DIM)
        kg = k_norm_gain[l].reshape(1, HEAD_DIM)
        chan_mats = _channel_matrices(w_fourier[l])
        x_prompt = _layer(x_prompt, gain, w_in_l, qg, kg, chan_mats, w_out_l, perm, unperm)
        x_sample = _layer(x_sample, gain, w_in_l, qg, kg, chan_mats, w_out_l, perm, unperm)
    return (x_prompt, x_sample)
```

```python
import functools
import math

import jax
import jax.numpy as jnp
import numpy as np
from jax import lax
from jax.experimental import pallas as pl
from jax.experimental.pallas import tpu as pltpu

D_MODEL = 2048
ATTN_WIDTH = 1024
FOURIER_WIDTH = 1024
HEAD_DIM = 128
N_HEADS = ATTN_WIDTH // HEAD_DIM
N_GROUPS = 4
GROUP_DIM = FOURIER_WIDTH // N_GROUPS
ROPE_THETA = 500000.0
ROPE_DIM = HEAD_DIM // 4
ROPE_HALF = ROPE_DIM // 2
HALF_KEYS = 64
RMS_EPS = 1e-6
RADIX = 16
ROW_GROUP = RADIX * RADIX
Q_BLOCK = 128
FOURIER1_ROWS = 512
OUTPROJ_GROUPS = 2
BLOCKS_PER_GROUP = 4
MASK_VALUE = -1e30

VMEM_LIMIT_BYTES = 56 * 1024 * 1024

F32 = jnp.float32
BF16 = jnp.bfloat16


def _params(*semantics):
    return pltpu.CompilerParams(dimension_semantics=semantics, vmem_limit_bytes=VMEM_LIMIT_BYTES)


def _resident(shape):
    return pl.BlockSpec(shape, lambda *_: (0,) * len(shape), pipeline_mode=pl.Buffered(1))


def _rope_tables(seq, scale):
    expo = np.arange(ROPE_HALF, dtype=np.float32) / np.float32(ROPE_HALF)
    inv_freq = (np.float32(1.0) / np.power(np.float32(ROPE_THETA), expo)).astype(np.float32)
    ang = (np.arange(seq, dtype=np.float32)[:, None] * inv_freq[None, :]).astype(np.float64)
    cos, sin = np.cos(ang), np.sin(ang)
    a = np.ones((seq, HEAD_DIM))
    b = np.zeros((seq, HEAD_DIM))
    c = np.zeros((seq, HEAD_DIM))
    a[:, :ROPE_HALF] = cos
    a[:, ROPE_HALF:ROPE_DIM] = cos
    b[:, ROPE_HALF:ROPE_DIM] = sin
    c[:, :ROPE_HALF] = -sin
    return tuple(jnp.asarray((t * scale).astype(np.float32)) for t in (a, b, c))


def _dft_cos_sin(n, rows, cols, scale=1.0):
    m = (np.asarray(rows, dtype=np.int64)[:, None] * np.asarray(cols, dtype=np.int64)[None, :]) % n
    ang = 2.0 * np.pi * m.astype(np.float64) / n
    return np.cos(ang) * scale, np.sin(ang) * scale


def _group_permutation():
    p = np.zeros((ROW_GROUP, ROW_GROUP), np.float32)
    j, r = np.meshgrid(np.arange(RADIX), np.arange(RADIX), indexing="ij")
    p[(RADIX * r + j).ravel(), (RADIX * j + r).ravel()] = 1.0
    return p


def _bf16_const(a):
    return jnp.asarray(np.asarray(a, np.float32)).astype(BF16)


def _silu(a):
    return a * (1.0 / (1.0 + jnp.exp(-a)))


def _inproj_kernel(x_ref, gain_ref, w_ref, qg_ref, kg_ref,
                   qa_ref, qb_ref, qc_ref, ka_ref, kb_ref, kc_ref, perm_ref,
                   q16_ref, k1_ref, k16_ref, v1_ref, v16_ref, ga_ref, u16_ref, gf_ref):
    x = x_ref[...]
    ms = jnp.mean(x * x, axis=-1, keepdims=True)
    h = (x * lax.rsqrt(ms + RMS_EPS) * gain_ref[...]).astype(BF16)
    perm = perm_ref[...]

    def column_tile(j):
        cols = slice(j * ATTN_WIDTH, (j + 1) * ATTN_WIDTH)
        return jnp.dot(h, w_ref[:, cols], preferred_element_type=F32)

    def norm_rope(acc, g_ref, a_ref, b_ref, c_ref):
        heads = []
        for hd in range(N_HEADS):
            a = acc[:, hd * HEAD_DIM:(hd + 1) * HEAD_DIM]
            ms_h = jnp.mean(a * a, axis=-1, keepdims=True)
            n = a * lax.rsqrt(ms_h + RMS_EPS) * g_ref[...]
            r = (n * a_ref[...]
                 + pltpu.roll(n, ROPE_HALF, 1) * b_ref[...]
                 + pltpu.roll(n, HEAD_DIM - ROPE_HALF, 1) * c_ref[...])
            heads.append(r.astype(BF16))
        return jnp.concatenate(heads, axis=1)

    def to_mod16(t):
        return jnp.dot(perm, t, preferred_element_type=F32)

    def store_heads(ref, t):
        for hd in range(N_HEADS):
            ref[hd] = t[:, hd * HEAD_DIM:(hd + 1) * HEAD_DIM]

    def store_head_pieces(ref, t):
        for hd in range(N_HEADS):
            for r in range(RADIX):
                ref[hd, r] = t[r * RADIX:(r + 1) * RADIX, hd * HEAD_DIM:(hd + 1) * HEAD_DIM]

    acc_q = column_tile(0)
    acc_k = column_tile(1)
    q = norm_rope(acc_q, qg_ref, qa_ref, qb_ref, qc_ref)
    acc_v = column_tile(2)
    store_head_pieces(q16_ref, to_mod16(q))

    k = norm_rope(acc_k, kg_ref, ka_ref, kb_ref, kc_ref)
    acc_ga = column_tile(3)
    store_heads(k1_ref, k)
    store_head_pieces(k16_ref, to_mod16(k).astype(BF16))

    v = acc_v.astype(BF16)
    acc_u = column_tile(4)
    store_heads(v1_ref, v)
    store_head_pieces(v16_ref, to_mod16(v).astype(BF16))

    ga_ref[...] = _silu(acc_ga).astype(BF16)
    acc_gf = column_tile(5)

    u16 = to_mod16(acc_u.astype(BF16)).astype(BF16)
    for r in range(RADIX):
        u16_ref[r] = u16[r * RADIX:(r + 1) * RADIX, :]

    gf_ref[...] = _silu(acc_gf).astype(BF16)


def _inproj(x2d, batch, seq, gain, w_in, qg, kg, perm):
    t = x2d.shape[0]
    groups = seq // ROW_GROUP
    sub = seq // RADIX
    qa, qb, qc = _rope_tables(seq, HEAD_DIM ** -0.5 * math.log2(math.e))
    ka, kb, kc = _rope_tables(seq, 1.0)
    rope_spec = pl.BlockSpec((ROW_GROUP, HEAD_DIM), lambda i: (i % groups, 0))
    tok_spec = pl.BlockSpec((ROW_GROUP, ATTN_WIDTH), lambda i: (i, 0))
    head_nat_spec = pl.BlockSpec((N_HEADS, ROW_GROUP, HEAD_DIM), lambda i: (0, i, 0))
    head_m16_spec = pl.BlockSpec((N_HEADS, None, RADIX, RADIX, HEAD_DIM),
                                 lambda i: (0, i // groups, 0, i % groups, 0))
    head_nat = jax.ShapeDtypeStruct((N_HEADS, t, HEAD_DIM), BF16)
    head_m16 = lambda dt: jax.ShapeDtypeStruct((N_HEADS, batch, RADIX, sub, HEAD_DIM), dt)
    return pl.pallas_call(
        _inproj_kernel,
        grid=(t // ROW_GROUP,),
        in_specs=[
            pl.BlockSpec((ROW_GROUP, D_MODEL), lambda i: (i, 0)),
            _resident((1, D_MODEL)),
            _resident(w_in.shape),
            _resident((1, HEAD_DIM)), _resident((1, HEAD_DIM)),
            rope_spec, rope_spec, rope_spec, rope_spec, rope_spec, rope_spec,
            _resident((ROW_GROUP, ROW_GROUP)),
        ],
        out_specs=[
            head_m16_spec, head_nat_spec, head_m16_spec, head_nat_spec, head_m16_spec,
            tok_spec,
            pl.BlockSpec((None, RADIX, RADIX, FOURIER_WIDTH), lambda i: (i // groups, 0, i % groups, 0)),
            tok_spec,
        ],
        out_shape=[
            head_m16(F32), head_nat, head_m16(BF16), head_nat, head_m16(BF16),
            jax.ShapeDtypeStruct((t, ATTN_WIDTH), BF16),
            jax.ShapeDtypeStruct((batch, RADIX, sub, FOURIER_WIDTH), BF16),
            jax.ShapeDtypeStruct((t, FOURIER_WIDTH), BF16),
        ],
        compiler_params=_params("arbitrary"),
        name="inproj",
    )(x2d, gain, w_in, qg, kg, qa, qb, qc, ka, kb, kc, perm)


def _window_start(first, half, total, window):
    return min(max(first - half, 0), total - window)


class _Pattern:
    def __init__(self, q_first, k_first, diff0, scale):
        deltas = [scale * (q0 - k0) for q0, k0 in zip(q_first, k_first)]
        uniq = sorted(set(deltas))
        self.q_first, self.k_first = q_first, k_first
        self.table_of = [uniq.index(d) for d in deltas]
        self.bias = jnp.asarray(np.stack(
            [np.where(np.abs(diff0 + d) <= HALF_KEYS, 0.0, MASK_VALUE) for d in uniq]).astype(np.float32))


def _attention_patterns(seq):
    sub = seq // RADIX
    qi = np.arange(Q_BLOCK)[:, None]
    win16 = min(Q_BLOCK + 2 * HALF_KEYS, sub)
    q16 = list(range(0, sub, Q_BLOCK))
    p16 = _Pattern(q16, [_window_start(q0, HALF_KEYS, sub, win16) for q0 in q16],
                   qi - np.arange(win16)[None, :], 1)
    q_rows, k_rows = Q_BLOCK // 4, Q_BLOCK // 4 + 2 * HALF_KEYS // 4
    kc = np.arange(4 * k_rows)[None, :]
    q4 = list(range(0, sub, q_rows))
    p4 = _Pattern(q4, [_window_start(q0, HALF_KEYS // 4, sub, k_rows) for q0 in q4],
                  4 * (qi % q_rows - kc % k_rows) + (qi // q_rows - kc // k_rows), 4)
    win1 = Q_BLOCK + 2 * HALF_KEYS
    q1 = list(range(0, seq, Q_BLOCK))
    p1 = _Pattern(q1, [_window_start(q0, HALF_KEYS, seq, win1) for q0 in q1],
                  RADIX * (qi % (Q_BLOCK // RADIX)) + qi // (Q_BLOCK // RADIX) - np.arange(win1)[None, :], 1)
    return p16, p4, p1


def _attn_kernel(q16_ref, k1_ref, v1_ref, k16_ref, v16_ref, b16_ref, b4_ref, b1_ref,
                 o_ref, acc_s, m_s, l_s, p0_s, w0_s, p1_s, w1_s, *, pat16, pat4, pat1):
    def gather(ref, pieces):
        return jnp.concatenate([ref[r, pl.ds(s0, n), :] for r, s0, n in pieces], axis=0)

    def scatter(ref, pieces, value):
        off = 0
        for r, s0, n in pieces:
            ref[r, pl.ds(s0, n), :] = value[off:off + n]
            off += n

    def scores_stage(blocks, first, p_scr, w_scr):
        for i, (q, k, _, bias, pieces) in enumerate(blocks):
            s = lax.dot_general(q(), k(), (((1,), (1,)), ((), ())), preferred_element_type=F32) + bias()
            m_new = jnp.broadcast_to(jnp.max(s, axis=-1, keepdims=True), (Q_BLOCK, HEAD_DIM))
            if not first:
                m_old = gather(m_s, pieces)
                m_new = jnp.maximum(m_old, m_new)
                w_scr[i] = jnp.exp2(m_old - m_new)
            width = s.shape[1]
            p = jnp.exp2(s - jnp.concatenate([m_new] * (width // HEAD_DIM), axis=1))
            p_scr[i, :, :width] = p.astype(BF16)
            scatter(m_s, pieces, m_new)

    def values_stage(blocks, first, p_scr, w_scr):
        for i, (_, _, v, _, pieces) in enumerate(blocks):
            vw = v()
            v_aug = jnp.concatenate([vw, jnp.ones(vw.shape, BF16)], axis=1)
            pv = jnp.dot(p_scr[i, :, :vw.shape[0]], v_aug, preferred_element_type=F32)
            acc, l = pv[:, :HEAD_DIM], pv[:, HEAD_DIM:]
            if not first:
                w_old = w_scr[i]
                acc = w_old * gather(acc_s, pieces) + acc
                l = w_old * gather(l_s, pieces) + l
            scatter(l_s, pieces, l)
            scatter(acc_s, pieces, acc)

    groups = []

    def add_groups(blocks, first):
        groups.extend((blocks[i:i + BLOCKS_PER_GROUP], first) for i in range(0, len(blocks), BLOCKS_PER_GROUP))

    win16 = b16_ref.shape[2]
    blocks = []
    for r in range(RADIX):
        for blk, (l0, ws) in enumerate(zip(pat16.q_first, pat16.k_first)):
            blocks.append((
                lambda r=r, l0=l0: q16_ref[r, l0:l0 + Q_BLOCK, :].astype(BF16),
                lambda r=r, ws=ws: k16_ref[r, ws:ws + win16, :],
                lambda r=r, ws=ws: v16_ref[r, ws:ws + win16, :],
                lambda blk=blk: b16_ref[pat16.table_of[blk]],
                [(r, l0, Q_BLOCK)]))
    add_groups(blocks, True)

    q_rows4 = Q_BLOCK // 4
    k_rows4 = b4_ref.shape[2] // 4
    blocks = []
    for r4 in range(4):
        res = [r4 + 4 * a for a in range(4)]
        for blk, (l0, ws) in enumerate(zip(pat4.q_first, pat4.k_first)):
            q_pieces = [(r, l0, q_rows4) for r in res]
            k_pieces = [(r, ws, k_rows4) for r in res]
            blocks.append((
                lambda p=q_pieces: gather(q16_ref, p).astype(BF16),
                lambda p=k_pieces: gather(k16_ref, p),
                lambda p=k_pieces: gather(v16_ref, p),
                lambda blk=blk: b4_ref[pat4.table_of[blk]],
                q_pieces))
    add_groups(blocks, False)

    q_rows1 = Q_BLOCK // RADIX
    win1 = b1_ref.shape[2]
    blocks = []
    for blk, (n0, ws) in enumerate(zip(pat1.q_first, pat1.k_first)):
        q_pieces = [(r, n0 // RADIX, q_rows1) for r in range(RADIX)]
        blocks.append((
            lambda p=q_pieces: gather(q16_ref, p).astype(BF16),
            lambda ws=ws: k1_ref[ws:ws + win1, :],
            lambda ws=ws: v1_ref[ws:ws + win1, :],
            lambda blk=blk: b1_ref[pat1.table_of[blk]],
            q_pieces))
    add_groups(blocks, False)

    slots = ((p0_s, w0_s), (p1_s, w1_s))
    scores_stage(*groups[0], *slots[0])
    for g in range(1, len(groups)):
        scores_stage(*groups[g], *slots[g % 2])
        values_stage(*groups[g - 1], *slots[(g - 1) % 2])
    values_stage(*groups[-1], *slots[(len(groups) - 1) % 2])

    def finish(r, carry):
        o_ref[r] = (acc_s[r] * (1.0 / l_s[r])).astype(BF16)
        return carry

    lax.fori_loop(0, RADIX, finish, 0)


def _attention(q16, k1, v1, k16, v16, batch, seq):
    sub = seq // RADIX
    pat16, pat4, pat1 = _attention_patterns(seq)
    m16_spec = pl.BlockSpec((None, None, RADIX, sub, HEAD_DIM), lambda b, h: (h, b, 0, 0, 0))
    nat_spec = pl.BlockSpec((None, seq, HEAD_DIM), lambda b, h: (h, b, 0))
    state = pltpu.VMEM((RADIX, sub, HEAD_DIM), F32)
    probs = pltpu.VMEM((BLOCKS_PER_GROUP, Q_BLOCK, Q_BLOCK + 2 * HALF_KEYS), BF16)
    rescale = pltpu.VMEM((BLOCKS_PER_GROUP, Q_BLOCK, HEAD_DIM), F32)
    return pl.pallas_call(
        functools.partial(_attn_kernel, pat16=pat16, pat4=pat4, pat1=pat1),
        grid=(batch, N_HEADS),
        in_specs=[m16_spec, nat_spec, nat_spec, m16_spec, m16_spec,
                  _resident(pat16.bias.shape), _resident(pat4.bias.shape), _resident(pat1.bias.shape)],
        out_specs=m16_spec,
        out_shape=jax.ShapeDtypeStruct((N_HEADS, batch, RADIX, sub, HEAD_DIM), BF16),
        scratch_shapes=[state, state, state, probs, rescale, probs, rescale],
        compiler_params=_params("parallel", "parallel"),
        name="attention",
    )(q16, k1, v1, k16, v16, pat16.bias, pat4.bias, pat1.bias)


def _chanmat_kernel(c_ref, s_ref, w_ref, m_ref):
    for g in range(N_GROUPS):
        w = w_ref[g]
        mc = jnp.dot(c_ref[...], w, preferred_element_type=F32, precision=lax.Precision.HIGHEST)
        ms = jnp.dot(s_ref[...], w, preferred_element_type=F32, precision=lax.Precision.HIGHEST)
        m_ref[g, :, :GROUP_DIM] = mc.astype(BF16)
        m_ref[g, :, GROUP_DIM:] = ms.astype(BF16)


def _channel_matrices(w_fourier):
    idx = np.arange(GROUP_DIM)
    cc, sc = _dft_cos_sin(GROUP_DIM, idx, idx, GROUP_DIM ** -0.5)
    return pl.pallas_call(
        _chanmat_kernel,
        out_shape=jax.ShapeDtypeStruct((N_GROUPS, GROUP_DIM, 2 * GROUP_DIM), BF16),
        name="chanmat",
    )(jnp.asarray(cc, F32), jnp.asarray(sc, F32), w_fourier)


def _fourier_kernel(u_ref, m_ref, c_ref, smc_ref, cps_ref, mc_ref, ms_ref, gf_ref, o_ref, yr_s, yi_s,
                    *, stage1_steps, tiles):
    t = pl.program_id(1)
    dot = functools.partial(jnp.dot, preferred_element_type=F32)
    per_step = u_ref.shape[0]

    @pl.when(t < stage1_steps)
    def _():
        z = []
        for j in range(per_step):
            a_parts, b_parts, ab_parts = [], [], []
            for g in range(N_GROUPS):
                ab = dot(u_ref[j, :, g * GROUP_DIM:(g + 1) * GROUP_DIM], m_ref[g])
                a, b = ab[:, :GROUP_DIM], ab[:, GROUP_DIM:]
                a_parts.append(a.astype(BF16))
                b_parts.append(b.astype(BF16))
                ab_parts.append((a + b).astype(BF16))
            z.append(tuple(jnp.concatenate(p, axis=1) for p in (a_parts, b_parts, ab_parts)))
        c, s_minus_c, c_plus_s = c_ref[...], smc_ref[...], cps_ref[...]
        for j, (a, b, a_plus_b) in enumerate(z):
            n2 = t * per_step + j
            k1 = dot(c, a_plus_b)
            yr_s[n2] = (k1 - dot(c_plus_s, b)).astype(BF16)
            yi_s[n2] = (k1 + dot(s_minus_c, a)).astype(BF16)

    @pl.when(t >= stage1_steps)
    def _():
        first_row = (t - stage1_steps) * (tiles * RADIX)
        for i in range(tiles):
            rows = pl.ds(pl.multiple_of(first_row + i * RADIX, RADIX), RADIX)
            yr = jnp.concatenate([yr_s[n2, rows, :] for n2 in range(RADIX)], axis=0)
            yi = jnp.concatenate([yi_s[n2, rows, :] for n2 in range(RADIX)], axis=0)
            x = dot(mc_ref[i], yr) + dot(ms_ref[i], yi)
            out_rows = slice(i * RADIX, (i + 1) * RADIX)
            for k2 in range(RADIX):
                piece = x[k2 * RADIX:(k2 + 1) * RADIX, :] * gf_ref[k2, out_rows, :].astype(F32)
                o_ref[k2, out_rows, :] = piece.astype(BF16)


def _stage2_matrices(seq):
    s1 = seq // RADIX
    tiles = s1 // RADIX
    c, k2, k1, n2 = np.meshgrid(np.arange(tiles), np.arange(RADIX), np.arange(RADIX), np.arange(RADIX),
                                indexing="ij")
    k = RADIX * c + k1 + s1 * k2
    ang = 2.0 * np.pi * ((n2 * k) % seq).astype(np.float64) / seq
    mc = np.zeros((tiles, ROW_GROUP, ROW_GROUP), np.float32)
    ms = np.zeros((tiles, ROW_GROUP, ROW_GROUP), np.float32)
    mc[c, RADIX * k2 + k1, RADIX * n2 + k1] = np.cos(ang) * seq ** -0.5
    ms[c, RADIX * k2 + k1, RADIX * n2 + k1] = -np.sin(ang) * seq ** -0.5
    return _bf16_const(mc), _bf16_const(ms)


def _fourier(u16, chan_mats, gf, batch, seq, tiles=4):
    s1 = seq // RADIX
    idx = np.arange(s1)
    c1, sn1 = _dft_cos_sin(s1, idx, idx)
    mc, ms = _stage2_matrices(seq)
    per_step = FOURIER1_ROWS // s1
    stage1_steps = RADIX // per_step
    rows = tiles * RADIX
    stage2 = lambda t: jnp.maximum(t - stage1_steps, 0)
    nat_spec = pl.BlockSpec((None, RADIX, rows, FOURIER_WIDTH), lambda b, t: (b, 0, stage2(t), 0))
    m_spec = pl.BlockSpec((tiles, ROW_GROUP, ROW_GROUP), lambda b, t: (stage2(t), 0, 0))
    y_scratch = pltpu.VMEM((RADIX, s1, FOURIER_WIDTH), BF16)
    out = pl.pallas_call(
        functools.partial(_fourier_kernel, stage1_steps=stage1_steps, tiles=tiles),
        grid=(batch, stage1_steps + s1 // rows),
        in_specs=[pl.BlockSpec((None, per_step, s1, FOURIER_WIDTH),
                               lambda b, t: (b, jnp.minimum(t, stage1_steps - 1), 0, 0)),
                  _resident(chan_mats.shape), _resident((s1, s1)), _resident((s1, s1)), _resident((s1, s1)),
                  m_spec, m_spec, nat_spec],
        out_specs=nat_spec,
        out_shape=jax.ShapeDtypeStruct((batch, RADIX, s1, FOURIER_WIDTH), BF16),
        scratch_shapes=[y_scratch, y_scratch],
        compiler_params=_params("parallel", "arbitrary"),
        name="fourier",
    )(u16, chan_mats, _bf16_const(c1), _bf16_const(sn1 - c1), _bf16_const(c1 + sn1), mc, ms,
      gf.reshape(batch, RADIX, s1, FOURIER_WIDTH))
    return out.reshape(batch * seq, FOURIER_WIDTH)


def _outproj_kernel(x_ref, o16_ref, ga_ref, mf_ref, unperm_ref, w_ref, y_ref):
    unperm = unperm_ref[...]
    attn = []
    for g in range(OUTPROJ_GROUPS):
        rows = slice(g * RADIX, (g + 1) * RADIX)
        heads = [jnp.concatenate([o16_ref[hd, r, rows, :] for r in range(RADIX)], axis=0)
                 for hd in range(N_HEADS)]
        attn16 = jnp.concatenate(heads, axis=1)
        attn.append(jnp.dot(unperm, attn16, preferred_element_type=F32))
    y = x_ref[...] + jnp.dot(mf_ref[...], w_ref[ATTN_WIDTH:, :], preferred_element_type=F32)
    mix_a = (jnp.concatenate(attn, axis=0) * ga_ref[...].astype(F32)).astype(BF16)
    y_ref[...] = y + jnp.dot(mix_a, w_ref[:ATTN_WIDTH, :], preferred_element_type=F32)


def _outproj(x2d, o16, ga, mix_f, unperm, w_out, seq):
    t = x2d.shape[0]
    rows = OUTPROJ_GROUPS * ROW_GROUP
    steps_per_seq = seq // rows
    tok_spec = pl.BlockSpec((rows, ATTN_WIDTH), lambda i: (i, 0))
    x_spec = pl.BlockSpec((rows, D_MODEL), lambda i: (i, 0))
    return pl.pallas_call(
        _outproj_kernel,
        grid=(t // rows,),
        in_specs=[x_spec,
                  pl.BlockSpec((N_HEADS, None, RADIX, OUTPROJ_GROUPS * RADIX, HEAD_DIM),
                               lambda i: (0, i // steps_per_seq, 0, i % steps_per_seq, 0)),
                  tok_spec, tok_spec,
                  _resident((ROW_GROUP, ROW_GROUP)),
                  _resident((D_MODEL, D_MODEL))],
        out_specs=x_spec,
        out_shape=jax.ShapeDtypeStruct((t, D_MODEL), F32),
        compiler_params=_params("arbitrary"),
        name="outproj",
    )(x2d, o16, ga, mix_f, unperm, w_out)


def _layer(x, gain, w_in, qg, kg, chan_mats, w_out, perm, unperm):
    batch, seq, _ = x.shape
    x2d = x.reshape(batch * seq, D_MODEL)
    q16, k1, k16, v1, v16, ga, u16, gf = _inproj(x2d, batch, seq, gain, w_in, qg, kg, perm)
    o16 = _attention(q16, k1, v1, k16, v16, batch, seq)
    mix_f = _fourier(u16, chan_mats, gf, batch, seq)
    y = _outproj(x2d, o16, ga, mix_f, unperm, w_out, seq)
    return y.reshape(batch, seq, D_MODEL)


def kernel(x_prompt, x_sample, rms_gain, w_in, q_norm_gain, k_norm_gain, w_fourier, w_out):
    depth = rms_gain.shape[0]
    p = _group_permutation()
    perm, unperm = _bf16_const(p), _bf16_const(p.T)
    for l in range(depth):
        gain = rms_gain[l].reshape(1, D_MODEL)
        w_in_l = w_in[l].astype(BF16)
        w_out_l = w_out[l].astype(BF16)
        qg = q_norm_gain[l].reshape(1, HEAD_DIM)
        kg = k_norm_gain[l].reshape(1, HEAD_DIM)
        chan_mats = _channel_matrices(w_fourier[l])
        x_prompt = _layer(x_prompt, gain, w_in_l, qg, kg, chan_mats, w_out_l, perm, unperm)
        x_sample = _layer(x_sample, gain, w_in_l, qg, kg, chan_mats, w_out_l, perm, unperm)
    return (x_prompt, x_sample)
```

```python
import functools
import math

import jax
import jax.numpy as jnp
import numpy as np
from jax import lax
from jax.experimental import pallas as pl
from jax.experimental.pallas import tpu as pltpu

D_MODEL = 2048
ATTN_WIDTH = 1024
FOURIER_WIDTH = 1024
HEAD_DIM = 128
N_HEADS = ATTN_WIDTH // HEAD_DIM
N_GROUPS = 4
GROUP_DIM = FOURIER_WIDTH // N_GROUPS
ROPE_THETA = 500000.0
ROPE_DIM = HEAD_DIM // 4
ROPE_HALF = ROPE_DIM // 2
HALF_KEYS = 64
RMS_EPS = 1e-6
RADIX = 16
ROW_GROUP = RADIX * RADIX
Q_BLOCK = 128
FOURIER1_ROWS = 512
INPROJ_GROUPS = 1
OUTPROJ_GROUPS = 2
BLOCKS_PER_GROUP = 4
MASK_VALUE = -1e30

VMEM_LIMIT_BYTES = 56 * 1024 * 1024

F32 = jnp.float32
BF16 = jnp.bfloat16


def _params(*semantics):
    return pltpu.CompilerParams(dimension_semantics=semantics, vmem_limit_bytes=VMEM_LIMIT_BYTES)


def _resident(shape):
    return pl.BlockSpec(shape, lambda *_: (0,) * len(shape), pipeline_mode=pl.Buffered(1))


def _rope_tables(seq, scale):
    expo = np.arange(ROPE_HALF, dtype=np.float32) / np.float32(ROPE_HALF)
    inv_freq = (np.float32(1.0) / np.power(np.float32(ROPE_THETA), expo)).astype(np.float32)
    ang = (np.arange(seq, dtype=np.float32)[:, None] * inv_freq[None, :]).astype(np.float64)
    cos, sin = np.cos(ang), np.sin(ang)
    a = np.ones((seq, HEAD_DIM))
    b = np.zeros((seq, HEAD_DIM))
    c = np.zeros((seq, HEAD_DIM))
    a[:, :ROPE_HALF] = cos
    a[:, ROPE_HALF:ROPE_DIM] = cos
    b[:, ROPE_HALF:ROPE_DIM] = sin
    c[:, :ROPE_HALF] = -sin
    return tuple(jnp.asarray((t * scale).astype(np.float32)) for t in (a, b, c))


def _dft_cos_sin(n, rows, cols, scale=1.0):
    m = (np.asarray(rows, dtype=np.int64)[:, None] * np.asarray(cols, dtype=np.int64)[None, :]) % n
    ang = 2.0 * np.pi * m.astype(np.float64) / n
    return np.cos(ang) * scale, np.sin(ang) * scale


def _group_permutation():
    p = np.zeros((ROW_GROUP, ROW_GROUP), np.float32)
    j, r = np.meshgrid(np.arange(RADIX), np.arange(RADIX), indexing="ij")
    p[(RADIX * r + j).ravel(), (RADIX * j + r).ravel()] = 1.0
    return p


def _bf16_const(a):
    return jnp.asarray(np.asarray(a, np.float32)).astype(BF16)


def _silu(a):
    return a * (1.0 / (1.0 + jnp.exp(-a)))


def _inproj_kernel(x_ref, gain_ref, w_ref, qg_ref, kg_ref,
                   qa_ref, qb_ref, qc_ref, ka_ref, kb_ref, kc_ref, perm_ref,
                   q16_ref, k1_ref, k16_ref, v1_ref, v16_ref, ga_ref, u16_ref, gf_ref):
    x = x_ref[...]
    ms = jnp.mean(x * x, axis=-1, keepdims=True)
    h = (x * lax.rsqrt(ms + RMS_EPS) * gain_ref[...]).astype(BF16)
    perm = perm_ref[...]

    def column_tile(j):
        cols = slice(j * ATTN_WIDTH, (j + 1) * ATTN_WIDTH)
        return jnp.dot(h, w_ref[:, cols], preferred_element_type=F32)

    def norm_rope(acc, g_ref, a_ref, b_ref, c_ref):
        heads = []
        for hd in range(N_HEADS):
            a = acc[:, hd * HEAD_DIM:(hd + 1) * HEAD_DIM]
            ms_h = jnp.mean(a * a, axis=-1, keepdims=True)
            n = a * lax.rsqrt(ms_h + RMS_EPS) * g_ref[...]
            r = (n * a_ref[...]
                 + pltpu.roll(n, ROPE_HALF, 1) * b_ref[...]
                 + pltpu.roll(n, HEAD_DIM - ROPE_HALF, 1) * c_ref[...])
            heads.append(r.astype(BF16))
        return jnp.concatenate(heads, axis=1)

    def to_mod16(t):
        return [jnp.dot(perm, t[g * ROW_GROUP:(g + 1) * ROW_GROUP], preferred_element_type=F32)
                for g in range(INPROJ_GROUPS)]

    def store_heads(ref, t):
        for hd in range(N_HEADS):
            ref[hd] = t[:, hd * HEAD_DIM:(hd + 1) * HEAD_DIM]

    def store_head_pieces(ref, groups):
        for g, t in enumerate(groups):
            for hd in range(N_HEADS):
                for r in range(RADIX):
                    ref[hd, r, g * RADIX:(g + 1) * RADIX, :] = (
                        t[r * RADIX:(r + 1) * RADIX, hd * HEAD_DIM:(hd + 1) * HEAD_DIM].astype(ref.dtype))

    acc_q = column_tile(0)
    acc_k = column_tile(1)
    q = norm_rope(acc_q, qg_ref, qa_ref, qb_ref, qc_ref)
    acc_v = column_tile(2)
    store_head_pieces(q16_ref, to_mod16(q))

    k = norm_rope(acc_k, kg_ref, ka_ref, kb_ref, kc_ref)
    acc_ga = column_tile(3)
    store_heads(k1_ref, k)
    store_head_pieces(k16_ref, to_mod16(k))

    v = acc_v.astype(BF16)
    acc_u = column_tile(4)
    store_heads(v1_ref, v)
    store_head_pieces(v16_ref, to_mod16(v))

    ga_ref[...] = _silu(acc_ga).astype(BF16)
    acc_gf = column_tile(5)

    for g, u16 in enumerate(to_mod16(acc_u.astype(BF16))):
        for r in range(RADIX):
            u16_ref[r, g * RADIX:(g + 1) * RADIX, :] = u16[r * RADIX:(r + 1) * RADIX, :].astype(BF16)

    gf_ref[...] = _silu(acc_gf).astype(BF16)


def _inproj(x2d, batch, seq, gain, w_in, qg, kg, perm):
    t = x2d.shape[0]
    rows = INPROJ_GROUPS * ROW_GROUP
    pieces = INPROJ_GROUPS * RADIX
    groups = seq // rows
    sub = seq // RADIX
    qa, qb, qc = _rope_tables(seq, HEAD_DIM ** -0.5 * math.log2(math.e))
    ka, kb, kc = _rope_tables(seq, 1.0)
    rope_spec = pl.BlockSpec((rows, HEAD_DIM), lambda i: (i % groups, 0))
    tok_spec = pl.BlockSpec((rows, ATTN_WIDTH), lambda i: (i, 0))
    head_nat_spec = pl.BlockSpec((N_HEADS, rows, HEAD_DIM), lambda i: (0, i, 0))
    head_m16_spec = pl.BlockSpec((N_HEADS, None, RADIX, pieces, HEAD_DIM),
                                 lambda i: (0, i // groups, 0, i % groups, 0))
    head_nat = jax.ShapeDtypeStruct((N_HEADS, t, HEAD_DIM), BF16)
    head_m16 = lambda dt: jax.ShapeDtypeStruct((N_HEADS, batch, RADIX, sub, HEAD_DIM), dt)
    return pl.pallas_call(
        _inproj_kernel,
        grid=(t // rows,),
        in_specs=[
            pl.BlockSpec((rows, D_MODEL), lambda i: (i, 0)),
            _resident((1, D_MODEL)),
            _resident(w_in.shape),
            _resident((1, HEAD_DIM)), _resident((1, HEAD_DIM)),
            rope_spec, rope_spec, rope_spec, rope_spec, rope_spec, rope_spec,
            _resident((ROW_GROUP, ROW_GROUP)),
        ],
        out_specs=[
            head_m16_spec, head_nat_spec, head_m16_spec, head_nat_spec, head_m16_spec,
            tok_spec,
            pl.BlockSpec((None, RADIX, pieces, FOURIER_WIDTH), lambda i: (i // groups, 0, i % groups, 0)),
            tok_spec,
        ],
        out_shape=[
            head_m16(F32), head_nat, head_m16(BF16), head_nat, head_m16(BF16),
            jax.ShapeDtypeStruct((t, ATTN_WIDTH), BF16),
            jax.ShapeDtypeStruct((batch, RADIX, sub, FOURIER_WIDTH), BF16),
            jax.ShapeDtypeStruct((t, FOURIER_WIDTH), BF16),
        ],
        compiler_params=_params("arbitrary"),
        name="inproj",
    )(x2d, gain, w_in, qg, kg, qa, qb, qc, ka, kb, kc, perm)


def _window_start(first, half, total, window):
    return min(max(first - half, 0), total - window)


class _Pattern:
    def __init__(self, q_first, k_first, diff0, scale):
        deltas = [scale * (q0 - k0) for q0, k0 in zip(q_first, k_first)]
        uniq = sorted(set(deltas))
        self.q_first, self.k_first = q_first, k_first
        self.table_of = [uniq.index(d) for d in deltas]
        self.bias = jnp.asarray(np.stack(
            [np.where(np.abs(diff0 + d) <= HALF_KEYS, 0.0, MASK_VALUE) for d in uniq]).astype(np.float32)
        ).astype(BF16)


def _attention_patterns(seq):
    sub = seq // RADIX
    qi = np.arange(Q_BLOCK)[:, None]
    win16 = min(Q_BLOCK + 2 * HALF_KEYS, sub)
    q16 = list(range(0, sub, Q_BLOCK))
    p16 = _Pattern(q16, [_window_start(q0, HALF_KEYS, sub, win16) for q0 in q16],
                   qi - np.arange(win16)[None, :], 1)
    q_rows, k_rows = Q_BLOCK // 4, Q_BLOCK // 4 + 2 * HALF_KEYS // 4
    kc = np.arange(4 * k_rows)[None, :]
    q4 = list(range(0, sub, q_rows))
    p4 = _Pattern(q4, [_window_start(q0, HALF_KEYS // 4, sub, k_rows) for q0 in q4],
                  4 * (qi % q_rows - kc % k_rows) + (qi // q_rows - kc // k_rows), 4)
    win1 = Q_BLOCK + 2 * HALF_KEYS
    q1 = list(range(0, seq, Q_BLOCK))
    p1 = _Pattern(q1, [_window_start(q0, HALF_KEYS, seq, win1) for q0 in q1],
                  RADIX * (qi % (Q_BLOCK // RADIX)) + qi // (Q_BLOCK // RADIX) - np.arange(win1)[None, :], 1)
    return p16, p4, p1


def _attn_kernel(q16_ref, k1_ref, v1_ref, k16_ref, v16_ref, b16_ref, b4_ref, b1_ref,
                 o_ref, acc_s, m_s, l_s, p0_s, w0_s, p1_s, w1_s, *, pat16, pat4, pat1):
    def gather(ref, pieces):
        return jnp.concatenate([ref[r, pl.ds(s0, n), :] for r, s0, n in pieces], axis=0)

    def scatter(ref, pieces, value):
        off = 0
        for r, s0, n in pieces:
            ref[r, pl.ds(s0, n), :] = value[off:off + n]
            off += n

    def scores_stage(blocks, first, p_scr, w_scr):
        for i, (q, k, _, bias, pieces) in enumerate(blocks):
            s = lax.dot_general(q(), k(), (((1,), (1,)), ((), ())), preferred_element_type=F32)
            s = s.astype(BF16) + bias()
            m_new = jnp.broadcast_to(jnp.max(s, axis=-1, keepdims=True), (Q_BLOCK, HEAD_DIM)).astype(F32)
            if not first:
                m_old = gather(m_s, pieces)
                m_new = jnp.maximum(m_old, m_new)
                w_scr[i] = jnp.exp2(m_old - m_new)
            width = s.shape[1]
            m_b = m_new.astype(BF16)
            p_scr[i, :, :width] = jnp.exp2(s - jnp.concatenate([m_b] * (width // HEAD_DIM), axis=1))
            scatter(m_s, pieces, m_new)

    def values_stage(blocks, first, p_scr, w_scr):
        for i, (_, _, v, _, pieces) in enumerate(blocks):
            vw = v()
            v_aug = jnp.concatenate([vw, jnp.ones(vw.shape, BF16)], axis=1)
            pv = jnp.dot(p_scr[i, :, :vw.shape[0]], v_aug, preferred_element_type=F32)
            acc, l = pv[:, :HEAD_DIM], pv[:, HEAD_DIM:]
            if not first:
                w_old = w_scr[i]
                acc = w_old * gather(acc_s, pieces) + acc
                l = w_old * gather(l_s, pieces) + l
            scatter(l_s, pieces, l)
            scatter(acc_s, pieces, acc)

    groups = []

    def add_groups(blocks, first):
        groups.extend((blocks[i:i + BLOCKS_PER_GROUP], first) for i in range(0, len(blocks), BLOCKS_PER_GROUP))

    win16 = b16_ref.shape[2]
    blocks = []
    for r in range(RADIX):
        for blk, (l0, ws) in enumerate(zip(pat16.q_first, pat16.k_first)):
            blocks.append((
                lambda r=r, l0=l0: q16_ref[r, l0:l0 + Q_BLOCK, :].astype(BF16),
                lambda r=r, ws=ws: k16_ref[r, ws:ws + win16, :],
                lambda r=r, ws=ws: v16_ref[r, ws:ws + win16, :],
                lambda blk=blk: b16_ref[pat16.table_of[blk]],
                [(r, l0, Q_BLOCK)]))
    add_groups(blocks, True)

    q_rows4 = Q_BLOCK // 4
    k_rows4 = b4_ref.shape[2] // 4
    blocks = []
    for r4 in range(4):
        res = [r4 + 4 * a for a in range(4)]
        for blk, (l0, ws) in enumerate(zip(pat4.q_first, pat4.k_first)):
            q_pieces = [(r, l0, q_rows4) for r in res]
            k_pieces = [(r, ws, k_rows4) for r in res]
            blocks.append((
                lambda p=q_pieces: gather(q16_ref, p).astype(BF16),
                lambda p=k_pieces: gather(k16_ref, p),
                lambda p=k_pieces: gather(v16_ref, p),
                lambda blk=blk: b4_ref[pat4.table_of[blk]],
                q_pieces))
    add_groups(blocks, False)

    q_rows1 = Q_BLOCK // RADIX
    win1 = b1_ref.shape[2]
    blocks = []
    for blk, (n0, ws) in enumerate(zip(pat1.q_first, pat1.k_first)):
        q_pieces = [(r, n0 // RADIX, q_rows1) for r in range(RADIX)]
        blocks.append((
            lambda p=q_pieces: gather(q16_ref, p).astype(BF16),
            lambda ws=ws: k1_ref[ws:ws + win1, :],
            lambda ws=ws: v1_ref[ws:ws + win1, :],
            lambda blk=blk: b1_ref[pat1.table_of[blk]],
            q_pieces))
    add_groups(blocks, False)

    slots = ((p0_s, w0_s), (p1_s, w1_s))
    scores_stage(*groups[0], *slots[0])
    for g in range(1, len(groups)):
        scores_stage(*groups[g], *slots[g % 2])
        values_stage(*groups[g - 1], *slots[(g - 1) % 2])
    values_stage(*groups[-1], *slots[(len(groups) - 1) % 2])

    def finish(r, carry):
        o_ref[r] = (acc_s[r] * (1.0 / l_s[r])).astype(BF16)
        return carry

    lax.fori_loop(0, RADIX, finish, 0)


def _attention(q16, k1, v1, k16, v16, batch, seq):
    sub = seq // RADIX
    pat16, pat4, pat1 = _attention_patterns(seq)
    m16_spec = pl.BlockSpec((None, None, RADIX, sub, HEAD_DIM), lambda b, h: (h, b, 0, 0, 0))
    nat_spec = pl.BlockSpec((None, seq, HEAD_DIM), lambda b, h: (h, b, 0))
    state = pltpu.VMEM((RADIX, sub, HEAD_DIM), F32)
    probs = pltpu.VMEM((BLOCKS_PER_GROUP, Q_BLOCK, Q_BLOCK + 2 * HALF_KEYS), BF16)
    rescale = pltpu.VMEM((BLOCKS_PER_GROUP, Q_BLOCK, HEAD_DIM), F32)
    return pl.pallas_call(
        functools.partial(_attn_kernel, pat16=pat16, pat4=pat4, pat1=pat1),
        grid=(batch, N_HEADS),
        in_specs=[m16_spec, nat_spec, nat_spec, m16_spec, m16_spec,
                  _resident(pat16.bias.shape), _resident(pat4.bias.shape), _resident(pat1.bias.shape)],
        out_specs=m16_spec,
        out_shape=jax.ShapeDtypeStruct((N_HEADS, batch, RADIX, sub, HEAD_DIM), BF16),
        scratch_shapes=[state, state, state, probs, rescale, probs, rescale],
        compiler_params=_params("parallel", "parallel"),
        name="attention",
    )(q16, k1, v1, k16, v16, pat16.bias, pat4.bias, pat1.bias)


def _chanmat_kernel(c_ref, s_ref, w_ref, m_ref):
    for g in range(N_GROUPS):
        w = w_ref[g]
        mc = jnp.dot(c_ref[...], w, preferred_element_type=F32, precision=lax.Precision.HIGHEST)
        ms = jnp.dot(s_ref[...], w, preferred_element_type=F32, precision=lax.Precision.HIGHEST)
        m_ref[g, :, :GROUP_DIM] = mc.astype(BF16)
        m_ref[g, :, GROUP_DIM:] = ms.astype(BF16)


def _channel_matrices(w_fourier):
    idx = np.arange(GROUP_DIM)
    cc, sc = _dft_cos_sin(GROUP_DIM, idx, idx, GROUP_DIM ** -0.5)
    return pl.pallas_call(
        _chanmat_kernel,
        out_shape=jax.ShapeDtypeStruct((N_GROUPS, GROUP_DIM, 2 * GROUP_DIM), BF16),
        name="chanmat",
    )(jnp.asarray(cc, F32), jnp.asarray(sc, F32), w_fourier)


def _fourier_kernel(u_ref, m_ref, c_ref, smc_ref, cps_ref, mc_ref, ms_ref, gf_ref, o_ref, yr_s, yi_s,
                    *, stage1_steps, tiles):
    t = pl.program_id(1)
    dot = functools.partial(jnp.dot, preferred_element_type=F32)
    per_step = u_ref.shape[0]

    @pl.when(t < stage1_steps)
    def _():
        z = []
        for j in range(per_step):
            a_parts, b_parts, ab_parts = [], [], []
            for g in range(N_GROUPS):
                ab = dot(u_ref[j, :, g * GROUP_DIM:(g + 1) * GROUP_DIM], m_ref[g])
                a, b = ab[:, :GROUP_DIM], ab[:, GROUP_DIM:]
                a_parts.append(a.astype(BF16))
                b_parts.append(b.astype(BF16))
                ab_parts.append((a + b).astype(BF16))
            z.append(tuple(jnp.concatenate(p, axis=1) for p in (a_parts, b_parts, ab_parts)))
        c, s_minus_c, c_plus_s = c_ref[...], smc_ref[...], cps_ref[...]
        for j, (a, b, a_plus_b) in enumerate(z):
            n2 = t * per_step + j
            k1 = dot(c, a_plus_b)
            yr_s[n2] = (k1 - dot(c_plus_s, b)).astype(BF16)
            yi_s[n2] = (k1 + dot(s_minus_c, a)).astype(BF16)

    @pl.when(t >= stage1_steps)
    def _():
        first_row = (t - stage1_steps) * (tiles * RADIX)
        for i in range(tiles):
            rows = pl.ds(pl.multiple_of(first_row + i * RADIX, RADIX), RADIX)
            yr = jnp.concatenate([yr_s[n2, rows, :] for n2 in range(RADIX)], axis=0)
            yi = jnp.concatenate([yi_s[n2, rows, :] for n2 in range(RADIX)], axis=0)
            x = dot(mc_ref[i], yr) + dot(ms_ref[i], yi)
            out_rows = slice(i * RADIX, (i + 1) * RADIX)
            for k2 in range(RADIX):
                piece = x[k2 * RADIX:(k2 + 1) * RADIX, :] * gf_ref[k2, out_rows, :].astype(F32)
                o_ref[k2, out_rows, :] = piece.astype(BF16)


def _stage2_matrices(seq):
    s1 = seq // RADIX
    tiles = s1 // RADIX
    c, k2, k1, n2 = np.meshgrid(np.arange(tiles), np.arange(RADIX), np.arange(RADIX), np.arange(RADIX),
                                indexing="ij")
    k = RADIX * c + k1 + s1 * k2
    ang = 2.0 * np.pi * ((n2 * k) % seq).astype(np.float64) / seq
    mc = np.zeros((tiles, ROW_GROUP, ROW_GROUP), np.float32)
    ms = np.zeros((tiles, ROW_GROUP, ROW_GROUP), np.float32)
    mc[c, RADIX * k2 + k1, RADIX * n2 + k1] = np.cos(ang) * seq ** -0.5
    ms[c, RADIX * k2 + k1, RADIX * n2 + k1] = -np.sin(ang) * seq ** -0.5
    return _bf16_const(mc), _bf16_const(ms)


def _fourier(u16, chan_mats, gf, batch, seq, tiles=4):
    s1 = seq // RADIX
    idx = np.arange(s1)
    c1, sn1 = _dft_cos_sin(s1, idx, idx)
    mc, ms = _stage2_matrices(seq)
    per_step = FOURIER1_ROWS // s1
    stage1_steps = RADIX // per_step
    rows = tiles * RADIX
    stage2 = lambda t: jnp.maximum(t - stage1_steps, 0)
    nat_spec = pl.BlockSpec((None, RADIX, rows, FOURIER_WIDTH), lambda b, t: (b, 0, stage2(t), 0))
    m_spec = pl.BlockSpec((tiles, ROW_GROUP, ROW_GROUP), lambda b, t: (stage2(t), 0, 0))
    y_scratch = pltpu.VMEM((RADIX, s1, FOURIER_WIDTH), BF16)
    out = pl.pallas_call(
        functools.partial(_fourier_kernel, stage1_steps=stage1_steps, tiles=tiles),
        grid=(batch, stage1_steps + s1 // rows),
        in_specs=[pl.BlockSpec((None, per_step, s1, FOURIER_WIDTH),
                               lambda b, t: (b, jnp.minimum(t, stage1_steps - 1), 0, 0)),
                  _resident(chan_mats.shape), _resident((s1, s1)), _resident((s1, s1)), _resident((s1, s1)),
                  m_spec, m_spec, nat_spec],
        out_specs=nat_spec,
        out_shape=jax.ShapeDtypeStruct((batch, RADIX, s1, FOURIER_WIDTH), BF16),
        scratch_shapes=[y_scratch, y_scratch],
        compiler_params=_params("parallel", "arbitrary"),
        name="fourier",
    )(u16, chan_mats, _bf16_const(c1), _bf16_const(sn1 - c1), _bf16_const(c1 + sn1), mc, ms,
      gf.reshape(batch, RADIX, s1, FOURIER_WIDTH))
    return out.reshape(batch * seq, FOURIER_WIDTH)


def _outproj_kernel(x_ref, o16_ref, ga_ref, mf_ref, unperm_ref, w_ref, y_ref):
    unperm = unperm_ref[...]
    attn = []
    for g in range(OUTPROJ_GROUPS):
        rows = slice(g * RADIX, (g + 1) * RADIX)
        heads = [jnp.concatenate([o16_ref[hd, r, rows, :] for r in range(RADIX)], axis=0)
                 for hd in range(N_HEADS)]
        attn16 = jnp.concatenate(heads, axis=1)
        attn.append(jnp.dot(unperm, attn16, preferred_element_type=F32))
    y = x_ref[...] + jnp.dot(mf_ref[...], w_ref[ATTN_WIDTH:, :], preferred_element_type=F32)
    mix_a = (jnp.concatenate(attn, axis=0) * ga_ref[...].astype(F32)).astype(BF16)
    y_ref[...] = y + jnp.dot(mix_a, w_ref[:ATTN_WIDTH, :], preferred_element_type=F32)


def _outproj(x2d, o16, ga, mix_f, unperm, w_out, seq):
    t = x2d.shape[0]
    rows = OUTPROJ_GROUPS * ROW_GROUP
    steps_per_seq = seq // rows
    tok_spec = pl.BlockSpec((rows, ATTN_WIDTH), lambda i: (i, 0))
    x_spec = pl.BlockSpec((rows, D_MODEL), lambda i: (i, 0))
    return pl.pallas_call(
        _outproj_kernel,
        grid=(t // rows,),
        in_specs=[x_spec,
                  pl.BlockSpec((N_HEADS, None, RADIX, OUTPROJ_GROUPS * RADIX, HEAD_DIM),
                               lambda i: (0, i // steps_per_seq, 0, i % steps_per_seq, 0)),
                  tok_spec, tok_spec,
                  _resident((ROW_GROUP, ROW_GROUP)),
                  _resident((D_MODEL, D_MODEL))],
        out_specs=x_spec,
        out_shape=jax.ShapeDtypeStruct((t, D_MODEL), F32),
        compiler_params=_params("arbitrary"),
        name="outproj",
    )(x2d, o16, ga, mix_f, unperm, w_out)


def _layer(x, gain, w_in, qg, kg, chan_mats, w_out, perm, unperm):
    batch, seq, _ = x.shape
    x2d = x.reshape(batch * seq, D_MODEL)
    q16, k1, k16, v1, v16, ga, u16, gf = _inproj(x2d, batch, seq, gain, w_in, qg, kg, perm)
    o16 = _attention(q16, k1, v1, k16, v16, batch, seq)
    mix_f = _fourier(u16, chan_mats, gf, batch, seq)
    y = _outproj(x2d, o16, ga, mix_f, unperm, w_out, seq)
    return y.reshape(batch, seq, D_MODEL)


def kernel(x_prompt, x_sample, rms_gain, w_in, q_norm_gain, k_norm_gain, w_fourier, w_out):
    depth = rms_gain.shape[0]
    p = _group_permutation()
    perm, unperm = _bf16_const(p), _bf16_const(p.T)
    for l in range(depth):
        gain = rms_gain[l].reshape(1, D_MODEL)
        w_in_l = w_in[l].astype(BF16)
        w_out_l = w_out[l].astype(BF16)
        qg = q_norm_gain[l].reshape(1, HEAD_DIM)
        kg = k_norm_gain[l].reshape(1, HEAD_DIM)
        chan_mats = _channel_matrices(w_fourier[l])
        x_prompt = _layer(x_prompt, gain, w_in_l, qg, kg, chan_mats, w_out_l, perm, unperm)
        x_sample = _layer(x_sample, gain, w_in_l, qg, kg, chan_mats, w_out_l, perm, unperm)
    return (x_prompt, x_sample)
```

```python
import functools
import math

import jax
import jax.numpy as jnp
import numpy as np
from jax import lax
from jax.experimental import pallas as pl
from jax.experimental.pallas import tpu as pltpu

D_MODEL = 2048
ATTN_WIDTH = 1024
FOURIER_WIDTH = 1024
HEAD_DIM = 128
N_HEADS = ATTN_WIDTH // HEAD_DIM
N_GROUPS = 4
GROUP_DIM = FOURIER_WIDTH // N_GROUPS
ROPE_THETA = 500000.0
ROPE_DIM = HEAD_DIM // 4
ROPE_HALF = ROPE_DIM // 2
HALF_KEYS = 64
RMS_EPS = 1e-6
RADIX = 16
ROW_GROUP = RADIX * RADIX
Q_BLOCK = 128
FOURIER1_ROWS = 512
ATTN_HEADS_PER_STEP = 2
INPROJ_GROUPS = 1
OUTPROJ_GROUPS = 2
BLOCKS_PER_GROUP = 4
MASK_VALUE = -1e30

VMEM_LIMIT_BYTES = 56 * 1024 * 1024

F32 = jnp.float32
BF16 = jnp.bfloat16


def _params(*semantics):
    return pltpu.CompilerParams(dimension_semantics=semantics, vmem_limit_bytes=VMEM_LIMIT_BYTES)


def _resident(shape):
    return pl.BlockSpec(shape, lambda *_: (0,) * len(shape), pipeline_mode=pl.Buffered(1))


def _rope_tables(seq, scale):
    expo = np.arange(ROPE_HALF, dtype=np.float32) / np.float32(ROPE_HALF)
    inv_freq = (np.float32(1.0) / np.power(np.float32(ROPE_THETA), expo)).astype(np.float32)
    ang = (np.arange(seq, dtype=np.float32)[:, None] * inv_freq[None, :]).astype(np.float64)
    cos, sin = np.cos(ang), np.sin(ang)
    a = np.ones((seq, HEAD_DIM))
    b = np.zeros((seq, HEAD_DIM))
    c = np.zeros((seq, HEAD_DIM))
    a[:, :ROPE_HALF] = cos
    a[:, ROPE_HALF:ROPE_DIM] = cos
    b[:, ROPE_HALF:ROPE_DIM] = sin
    c[:, :ROPE_HALF] = -sin
    return tuple(jnp.asarray((t * scale).astype(np.float32)) for t in (a, b, c))


def _dft_cos_sin(n, rows, cols, scale=1.0):
    m = (np.asarray(rows, dtype=np.int64)[:, None] * np.asarray(cols, dtype=np.int64)[None, :]) % n
    ang = 2.0 * np.pi * m.astype(np.float64) / n
    return np.cos(ang) * scale, np.sin(ang) * scale


def _group_permutation():
    p = np.zeros((ROW_GROUP, ROW_GROUP), np.float32)
    j, r = np.meshgrid(np.arange(RADIX), np.arange(RADIX), indexing="ij")
    p[(RADIX * r + j).ravel(), (RADIX * j + r).ravel()] = 1.0
    return p


def _bf16_const(a):
    return jnp.asarray(np.asarray(a, np.float32)).astype(BF16)


def _silu(a):
    return a * (1.0 / (1.0 + jnp.exp(-a)))


def _inproj_kernel(x_ref, gain_ref, w_ref, qg_ref, kg_ref,
                   qa_ref, qb_ref, qc_ref, ka_ref, kb_ref, kc_ref, perm_ref,
                   q16_ref, k1_ref, k16_ref, v1_ref, v16_ref, ga_ref, u16_ref, gf_ref):
    x = x_ref[...]
    ms = jnp.mean(x * x, axis=-1, keepdims=True)
    h = (x * lax.rsqrt(ms + RMS_EPS) * gain_ref[...]).astype(BF16)
    perm = perm_ref[...]

    def column_tile(j):
        cols = slice(j * ATTN_WIDTH, (j + 1) * ATTN_WIDTH)
        return jnp.dot(h, w_ref[:, cols], preferred_element_type=F32)

    def norm_rope(acc, g_ref, a_ref, b_ref, c_ref):
        heads = []
        for hd in range(N_HEADS):
            a = acc[:, hd * HEAD_DIM:(hd + 1) * HEAD_DIM]
            ms_h = jnp.mean(a * a, axis=-1, keepdims=True)
            n = a * lax.rsqrt(ms_h + RMS_EPS) * g_ref[...]
            r = (n * a_ref[...]
                 + pltpu.roll(n, ROPE_HALF, 1) * b_ref[...]
                 + pltpu.roll(n, HEAD_DIM - ROPE_HALF, 1) * c_ref[...])
            heads.append(r.astype(BF16))
        return jnp.concatenate(heads, axis=1)

    def to_mod16(t):
        return [jnp.dot(perm, t[g * ROW_GROUP:(g + 1) * ROW_GROUP], preferred_element_type=F32)
                for g in range(INPROJ_GROUPS)]

    def store_heads(ref, t):
        for hd in range(N_HEADS):
            ref[hd] = t[:, hd * HEAD_DIM:(hd + 1) * HEAD_DIM]

    def store_head_pieces(ref, groups):
        for g, t in enumerate(groups):
            for hd in range(N_HEADS):
                for r in range(RADIX):
                    ref[hd, r, g * RADIX:(g + 1) * RADIX, :] = (
                        t[r * RADIX:(r + 1) * RADIX, hd * HEAD_DIM:(hd + 1) * HEAD_DIM].astype(ref.dtype))

    acc_q = column_tile(0)
    acc_k = column_tile(1)
    q = norm_rope(acc_q, qg_ref, qa_ref, qb_ref, qc_ref)
    acc_v = column_tile(2)
    store_head_pieces(q16_ref, to_mod16(q))

    k = norm_rope(acc_k, kg_ref, ka_ref, kb_ref, kc_ref)
    acc_ga = column_tile(3)
    store_heads(k1_ref, k)
    store_head_pieces(k16_ref, to_mod16(k))

    v = acc_v.astype(BF16)
    acc_u = column_tile(4)
    store_heads(v1_ref, v)
    store_head_pieces(v16_ref, to_mod16(v))

    ga_ref[...] = _silu(acc_ga).astype(BF16)
    acc_gf = column_tile(5)

    for g, u16 in enumerate(to_mod16(acc_u.astype(BF16))):
        for r in range(RADIX):
            u16_ref[r, g * RADIX:(g + 1) * RADIX, :] = u16[r * RADIX:(r + 1) * RADIX, :].astype(BF16)

    gf_ref[...] = _silu(acc_gf).astype(BF16)


def _inproj(x2d, batch, seq, gain, w_in, qg, kg, perm):
    t = x2d.shape[0]
    rows = INPROJ_GROUPS * ROW_GROUP
    pieces = INPROJ_GROUPS * RADIX
    groups = seq // rows
    sub = seq // RADIX
    qa, qb, qc = _rope_tables(seq, HEAD_DIM ** -0.5 * math.log2(math.e))
    ka, kb, kc = _rope_tables(seq, 1.0)
    rope_spec = pl.BlockSpec((rows, HEAD_DIM), lambda i: (i % groups, 0))
    tok_spec = pl.BlockSpec((rows, ATTN_WIDTH), lambda i: (i, 0))
    head_nat_spec = pl.BlockSpec((N_HEADS, rows, HEAD_DIM), lambda i: (0, i, 0))
    head_m16_spec = pl.BlockSpec((N_HEADS, None, RADIX, pieces, HEAD_DIM),
                                 lambda i: (0, i // groups, 0, i % groups, 0))
    head_nat = jax.ShapeDtypeStruct((N_HEADS, t, HEAD_DIM), BF16)
    head_m16 = lambda dt: jax.ShapeDtypeStruct((N_HEADS, batch, RADIX, sub, HEAD_DIM), dt)
    return pl.pallas_call(
        _inproj_kernel,
        grid=(t // rows,),
        in_specs=[
            pl.BlockSpec((rows, D_MODEL), lambda i: (i, 0)),
            _resident((1, D_MODEL)),
            _resident(w_in.shape),
            _resident((1, HEAD_DIM)), _resident((1, HEAD_DIM)),
            rope_spec, rope_spec, rope_spec, rope_spec, rope_spec, rope_spec,
            _resident((ROW_GROUP, ROW_GROUP)),
        ],
        out_specs=[
            head_m16_spec, head_nat_spec, head_m16_spec, head_nat_spec, head_m16_spec,
            tok_spec,
            pl.BlockSpec((None, RADIX, pieces, FOURIER_WIDTH), lambda i: (i // groups, 0, i % groups, 0)),
            tok_spec,
        ],
        out_shape=[
            head_m16(F32), head_nat, head_m16(BF16), head_nat, head_m16(BF16),
            jax.ShapeDtypeStruct((t, ATTN_WIDTH), BF16),
            jax.ShapeDtypeStruct((batch, RADIX, sub, FOURIER_WIDTH), BF16),
            jax.ShapeDtypeStruct((t, FOURIER_WIDTH), BF16),
        ],
        compiler_params=_params("arbitrary"),
        name="inproj",
    )(x2d, gain, w_in, qg, kg, qa, qb, qc, ka, kb, kc, perm)


def _window_start(first, half, total, window):
    return min(max(first - half, 0), total - window)


class _Pattern:
    def __init__(self, q_first, k_first, diff0, scale):
        deltas = [scale * (q0 - k0) for q0, k0 in zip(q_first, k_first)]
        uniq = sorted(set(deltas))
        self.q_first, self.k_first = q_first, k_first
        self.table_of = [uniq.index(d) for d in deltas]
        self.bias = jnp.asarray(np.stack(
            [np.where(np.abs(diff0 + d) <= HALF_KEYS, 0.0, MASK_VALUE) for d in uniq]).astype(np.float32)
        ).astype(BF16)


def _attention_patterns(seq):
    sub = seq // RADIX
    qi = np.arange(Q_BLOCK)[:, None]
    win16 = min(Q_BLOCK + 2 * HALF_KEYS, sub)
    q16 = list(range(0, sub, Q_BLOCK))
    p16 = _Pattern(q16, [_window_start(q0, HALF_KEYS, sub, win16) for q0 in q16],
                   qi - np.arange(win16)[None, :], 1)
    q_rows, k_rows = Q_BLOCK // 4, Q_BLOCK // 4 + 2 * HALF_KEYS // 4
    kc = np.arange(4 * k_rows)[None, :]
    q4 = list(range(0, sub, q_rows))
    p4 = _Pattern(q4, [_window_start(q0, HALF_KEYS // 4, sub, k_rows) for q0 in q4],
                  4 * (qi % q_rows - kc % k_rows) + (qi // q_rows - kc // k_rows), 4)
    win1 = Q_BLOCK + 2 * HALF_KEYS
    q1 = list(range(0, seq, Q_BLOCK))
    p1 = _Pattern(q1, [_window_start(q0, HALF_KEYS, seq, win1) for q0 in q1],
                  RADIX * (qi % (Q_BLOCK // RADIX)) + qi // (Q_BLOCK // RADIX) - np.arange(win1)[None, :], 1)
    return p16, p4, p1


def _attn_kernel(q16_ref, k1_ref, v1_ref, k16_ref, v16_ref, b16_ref, b4_ref, b1_ref,
                 o_ref, acc_s, m_s, l_s, p0_s, w0_s, p1_s, w1_s, *, pat16, pat4, pat1):
    def gather(ref, pieces):
        return jnp.concatenate([ref[r, pl.ds(s0, n), :] for r, s0, n in pieces], axis=0)

    def scatter(ref, pieces, value):
        off = 0
        for r, s0, n in pieces:
            ref[r, pl.ds(s0, n), :] = value[off:off + n]
            off += n

    def scores_stage(blocks, first, p_scr, w_scr):
        for i, (q, k, _, bias, pieces, (_, m_h, _)) in enumerate(blocks):
            s = lax.dot_general(q(), k(), (((1,), (1,)), ((), ())), preferred_element_type=F32)
            s = s.astype(BF16) + bias()
            m_new = jnp.broadcast_to(jnp.max(s, axis=-1, keepdims=True), (Q_BLOCK, HEAD_DIM)).astype(F32)
            if not first:
                m_old = gather(m_h, pieces)
                m_new = jnp.maximum(m_old, m_new)
                w_scr[i] = jnp.exp2(m_old - m_new)
            width = s.shape[1]
            m_b = m_new.astype(BF16)
            p_scr[i, :, :width] = jnp.exp2(s - jnp.concatenate([m_b] * (width // HEAD_DIM), axis=1))
            scatter(m_h, pieces, m_new)

    def values_stage(blocks, first, p_scr, w_scr):
        for i, (_, _, v, _, pieces, (acc_h, _, l_h)) in enumerate(blocks):
            vw = v()
            v_aug = jnp.concatenate([vw, jnp.ones(vw.shape, BF16)], axis=1)
            pv = jnp.dot(p_scr[i, :, :vw.shape[0]], v_aug, preferred_element_type=F32)
            acc, l = pv[:, :HEAD_DIM], pv[:, HEAD_DIM:]
            if not first:
                w_old = w_scr[i]
                acc = w_old * gather(acc_h, pieces) + acc
                l = w_old * gather(l_h, pieces) + l
            scatter(l_h, pieces, l)
            scatter(acc_h, pieces, acc)

    win16 = b16_ref.shape[2]
    q_rows4 = Q_BLOCK // 4
    k_rows4 = b4_ref.shape[2] // 4
    q_rows1 = Q_BLOCK // RADIX
    win1 = b1_ref.shape[2]
    blocks16, blocks4, blocks1 = [], [], []
    for hd in range(q16_ref.shape[0]):
        q16, k1, v1, k16, v16 = (ref.at[hd] for ref in (q16_ref, k1_ref, v1_ref, k16_ref, v16_ref))
        state = (acc_s.at[hd], m_s.at[hd], l_s.at[hd])

        for r in range(RADIX):
            for blk, (l0, ws) in enumerate(zip(pat16.q_first, pat16.k_first)):
                blocks16.append((
                    lambda q16=q16, r=r, l0=l0: q16[r, l0:l0 + Q_BLOCK, :].astype(BF16),
                    lambda k16=k16, r=r, ws=ws: k16[r, ws:ws + win16, :],
                    lambda v16=v16, r=r, ws=ws: v16[r, ws:ws + win16, :],
                    lambda blk=blk: b16_ref[pat16.table_of[blk]],
                    [(r, l0, Q_BLOCK)], state))

        for r4 in range(4):
            res = [r4 + 4 * a for a in range(4)]
            for blk, (l0, ws) in enumerate(zip(pat4.q_first, pat4.k_first)):
                q_pieces = [(r, l0, q_rows4) for r in res]
                k_pieces = [(r, ws, k_rows4) for r in res]
                blocks4.append((
                    lambda q16=q16, p=q_pieces: gather(q16, p).astype(BF16),
                    lambda k16=k16, p=k_pieces: gather(k16, p),
                    lambda v16=v16, p=k_pieces: gather(v16, p),
                    lambda blk=blk: b4_ref[pat4.table_of[blk]],
                    q_pieces, state))

        for blk, (n0, ws) in enumerate(zip(pat1.q_first, pat1.k_first)):
            q_pieces = [(r, n0 // RADIX, q_rows1) for r in range(RADIX)]
            blocks1.append((
                lambda q16=q16, p=q_pieces: gather(q16, p).astype(BF16),
                lambda k1=k1, ws=ws: k1[ws:ws + win1, :],
                lambda v1=v1, ws=ws: v1[ws:ws + win1, :],
                lambda blk=blk: b1_ref[pat1.table_of[blk]],
                q_pieces, state))

    groups = []
    for blocks, first in ((blocks16, True), (blocks4, False), (blocks1, False)):
        groups.extend((blocks[i:i + BLOCKS_PER_GROUP], first) for i in range(0, len(blocks), BLOCKS_PER_GROUP))

    slots = ((p0_s, w0_s), (p1_s, w1_s))
    scores_stage(*groups[0], *slots[0])
    for g in range(1, len(groups)):
        scores_stage(*groups[g], *slots[g % 2])
        values_stage(*groups[g - 1], *slots[(g - 1) % 2])
    values_stage(*groups[-1], *slots[(len(groups) - 1) % 2])

    def finish(i, carry):
        hd, r = i // RADIX, i % RADIX
        o_ref[hd, r] = (acc_s[hd, r] * (1.0 / l_s[hd, r])).astype(BF16)
        return carry

    lax.fori_loop(0, q16_ref.shape[0] * RADIX, finish, 0)


def _attention(q16, k1, v1, k16, v16, batch, seq):
    sub = seq // RADIX
    heads = ATTN_HEADS_PER_STEP
    pat16, pat4, pat1 = _attention_patterns(seq)
    m16_spec = pl.BlockSpec((heads, None, RADIX, sub, HEAD_DIM), lambda b, h: (h, b, 0, 0, 0))
    nat_spec = pl.BlockSpec((heads, seq, HEAD_DIM), lambda b, h: (h, b, 0))
    state = pltpu.VMEM((heads, RADIX, sub, HEAD_DIM), F32)
    probs = pltpu.VMEM((BLOCKS_PER_GROUP, Q_BLOCK, Q_BLOCK + 2 * HALF_KEYS), BF16)
    rescale = pltpu.VMEM((BLOCKS_PER_GROUP, Q_BLOCK, HEAD_DIM), F32)
    return pl.pallas_call(
        functools.partial(_attn_kernel, pat16=pat16, pat4=pat4, pat1=pat1),
        grid=(batch, N_HEADS // heads),
        in_specs=[m16_spec, nat_spec, nat_spec, m16_spec, m16_spec,
                  _resident(pat16.bias.shape), _resident(pat4.bias.shape), _resident(pat1.bias.shape)],
        out_specs=m16_spec,
        out_shape=jax.ShapeDtypeStruct((N_HEADS, batch, RADIX, sub, HEAD_DIM), BF16),
        scratch_shapes=[state, state, state, probs, rescale, probs, rescale],
        compiler_params=_params("parallel", "parallel"),
        name="attention",
    )(q16, k1, v1, k16, v16, pat16.bias, pat4.bias, pat1.bias)


def _chanmat_kernel(c_ref, s_ref, w_ref, m_ref):
    for g in range(N_GROUPS):
        w = w_ref[g]
        mc = jnp.dot(c_ref[...], w, preferred_element_type=F32, precision=lax.Precision.HIGHEST)
        ms = jnp.dot(s_ref[...], w, preferred_element_type=F32, precision=lax.Precision.HIGHEST)
        m_ref[g, :, :GROUP_DIM] = mc.astype(BF16)
        m_ref[g, :, GROUP_DIM:] = ms.astype(BF16)


def _channel_matrices(w_fourier):
    idx = np.arange(GROUP_DIM)
    cc, sc = _dft_cos_sin(GROUP_DIM, idx, idx, GROUP_DIM ** -0.5)
    return pl.pallas_call(
        _chanmat_kernel,
        out_shape=jax.ShapeDtypeStruct((N_GROUPS, GROUP_DIM, 2 * GROUP_DIM), BF16),
        name="chanmat",
    )(jnp.asarray(cc, F32), jnp.asarray(sc, F32), w_fourier)


def _fourier_kernel(u_ref, m_ref, c_ref, smc_ref, cps_ref, mc_ref, ms_ref, gf_ref, o_ref, yr_s, yi_s,
                    *, stage1_steps, tiles):
    t = pl.program_id(1)
    dot = functools.partial(jnp.dot, preferred_element_type=F32)
    per_step = u_ref.shape[0]

    @pl.when(t < stage1_steps)
    def _():
        z = []
        for j in range(per_step):
            a_parts, b_parts, ab_parts = [], [], []
            for g in range(N_GROUPS):
                ab = dot(u_ref[j, :, g * GROUP_DIM:(g + 1) * GROUP_DIM], m_ref[g])
                a, b = ab[:, :GROUP_DIM], ab[:, GROUP_DIM:]
                a_parts.append(a.astype(BF16))
                b_parts.append(b.astype(BF16))
                ab_parts.append((a + b).astype(BF16))
            z.append(tuple(jnp.concatenate(p, axis=1) for p in (a_parts, b_parts, ab_parts)))
        c, s_minus_c, c_plus_s = c_ref[...], smc_ref[...], cps_ref[...]
        for j, (a, b, a_plus_b) in enumerate(z):
            n2 = t * per_step + j
            k1 = dot(c, a_plus_b)
            yr_s[n2] = (k1 - dot(c_plus_s, b)).astype(BF16)
            yi_s[n2] = (k1 + dot(s_minus_c, a)).astype(BF16)

    @pl.when(t >= stage1_steps)
    def _():
        first_row = (t - stage1_steps) * (tiles * RADIX)
        for i in range(tiles):
            rows = pl.ds(pl.multiple_of(first_row + i * RADIX, RADIX), RADIX)
            yr = jnp.concatenate([yr_s[n2, rows, :] for n2 in range(RADIX)], axis=0)
            yi = jnp.concatenate([yi_s[n2, rows, :] for n2 in range(RADIX)], axis=0)
            x = dot(mc_ref[i], yr) + dot(ms_ref[i], yi)
            out_rows = slice(i * RADIX, (i + 1) * RADIX)
            for k2 in range(RADIX):
                piece = x[k2 * RADIX:(k2 + 1) * RADIX, :] * gf_ref[k2, out_rows, :].astype(F32)
                o_ref[k2, out_rows, :] = piece.astype(BF16)


def _stage2_matrices(seq):
    s1 = seq // RADIX
    tiles = s1 // RADIX
    c, k2, k1, n2 = np.meshgrid(np.arange(tiles), np.arange(RADIX), np.arange(RADIX), np.arange(RADIX),
                                indexing="ij")
    k = RADIX * c + k1 + s1 * k2
    ang = 2.0 * np.pi * ((n2 * k) % seq).astype(np.float64) / seq
    mc = np.zeros((tiles, ROW_GROUP, ROW_GROUP), np.float32)
    ms = np.zeros((tiles, ROW_GROUP, ROW_GROUP), np.float32)
    mc[c, RADIX * k2 + k1, RADIX * n2 + k1] = np.cos(ang) * seq ** -0.5
    ms[c, RADIX * k2 + k1, RADIX * n2 + k1] = -np.sin(ang) * seq ** -0.5
    return _bf16_const(mc), _bf16_const(ms)


def _fourier(u16, chan_mats, gf, batch, seq, tiles=4):
    s1 = seq // RADIX
    idx = np.arange(s1)
    c1, sn1 = _dft_cos_sin(s1, idx, idx)
    mc, ms = _stage2_matrices(seq)
    per_step = FOURIER1_ROWS // s1
    stage1_steps = RADIX // per_step
    rows = tiles * RADIX
    stage2 = lambda t: jnp.maximum(t - stage1_steps, 0)
    nat_spec = pl.BlockSpec((None, RADIX, rows, FOURIER_WIDTH), lambda b, t: (b, 0, stage2(t), 0))
    m_spec = pl.BlockSpec((tiles, ROW_GROUP, ROW_GROUP), lambda b, t: (stage2(t), 0, 0))
    y_scratch = pltpu.VMEM((RADIX, s1, FOURIER_WIDTH), BF16)
    out = pl.pallas_call(
        functools.partial(_fourier_kernel, stage1_steps=stage1_steps, tiles=tiles),
        grid=(batch, stage1_steps + s1 // rows),
        in_specs=[pl.BlockSpec((None, per_step, s1, FOURIER_WIDTH),
                               lambda b, t: (b, jnp.minimum(t, stage1_steps - 1), 0, 0)),
                  _resident(chan_mats.shape), _resident((s1, s1)), _resident((s1, s1)), _resident((s1, s1)),
                  m_spec, m_spec, nat_spec],
        out_specs=nat_spec,
        out_shape=jax.ShapeDtypeStruct((batch, RADIX, s1, FOURIER_WIDTH), BF16),
        scratch_shapes=[y_scratch, y_scratch],
        compiler_params=_params("parallel", "arbitrary"),
        name="fourier",
    )(u16, chan_mats, _bf16_const(c1), _bf16_const(sn1 - c1), _bf16_const(c1 + sn1), mc, ms,
      gf.reshape(batch, RADIX, s1, FOURIER_WIDTH))
    return out.reshape(batch * seq, FOURIER_WIDTH)


def _outproj_kernel(x_ref, o16_ref, ga_ref, mf_ref, unperm_ref, w_ref, y_ref):
    unperm = unperm_ref[...]
    attn = []
    for g in range(OUTPROJ_GROUPS):
        rows = slice(g * RADIX, (g + 1) * RADIX)
        heads = [jnp.concatenate([o16_ref[hd, r, rows, :] for r in range(RADIX)], axis=0)
                 for hd in range(N_HEADS)]
        attn16 = jnp.concatenate(heads, axis=1)
        attn.append(jnp.dot(unperm, attn16, preferred_element_type=F32))
    y = x_ref[...] + jnp.dot(mf_ref[...], w_ref[ATTN_WIDTH:, :], preferred_element_type=F32)
    mix_a = (jnp.concatenate(attn, axis=0) * ga_ref[...].astype(F32)).astype(BF16)
    y_ref[...] = y + jnp.dot(mix_a, w_ref[:ATTN_WIDTH, :], preferred_element_type=F32)


def _outproj(x2d, o16, ga, mix_f, unperm, w_out, seq):
    t = x2d.shape[0]
    rows = OUTPROJ_GROUPS * ROW_GROUP
    steps_per_seq = seq // rows
    tok_spec = pl.BlockSpec((rows, ATTN_WIDTH), lambda i: (i, 0))
    x_spec = pl.BlockSpec((rows, D_MODEL), lambda i: (i, 0))
    return pl.pallas_call(
        _outproj_kernel,
        grid=(t // rows,),
        in_specs=[x_spec,
                  pl.BlockSpec((N_HEADS, None, RADIX, OUTPROJ_GROUPS * RADIX, HEAD_DIM),
                               lambda i: (0, i // steps_per_seq, 0, i % steps_per_seq, 0)),
                  tok_spec, tok_spec,
                  _resident((ROW_GROUP, ROW_GROUP)),
                  _resident((D_MODEL, D_MODEL))],
        out_specs=x_spec,
        out_shape=jax.ShapeDtypeStruct((t, D_MODEL), F32),
        compiler_params=_params("arbitrary"),
        name="outproj",
    )(x2d, o16, ga, mix_f, unperm, w_out)


def _layer(x, gain, w_in, qg, kg, chan_mats, w_out, perm, unperm):
    batch, seq, _ = x.shape
    x2d = x.reshape(batch * seq, D_MODEL)
    q16, k1, k16, v1, v16, ga, u16, gf = _inproj(x2d, batch, seq, gain, w_in, qg, kg, perm)
    o16 = _attention(q16, k1, v1, k16, v16, batch, seq)
    mix_f = _fourier(u16, chan_mats, gf, batch, seq)
    y = _outproj(x2d, o16, ga, mix_f, unperm, w_out, seq)
    return y.reshape(batch, seq, D_MODEL)


def kernel(x_prompt, x_sample, rms_gain, w_in, q_norm_gain, k_norm_gain, w_fourier, w_out):
    depth = rms_gain.shape[0]
    p = _group_permutation()
    perm, unperm = _bf16_const(p), _bf16_const(p.T)
    for l in range(depth):
        gain = rms_gain[l].reshape(1, D_MODEL)
        w_in_l = w_in[l].astype(BF16)
        w_out_l = w_out[l].astype(BF16)
        qg = q_norm_gain[l].reshape(1, HEAD_DIM)
        kg = k_norm_gain[l].reshape(1, HEAD_DIM)
        chan_mats = _channel_matrices(w_fourier[l])
        x_prompt = _layer(x_prompt, gain, w_in_l, qg, kg, chan_mats, w_out_l, perm, unperm)
        x_sample = _layer(x_sample, gain, w_in_l, qg, kg, chan_mats, w_out_l, perm, unperm)
    return (x_prompt, x_sample)
```

```python
import functools
import math

import jax
import jax.numpy as jnp
import numpy as np
from jax import lax
from jax.experimental import pallas as pl
from jax.experimental.pallas import tpu as pltpu

D_MODEL = 2048
ATTN_WIDTH = 1024
FOURIER_WIDTH = 1024
HEAD_DIM = 128
N_HEADS = ATTN_WIDTH // HEAD_DIM
N_GROUPS = 4
GROUP_DIM = FOURIER_WIDTH // N_GROUPS
ROPE_THETA = 500000.0
ROPE_DIM = HEAD_DIM // 4
ROPE_HALF = ROPE_DIM // 2
HALF_KEYS = 64
RMS_EPS = 1e-6
RADIX = 16
ROW_GROUP = RADIX * RADIX
Q_BLOCK = 128
FOURIER1_ROWS = 512
ATTN_HEADS_PER_STEP = 2
STATE_PAD_ROWS = 8
INPROJ_GROUPS = 1
OUTPROJ_GROUPS = 2
BLOCKS_PER_GROUP = 4
MASK_VALUE = -1e30

VMEM_LIMIT_BYTES = 56 * 1024 * 1024

F32 = jnp.float32
BF16 = jnp.bfloat16


def _params(*semantics):
    return pltpu.CompilerParams(dimension_semantics=semantics, vmem_limit_bytes=VMEM_LIMIT_BYTES)


def _resident(shape):
    return pl.BlockSpec(shape, lambda *_: (0,) * len(shape), pipeline_mode=pl.Buffered(1))


def _rope_tables(seq, scale):
    expo = np.arange(ROPE_HALF, dtype=np.float32) / np.float32(ROPE_HALF)
    inv_freq = (np.float32(1.0) / np.power(np.float32(ROPE_THETA), expo)).astype(np.float32)
    ang = (np.arange(seq, dtype=np.float32)[:, None] * inv_freq[None, :]).astype(np.float64)
    cos, sin = np.cos(ang), np.sin(ang)
    a = np.ones((seq, HEAD_DIM))
    b = np.zeros((seq, HEAD_DIM))
    c = np.zeros((seq, HEAD_DIM))
    a[:, :ROPE_HALF] = cos
    a[:, ROPE_HALF:ROPE_DIM] = cos
    b[:, ROPE_HALF:ROPE_DIM] = sin
    c[:, :ROPE_HALF] = -sin
    return tuple(jnp.asarray((t * scale).astype(np.float32)) for t in (a, b, c))


def _dft_cos_sin(n, rows, cols, scale=1.0):
    m = (np.asarray(rows, dtype=np.int64)[:, None] * np.asarray(cols, dtype=np.int64)[None, :]) % n
    ang = 2.0 * np.pi * m.astype(np.float64) / n
    return np.cos(ang) * scale, np.sin(ang) * scale


def _group_permutation():
    p = np.zeros((ROW_GROUP, ROW_GROUP), np.float32)
    j, r = np.meshgrid(np.arange(RADIX), np.arange(RADIX), indexing="ij")
    p[(RADIX * r + j).ravel(), (RADIX * j + r).ravel()] = 1.0
    return p


def _bf16_const(a):
    return jnp.asarray(np.asarray(a, np.float32)).astype(BF16)


def _silu(a):
    return a * (1.0 / (1.0 + jnp.exp(-a)))


def _inproj_kernel(x_ref, gain_ref, w_ref, qg_ref, kg_ref,
                   qa_ref, qb_ref, qc_ref, ka_ref, kb_ref, kc_ref, perm_ref,
                   q16_ref, k1_ref, k16_ref, v1_ref, v16_ref, ga_ref, u16_ref, gf_ref):
    x = x_ref[...]
    ms = jnp.mean(x * x, axis=-1, keepdims=True)
    h = (x * lax.rsqrt(ms + RMS_EPS) * gain_ref[...]).astype(BF16)
    perm = perm_ref[...]

    def column_tile(j):
        cols = slice(j * ATTN_WIDTH, (j + 1) * ATTN_WIDTH)
        return jnp.dot(h, w_ref[:, cols], preferred_element_type=F32)

    def norm_rope(acc, g_ref, a_ref, b_ref, c_ref):
        heads = []
        for hd in range(N_HEADS):
            a = acc[:, hd * HEAD_DIM:(hd + 1) * HEAD_DIM]
            ms_h = jnp.mean(a * a, axis=-1, keepdims=True)
            n = a * lax.rsqrt(ms_h + RMS_EPS) * g_ref[...]
            r = (n * a_ref[...]
                 + pltpu.roll(n, ROPE_HALF, 1) * b_ref[...]
                 + pltpu.roll(n, HEAD_DIM - ROPE_HALF, 1) * c_ref[...])
            heads.append(r.astype(BF16))
        return jnp.concatenate(heads, axis=1)

    def to_mod16(t):
        return [jnp.dot(perm, t[g * ROW_GROUP:(g + 1) * ROW_GROUP], preferred_element_type=F32)
                for g in range(INPROJ_GROUPS)]

    def store_heads(ref, t):
        for hd in range(N_HEADS):
            ref[hd] = t[:, hd * HEAD_DIM:(hd + 1) * HEAD_DIM]

    def store_head_pieces(ref, groups):
        for g, t in enumerate(groups):
            for hd in range(N_HEADS):
                for r in range(RADIX):
                    ref[hd, r, g * RADIX:(g + 1) * RADIX, :] = (
                        t[r * RADIX:(r + 1) * RADIX, hd * HEAD_DIM:(hd + 1) * HEAD_DIM].astype(ref.dtype))

    acc_q = column_tile(0)
    acc_k = column_tile(1)
    q = norm_rope(acc_q, qg_ref, qa_ref, qb_ref, qc_ref)
    acc_v = column_tile(2)
    store_head_pieces(q16_ref, to_mod16(q))

    k = norm_rope(acc_k, kg_ref, ka_ref, kb_ref, kc_ref)
    acc_ga = column_tile(3)
    store_heads(k1_ref, k)
    store_head_pieces(k16_ref, to_mod16(k))

    v = acc_v.astype(BF16)
    acc_u = column_tile(4)
    store_heads(v1_ref, v)
    store_head_pieces(v16_ref, to_mod16(v))

    ga_ref[...] = _silu(acc_ga).astype(BF16)
    acc_gf = column_tile(5)

    for g, u16 in enumerate(to_mod16(acc_u.astype(BF16))):
        for r in range(RADIX):
            u16_ref[r, g * RADIX:(g + 1) * RADIX, :] = u16[r * RADIX:(r + 1) * RADIX, :].astype(BF16)

    gf_ref[...] = _silu(acc_gf).astype(BF16)


def _inproj(x2d, batch, seq, gain, w_in, qg, kg, perm):
    t = x2d.shape[0]
    rows = INPROJ_GROUPS * ROW_GROUP
    pieces = INPROJ_GROUPS * RADIX
    groups = seq // rows
    sub = seq // RADIX
    qa, qb, qc = _rope_tables(seq, HEAD_DIM ** -0.5 * math.log2(math.e))
    ka, kb, kc = _rope_tables(seq, 1.0)
    rope_spec = pl.BlockSpec((rows, HEAD_DIM), lambda i: (i % groups, 0))
    tok_spec = pl.BlockSpec((rows, ATTN_WIDTH), lambda i: (i, 0))
    head_nat_spec = pl.BlockSpec((N_HEADS, rows, HEAD_DIM), lambda i: (0, i, 0))
    head_m16_spec = pl.BlockSpec((N_HEADS, None, RADIX, pieces, HEAD_DIM),
                                 lambda i: (0, i // groups, 0, i % groups, 0))
    head_nat = jax.ShapeDtypeStruct((N_HEADS, t, HEAD_DIM), BF16)
    head_m16 = lambda dt: jax.ShapeDtypeStruct((N_HEADS, batch, RADIX, sub, HEAD_DIM), dt)
    return pl.pallas_call(
        _inproj_kernel,
        grid=(t // rows,),
        in_specs=[
            pl.BlockSpec((rows, D_MODEL), lambda i: (i, 0)),
            _resident((1, D_MODEL)),
            _resident(w_in.shape),
            _resident((1, HEAD_DIM)), _resident((1, HEAD_DIM)),
            rope_spec, rope_spec, rope_spec, rope_spec, rope_spec, rope_spec,
            _resident((ROW_GROUP, ROW_GROUP)),
        ],
        out_specs=[
            head_m16_spec, head_nat_spec, head_m16_spec, head_nat_spec, head_m16_spec,
            tok_spec,
            pl.BlockSpec((None, RADIX, pieces, FOURIER_WIDTH), lambda i: (i // groups, 0, i % groups, 0)),
            tok_spec,
        ],
        out_shape=[
            head_m16(F32), head_nat, head_m16(BF16), head_nat, head_m16(BF16),
            jax.ShapeDtypeStruct((t, ATTN_WIDTH), BF16),
            jax.ShapeDtypeStruct((batch, RADIX, sub, FOURIER_WIDTH), BF16),
            jax.ShapeDtypeStruct((t, FOURIER_WIDTH), BF16),
        ],
        compiler_params=_params("arbitrary"),
        name="inproj",
    )(x2d, gain, w_in, qg, kg, qa, qb, qc, ka, kb, kc, perm)


def _window_start(first, half, total, window):
    return min(max(first - half, 0), total - window)


class _Pattern:
    def __init__(self, q_first, k_first, diff0, scale):
        deltas = [scale * (q0 - k0) for q0, k0 in zip(q_first, k_first)]
        uniq = sorted(set(deltas))
        self.q_first, self.k_first = q_first, k_first
        self.table_of = [uniq.index(d) for d in deltas]
        self.bias = jnp.asarray(np.stack(
            [np.where(np.abs(diff0 + d) <= HALF_KEYS, 0.0, MASK_VALUE) for d in uniq]).astype(np.float32)
        ).astype(BF16)


def _attention_patterns(seq):
    sub = seq // RADIX
    qi = np.arange(Q_BLOCK)[:, None]
    win16 = min(Q_BLOCK + 2 * HALF_KEYS, sub)
    q16 = list(range(0, sub, Q_BLOCK))
    p16 = _Pattern(q16, [_window_start(q0, HALF_KEYS, sub, win16) for q0 in q16],
                   qi - np.arange(win16)[None, :], 1)
    q_rows, k_rows = Q_BLOCK // 4, Q_BLOCK // 4 + 2 * HALF_KEYS // 4
    kc = np.arange(4 * k_rows)[None, :]
    q4 = list(range(0, sub, q_rows))
    p4 = _Pattern(q4, [_window_start(q0, HALF_KEYS // 4, sub, k_rows) for q0 in q4],
                  4 * (qi % q_rows - kc % k_rows) + (qi // q_rows - kc // k_rows), 4)
    win1 = Q_BLOCK + 2 * HALF_KEYS
    q1 = list(range(0, seq, Q_BLOCK))
    p1 = _Pattern(q1, [_window_start(q0, HALF_KEYS, seq, win1) for q0 in q1],
                  RADIX * (qi % (Q_BLOCK // RADIX)) + qi // (Q_BLOCK // RADIX) - np.arange(win1)[None, :], 1)
    return p16, p4, p1


def _attn_kernel(q16_ref, k1_ref, v1_ref, k16_ref, v16_ref, b16_ref, b4_ref, b1_ref,
                 o_ref, acc_s, m_s, l_s, p0_s, w0_s, p1_s, w1_s, *, pat16, pat4, pat1):
    def gather(ref, pieces):
        return jnp.concatenate([ref[r, pl.ds(s0, n), :] for r, s0, n in pieces], axis=0)

    def scatter(ref, pieces, value):
        off = 0
        for r, s0, n in pieces:
            ref[r, pl.ds(s0, n), :] = value[off:off + n]
            off += n

    def scores_stage(blocks, first, p_scr, w_scr):
        for i, (q, k, _, bias, pieces, (_, m_h, _)) in enumerate(blocks):
            s = lax.dot_general(q(), k(), (((1,), (1,)), ((), ())), preferred_element_type=F32)
            s = s.astype(BF16) + bias()
            m_new = jnp.broadcast_to(jnp.max(s, axis=-1, keepdims=True), (Q_BLOCK, HEAD_DIM)).astype(F32)
            if not first:
                m_old = gather(m_h, pieces)
                m_new = jnp.maximum(m_old, m_new)
                w_scr[i] = jnp.exp2(m_old - m_new)
            width = s.shape[1]
            m_b = m_new.astype(BF16)
            p_scr[i, :, :width] = jnp.exp2(s - jnp.concatenate([m_b] * (width // HEAD_DIM), axis=1))
            scatter(m_h, pieces, m_new)

    def values_stage(blocks, first, p_scr, w_scr):
        for i, (_, _, v, _, pieces, (acc_h, _, l_h)) in enumerate(blocks):
            vw = v()
            v_aug = jnp.concatenate([vw, jnp.ones(vw.shape, BF16)], axis=1)
            pv = jnp.dot(p_scr[i, :, :vw.shape[0]], v_aug, preferred_element_type=F32)
            acc, l = pv[:, :HEAD_DIM], pv[:, HEAD_DIM:]
            if not first:
                w_old = w_scr[i]
                acc = w_old * gather(acc_h, pieces) + acc
                l = w_old * gather(l_h, pieces) + l
            scatter(l_h, pieces, l)
            scatter(acc_h, pieces, acc)

    win16 = b16_ref.shape[2]
    q_rows4 = Q_BLOCK // 4
    k_rows4 = b4_ref.shape[2] // 4
    q_rows1 = Q_BLOCK // RADIX
    win1 = b1_ref.shape[2]
    blocks16, blocks4, blocks1 = [], [], []
    for hd in range(q16_ref.shape[0]):
        q16, k1, v1, k16, v16 = (ref.at[hd] for ref in (q16_ref, k1_ref, v1_ref, k16_ref, v16_ref))
        state = (acc_s.at[hd], m_s.at[hd], l_s.at[hd])

        for r in range(RADIX):
            for blk, (l0, ws) in enumerate(zip(pat16.q_first, pat16.k_first)):
                blocks16.append((
                    lambda q16=q16, r=r, l0=l0: q16[r, l0:l0 + Q_BLOCK, :].astype(BF16),
                    lambda k16=k16, r=r, ws=ws: k16[r, ws:ws + win16, :],
                    lambda v16=v16, r=r, ws=ws: v16[r, ws:ws + win16, :],
                    lambda blk=blk: b16_ref[pat16.table_of[blk]],
                    [(r, l0, Q_BLOCK)], state))

        for r4 in range(4):
            res = [r4 + 4 * a for a in range(4)]
            for blk, (l0, ws) in enumerate(zip(pat4.q_first, pat4.k_first)):
                q_pieces = [(r, l0, q_rows4) for r in res]
                k_pieces = [(r, ws, k_rows4) for r in res]
                blocks4.append((
                    lambda q16=q16, p=q_pieces: gather(q16, p).astype(BF16),
                    lambda k16=k16, p=k_pieces: gather(k16, p),
                    lambda v16=v16, p=k_pieces: gather(v16, p),
                    lambda blk=blk: b4_ref[pat4.table_of[blk]],
                    q_pieces, state))

        for blk, (n0, ws) in enumerate(zip(pat1.q_first, pat1.k_first)):
            q_pieces = [(r, n0 // RADIX, q_rows1) for r in range(RADIX)]
            blocks1.append((
                lambda q16=q16, p=q_pieces: gather(q16, p).astype(BF16),
                lambda k1=k1, ws=ws: k1[ws:ws + win1, :],
                lambda v1=v1, ws=ws: v1[ws:ws + win1, :],
                lambda blk=blk: b1_ref[pat1.table_of[blk]],
                q_pieces, state))

    groups = []
    for blocks, first in ((blocks16, True), (blocks4, False), (blocks1, False)):
        groups.extend((blocks[i:i + BLOCKS_PER_GROUP], first) for i in range(0, len(blocks), BLOCKS_PER_GROUP))

    slots = ((p0_s, w0_s), (p1_s, w1_s))
    scores_stage(*groups[0], *slots[0])
    for g in range(1, len(groups)):
        scores_stage(*groups[g], *slots[g % 2])
        values_stage(*groups[g - 1], *slots[(g - 1) % 2])
    values_stage(*groups[-1], *slots[(len(groups) - 1) % 2])

    def finish(i, carry):
        hd, r = i // RADIX, i % RADIX
        rows = o_ref.shape[2]
        o_ref[hd, r] = (acc_s[hd, r, :rows, :] * (1.0 / l_s[hd, r, :rows, :])).astype(BF16)
        return carry

    lax.fori_loop(0, q16_ref.shape[0] * RADIX, finish, 0)


def _attention(q16, k1, v1, k16, v16, batch, seq):
    sub = seq // RADIX
    heads = ATTN_HEADS_PER_STEP
    pat16, pat4, pat1 = _attention_patterns(seq)
    m16_spec = pl.BlockSpec((heads, None, RADIX, sub, HEAD_DIM), lambda b, h: (h, b, 0, 0, 0))
    nat_spec = pl.BlockSpec((heads, seq, HEAD_DIM), lambda b, h: (h, b, 0))
    state = pltpu.VMEM((heads, RADIX, sub + STATE_PAD_ROWS, HEAD_DIM), F32)
    probs = pltpu.VMEM((BLOCKS_PER_GROUP, Q_BLOCK, Q_BLOCK + 2 * HALF_KEYS), BF16)
    rescale = pltpu.VMEM((BLOCKS_PER_GROUP, Q_BLOCK, HEAD_DIM), F32)
    return pl.pallas_call(
        functools.partial(_attn_kernel, pat16=pat16, pat4=pat4, pat1=pat1),
        grid=(batch, N_HEADS // heads),
        in_specs=[m16_spec, nat_spec, nat_spec, m16_spec, m16_spec,
                  _resident(pat16.bias.shape), _resident(pat4.bias.shape), _resident(pat1.bias.shape)],
        out_specs=m16_spec,
        out_shape=jax.ShapeDtypeStruct((N_HEADS, batch, RADIX, sub, HEAD_DIM), BF16),
        scratch_shapes=[state, state, state, probs, rescale, probs, rescale],
        compiler_params=_params("parallel", "parallel"),
        name="attention",
    )(q16, k1, v1, k16, v16, pat16.bias, pat4.bias, pat1.bias)


def _chanmat_kernel(c_ref, s_ref, w_ref, m_ref):
    for g in range(N_GROUPS):
        w = w_ref[g]
        mc = jnp.dot(c_ref[...], w, preferred_element_type=F32, precision=lax.Precision.HIGHEST)
        ms = jnp.dot(s_ref[...], w, preferred_element_type=F32, precision=lax.Precision.HIGHEST)
        m_ref[g, :, :GROUP_DIM] = mc.astype(BF16)
        m_ref[g, :, GROUP_DIM:] = ms.astype(BF16)


def _channel_matrices(w_fourier):
    idx = np.arange(GROUP_DIM)
    cc, sc = _dft_cos_sin(GROUP_DIM, idx, idx, GROUP_DIM ** -0.5)
    return pl.pallas_call(
        _chanmat_kernel,
        out_shape=jax.ShapeDtypeStruct((N_GROUPS, GROUP_DIM, 2 * GROUP_DIM), BF16),
        name="chanmat",
    )(jnp.asarray(cc, F32), jnp.asarray(sc, F32), w_fourier)


def _fourier_kernel(u_ref, m_ref, c_ref, smc_ref, cps_ref, mc_ref, ms_ref, gf_ref, o_ref, yr_s, yi_s,
                    *, stage1_steps, tiles):
    t = pl.program_id(1)
    dot = functools.partial(jnp.dot, preferred_element_type=F32)
    per_step = u_ref.shape[0]

    @pl.when(t < stage1_steps)
    def _():
        z = []
        for j in range(per_step):
            a_parts, b_parts, ab_parts = [], [], []
            for g in range(N_GROUPS):
                ab = dot(u_ref[j, :, g * GROUP_DIM:(g + 1) * GROUP_DIM], m_ref[g])
                a, b = ab[:, :GROUP_DIM], ab[:, GROUP_DIM:]
                a_parts.append(a.astype(BF16))
                b_parts.append(b.astype(BF16))
                ab_parts.append((a + b).astype(BF16))
            z.append(tuple(jnp.concatenate(p, axis=1) for p in (a_parts, b_parts, ab_parts)))
        c, s_minus_c, c_plus_s = c_ref[...], smc_ref[...], cps_ref[...]
        for j, (a, b, a_plus_b) in enumerate(z):
            n2 = t * per_step + j
            k1 = dot(c, a_plus_b)
            yr_s[n2] = (k1 - dot(c_plus_s, b)).astype(BF16)
            yi_s[n2] = (k1 + dot(s_minus_c, a)).astype(BF16)

    @pl.when(t >= stage1_steps)
    def _():
        first_row = (t - stage1_steps) * (tiles * RADIX)
        for i in range(tiles):
            rows = pl.ds(pl.multiple_of(first_row + i * RADIX, RADIX), RADIX)
            yr = jnp.concatenate([yr_s[n2, rows, :] for n2 in range(RADIX)], axis=0)
            yi = jnp.concatenate([yi_s[n2, rows, :] for n2 in range(RADIX)], axis=0)
            x = dot(mc_ref[i], yr) + dot(ms_ref[i], yi)
            out_rows = slice(i * RADIX, (i + 1) * RADIX)
            for k2 in range(RADIX):
                piece = x[k2 * RADIX:(k2 + 1) * RADIX, :] * gf_ref[k2, out_rows, :].astype(F32)
                o_ref[k2, out_rows, :] = piece.astype(BF16)


def _stage2_matrices(seq):
    s1 = seq // RADIX
    tiles = s1 // RADIX
    c, k2, k1, n2 = np.meshgrid(np.arange(tiles), np.arange(RADIX), np.arange(RADIX), np.arange(RADIX),
                                indexing="ij")
    k = RADIX * c + k1 + s1 * k2
    ang = 2.0 * np.pi * ((n2 * k) % seq).astype(np.float64) / seq
    mc = np.zeros((tiles, ROW_GROUP, ROW_GROUP), np.float32)
    ms = np.zeros((tiles, ROW_GROUP, ROW_GROUP), np.float32)
    mc[c, RADIX * k2 + k1, RADIX * n2 + k1] = np.cos(ang) * seq ** -0.5
    ms[c, RADIX * k2 + k1, RADIX * n2 + k1] = -np.sin(ang) * seq ** -0.5
    return _bf16_const(mc), _bf16_const(ms)


def _fourier(u16, chan_mats, gf, batch, seq, tiles=4):
    s1 = seq // RADIX
    idx = np.arange(s1)
    c1, sn1 = _dft_cos_sin(s1, idx, idx)
    mc, ms = _stage2_matrices(seq)
    per_step = FOURIER1_ROWS // s1
    stage1_steps = RADIX // per_step
    rows = tiles * RADIX
    stage2 = lambda t: jnp.maximum(t - stage1_steps, 0)
    nat_spec = pl.BlockSpec((None, RADIX, rows, FOURIER_WIDTH), lambda b, t: (b, 0, stage2(t), 0))
    m_spec = pl.BlockSpec((tiles, ROW_GROUP, ROW_GROUP), lambda b, t: (stage2(t), 0, 0))
    y_scratch = pltpu.VMEM((RADIX, s1, FOURIER_WIDTH), BF16)
    out = pl.pallas_call(
        functools.partial(_fourier_kernel, stage1_steps=stage1_steps, tiles=tiles),
        grid=(batch, stage1_steps + s1 // rows),
        in_specs=[pl.BlockSpec((None, per_step, s1, FOURIER_WIDTH),
                               lambda b, t: (b, jnp.minimum(t, stage1_steps - 1), 0, 0)),
                  _resident(chan_mats.shape), _resident((s1, s1)), _resident((s1, s1)), _resident((s1, s1)),
                  m_spec, m_spec, nat_spec],
        out_specs=nat_spec,
        out_shape=jax.ShapeDtypeStruct((batch, RADIX, s1, FOURIER_WIDTH), BF16),
        scratch_shapes=[y_scratch, y_scratch],
        compiler_params=_params("parallel", "arbitrary"),
        name="fourier",
    )(u16, chan_mats, _bf16_const(c1), _bf16_const(sn1 - c1), _bf16_const(c1 + sn1), mc, ms,
      gf.reshape(batch, RADIX, s1, FOURIER_WIDTH))
    return out.reshape(batch * seq, FOURIER_WIDTH)


def _outproj_kernel(x_ref, o16_ref, ga_ref, mf_ref, unperm_ref, w_ref, y_ref):
    unperm = unperm_ref[...]
    attn = []
    for g in range(OUTPROJ_GROUPS):
        rows = slice(g * RADIX, (g + 1) * RADIX)
        heads = [jnp.concatenate([o16_ref[hd, r, rows, :] for r in range(RADIX)], axis=0)
                 for hd in range(N_HEADS)]
        attn16 = jnp.concatenate(heads, axis=1)
        attn.append(jnp.dot(unperm, attn16, preferred_element_type=F32))
    y = x_ref[...] + jnp.dot(mf_ref[...], w_ref[ATTN_WIDTH:, :], preferred_element_type=F32)
    mix_a = (jnp.concatenate(attn, axis=0) * ga_ref[...].astype(F32)).astype(BF16)
    y_ref[...] = y + jnp.dot(mix_a, w_ref[:ATTN_WIDTH, :], preferred_element_type=F32)


def _outproj(x2d, o16, ga, mix_f, unperm, w_out, seq):
    t = x2d.shape[0]
    rows = OUTPROJ_GROUPS * ROW_GROUP
    steps_per_seq = seq // rows
    tok_spec = pl.BlockSpec((rows, ATTN_WIDTH), lambda i: (i, 0))
    x_spec = pl.BlockSpec((rows, D_MODEL), lambda i: (i, 0))
    return pl.pallas_call(
        _outproj_kernel,
        grid=(t // rows,),
        in_specs=[x_spec,
                  pl.BlockSpec((N_HEADS, None, RADIX, OUTPROJ_GROUPS * RADIX, HEAD_DIM),
                               lambda i: (0, i // steps_per_seq, 0, i % steps_per_seq, 0)),
                  tok_spec, tok_spec,
                  _resident((ROW_GROUP, ROW_GROUP)),
                  _resident((D_MODEL, D_MODEL))],
        out_specs=x_spec,
        out_shape=jax.ShapeDtypeStruct((t, D_MODEL), F32),
        compiler_params=_params("arbitrary"),
        name="outproj",
    )(x2d, o16, ga, mix_f, unperm, w_out)


def _layer(x, gain, w_in, qg, kg, chan_mats, w_out, perm, unperm):
    batch, seq, _ = x.shape
    x2d = x.reshape(batch * seq, D_MODEL)
    q16, k1, k16, v1, v16, ga, u16, gf = _inproj(x2d, batch, seq, gain, w_in, qg, kg, perm)
    o16 = _attention(q16, k1, v1, k16, v16, batch, seq)
    mix_f = _fourier(u16, chan_mats, gf, batch, seq)
    y = _outproj(x2d, o16, ga, mix_f, unperm, w_out, seq)
    return y.reshape(batch, seq, D_MODEL)


def kernel(x_prompt, x_sample, rms_gain, w_in, q_norm_gain, k_norm_gain, w_fourier, w_out):
    depth = rms_gain.shape[0]
    p = _group_permutation()
    perm, unperm = _bf16_const(p), _bf16_const(p.T)
    for l in range(depth):
        gain = rms_gain[l].reshape(1, D_MODEL)
        w_in_l = w_in[l].astype(BF16)
        w_out_l = w_out[l].astype(BF16)
        qg = q_norm_gain[l].reshape(1, HEAD_DIM)
        kg = k_norm_gain[l].reshape(1, HEAD_DIM)
        chan_mats = _channel_matrices(w_fourier[l])
        x_prompt = _layer(x_prompt, gain, w_in_l, qg, kg, chan_mats, w_out_l, perm, unperm)
        x_sample = _layer(x_sample, gain, w_in_l, qg, kg, chan_mats, w_out_l, perm, unperm)
    return (x_prompt, x_sample)
```

```python
import functools
import math

import jax
import jax.numpy as jnp
import numpy as np
from jax import lax
from jax.experimental import pallas as pl
from jax.experimental.pallas import tpu as pltpu

D_MODEL = 2048
ATTN_WIDTH = 1024
FOURIER_WIDTH = 1024
HEAD_DIM = 128
N_HEADS = ATTN_WIDTH // HEAD_DIM
N_GROUPS = 4
GROUP_DIM = FOURIER_WIDTH // N_GROUPS
ROPE_THETA = 500000.0
ROPE_DIM = HEAD_DIM // 4
ROPE_HALF = ROPE_DIM // 2
HALF_KEYS = 64
RMS_EPS = 1e-6
RADIX = 16
ROW_GROUP = RADIX * RADIX
Q_BLOCK = 128
FOURIER1_ROWS = 1024
FOURIER2_TILES = 8
ATTN_HEADS_PER_STEP = 2
INPROJ_GROUPS = 1
OUTPROJ_GROUPS = 2
BLOCKS_PER_GROUP = 4
MASK_VALUE = -1e30

VMEM_LIMIT_BYTES = 56 * 1024 * 1024

F32 = jnp.float32
BF16 = jnp.bfloat16


def _params(*semantics):
    return pltpu.CompilerParams(dimension_semantics=semantics, vmem_limit_bytes=VMEM_LIMIT_BYTES)


def _resident(shape):
    return pl.BlockSpec(shape, lambda *_: (0,) * len(shape), pipeline_mode=pl.Buffered(1))


def _rope_tables(seq, scale):
    expo = np.arange(ROPE_HALF, dtype=np.float32) / np.float32(ROPE_HALF)
    inv_freq = (np.float32(1.0) / np.power(np.float32(ROPE_THETA), expo)).astype(np.float32)
    ang = (np.arange(seq, dtype=np.float32)[:, None] * inv_freq[None, :]).astype(np.float64)
    cos, sin = np.cos(ang), np.sin(ang)
    a = np.ones((seq, HEAD_DIM))
    b = np.zeros((seq, HEAD_DIM))
    c = np.zeros((seq, HEAD_DIM))
    a[:, :ROPE_HALF] = cos
    a[:, ROPE_HALF:ROPE_DIM] = cos
    b[:, ROPE_HALF:ROPE_DIM] = sin
    c[:, :ROPE_HALF] = -sin
    return tuple(jnp.asarray((t * scale).astype(np.float32)) for t in (a, b, c))


def _dft_cos_sin(n, rows, cols, scale=1.0):
    m = (np.asarray(rows, dtype=np.int64)[:, None] * np.asarray(cols, dtype=np.int64)[None, :]) % n
    ang = 2.0 * np.pi * m.astype(np.float64) / n
    return np.cos(ang) * scale, np.sin(ang) * scale


def _group_permutation():
    p = np.zeros((ROW_GROUP, ROW_GROUP), np.float32)
    j, r = np.meshgrid(np.arange(RADIX), np.arange(RADIX), indexing="ij")
    p[(RADIX * r + j).ravel(), (RADIX * j + r).ravel()] = 1.0
    return p


def _bf16_const(a):
    return jnp.asarray(np.asarray(a, np.float32)).astype(BF16)


def _silu(a):
    return a * (1.0 / (1.0 + jnp.exp(-a)))


def _inproj_kernel(x_ref, gain_ref, w_ref, qg_ref, kg_ref,
                   qa_ref, qb_ref, qc_ref, ka_ref, kb_ref, kc_ref, perm_ref,
                   q16_ref, k1_ref, k16_ref, v1_ref, v16_ref, ga_ref, u16_ref, gf_ref):
    x = x_ref[...]
    ms = jnp.mean(x * x, axis=-1, keepdims=True)
    h = (x * lax.rsqrt(ms + RMS_EPS) * gain_ref[...]).astype(BF16)
    perm = perm_ref[...]

    def column_tile(j):
        cols = slice(j * ATTN_WIDTH, (j + 1) * ATTN_WIDTH)
        return jnp.dot(h, w_ref[:, cols], preferred_element_type=F32)

    def norm_rope(acc, g_ref, a_ref, b_ref, c_ref):
        heads = []
        for hd in range(N_HEADS):
            a = acc[:, hd * HEAD_DIM:(hd + 1) * HEAD_DIM]
            ms_h = jnp.mean(a * a, axis=-1, keepdims=True)
            n = a * lax.rsqrt(ms_h + RMS_EPS) * g_ref[...]
            r = (n * a_ref[...]
                 + pltpu.roll(n, ROPE_HALF, 1) * b_ref[...]
                 + pltpu.roll(n, HEAD_DIM - ROPE_HALF, 1) * c_ref[...])
            heads.append(r.astype(BF16))
        return jnp.concatenate(heads, axis=1)

    def to_mod16(t):
        return [jnp.dot(perm, t[g * ROW_GROUP:(g + 1) * ROW_GROUP], preferred_element_type=F32)
                for g in range(INPROJ_GROUPS)]

    def store_heads(ref, t):
        for hd in range(N_HEADS):
            ref[hd] = t[:, hd * HEAD_DIM:(hd + 1) * HEAD_DIM]

    def store_head_pieces(ref, groups):
        for g, t in enumerate(groups):
            for hd in range(N_HEADS):
                for r in range(RADIX):
                    ref[hd, r, g * RADIX:(g + 1) * RADIX, :] = (
                        t[r * RADIX:(r + 1) * RADIX, hd * HEAD_DIM:(hd + 1) * HEAD_DIM].astype(ref.dtype))

    acc_q = column_tile(0)
    acc_k = column_tile(1)
    q = norm_rope(acc_q, qg_ref, qa_ref, qb_ref, qc_ref)
    acc_v = column_tile(2)
    store_head_pieces(q16_ref, to_mod16(q))

    k = norm_rope(acc_k, kg_ref, ka_ref, kb_ref, kc_ref)
    acc_ga = column_tile(3)
    store_heads(k1_ref, k)
    store_head_pieces(k16_ref, to_mod16(k))

    v = acc_v.astype(BF16)
    acc_u = column_tile(4)
    store_heads(v1_ref, v)
    store_head_pieces(v16_ref, to_mod16(v))

    ga_ref[...] = _silu(acc_ga).astype(BF16)
    acc_gf = column_tile(5)

    for g, u16 in enumerate(to_mod16(acc_u.astype(BF16))):
        for r in range(RADIX):
            u16_ref[r, g * RADIX:(g + 1) * RADIX, :] = u16[r * RADIX:(r + 1) * RADIX, :].astype(BF16)

    gf_ref[...] = _silu(acc_gf).astype(BF16)


def _inproj(x2d, batch, seq, gain, w_in, qg, kg, perm):
    t = x2d.shape[0]
    rows = INPROJ_GROUPS * ROW_GROUP
    pieces = INPROJ_GROUPS * RADIX
    groups = seq // rows
    sub = seq // RADIX
    qa, qb, qc = _rope_tables(seq, HEAD_DIM ** -0.5 * math.log2(math.e))
    ka, kb, kc = _rope_tables(seq, 1.0)
    rope_spec = pl.BlockSpec((rows, HEAD_DIM), lambda i: (i % groups, 0))
    tok_spec = pl.BlockSpec((rows, ATTN_WIDTH), lambda i: (i, 0))
    head_nat_spec = pl.BlockSpec((N_HEADS, rows, HEAD_DIM), lambda i: (0, i, 0))
    head_m16_spec = pl.BlockSpec((N_HEADS, None, RADIX, pieces, HEAD_DIM),
                                 lambda i: (0, i // groups, 0, i % groups, 0))
    head_nat = jax.ShapeDtypeStruct((N_HEADS, t, HEAD_DIM), BF16)
    head_m16 = lambda dt: jax.ShapeDtypeStruct((N_HEADS, batch, RADIX, sub, HEAD_DIM), dt)
    return pl.pallas_call(
        _inproj_kernel,
        grid=(t // rows,),
        in_specs=[
            pl.BlockSpec((rows, D_MODEL), lambda i: (i, 0)),
            _resident((1, D_MODEL)),
            _resident(w_in.shape),
            _resident((1, HEAD_DIM)), _resident((1, HEAD_DIM)),
            rope_spec, rope_spec, rope_spec, rope_spec, rope_spec, rope_spec,
            _resident((ROW_GROUP, ROW_GROUP)),
        ],
        out_specs=[
            head_m16_spec, head_nat_spec, head_m16_spec, head_nat_spec, head_m16_spec,
            tok_spec,
            pl.BlockSpec((None, RADIX, pieces, FOURIER_WIDTH), lambda i: (i // groups, 0, i % groups, 0)),
            tok_spec,
        ],
        out_shape=[
            head_m16(F32), head_nat, head_m16(BF16), head_nat, head_m16(BF16),
            jax.ShapeDtypeStruct((t, ATTN_WIDTH), BF16),
            jax.ShapeDtypeStruct((batch, RADIX, sub, FOURIER_WIDTH), BF16),
            jax.ShapeDtypeStruct((t, FOURIER_WIDTH), BF16),
        ],
        compiler_params=_params("arbitrary"),
        name="inproj",
    )(x2d, gain, w_in, qg, kg, qa, qb, qc, ka, kb, kc, perm)


def _window_start(first, half, total, window):
    return min(max(first - half, 0), total - window)


class _Pattern:
    def __init__(self, q_first, k_first, diff0, scale):
        deltas = [scale * (q0 - k0) for q0, k0 in zip(q_first, k_first)]
        uniq = sorted(set(deltas))
        self.q_first, self.k_first = q_first, k_first
        self.table_of = [uniq.index(d) for d in deltas]
        self.bias = jnp.asarray(np.stack(
            [np.where(np.abs(diff0 + d) <= HALF_KEYS, 0.0, MASK_VALUE) for d in uniq]).astype(np.float32)
        ).astype(BF16)


def _attention_patterns(seq):
    sub = seq // RADIX
    qi = np.arange(Q_BLOCK)[:, None]
    win16 = min(Q_BLOCK + 2 * HALF_KEYS, sub)
    q16 = list(range(0, sub, Q_BLOCK))
    p16 = _Pattern(q16, [_window_start(q0, HALF_KEYS, sub, win16) for q0 in q16],
                   qi - np.arange(win16)[None, :], 1)
    q_rows, k_rows = Q_BLOCK // 4, Q_BLOCK // 4 + 2 * HALF_KEYS // 4
    kc = np.arange(4 * k_rows)[None, :]
    q4 = list(range(0, sub, q_rows))
    p4 = _Pattern(q4, [_window_start(q0, HALF_KEYS // 4, sub, k_rows) for q0 in q4],
                  4 * (qi % q_rows - kc % k_rows) + (qi // q_rows - kc // k_rows), 4)
    win1 = Q_BLOCK + 2 * HALF_KEYS
    q1 = list(range(0, seq, Q_BLOCK))
    p1 = _Pattern(q1, [_window_start(q0, HALF_KEYS, seq, win1) for q0 in q1],
                  RADIX * (qi % (Q_BLOCK // RADIX)) + qi // (Q_BLOCK // RADIX) - np.arange(win1)[None, :], 1)
    return p16, p4, p1


def _attn_kernel(q16_ref, k1_ref, v1_ref, k16_ref, v16_ref, b16_ref, b4_ref, b1_ref,
                 o_ref, acc_s, m_s, l_s, p0_s, w0_s, p1_s, w1_s, *, pat16, pat4, pat1):
    def gather(ref, pieces):
        return jnp.concatenate([ref[r, pl.ds(s0, n), :] for r, s0, n in pieces], axis=0)

    def scatter(ref, pieces, value):
        off = 0
        for r, s0, n in pieces:
            ref[r, pl.ds(s0, n), :] = value[off:off + n]
            off += n

    def scores_stage(blocks, first, p_scr, w_scr):
        for i, (q, k, _, bias, pieces, (_, m_h, _)) in enumerate(blocks):
            s = lax.dot_general(q(), k(), (((1,), (1,)), ((), ())), preferred_element_type=F32)
            s = s.astype(BF16) + bias()
            m_new = jnp.broadcast_to(jnp.max(s, axis=-1, keepdims=True), (Q_BLOCK, HEAD_DIM)).astype(F32)
            if not first:
                m_old = gather(m_h, pieces)
                m_new = jnp.maximum(m_old, m_new)
                w_scr[i] = jnp.exp2(m_old - m_new)
            width = s.shape[1]
            m_b = m_new.astype(BF16)
            p_scr[i, :, :width] = jnp.exp2(s - jnp.concatenate([m_b] * (width // HEAD_DIM), axis=1))
            scatter(m_h, pieces, m_new)

    def values_stage(blocks, first, p_scr, w_scr):
        for i, (_, _, v, _, pieces, (acc_h, _, l_h)) in enumerate(blocks):
            vw = v()
            v_aug = jnp.concatenate([vw, jnp.ones(vw.shape, BF16)], axis=1)
            pv = jnp.dot(p_scr[i, :, :vw.shape[0]], v_aug, preferred_element_type=F32)
            acc, l = pv[:, :HEAD_DIM], pv[:, HEAD_DIM:]
            if not first:
                w_old = w_scr[i]
                acc = w_old * gather(acc_h, pieces) + acc
                l = w_old * gather(l_h, pieces) + l
            scatter(l_h, pieces, l)
            scatter(acc_h, pieces, acc)

    win16 = b16_ref.shape[2]
    q_rows4 = Q_BLOCK // 4
    k_rows4 = b4_ref.shape[2] // 4
    q_rows1 = Q_BLOCK // RADIX
    win1 = b1_ref.shape[2]
    blocks16, blocks4, blocks1 = [], [], []
    for hd in range(q16_ref.shape[0]):
        q16, k1, v1, k16, v16 = (ref.at[hd] for ref in (q16_ref, k1_ref, v1_ref, k16_ref, v16_ref))
        state = (acc_s.at[hd], m_s.at[hd], l_s.at[hd])

        for r in range(RADIX):
            for blk, (l0, ws) in enumerate(zip(pat16.q_first, pat16.k_first)):
                blocks16.append((
                    lambda q16=q16, r=r, l0=l0: q16[r, l0:l0 + Q_BLOCK, :].astype(BF16),
                    lambda k16=k16, r=r, ws=ws: k16[r, ws:ws + win16, :],
                    lambda v16=v16, r=r, ws=ws: v16[r, ws:ws + win16, :],
                    lambda blk=blk: b16_ref[pat16.table_of[blk]],
                    [(r, l0, Q_BLOCK)], state))

        for r4 in range(4):
            res = [r4 + 4 * a for a in range(4)]
            for blk, (l0, ws) in enumerate(zip(pat4.q_first, pat4.k_first)):
                q_pieces = [(r, l0, q_rows4) for r in res]
                k_pieces = [(r, ws, k_rows4) for r in res]
                blocks4.append((
                    lambda q16=q16, p=q_pieces: gather(q16, p).astype(BF16),
                    lambda k16=k16, p=k_pieces: gather(k16, p),
                    lambda v16=v16, p=k_pieces: gather(v16, p),
                    lambda blk=blk: b4_ref[pat4.table_of[blk]],
                    q_pieces, state))

        for blk, (n0, ws) in enumerate(zip(pat1.q_first, pat1.k_first)):
            q_pieces = [(r, n0 // RADIX, q_rows1) for r in range(RADIX)]
            blocks1.append((
                lambda q16=q16, p=q_pieces: gather(q16, p).astype(BF16),
                lambda k1=k1, ws=ws: k1[ws:ws + win1, :],
                lambda v1=v1, ws=ws: v1[ws:ws + win1, :],
                lambda blk=blk: b1_ref[pat1.table_of[blk]],
                q_pieces, state))

    groups = []
    for blocks, first in ((blocks16, True), (blocks4, False), (blocks1, False)):
        groups.extend((blocks[i:i + BLOCKS_PER_GROUP], first) for i in range(0, len(blocks), BLOCKS_PER_GROUP))

    slots = ((p0_s, w0_s), (p1_s, w1_s))
    scores_stage(*groups[0], *slots[0])
    for g in range(1, len(groups)):
        scores_stage(*groups[g], *slots[g % 2])
        values_stage(*groups[g - 1], *slots[(g - 1) % 2])
    values_stage(*groups[-1], *slots[(len(groups) - 1) % 2])

    def finish(i, carry):
        hd, r = i // RADIX, i % RADIX
        o_ref[hd, r] = (acc_s[hd, r] * (1.0 / l_s[hd, r])).astype(BF16)
        return carry

    lax.fori_loop(0, q16_ref.shape[0] * RADIX, finish, 0)


def _attention(q16, k1, v1, k16, v16, batch, seq):
    sub = seq // RADIX
    heads = ATTN_HEADS_PER_STEP
    pat16, pat4, pat1 = _attention_patterns(seq)
    m16_spec = pl.BlockSpec((heads, None, RADIX, sub, HEAD_DIM), lambda b, h: (h, b, 0, 0, 0))
    nat_spec = pl.BlockSpec((heads, seq, HEAD_DIM), lambda b, h: (h, b, 0))
    state = pltpu.VMEM((heads, RADIX, sub, HEAD_DIM), F32)
    probs = pltpu.VMEM((BLOCKS_PER_GROUP, Q_BLOCK, Q_BLOCK + 2 * HALF_KEYS), BF16)
    rescale = pltpu.VMEM((BLOCKS_PER_GROUP, Q_BLOCK, HEAD_DIM), F32)
    return pl.pallas_call(
        functools.partial(_attn_kernel, pat16=pat16, pat4=pat4, pat1=pat1),
        grid=(batch, N_HEADS // heads),
        in_specs=[m16_spec, nat_spec, nat_spec, m16_spec, m16_spec,
                  _resident(pat16.bias.shape), _resident(pat4.bias.shape), _resident(pat1.bias.shape)],
        out_specs=m16_spec,
        out_shape=jax.ShapeDtypeStruct((N_HEADS, batch, RADIX, sub, HEAD_DIM), BF16),
        scratch_shapes=[state, state, state, probs, rescale, probs, rescale],
        compiler_params=_params("parallel", "parallel"),
        name="attention",
    )(q16, k1, v1, k16, v16, pat16.bias, pat4.bias, pat1.bias)


def _chanmat_kernel(c_ref, s_ref, w_ref, m_ref):
    for g in range(N_GROUPS):
        w = w_ref[g]
        mc = jnp.dot(c_ref[...], w, preferred_element_type=F32, precision=lax.Precision.HIGHEST)
        ms = jnp.dot(s_ref[...], w, preferred_element_type=F32, precision=lax.Precision.HIGHEST)
        m_ref[g, :, :GROUP_DIM] = mc.astype(BF16)
        m_ref[g, :, GROUP_DIM:] = ms.astype(BF16)


def _channel_matrices(w_fourier):
    idx = np.arange(GROUP_DIM)
    cc, sc = _dft_cos_sin(GROUP_DIM, idx, idx, GROUP_DIM ** -0.5)
    return pl.pallas_call(
        _chanmat_kernel,
        out_shape=jax.ShapeDtypeStruct((N_GROUPS, GROUP_DIM, 2 * GROUP_DIM), BF16),
        name="chanmat",
    )(jnp.asarray(cc, F32), jnp.asarray(sc, F32), w_fourier)


def _fourier_kernel(u_ref, m_ref, c_ref, smc_ref, cps_ref, mc_ref, ms_ref, gf_ref, o_ref, yr_s, yi_s,
                    *, stage1_steps, tiles):
    t = pl.program_id(1)
    dot = functools.partial(jnp.dot, preferred_element_type=F32)
    per_step = u_ref.shape[0]

    @pl.when(t < stage1_steps)
    def _():
        z = []
        for j in range(per_step):
            a_parts, b_parts, ab_parts = [], [], []
            for g in range(N_GROUPS):
                ab = dot(u_ref[j, :, g * GROUP_DIM:(g + 1) * GROUP_DIM], m_ref[g])
                a, b = ab[:, :GROUP_DIM], ab[:, GROUP_DIM:]
                a_parts.append(a.astype(BF16))
                b_parts.append(b.astype(BF16))
                ab_parts.append((a + b).astype(BF16))
            z.append(tuple(jnp.concatenate(p, axis=1) for p in (a_parts, b_parts, ab_parts)))
        c, s_minus_c, c_plus_s = c_ref[...], smc_ref[...], cps_ref[...]
        for j, (a, b, a_plus_b) in enumerate(z):
            n2 = t * per_step + j
            k1 = dot(c, a_plus_b)
            yr_s[n2] = (k1 - dot(c_plus_s, b)).astype(BF16)
            yi_s[n2] = (k1 + dot(s_minus_c, a)).astype(BF16)

    @pl.when(t >= stage1_steps)
    def _():
        first_row = (t - stage1_steps) * (tiles * RADIX)
        for i in range(tiles):
            rows = pl.ds(pl.multiple_of(first_row + i * RADIX, RADIX), RADIX)
            yr = jnp.concatenate([yr_s[n2, rows, :] for n2 in range(RADIX)], axis=0)
            yi = jnp.concatenate([yi_s[n2, rows, :] for n2 in range(RADIX)], axis=0)
            x = dot(mc_ref[i], yr) + dot(ms_ref[i], yi)
            out_rows = slice(i * RADIX, (i + 1) * RADIX)
            for k2 in range(RADIX):
                piece = x[k2 * RADIX:(k2 + 1) * RADIX, :] * gf_ref[k2, out_rows, :].astype(F32)
                o_ref[k2, out_rows, :] = piece.astype(BF16)


def _stage2_matrices(seq):
    s1 = seq // RADIX
    tiles = s1 // RADIX
    c, k2, k1, n2 = np.meshgrid(np.arange(tiles), np.arange(RADIX), np.arange(RADIX), np.arange(RADIX),
                                indexing="ij")
    k = RADIX * c + k1 + s1 * k2
    ang = 2.0 * np.pi * ((n2 * k) % seq).astype(np.float64) / seq
    mc = np.zeros((tiles, ROW_GROUP, ROW_GROUP), np.float32)
    ms = np.zeros((tiles, ROW_GROUP, ROW_GROUP), np.float32)
    mc[c, RADIX * k2 + k1, RADIX * n2 + k1] = np.cos(ang) * seq ** -0.5
    ms[c, RADIX * k2 + k1, RADIX * n2 + k1] = -np.sin(ang) * seq ** -0.5
    return _bf16_const(mc), _bf16_const(ms)


def _fourier(u16, chan_mats, gf, batch, seq, tiles=FOURIER2_TILES):
    s1 = seq // RADIX
    idx = np.arange(s1)
    c1, sn1 = _dft_cos_sin(s1, idx, idx)
    mc, ms = _stage2_matrices(seq)
    per_step = FOURIER1_ROWS // s1
    stage1_steps = RADIX // per_step
    rows = tiles * RADIX
    stage2 = lambda t: jnp.maximum(t - stage1_steps, 0)
    nat_spec = pl.BlockSpec((None, RADIX, rows, FOURIER_WIDTH), lambda b, t: (b, 0, stage2(t), 0))
    m_spec = pl.BlockSpec((tiles, ROW_GROUP, ROW_GROUP), lambda b, t: (stage2(t), 0, 0))
    y_scratch = pltpu.VMEM((RADIX, s1, FOURIER_WIDTH), BF16)
    out = pl.pallas_call(
        functools.partial(_fourier_kernel, stage1_steps=stage1_steps, tiles=tiles),
        grid=(batch, stage1_steps + s1 // rows),
        in_specs=[pl.BlockSpec((None, per_step, s1, FOURIER_WIDTH),
                               lambda b, t: (b, jnp.minimum(t, stage1_steps - 1), 0, 0)),
                  _resident(chan_mats.shape), _resident((s1, s1)), _resident((s1, s1)), _resident((s1, s1)),
                  m_spec, m_spec, nat_spec],
        out_specs=nat_spec,
        out_shape=jax.ShapeDtypeStruct((batch, RADIX, s1, FOURIER_WIDTH), BF16),
        scratch_shapes=[y_scratch, y_scratch],
        compiler_params=_params("parallel", "arbitrary"),
        name="fourier",
    )(u16, chan_mats, _bf16_const(c1), _bf16_const(sn1 - c1), _bf16_const(c1 + sn1), mc, ms,
      gf.reshape(batch, RADIX, s1, FOURIER_WIDTH))
    return out.reshape(batch * seq, FOURIER_WIDTH)


def _outproj_kernel(x_ref, o16_ref, ga_ref, mf_ref, unperm_ref, w_ref, y_ref):
    unperm = unperm_ref[...]
    attn = []
    for g in range(OUTPROJ_GROUPS):
        rows = slice(g * RADIX, (g + 1) * RADIX)
        heads = [jnp.concatenate([o16_ref[hd, r, rows, :] for r in range(RADIX)], axis=0)
                 for hd in range(N_HEADS)]
        attn16 = jnp.concatenate(heads, axis=1)
        attn.append(jnp.dot(unperm, attn16, preferred_element_type=F32))
    y = x_ref[...] + jnp.dot(mf_ref[...], w_ref[ATTN_WIDTH:, :], preferred_element_type=F32)
    mix_a = (jnp.concatenate(attn, axis=0) * ga_ref[...].astype(F32)).astype(BF16)
    y_ref[...] = y + jnp.dot(mix_a, w_ref[:ATTN_WIDTH, :], preferred_element_type=F32)


def _outproj(x2d, o16, ga, mix_f, unperm, w_out, seq):
    t = x2d.shape[0]
    rows = OUTPROJ_GROUPS * ROW_GROUP
    steps_per_seq = seq // rows
    tok_spec = pl.BlockSpec((rows, ATTN_WIDTH), lambda i: (i, 0))
    x_spec = pl.BlockSpec((rows, D_MODEL), lambda i: (i, 0))
    return pl.pallas_call(
        _outproj_kernel,
        grid=(t // rows,),
        in_specs=[x_spec,
                  pl.BlockSpec((N_HEADS, None, RADIX, OUTPROJ_GROUPS * RADIX, HEAD_DIM),
                               lambda i: (0, i // steps_per_seq, 0, i % steps_per_seq, 0)),
                  tok_spec, tok_spec,
                  _resident((ROW_GROUP, ROW_GROUP)),
                  _resident((D_MODEL, D_MODEL))],
        out_specs=x_spec,
        out_shape=jax.ShapeDtypeStruct((t, D_MODEL), F32),
        compiler_params=_params("arbitrary"),
        name="outproj",
    )(x2d, o16, ga, mix_f, unperm, w_out)


def _layer(x, gain, w_in, qg, kg, chan_mats, w_out, perm, unperm):
    batch, seq, _ = x.shape
    x2d = x.reshape(batch * seq, D_MODEL)
    q16, k1, k16, v1, v16, ga, u16, gf = _inproj(x2d, batch, seq, gain, w_in, qg, kg, perm)
    o16 = _attention(q16, k1, v1, k16, v16, batch, seq)
    mix_f = _fourier(u16, chan_mats, gf, batch, seq)
    y = _outproj(x2d, o16, ga, mix_f, unperm, w_out, seq)
    return y.reshape(batch, seq, D_MODEL)


def kernel(x_prompt, x_sample, rms_gain, w_in, q_norm_gain, k_norm_gain, w_fourier, w_out):
    depth = rms_gain.shape[0]
    p = _group_permutation()
    perm, unperm = _bf16_const(p), _bf16_const(p.T)
    for l in range(depth):
        gain = rms_gain[l].reshape(1, D_MODEL)
        w_in_l = w_in[l].astype(BF16)
        w_out_l = w_out[l].astype(BF16)
        qg = q_norm_gain[l].reshape(1, HEAD_DIM)
        kg = k_norm_gain[l].reshape(1, HEAD_DIM)
        chan_mats = _channel_matrices(w_fourier[l])
        x_prompt = _layer(x_prompt, gain, w_in_l, qg, kg, chan_mats, w_out_l, perm, unperm)
        x_sample = _layer(x_sample, gain, w_in_l, qg, kg, chan_mats, w_out_l, perm, unperm)
    return (x_prompt, x_sample)
```

```python
import functools
import math

import jax
import jax.numpy as jnp
import numpy as np
from jax import lax
from jax.experimental import pallas as pl
from jax.experimental.pallas import tpu as pltpu

D_MODEL = 2048
ATTN_WIDTH = 1024
FOURIER_WIDTH = 1024
HEAD_DIM = 128
N_HEADS = ATTN_WIDTH // HEAD_DIM
N_GROUPS = 4
GROUP_DIM = FOURIER_WIDTH // N_GROUPS
ROPE_THETA = 500000.0
ROPE_DIM = HEAD_DIM // 4
ROPE_HALF = ROPE_DIM // 2
HALF_KEYS = 64
RMS_EPS = 1e-6
RADIX = 16
ROW_GROUP = RADIX * RADIX
Q_BLOCK = 128
FOURIER1_ROWS = 1024
FOURIER2_TILES = 8
ATTN_HEADS_PER_STEP = 2
INPROJ_GROUPS = 1
OUTPROJ_GROUPS = 2
INPROJ_WEIGHT_CHUNK_ROWS = 64
OUTPROJ_WEIGHT_CHUNK_ROWS = 256
BLOCKS_PER_GROUP = 4
MASK_VALUE = -1e30

VMEM_LIMIT_BYTES = 56 * 1024 * 1024

F32 = jnp.float32
BF16 = jnp.bfloat16


def _params(*semantics):
    return pltpu.CompilerParams(dimension_semantics=semantics, vmem_limit_bytes=VMEM_LIMIT_BYTES)


def _resident(shape):
    return pl.BlockSpec(shape, lambda *_: (0,) * len(shape), pipeline_mode=pl.Buffered(1))


def _rope_tables(seq, scale):
    expo = np.arange(ROPE_HALF, dtype=np.float32) / np.float32(ROPE_HALF)
    inv_freq = (np.float32(1.0) / np.power(np.float32(ROPE_THETA), expo)).astype(np.float32)
    ang = (np.arange(seq, dtype=np.float32)[:, None] * inv_freq[None, :]).astype(np.float64)
    cos, sin = np.cos(ang), np.sin(ang)
    a = np.ones((seq, HEAD_DIM))
    b = np.zeros((seq, HEAD_DIM))
    c = np.zeros((seq, HEAD_DIM))
    a[:, :ROPE_HALF] = cos
    a[:, ROPE_HALF:ROPE_DIM] = cos
    b[:, ROPE_HALF:ROPE_DIM] = sin
    c[:, :ROPE_HALF] = -sin
    return tuple(jnp.asarray((t * scale).astype(np.float32)) for t in (a, b, c))


def _dft_cos_sin(n, rows, cols, scale=1.0):
    m = (np.asarray(rows, dtype=np.int64)[:, None] * np.asarray(cols, dtype=np.int64)[None, :]) % n
    ang = 2.0 * np.pi * m.astype(np.float64) / n
    return np.cos(ang) * scale, np.sin(ang) * scale


def _group_permutation():
    p = np.zeros((ROW_GROUP, ROW_GROUP), np.float32)
    j, r = np.meshgrid(np.arange(RADIX), np.arange(RADIX), indexing="ij")
    p[(RADIX * r + j).ravel(), (RADIX * j + r).ravel()] = 1.0
    return p


def _bf16_const(a):
    return jnp.asarray(np.asarray(a, np.float32)).astype(BF16)


def _silu(a):
    return a * (1.0 / (1.0 + jnp.exp(-a)))


def _load_weight_as_bf16(w_hbm, w_ref, stage, sem):
    @pl.when(pl.program_id(0) == 0)
    def _():
        chunk = stage.shape[1]
        n_chunks = w_ref.shape[0] // chunk

        def copy(c):
            return pltpu.make_async_copy(w_hbm.at[pl.ds(c * chunk, chunk), :], stage.at[c % 2], sem.at[c % 2])

        copy(0).start()
        for c in range(n_chunks):
            if c + 1 < n_chunks:
                copy(c + 1).start()
            copy(c).wait()
            w_ref[c * chunk:(c + 1) * chunk, :] = stage[c % 2].astype(BF16)


def _weight_scratch(w, chunk_rows):
    return [pltpu.VMEM(w.shape, BF16), pltpu.VMEM((2, chunk_rows, w.shape[1]), F32),
            pltpu.SemaphoreType.DMA((2,))]


def _inproj_kernel(x_ref, gain_ref, w_hbm, qg_ref, kg_ref,
                   qa_ref, qb_ref, qc_ref, ka_ref, kb_ref, kc_ref, perm_ref,
                   q16_ref, k1_ref, k16_ref, v1_ref, v16_ref, ga_ref, u16_ref, gf_ref,
                   w_ref, stage, sem):
    _load_weight_as_bf16(w_hbm, w_ref, stage, sem)
    x = x_ref[...]
    ms = jnp.mean(x * x, axis=-1, keepdims=True)
    h = (x * lax.rsqrt(ms + RMS_EPS) * gain_ref[...]).astype(BF16)
    perm = perm_ref[...]

    def column_tile(j):
        cols = slice(j * ATTN_WIDTH, (j + 1) * ATTN_WIDTH)
        return jnp.dot(h, w_ref[:, cols], preferred_element_type=F32)

    def norm_rope(acc, g_ref, a_ref, b_ref, c_ref):
        heads = []
        for hd in range(N_HEADS):
            a = acc[:, hd * HEAD_DIM:(hd + 1) * HEAD_DIM]
            ms_h = jnp.mean(a * a, axis=-1, keepdims=True)
            n = a * lax.rsqrt(ms_h + RMS_EPS) * g_ref[...]
            r = (n * a_ref[...]
                 + pltpu.roll(n, ROPE_HALF, 1) * b_ref[...]
                 + pltpu.roll(n, HEAD_DIM - ROPE_HALF, 1) * c_ref[...])
            heads.append(r.astype(BF16))
        return jnp.concatenate(heads, axis=1)

    def to_mod16(t):
        return [jnp.dot(perm, t[g * ROW_GROUP:(g + 1) * ROW_GROUP], preferred_element_type=F32)
                for g in range(INPROJ_GROUPS)]

    def store_heads(ref, t):
        for hd in range(N_HEADS):
            ref[hd] = t[:, hd * HEAD_DIM:(hd + 1) * HEAD_DIM]

    def store_head_pieces(ref, groups):
        for g, t in enumerate(groups):
            for hd in range(N_HEADS):
                for r in range(RADIX):
                    ref[hd, r, g * RADIX:(g + 1) * RADIX, :] = (
                        t[r * RADIX:(r + 1) * RADIX, hd * HEAD_DIM:(hd + 1) * HEAD_DIM].astype(ref.dtype))

    acc_q = column_tile(0)
    acc_k = column_tile(1)
    q = norm_rope(acc_q, qg_ref, qa_ref, qb_ref, qc_ref)
    acc_v = column_tile(2)
    store_head_pieces(q16_ref, to_mod16(q))

    k = norm_rope(acc_k, kg_ref, ka_ref, kb_ref, kc_ref)
    acc_ga = column_tile(3)
    store_heads(k1_ref, k)
    store_head_pieces(k16_ref, to_mod16(k))

    v = acc_v.astype(BF16)
    acc_u = column_tile(4)
    store_heads(v1_ref, v)
    store_head_pieces(v16_ref, to_mod16(v))

    ga_ref[...] = _silu(acc_ga).astype(BF16)
    acc_gf = column_tile(5)

    for g, u16 in enumerate(to_mod16(acc_u.astype(BF16))):
        for r in range(RADIX):
            u16_ref[r, g * RADIX:(g + 1) * RADIX, :] = u16[r * RADIX:(r + 1) * RADIX, :].astype(BF16)

    gf_ref[...] = _silu(acc_gf).astype(BF16)


def _inproj(x2d, batch, seq, gain, w_in, qg, kg, perm):
    t = x2d.shape[0]
    rows = INPROJ_GROUPS * ROW_GROUP
    pieces = INPROJ_GROUPS * RADIX
    groups = seq // rows
    sub = seq // RADIX
    qa, qb, qc = _rope_tables(seq, HEAD_DIM ** -0.5 * math.log2(math.e))
    ka, kb, kc = _rope_tables(seq, 1.0)
    rope_spec = pl.BlockSpec((rows, HEAD_DIM), lambda i: (i % groups, 0))
    tok_spec = pl.BlockSpec((rows, ATTN_WIDTH), lambda i: (i, 0))
    head_nat_spec = pl.BlockSpec((N_HEADS, rows, HEAD_DIM), lambda i: (0, i, 0))
    head_m16_spec = pl.BlockSpec((N_HEADS, None, RADIX, pieces, HEAD_DIM),
                                 lambda i: (0, i // groups, 0, i % groups, 0))
    head_nat = jax.ShapeDtypeStruct((N_HEADS, t, HEAD_DIM), BF16)
    head_m16 = lambda dt: jax.ShapeDtypeStruct((N_HEADS, batch, RADIX, sub, HEAD_DIM), dt)
    return pl.pallas_call(
        _inproj_kernel,
        grid=(t // rows,),
        in_specs=[
            pl.BlockSpec((rows, D_MODEL), lambda i: (i, 0)),
            _resident((1, D_MODEL)),
            pl.BlockSpec(memory_space=pl.ANY),
            _resident((1, HEAD_DIM)), _resident((1, HEAD_DIM)),
            rope_spec, rope_spec, rope_spec, rope_spec, rope_spec, rope_spec,
            _resident((ROW_GROUP, ROW_GROUP)),
        ],
        out_specs=[
            head_m16_spec, head_nat_spec, head_m16_spec, head_nat_spec, head_m16_spec,
            tok_spec,
            pl.BlockSpec((None, RADIX, pieces, FOURIER_WIDTH), lambda i: (i // groups, 0, i % groups, 0)),
            tok_spec,
        ],
        out_shape=[
            head_m16(F32), head_nat, head_m16(BF16), head_nat, head_m16(BF16),
            jax.ShapeDtypeStruct((t, ATTN_WIDTH), BF16),
            jax.ShapeDtypeStruct((batch, RADIX, sub, FOURIER_WIDTH), BF16),
            jax.ShapeDtypeStruct((t, FOURIER_WIDTH), BF16),
        ],
        scratch_shapes=_weight_scratch(w_in, INPROJ_WEIGHT_CHUNK_ROWS),
        compiler_params=_params("arbitrary"),
        name="inproj",
    )(x2d, gain, w_in, qg, kg, qa, qb, qc, ka, kb, kc, perm)


def _window_start(first, half, total, window):
    return min(max(first - half, 0), total - window)


class _Pattern:
    def __init__(self, q_first, k_first, diff0, scale):
        deltas = [scale * (q0 - k0) for q0, k0 in zip(q_first, k_first)]
        uniq = sorted(set(deltas))
        self.q_first, self.k_first = q_first, k_first
        self.table_of = [uniq.index(d) for d in deltas]
        self.bias = jnp.asarray(np.stack(
            [np.where(np.abs(diff0 + d) <= HALF_KEYS, 0.0, MASK_VALUE) for d in uniq]).astype(np.float32)
        ).astype(BF16)


def _attention_patterns(seq):
    sub = seq // RADIX
    qi = np.arange(Q_BLOCK)[:, None]
    win16 = min(Q_BLOCK + 2 * HALF_KEYS, sub)
    q16 = list(range(0, sub, Q_BLOCK))
    p16 = _Pattern(q16, [_window_start(q0, HALF_KEYS, sub, win16) for q0 in q16],
                   qi - np.arange(win16)[None, :], 1)
    q_rows, k_rows = Q_BLOCK // 4, Q_BLOCK // 4 + 2 * HALF_KEYS // 4
    kc = np.arange(4 * k_rows)[None, :]
    q4 = list(range(0, sub, q_rows))
    p4 = _Pattern(q4, [_window_start(q0, HALF_KEYS // 4, sub, k_rows) for q0 in q4],
                  4 * (qi % q_rows - kc % k_rows) + (qi // q_rows - kc // k_rows), 4)
    win1 = Q_BLOCK + 2 * HALF_KEYS
    q1 = list(range(0, seq, Q_BLOCK))
    p1 = _Pattern(q1, [_window_start(q0, HALF_KEYS, seq, win1) for q0 in q1],
                  RADIX * (qi % (Q_BLOCK // RADIX)) + qi // (Q_BLOCK // RADIX) - np.arange(win1)[None, :], 1)
    return p16, p4, p1


def _attn_kernel(q16_ref, k1_ref, v1_ref, k16_ref, v16_ref, b16_ref, b4_ref, b1_ref,
                 o_ref, acc_s, m_s, l_s, p0_s, w0_s, p1_s, w1_s, *, pat16, pat4, pat1):
    def gather(ref, pieces):
        return jnp.concatenate([ref[r, pl.ds(s0, n), :] for r, s0, n in pieces], axis=0)

    def scatter(ref, pieces, value):
        off = 0
        for r, s0, n in pieces:
            ref[r, pl.ds(s0, n), :] = value[off:off + n]
            off += n

    def scores_stage(blocks, first, p_scr, w_scr):
        for i, (q, k, _, bias, pieces, (_, m_h, _)) in enumerate(blocks):
            s = lax.dot_general(q(), k(), (((1,), (1,)), ((), ())), preferred_element_type=F32)
            s = s.astype(BF16) + bias()
            m_new = jnp.broadcast_to(jnp.max(s, axis=-1, keepdims=True), (Q_BLOCK, HEAD_DIM)).astype(F32)
            if not first:
                m_old = gather(m_h, pieces)
                m_new = jnp.maximum(m_old, m_new)
                w_scr[i] = jnp.exp2(m_old - m_new)
            width = s.shape[1]
            m_b = m_new.astype(BF16)
            p_scr[i, :, :width] = jnp.exp2(s - jnp.concatenate([m_b] * (width // HEAD_DIM), axis=1))
            scatter(m_h, pieces, m_new)

    def values_stage(blocks, first, p_scr, w_scr):
        for i, (_, _, v, _, pieces, (acc_h, _, l_h)) in enumerate(blocks):
            vw = v()
            v_aug = jnp.concatenate([vw, jnp.ones(vw.shape, BF16)], axis=1)
            pv = jnp.dot(p_scr[i, :, :vw.shape[0]], v_aug, preferred_element_type=F32)
            acc, l = pv[:, :HEAD_DIM], pv[:, HEAD_DIM:]
            if not first:
                w_old = w_scr[i]
                acc = w_old * gather(acc_h, pieces) + acc
                l = w_old * gather(l_h, pieces) + l
            scatter(l_h, pieces, l)
            scatter(acc_h, pieces, acc)

    win16 = b16_ref.shape[2]
    q_rows4 = Q_BLOCK // 4
    k_rows4 = b4_ref.shape[2] // 4
    q_rows1 = Q_BLOCK // RADIX
    win1 = b1_ref.shape[2]
    blocks16, blocks4, blocks1 = [], [], []
    for hd in range(q16_ref.shape[0]):
        q16, k1, v1, k16, v16 = (ref.at[hd] for ref in (q16_ref, k1_ref, v1_ref, k16_ref, v16_ref))
        state = (acc_s.at[hd], m_s.at[hd], l_s.at[hd])

        for r in range(RADIX):
            for blk, (l0, ws) in enumerate(zip(pat16.q_first, pat16.k_first)):
                blocks16.append((
                    lambda q16=q16, r=r, l0=l0: q16[r, l0:l0 + Q_BLOCK, :].astype(BF16),
                    lambda k16=k16, r=r, ws=ws: k16[r, ws:ws + win16, :],
                    lambda v16=v16, r=r, ws=ws: v16[r, ws:ws + win16, :],
                    lambda blk=blk: b16_ref[pat16.table_of[blk]],
                    [(r, l0, Q_BLOCK)], state))

        for r4 in range(4):
            res = [r4 + 4 * a for a in range(4)]
            for blk, (l0, ws) in enumerate(zip(pat4.q_first, pat4.k_first)):
                q_pieces = [(r, l0, q_rows4) for r in res]
                k_pieces = [(r, ws, k_rows4) for r in res]
                blocks4.append((
                    lambda q16=q16, p=q_pieces: gather(q16, p).astype(BF16),
                    lambda k16=k16, p=k_pieces: gather(k16, p),
                    lambda v16=v16, p=k_pieces: gather(v16, p),
                    lambda blk=blk: b4_ref[pat4.table_of[blk]],
                    q_pieces, state))

        for blk, (n0, ws) in enumerate(zip(pat1.q_first, pat1.k_first)):
            q_pieces = [(r, n0 // RADIX, q_rows1) for r in range(RADIX)]
            blocks1.append((
                lambda q16=q16, p=q_pieces: gather(q16, p).astype(BF16),
                lambda k1=k1, ws=ws: k1[ws:ws + win1, :],
                lambda v1=v1, ws=ws: v1[ws:ws + win1, :],
                lambda blk=blk: b1_ref[pat1.table_of[blk]],
                q_pieces, state))

    groups = []
    for blocks, first in ((blocks16, True), (blocks4, False), (blocks1, False)):
        groups.extend((blocks[i:i + BLOCKS_PER_GROUP], first) for i in range(0, len(blocks), BLOCKS_PER_GROUP))

    slots = ((p0_s, w0_s), (p1_s, w1_s))
    scores_stage(*groups[0], *slots[0])
    for g in range(1, len(groups)):
        scores_stage(*groups[g], *slots[g % 2])
        values_stage(*groups[g - 1], *slots[(g - 1) % 2])
    values_stage(*groups[-1], *slots[(len(groups) - 1) % 2])

    def finish(i, carry):
        hd, r = i // RADIX, i % RADIX
        o_ref[hd, r] = (acc_s[hd, r] * (1.0 / l_s[hd, r])).astype(BF16)
        return carry

    lax.fori_loop(0, q16_ref.shape[0] * RADIX, finish, 0)


def _attention(q16, k1, v1, k16, v16, batch, seq):
    sub = seq // RADIX
    heads = ATTN_HEADS_PER_STEP
    pat16, pat4, pat1 = _attention_patterns(seq)
    m16_spec = pl.BlockSpec((heads, None, RADIX, sub, HEAD_DIM), lambda b, h: (h, b, 0, 0, 0))
    nat_spec = pl.BlockSpec((heads, seq, HEAD_DIM), lambda b, h: (h, b, 0))
    state = pltpu.VMEM((heads, RADIX, sub, HEAD_DIM), F32)
    probs = pltpu.VMEM((BLOCKS_PER_GROUP, Q_BLOCK, Q_BLOCK + 2 * HALF_KEYS), BF16)
    rescale = pltpu.VMEM((BLOCKS_PER_GROUP, Q_BLOCK, HEAD_DIM), F32)
    return pl.pallas_call(
        functools.partial(_attn_kernel, pat16=pat16, pat4=pat4, pat1=pat1),
        grid=(batch, N_HEADS // heads),
        in_specs=[m16_spec, nat_spec, nat_spec, m16_spec, m16_spec,
                  _resident(pat16.bias.shape), _resident(pat4.bias.shape), _resident(pat1.bias.shape)],
        out_specs=m16_spec,
        out_shape=jax.ShapeDtypeStruct((N_HEADS, batch, RADIX, sub, HEAD_DIM), BF16),
        scratch_shapes=[state, state, state, probs, rescale, probs, rescale],
        compiler_params=_params("parallel", "parallel"),
        name="attention",
    )(q16, k1, v1, k16, v16, pat16.bias, pat4.bias, pat1.bias)


def _chanmat_kernel(c_ref, s_ref, w_ref, m_ref):
    for g in range(N_GROUPS):
        w = w_ref[g]
        mc = jnp.dot(c_ref[...], w, preferred_element_type=F32, precision=lax.Precision.HIGHEST)
        ms = jnp.dot(s_ref[...], w, preferred_element_type=F32, precision=lax.Precision.HIGHEST)
        m_ref[g, :, :GROUP_DIM] = mc.astype(BF16)
        m_ref[g, :, GROUP_DIM:] = ms.astype(BF16)


def _channel_matrices(w_fourier):
    idx = np.arange(GROUP_DIM)
    cc, sc = _dft_cos_sin(GROUP_DIM, idx, idx, GROUP_DIM ** -0.5)
    return pl.pallas_call(
        _chanmat_kernel,
        out_shape=jax.ShapeDtypeStruct((N_GROUPS, GROUP_DIM, 2 * GROUP_DIM), BF16),
        name="chanmat",
    )(jnp.asarray(cc, F32), jnp.asarray(sc, F32), w_fourier)


def _fourier_kernel(u_ref, m_ref, c_ref, smc_ref, cps_ref, mc_ref, ms_ref, gf_ref, o_ref, yr_s, yi_s,
                    *, stage1_steps, tiles):
    t = pl.program_id(1)
    dot = functools.partial(jnp.dot, preferred_element_type=F32)
    per_step = u_ref.shape[0]

    @pl.when(t < stage1_steps)
    def _():
        z = []
        for j in range(per_step):
            a_parts, b_parts, ab_parts = [], [], []
            for g in range(N_GROUPS):
                ab = dot(u_ref[j, :, g * GROUP_DIM:(g + 1) * GROUP_DIM], m_ref[g])
                a, b = ab[:, :GROUP_DIM], ab[:, GROUP_DIM:]
                a_parts.append(a.astype(BF16))
                b_parts.append(b.astype(BF16))
                ab_parts.append((a + b).astype(BF16))
            z.append(tuple(jnp.concatenate(p, axis=1) for p in (a_parts, b_parts, ab_parts)))
        c, s_minus_c, c_plus_s = c_ref[...], smc_ref[...], cps_ref[...]
        for j, (a, b, a_plus_b) in enumerate(z):
            n2 = t * per_step + j
            k1 = dot(c, a_plus_b)
            yr_s[n2] = (k1 - dot(c_plus_s, b)).astype(BF16)
            yi_s[n2] = (k1 + dot(s_minus_c, a)).astype(BF16)

    @pl.when(t >= stage1_steps)
    def _():
        first_row = (t - stage1_steps) * (tiles * RADIX)
        for i in range(tiles):
            rows = pl.ds(pl.multiple_of(first_row + i * RADIX, RADIX), RADIX)
            yr = jnp.concatenate([yr_s[n2, rows, :] for n2 in range(RADIX)], axis=0)
            yi = jnp.concatenate([yi_s[n2, rows, :] for n2 in range(RADIX)], axis=0)
            x = dot(mc_ref[i], yr) + dot(ms_ref[i], yi)
            out_rows = slice(i * RADIX, (i + 1) * RADIX)
            for k2 in range(RADIX):
                piece = x[k2 * RADIX:(k2 + 1) * RADIX, :] * gf_ref[k2, out_rows, :].astype(F32)
                o_ref[k2, out_rows, :] = piece.astype(BF16)


def _stage2_matrices(seq):
    s1 = seq // RADIX
    tiles = s1 // RADIX
    c, k2, k1, n2 = np.meshgrid(np.arange(tiles), np.arange(RADIX), np.arange(RADIX), np.arange(RADIX),
                                indexing="ij")
    k = RADIX * c + k1 + s1 * k2
    ang = 2.0 * np.pi * ((n2 * k) % seq).astype(np.float64) / seq
    mc = np.zeros((tiles, ROW_GROUP, ROW_GROUP), np.float32)
    ms = np.zeros((tiles, ROW_GROUP, ROW_GROUP), np.float32)
    mc[c, RADIX * k2 + k1, RADIX * n2 + k1] = np.cos(ang) * seq ** -0.5
    ms[c, RADIX * k2 + k1, RADIX * n2 + k1] = -np.sin(ang) * seq ** -0.5
    return _bf16_const(mc), _bf16_const(ms)


def _fourier(u16, chan_mats, gf, batch, seq, tiles=FOURIER2_TILES):
    s1 = seq // RADIX
    idx = np.arange(s1)
    c1, sn1 = _dft_cos_sin(s1, idx, idx)
    mc, ms = _stage2_matrices(seq)
    per_step = FOURIER1_ROWS // s1
    stage1_steps = RADIX // per_step
    rows = tiles * RADIX
    stage2 = lambda t: jnp.maximum(t - stage1_steps, 0)
    nat_spec = pl.BlockSpec((None, RADIX, rows, FOURIER_WIDTH), lambda b, t: (b, 0, stage2(t), 0))
    m_spec = pl.BlockSpec((tiles, ROW_GROUP, ROW_GROUP), lambda b, t: (stage2(t), 0, 0))
    y_scratch = pltpu.VMEM((RADIX, s1, FOURIER_WIDTH), BF16)
    out = pl.pallas_call(
        functools.partial(_fourier_kernel, stage1_steps=stage1_steps, tiles=tiles),
        grid=(batch, stage1_steps + s1 // rows),
        in_specs=[pl.BlockSpec((None, per_step, s1, FOURIER_WIDTH),
                               lambda b, t: (b, jnp.minimum(t, stage1_steps - 1), 0, 0)),
                  _resident(chan_mats.shape), _resident((s1, s1)), _resident((s1, s1)), _resident((s1, s1)),
                  m_spec, m_spec, nat_spec],
        out_specs=nat_spec,
        out_shape=jax.ShapeDtypeStruct((batch, RADIX, s1, FOURIER_WIDTH), BF16),
        scratch_shapes=[y_scratch, y_scratch],
        compiler_params=_params("parallel", "arbitrary"),
        name="fourier",
    )(u16, chan_mats, _bf16_const(c1), _bf16_const(sn1 - c1), _bf16_const(c1 + sn1), mc, ms,
      gf.reshape(batch, RADIX, s1, FOURIER_WIDTH))
    return out.reshape(batch * seq, FOURIER_WIDTH)


def _outproj_kernel(x_ref, o16_ref, ga_ref, mf_ref, unperm_ref, w_hbm, y_ref, w_ref, stage, sem):
    _load_weight_as_bf16(w_hbm, w_ref, stage, sem)
    unperm = unperm_ref[...]
    attn = []
    for g in range(OUTPROJ_GROUPS):
        rows = slice(g * RADIX, (g + 1) * RADIX)
        heads = [jnp.concatenate([o16_ref[hd, r, rows, :] for r in range(RADIX)], axis=0)
                 for hd in range(N_HEADS)]
        attn16 = jnp.concatenate(heads, axis=1)
        attn.append(jnp.dot(unperm, attn16, preferred_element_type=F32))
    y = x_ref[...] + jnp.dot(mf_ref[...], w_ref[ATTN_WIDTH:, :], preferred_element_type=F32)
    mix_a = (jnp.concatenate(attn, axis=0) * ga_ref[...].astype(F32)).astype(BF16)
    y_ref[...] = y + jnp.dot(mix_a, w_ref[:ATTN_WIDTH, :], preferred_element_type=F32)


def _outproj(x2d, o16, ga, mix_f, unperm, w_out, seq):
    t = x2d.shape[0]
    rows = OUTPROJ_GROUPS * ROW_GROUP
    steps_per_seq = seq // rows
    tok_spec = pl.BlockSpec((rows, ATTN_WIDTH), lambda i: (i, 0))
    x_spec = pl.BlockSpec((rows, D_MODEL), lambda i: (i, 0))
    return pl.pallas_call(
        _outproj_kernel,
        grid=(t // rows,),
        in_specs=[x_spec,
                  pl.BlockSpec((N_HEADS, None, RADIX, OUTPROJ_GROUPS * RADIX, HEAD_DIM),
                               lambda i: (0, i // steps_per_seq, 0, i % steps_per_seq, 0)),
                  tok_spec, tok_spec,
                  _resident((ROW_GROUP, ROW_GROUP)),
                  pl.BlockSpec(memory_space=pl.ANY)],
        out_specs=x_spec,
        out_shape=jax.ShapeDtypeStruct((t, D_MODEL), F32),
        scratch_shapes=_weight_scratch(w_out, OUTPROJ_WEIGHT_CHUNK_ROWS),
        compiler_params=_params("arbitrary"),
        name="outproj",
    )(x2d, o16, ga, mix_f, unperm, w_out)


def _layer(x, gain, w_in, qg, kg, chan_mats, w_out, perm, unperm):
    batch, seq, _ = x.shape
    x2d = x.reshape(batch * seq, D_MODEL)
    q16, k1, k16, v1, v16, ga, u16, gf = _inproj(x2d, batch, seq, gain, w_in, qg, kg, perm)
    o16 = _attention(q16, k1, v1, k16, v16, batch, seq)
    mix_f = _fourier(u16, chan_mats, gf, batch, seq)
    y = _outproj(x2d, o16, ga, mix_f, unperm, w_out, seq)
    return y.reshape(batch, seq, D_MODEL)


def kernel(x_prompt, x_sample, rms_gain, w_in, q_norm_gain, k_norm_gain, w_fourier, w_out):
    depth = rms_gain.shape[0]
    p = _group_permutation()
    perm, unperm = _bf16_const(p), _bf16_const(p.T)
    for l in range(depth):
        gain = rms_gain[l].reshape(1, D_MODEL)
        w_in_l, w_out_l = w_in[l], w_out[l]
        qg = q_norm_gain[l].reshape(1, HEAD_DIM)
        kg = k_norm_gain[l].reshape(1, HEAD_DIM)
        chan_mats = _channel_matrices(w_fourier[l])
        x_prompt = _layer(x_prompt, gain, w_in_l, qg, kg, chan_mats, w_out_l, perm, unperm)
        x_sample = _layer(x_sample, gain, w_in_l, qg, kg, chan_mats, w_out_l, perm, unperm)
    return (x_prompt, x_sample)
```

```python
import functools
import math

import jax
import jax.numpy as jnp
import numpy as np
from jax import lax
from jax.experimental import pallas as pl
from jax.experimental.pallas import tpu as pltpu

D_MODEL = 2048
ATTN_WIDTH = 1024
FOURIER_WIDTH = 1024
HEAD_DIM = 128
N_HEADS = ATTN_WIDTH // HEAD_DIM
N_GROUPS = 4
GROUP_DIM = FOURIER_WIDTH // N_GROUPS
ROPE_THETA = 500000.0
ROPE_DIM = HEAD_DIM // 4
ROPE_HALF = ROPE_DIM // 2
HALF_KEYS = 64
RMS_EPS = 1e-6
RADIX = 16
ROW_GROUP = RADIX * RADIX
Q_BLOCK = 128
FOURIER1_ROWS = 1024
FOURIER2_TILES = 8
ATTN_HEADS_PER_STEP = 2
INPROJ_GROUPS = 1
OUTPROJ_GROUPS = 2
BLOCKS_PER_GROUP = 4
MASK_VALUE = -1e30

VMEM_LIMIT_BYTES = 56 * 1024 * 1024

F32 = jnp.float32
BF16 = jnp.bfloat16


def _params(*semantics):
    return pltpu.CompilerParams(dimension_semantics=semantics, vmem_limit_bytes=VMEM_LIMIT_BYTES)


def _resident(shape):
    return pl.BlockSpec(shape, lambda *_: (0,) * len(shape), pipeline_mode=pl.Buffered(1))


def _rope_tables(seq, scale):
    expo = np.arange(ROPE_HALF, dtype=np.float32) / np.float32(ROPE_HALF)
    inv_freq = (np.float32(1.0) / np.power(np.float32(ROPE_THETA), expo)).astype(np.float32)
    ang = (np.arange(seq, dtype=np.float32)[:, None] * inv_freq[None, :]).astype(np.float64)
    cos, sin = np.cos(ang), np.sin(ang)
    a = np.ones((seq, HEAD_DIM))
    b = np.zeros((seq, HEAD_DIM))
    c = np.zeros((seq, HEAD_DIM))
    a[:, :ROPE_HALF] = cos
    a[:, ROPE_HALF:ROPE_DIM] = cos
    b[:, ROPE_HALF:ROPE_DIM] = sin
    c[:, :ROPE_HALF] = -sin
    return tuple(jnp.asarray((t * scale).astype(np.float32)) for t in (a, b, c))


def _dft_cos_sin(n, rows, cols, scale=1.0):
    m = (np.asarray(rows, dtype=np.int64)[:, None] * np.asarray(cols, dtype=np.int64)[None, :]) % n
    ang = 2.0 * np.pi * m.astype(np.float64) / n
    return np.cos(ang) * scale, np.sin(ang) * scale


def _group_permutation():
    p = np.zeros((ROW_GROUP, ROW_GROUP), np.float32)
    j, r = np.meshgrid(np.arange(RADIX), np.arange(RADIX), indexing="ij")
    p[(RADIX * r + j).ravel(), (RADIX * j + r).ravel()] = 1.0
    return p


def _bf16_const(a):
    return jnp.asarray(np.asarray(a, np.float32)).astype(BF16)


def _silu(a):
    return a * (1.0 / (1.0 + jnp.exp(-a)))


def _inproj_kernel(x_ref, gain_ref, w_ref, qg_ref, kg_ref,
                   qa_ref, qb_ref, qc_ref, ka_ref, kb_ref, kc_ref, perm_ref,
                   q16_ref, k1_ref, k16_ref, v1_ref, v16_ref, ga_ref, u16_ref, gf_ref):
    x = x_ref[...]
    ms = jnp.mean(x * x, axis=-1, keepdims=True)
    h = (x * lax.rsqrt(ms + RMS_EPS) * gain_ref[...]).astype(BF16)
    perm = perm_ref[...]

    def column_tile(j):
        cols = slice(j * ATTN_WIDTH, (j + 1) * ATTN_WIDTH)
        return jnp.dot(h, w_ref[:, cols], preferred_element_type=F32)

    def norm_rope(acc, g_ref, a_ref, b_ref, c_ref):
        heads = []
        for hd in range(N_HEADS):
            a = acc[:, hd * HEAD_DIM:(hd + 1) * HEAD_DIM]
            ms_h = jnp.mean(a * a, axis=-1, keepdims=True)
            n = a * lax.rsqrt(ms_h + RMS_EPS) * g_ref[...]
            r = (n * a_ref[...]
                 + pltpu.roll(n, ROPE_HALF, 1) * b_ref[...]
                 + pltpu.roll(n, HEAD_DIM - ROPE_HALF, 1) * c_ref[...])
            heads.append(r.astype(BF16))
        return jnp.concatenate(heads, axis=1)

    def to_mod16(t):
        return [jnp.dot(perm, t[g * ROW_GROUP:(g + 1) * ROW_GROUP], preferred_element_type=F32)
                for g in range(INPROJ_GROUPS)]

    def store_heads(ref, t):
        for hd in range(N_HEADS):
            ref[hd] = t[:, hd * HEAD_DIM:(hd + 1) * HEAD_DIM]

    def store_head_pieces(ref, groups):
        for g, t in enumerate(groups):
            for hd in range(N_HEADS):
                for r in range(RADIX):
                    ref[hd, r, g * RADIX:(g + 1) * RADIX, :] = (
                        t[r * RADIX:(r + 1) * RADIX, hd * HEAD_DIM:(hd + 1) * HEAD_DIM].astype(ref.dtype))

    acc_q = column_tile(0)
    acc_k = column_tile(1)
    q = norm_rope(acc_q, qg_ref, qa_ref, qb_ref, qc_ref)
    acc_v = column_tile(2)
    store_head_pieces(q16_ref, to_mod16(q))

    k = norm_rope(acc_k, kg_ref, ka_ref, kb_ref, kc_ref)
    acc_ga = column_tile(3)
    store_heads(k1_ref, k)
    store_head_pieces(k16_ref, to_mod16(k))

    v = acc_v.astype(BF16)
    acc_u = column_tile(4)
    store_heads(v1_ref, v)
    store_head_pieces(v16_ref, to_mod16(v))

    ga_ref[...] = _silu(acc_ga).astype(BF16)
    acc_gf = column_tile(5)

    for g, u16 in enumerate(to_mod16(acc_u.astype(BF16))):
        for r in range(RADIX):
            u16_ref[r, g * RADIX:(g + 1) * RADIX, :] = u16[r * RADIX:(r + 1) * RADIX, :].astype(BF16)

    gf_ref[...] = _silu(acc_gf).astype(BF16)


def _inproj(x2d, batch, seq, gain, w_in, qg, kg, perm):
    t = x2d.shape[0]
    rows = INPROJ_GROUPS * ROW_GROUP
    pieces = INPROJ_GROUPS * RADIX
    groups = seq // rows
    sub = seq // RADIX
    qa, qb, qc = _rope_tables(seq, HEAD_DIM ** -0.5 * math.log2(math.e))
    ka, kb, kc = _rope_tables(seq, 1.0)
    rope_spec = pl.BlockSpec((rows, HEAD_DIM), lambda i: (i % groups, 0))
    tok_spec = pl.BlockSpec((rows, ATTN_WIDTH), lambda i: (i, 0))
    head_nat_spec = pl.BlockSpec((N_HEADS, rows, HEAD_DIM), lambda i: (0, i, 0))
    head_m16_spec = pl.BlockSpec((N_HEADS, None, RADIX, pieces, HEAD_DIM),
                                 lambda i: (0, i // groups, 0, i % groups, 0))
    head_nat = jax.ShapeDtypeStruct((N_HEADS, t, HEAD_DIM), BF16)
    head_m16 = lambda dt: jax.ShapeDtypeStruct((N_HEADS, batch, RADIX, sub, HEAD_DIM), dt)
    return pl.pallas_call(
        _inproj_kernel,
        grid=(t // rows,),
        in_specs=[
            pl.BlockSpec((rows, D_MODEL), lambda i: (i, 0)),
            _resident((1, D_MODEL)),
            _resident(w_in.shape),
            _resident((1, HEAD_DIM)), _resident((1, HEAD_DIM)),
            rope_spec, rope_spec, rope_spec, rope_spec, rope_spec, rope_spec,
            _resident((ROW_GROUP, ROW_GROUP)),
        ],
        out_specs=[
            head_m16_spec, head_nat_spec, head_m16_spec, head_nat_spec, head_m16_spec,
            tok_spec,
            pl.BlockSpec((None, RADIX, pieces, FOURIER_WIDTH), lambda i: (i // groups, 0, i % groups, 0)),
            tok_spec,
        ],
        out_shape=[
            head_m16(F32), head_nat, head_m16(BF16), head_nat, head_m16(BF16),
            jax.ShapeDtypeStruct((t, ATTN_WIDTH), BF16),
            jax.ShapeDtypeStruct((batch, RADIX, sub, FOURIER_WIDTH), BF16),
            jax.ShapeDtypeStruct((t, FOURIER_WIDTH), BF16),
        ],
        compiler_params=_params("arbitrary"),
        name="inproj",
    )(x2d, gain, w_in, qg, kg, qa, qb, qc, ka, kb, kc, perm)


def _window_start(first, half, total, window):
    return min(max(first - half, 0), total - window)


class _Pattern:
    def __init__(self, q_first, k_first, diff0, scale):
        deltas = [scale * (q0 - k0) for q0, k0 in zip(q_first, k_first)]
        uniq = sorted(set(deltas))
        self.q_first, self.k_first = q_first, k_first
        self.table_of = [uniq.index(d) for d in deltas]
        self.bias = jnp.asarray(np.stack(
            [np.where(np.abs(diff0 + d) <= HALF_KEYS, 0.0, MASK_VALUE) for d in uniq]).astype(np.float32)
        ).astype(BF16)


def _attention_patterns(seq):
    sub = seq // RADIX
    qi = np.arange(Q_BLOCK)[:, None]
    win16 = min(Q_BLOCK + 2 * HALF_KEYS, sub)
    q16 = list(range(0, sub, Q_BLOCK))
    p16 = _Pattern(q16, [_window_start(q0, HALF_KEYS, sub, win16) for q0 in q16],
                   qi - np.arange(win16)[None, :], 1)
    q_rows, k_rows = Q_BLOCK // 4, Q_BLOCK // 4 + 2 * HALF_KEYS // 4
    kc = np.arange(4 * k_rows)[None, :]
    q4 = list(range(0, sub, q_rows))
    p4 = _Pattern(q4, [_window_start(q0, HALF_KEYS // 4, sub, k_rows) for q0 in q4],
                  4 * (qi % q_rows - kc % k_rows) + (qi // q_rows - kc // k_rows), 4)
    win1 = Q_BLOCK + 2 * HALF_KEYS
    q1 = list(range(0, seq, Q_BLOCK))
    p1 = _Pattern(q1, [_window_start(q0, HALF_KEYS, seq, win1) for q0 in q1],
                  RADIX * (qi % (Q_BLOCK // RADIX)) + qi // (Q_BLOCK // RADIX) - np.arange(win1)[None, :], 1)
    return p16, p4, p1


def _attn_kernel(q16_ref, k1_ref, v1_ref, k16_ref, v16_ref, b16_ref, b4_ref, b1_ref,
                 o_ref, acc_s, m_s, l_s, p0_s, w0_s, p1_s, w1_s, *, pat16, pat4, pat1):
    def gather(ref, pieces):
        return jnp.concatenate([ref[r, pl.ds(s0, n), :] for r, s0, n in pieces], axis=0)

    def scatter(ref, pieces, value):
        off = 0
        for r, s0, n in pieces:
            ref[r, pl.ds(s0, n), :] = value[off:off + n]
            off += n

    def scores_stage(blocks, first, p_scr, w_scr):
        for i, (q, k, _, bias, pieces, (_, m_h, _)) in enumerate(blocks):
            s = lax.dot_general(q(), k(), (((1,), (1,)), ((), ())), preferred_element_type=F32)
            s = s.astype(BF16) + bias()
            m_new = jnp.broadcast_to(jnp.max(s, axis=-1, keepdims=True), (Q_BLOCK, HEAD_DIM)).astype(F32)
            if not first:
                m_old = gather(m_h, pieces)
                m_new = jnp.maximum(m_old, m_new)
                w_scr[i] = jnp.exp2(m_old - m_new)
            width = s.shape[1]
            m_b = m_new.astype(BF16)
            p_scr[i, :, :width] = jnp.exp2(s - jnp.concatenate([m_b] * (width // HEAD_DIM), axis=1))
            scatter(m_h, pieces, m_new)

    def values_stage(blocks, first, p_scr, w_scr):
        for i, (_, _, v, _, pieces, (acc_h, _, l_h)) in enumerate(blocks):
            vw = v()
            v_aug = jnp.concatenate([vw, jnp.ones(vw.shape, BF16)], axis=1)
            pv = jnp.dot(p_scr[i, :, :vw.shape[0]], v_aug, preferred_element_type=F32)
            acc, l = pv[:, :HEAD_DIM], pv[:, HEAD_DIM:]
            if not first:
                w_old = w_scr[i]
                acc = w_old * gather(acc_h, pieces) + acc
                l = w_old * gather(l_h, pieces) + l
            scatter(l_h, pieces, l)
            scatter(acc_h, pieces, acc)

    win16 = b16_ref.shape[2]
    q_rows4 = Q_BLOCK // 4
    k_rows4 = b4_ref.shape[2] // 4
    q_rows1 = Q_BLOCK // RADIX
    win1 = b1_ref.shape[2]
    blocks16, blocks4, blocks1 = [], [], []
    for hd in range(q16_ref.shape[0]):
        q16, k1, v1, k16, v16 = (ref.at[hd] for ref in (q16_ref, k1_ref, v1_ref, k16_ref, v16_ref))
        state = (acc_s.at[hd], m_s.at[hd], l_s.at[hd])

        for r in range(RADIX):
            for blk, (l0, ws) in enumerate(zip(pat16.q_first, pat16.k_first)):
                blocks16.append((
                    lambda q16=q16, r=r, l0=l0: q16[r, l0:l0 + Q_BLOCK, :].astype(BF16),
                    lambda k16=k16, r=r, ws=ws: k16[r, ws:ws + win16, :],
                    lambda v16=v16, r=r, ws=ws: v16[r, ws:ws + win16, :],
                    lambda blk=blk: b16_ref[pat16.table_of[blk]],
                    [(r, l0, Q_BLOCK)], state))

        for r4 in range(4):
            res = [r4 + 4 * a for a in range(4)]
            for blk, (l0, ws) in enumerate(zip(pat4.q_first, pat4.k_first)):
                q_pieces = [(r, l0, q_rows4) for r in res]
                k_pieces = [(r, ws, k_rows4) for r in res]
                blocks4.append((
                    lambda q16=q16, p=q_pieces: gather(q16, p).astype(BF16),
                    lambda k16=k16, p=k_pieces: gather(k16, p),
                    lambda v16=v16, p=k_pieces: gather(v16, p),
                    lambda blk=blk: b4_ref[pat4.table_of[blk]],
                    q_pieces, state))

        for blk, (n0, ws) in enumerate(zip(pat1.q_first, pat1.k_first)):
            q_pieces = [(r, n0 // RADIX, q_rows1) for r in range(RADIX)]
            blocks1.append((
                lambda q16=q16, p=q_pieces: gather(q16, p).astype(BF16),
                lambda k1=k1, ws=ws: k1[ws:ws + win1, :],
                lambda v1=v1, ws=ws: v1[ws:ws + win1, :],
                lambda blk=blk: b1_ref[pat1.table_of[blk]],
                q_pieces, state))

    groups = []
    for blocks, first in ((blocks16, True), (blocks4, False), (blocks1, False)):
        groups.extend((blocks[i:i + BLOCKS_PER_GROUP], first) for i in range(0, len(blocks), BLOCKS_PER_GROUP))

    slots = ((p0_s, w0_s), (p1_s, w1_s))
    scores_stage(*groups[0], *slots[0])
    for g in range(1, len(groups)):
        scores_stage(*groups[g], *slots[g % 2])
        values_stage(*groups[g - 1], *slots[(g - 1) % 2])
    values_stage(*groups[-1], *slots[(len(groups) - 1) % 2])

    def finish(i, carry):
        hd, r = i // RADIX, i % RADIX
        o_ref[hd, r] = (acc_s[hd, r] * (1.0 / l_s[hd, r])).astype(BF16)
        return carry

    lax.fori_loop(0, q16_ref.shape[0] * RADIX, finish, 0)


def _attention(q16, k1, v1, k16, v16, batch, seq):
    sub = seq // RADIX
    heads = ATTN_HEADS_PER_STEP
    pat16, pat4, pat1 = _attention_patterns(seq)
    m16_spec = pl.BlockSpec((heads, None, RADIX, sub, HEAD_DIM), lambda b, h: (h, b, 0, 0, 0))
    nat_spec = pl.BlockSpec((heads, seq, HEAD_DIM), lambda b, h: (h, b, 0))
    state = pltpu.VMEM((heads, RADIX, sub, HEAD_DIM), F32)
    probs = pltpu.VMEM((BLOCKS_PER_GROUP, Q_BLOCK, Q_BLOCK + 2 * HALF_KEYS), BF16)
    rescale = pltpu.VMEM((BLOCKS_PER_GROUP, Q_BLOCK, HEAD_DIM), F32)
    return pl.pallas_call(
        functools.partial(_attn_kernel, pat16=pat16, pat4=pat4, pat1=pat1),
        grid=(batch, N_HEADS // heads),
        in_specs=[m16_spec, nat_spec, nat_spec, m16_spec, m16_spec,
                  _resident(pat16.bias.shape), _resident(pat4.bias.shape), _resident(pat1.bias.shape)],
        out_specs=m16_spec,
        out_shape=jax.ShapeDtypeStruct((N_HEADS, batch, RADIX, sub, HEAD_DIM), BF16),
        scratch_shapes=[state, state, state, probs, rescale, probs, rescale],
        compiler_params=_params("arbitrary", "arbitrary"),
        name="attention",
    )(q16, k1, v1, k16, v16, pat16.bias, pat4.bias, pat1.bias)


def _chanmat_kernel(c_ref, s_ref, w_ref, m_ref):
    for g in range(N_GROUPS):
        w = w_ref[g]
        mc = jnp.dot(c_ref[...], w, preferred_element_type=F32, precision=lax.Precision.HIGHEST)
        ms = jnp.dot(s_ref[...], w, preferred_element_type=F32, precision=lax.Precision.HIGHEST)
        m_ref[g, :, :GROUP_DIM] = mc.astype(BF16)
        m_ref[g, :, GROUP_DIM:] = ms.astype(BF16)


def _channel_matrices(w_fourier):
    idx = np.arange(GROUP_DIM)
    cc, sc = _dft_cos_sin(GROUP_DIM, idx, idx, GROUP_DIM ** -0.5)
    return pl.pallas_call(
        _chanmat_kernel,
        out_shape=jax.ShapeDtypeStruct((N_GROUPS, GROUP_DIM, 2 * GROUP_DIM), BF16),
        name="chanmat",
    )(jnp.asarray(cc, F32), jnp.asarray(sc, F32), w_fourier)


def _fourier_kernel(u_ref, m_ref, c_ref, smc_ref, cps_ref, mc_ref, ms_ref, gf_ref, o_ref, yr_s, yi_s,
                    *, stage1_steps, tiles):
    t = pl.program_id(1)
    dot = functools.partial(jnp.dot, preferred_element_type=F32)
    per_step = u_ref.shape[0]

    @pl.when(t < stage1_steps)
    def _():
        z = []
        for j in range(per_step):
            a_parts, b_parts, ab_parts = [], [], []
            for g in range(N_GROUPS):
                ab = dot(u_ref[j, :, g * GROUP_DIM:(g + 1) * GROUP_DIM], m_ref[g])
                a, b = ab[:, :GROUP_DIM], ab[:, GROUP_DIM:]
                a_parts.append(a.astype(BF16))
                b_parts.append(b.astype(BF16))
                ab_parts.append((a + b).astype(BF16))
            z.append(tuple(jnp.concatenate(p, axis=1) for p in (a_parts, b_parts, ab_parts)))
        c, s_minus_c, c_plus_s = c_ref[...], smc_ref[...], cps_ref[...]
        for j, (a, b, a_plus_b) in enumerate(z):
            n2 = t * per_step + j
            k1 = dot(c, a_plus_b)
            yr_s[n2] = (k1 - dot(c_plus_s, b)).astype(BF16)
            yi_s[n2] = (k1 + dot(s_minus_c, a)).astype(BF16)

    @pl.when(t >= stage1_steps)
    def _():
        first_row = (t - stage1_steps) * (tiles * RADIX)
        for i in range(tiles):
            rows = pl.ds(pl.multiple_of(first_row + i * RADIX, RADIX), RADIX)
            yr = jnp.concatenate([yr_s[n2, rows, :] for n2 in range(RADIX)], axis=0)
            yi = jnp.concatenate([yi_s[n2, rows, :] for n2 in range(RADIX)], axis=0)
            x = dot(mc_ref[i], yr) + dot(ms_ref[i], yi)
            out_rows = slice(i * RADIX, (i + 1) * RADIX)
            for k2 in range(RADIX):
                piece = x[k2 * RADIX:(k2 + 1) * RADIX, :] * gf_ref[k2, out_rows, :].astype(F32)
                o_ref[k2, out_rows, :] = piece.astype(BF16)


def _stage2_matrices(seq):
    s1 = seq // RADIX
    tiles = s1 // RADIX
    c, k2, k1, n2 = np.meshgrid(np.arange(tiles), np.arange(RADIX), np.arange(RADIX), np.arange(RADIX),
                                indexing="ij")
    k = RADIX * c + k1 + s1 * k2
    ang = 2.0 * np.pi * ((n2 * k) % seq).astype(np.float64) / seq
    mc = np.zeros((tiles, ROW_GROUP, ROW_GROUP), np.float32)
    ms = np.zeros((tiles, ROW_GROUP, ROW_GROUP), np.float32)
    mc[c, RADIX * k2 + k1, RADIX * n2 + k1] = np.cos(ang) * seq ** -0.5
    ms[c, RADIX * k2 + k1, RADIX * n2 + k1] = -np.sin(ang) * seq ** -0.5
    return _bf16_const(mc), _bf16_const(ms)


def _fourier(u16, chan_mats, gf, batch, seq, tiles=FOURIER2_TILES):
    s1 = seq // RADIX
    idx = np.arange(s1)
    c1, sn1 = _dft_cos_sin(s1, idx, idx)
    mc, ms = _stage2_matrices(seq)
    per_step = FOURIER1_ROWS // s1
    stage1_steps = RADIX // per_step
    rows = tiles * RADIX
    stage2 = lambda t: jnp.maximum(t - stage1_steps, 0)
    nat_spec = pl.BlockSpec((None, RADIX, rows, FOURIER_WIDTH), lambda b, t: (b, 0, stage2(t), 0))
    m_spec = pl.BlockSpec((tiles, ROW_GROUP, ROW_GROUP), lambda b, t: (stage2(t), 0, 0))
    y_scratch = pltpu.VMEM((RADIX, s1, FOURIER_WIDTH), BF16)
    out = pl.pallas_call(
        functools.partial(_fourier_kernel, stage1_steps=stage1_steps, tiles=tiles),
        grid=(batch, stage1_steps + s1 // rows),
        in_specs=[pl.BlockSpec((None, per_step, s1, FOURIER_WIDTH),
                               lambda b, t: (b, jnp.minimum(t, stage1_steps - 1), 0, 0)),
                  _resident(chan_mats.shape), _resident((s1, s1)), _resident((s1, s1)), _resident((s1, s1)),
                  m_spec, m_spec, nat_spec],
        out_specs=nat_spec,
        out_shape=jax.ShapeDtypeStruct((batch, RADIX, s1, FOURIER_WIDTH), BF16),
        scratch_shapes=[y_scratch, y_scratch],
        compiler_params=_params("arbitrary", "arbitrary"),
        name="fourier",
    )(u16, chan_mats, _bf16_const(c1), _bf16_const(sn1 - c1), _bf16_const(c1 + sn1), mc, ms,
      gf.reshape(batch, RADIX, s1, FOURIER_WIDTH))
    return out.reshape(batch * seq, FOURIER_WIDTH)


def _outproj_kernel(x_ref, o16_ref, ga_ref, mf_ref, unperm_ref, w_ref, y_ref):
    unperm = unperm_ref[...]
    attn = []
    for g in range(OUTPROJ_GROUPS):
        rows = slice(g * RADIX, (g + 1) * RADIX)
        heads = [jnp.concatenate([o16_ref[hd, r, rows, :] for r in range(RADIX)], axis=0)
                 for hd in range(N_HEADS)]
        attn16 = jnp.concatenate(heads, axis=1)
        attn.append(jnp.dot(unperm, attn16, preferred_element_type=F32))
    y = x_ref[...] + jnp.dot(mf_ref[...], w_ref[ATTN_WIDTH:, :], preferred_element_type=F32)
    mix_a = (jnp.concatenate(attn, axis=0) * ga_ref[...].astype(F32)).astype(BF16)
    y_ref[...] = y + jnp.dot(mix_a, w_ref[:ATTN_WIDTH, :], preferred_element_type=F32)


def _outproj(x2d, o16, ga, mix_f, unperm, w_out, seq):
    t = x2d.shape[0]
    rows = OUTPROJ_GROUPS * ROW_GROUP
    steps_per_seq = seq // rows
    tok_spec = pl.BlockSpec((rows, ATTN_WIDTH), lambda i: (i, 0))
    x_spec = pl.BlockSpec((rows, D_MODEL), lambda i: (i, 0))
    return pl.pallas_call(
        _outproj_kernel,
        grid=(t // rows,),
        in_specs=[x_spec,
                  pl.BlockSpec((N_HEADS, None, RADIX, OUTPROJ_GROUPS * RADIX, HEAD_DIM),
                               lambda i: (0, i // steps_per_seq, 0, i % steps_per_seq, 0)),
                  tok_spec, tok_spec,
                  _resident((ROW_GROUP, ROW_GROUP)),
                  _resident((D_MODEL, D_MODEL))],
        out_specs=x_spec,
        out_shape=jax.ShapeDtypeStruct((t, D_MODEL), F32),
        compiler_params=_params("arbitrary"),
        name="outproj",
    )(x2d, o16, ga, mix_f, unperm, w_out)


def _layer(x, gain, w_in, qg, kg, chan_mats, w_out, perm, unperm):
    batch, seq, _ = x.shape
    x2d = x.reshape(batch * seq, D_MODEL)
    q16, k1, k16, v1, v16, ga, u16, gf = _inproj(x2d, batch, seq, gain, w_in, qg, kg, perm)
    o16 = _attention(q16, k1, v1, k16, v16, batch, seq)
    mix_f = _fourier(u16, chan_mats, gf, batch, seq)
    y = _outproj(x2d, o16, ga, mix_f, unperm, w_out, seq)
    return y.reshape(batch, seq, D_MODEL)


def kernel(x_prompt, x_sample, rms_gain, w_in, q_norm_gain, k_norm_gain, w_fourier, w_out):
    depth = rms_gain.shape[0]
    p = _group_permutation()
    perm, unperm = _bf16_const(p), _bf16_const(p.T)
    for l in range(depth):
        gain = rms_gain[l].reshape(1, D_MODEL)
        w_in_l = w_in[l].astype(BF16)
        w_out_l = w_out[l].astype(BF16)
        qg = q_norm_gain[l].reshape(1, HEAD_DIM)
        kg = k_norm_gain[l].reshape(1, HEAD_DIM)
        chan_mats = _channel_matrices(w_fourier[l])
        x_prompt = _layer(x_prompt, gain, w_in_l, qg, kg, chan_mats, w_out_l, perm, unperm)
        x_sample = _layer(x_sample, gain, w_in_l, qg, kg, chan_mats, w_out_l, perm, unperm)
    return (x_prompt, x_sample)
```

```python
import functools
import math

import jax
import jax.numpy as jnp
import numpy as np
from jax import lax
from jax.experimental import pallas as pl
from jax.experimental.pallas import tpu as pltpu

D_MODEL = 2048
ATTN_WIDTH = 1024
FOURIER_WIDTH = 1024
HEAD_DIM = 128
N_HEADS = ATTN_WIDTH // HEAD_DIM
N_GROUPS = 4
GROUP_DIM = FOURIER_WIDTH // N_GROUPS
ROPE_THETA = 500000.0
ROPE_DIM = HEAD_DIM // 4
ROPE_HALF = ROPE_DIM // 2
HALF_KEYS = 64
RMS_EPS = 1e-6
RADIX = 16
ROW_GROUP = RADIX * RADIX
Q_BLOCK = 128
FOURIER1_ROWS = 1024
FOURIER2_TILES = 8
ATTN_HEADS_PER_STEP = 2
INPROJ_GROUPS = 1
OUTPROJ_GROUPS = 2
BLOCKS_PER_GROUP = 6
MASK_VALUE = -1e30

VMEM_LIMIT_BYTES = 56 * 1024 * 1024

F32 = jnp.float32
BF16 = jnp.bfloat16


def _params(*semantics):
    return pltpu.CompilerParams(dimension_semantics=semantics, vmem_limit_bytes=VMEM_LIMIT_BYTES)


def _resident(shape):
    return pl.BlockSpec(shape, lambda *_: (0,) * len(shape), pipeline_mode=pl.Buffered(1))


def _rope_tables(seq, scale):
    expo = np.arange(ROPE_HALF, dtype=np.float32) / np.float32(ROPE_HALF)
    inv_freq = (np.float32(1.0) / np.power(np.float32(ROPE_THETA), expo)).astype(np.float32)
    ang = (np.arange(seq, dtype=np.float32)[:, None] * inv_freq[None, :]).astype(np.float64)
    cos, sin = np.cos(ang), np.sin(ang)
    a = np.ones((seq, HEAD_DIM))
    b = np.zeros((seq, HEAD_DIM))
    c = np.zeros((seq, HEAD_DIM))
    a[:, :ROPE_HALF] = cos
    a[:, ROPE_HALF:ROPE_DIM] = cos
    b[:, ROPE_HALF:ROPE_DIM] = sin
    c[:, :ROPE_HALF] = -sin
    return tuple(jnp.asarray((t * scale).astype(np.float32)) for t in (a, b, c))


def _dft_cos_sin(n, rows, cols, scale=1.0):
    m = (np.asarray(rows, dtype=np.int64)[:, None] * np.asarray(cols, dtype=np.int64)[None, :]) % n
    ang = 2.0 * np.pi * m.astype(np.float64) / n
    return np.cos(ang) * scale, np.sin(ang) * scale


def _group_permutation():
    p = np.zeros((ROW_GROUP, ROW_GROUP), np.float32)
    j, r = np.meshgrid(np.arange(RADIX), np.arange(RADIX), indexing="ij")
    p[(RADIX * r + j).ravel(), (RADIX * j + r).ravel()] = 1.0
    return p


def _bf16_const(a):
    return jnp.asarray(np.asarray(a, np.float32)).astype(BF16)


def _silu(a):
    return a * (1.0 / (1.0 + jnp.exp(-a)))


def _inproj_kernel(x_ref, gain_ref, w_ref, qg_ref, kg_ref,
                   qa_ref, qb_ref, qc_ref, ka_ref, kb_ref, kc_ref, perm_ref,
                   q16_ref, k1_ref, k16_ref, v1_ref, v16_ref, ga_ref, u16_ref, gf_ref):
    x = x_ref[...]
    ms = jnp.mean(x * x, axis=-1, keepdims=True)
    h = (x * lax.rsqrt(ms + RMS_EPS) * gain_ref[...]).astype(BF16)
    perm = perm_ref[...]

    def column_tile(j):
        cols = slice(j * ATTN_WIDTH, (j + 1) * ATTN_WIDTH)
        return jnp.dot(h, w_ref[:, cols], preferred_element_type=F32)

    def norm_rope(acc, g_ref, a_ref, b_ref, c_ref):
        heads = []
        for hd in range(N_HEADS):
            a = acc[:, hd * HEAD_DIM:(hd + 1) * HEAD_DIM]
            ms_h = jnp.mean(a * a, axis=-1, keepdims=True)
            n = a * lax.rsqrt(ms_h + RMS_EPS) * g_ref[...]
            r = (n * a_ref[...]
                 + pltpu.roll(n, ROPE_HALF, 1) * b_ref[...]
                 + pltpu.roll(n, HEAD_DIM - ROPE_HALF, 1) * c_ref[...])
            heads.append(r.astype(BF16))
        return jnp.concatenate(heads, axis=1)

    def to_mod16(t):
        return [jnp.dot(perm, t[g * ROW_GROUP:(g + 1) * ROW_GROUP], preferred_element_type=F32)
                for g in range(INPROJ_GROUPS)]

    def store_heads(ref, t):
        for hd in range(N_HEADS):
            ref[hd] = t[:, hd * HEAD_DIM:(hd + 1) * HEAD_DIM]

    def store_head_pieces(ref, groups):
        for g, t in enumerate(groups):
            for hd in range(N_HEADS):
                for r in range(RADIX):
                    ref[hd, r, g * RADIX:(g + 1) * RADIX, :] = (
                        t[r * RADIX:(r + 1) * RADIX, hd * HEAD_DIM:(hd + 1) * HEAD_DIM].astype(ref.dtype))

    acc_q = column_tile(0)
    acc_k = column_tile(1)
    q = norm_rope(acc_q, qg_ref, qa_ref, qb_ref, qc_ref)
    acc_v = column_tile(2)
    store_head_pieces(q16_ref, to_mod16(q))

    k = norm_rope(acc_k, kg_ref, ka_ref, kb_ref, kc_ref)
    acc_ga = column_tile(3)
    store_heads(k1_ref, k)
    store_head_pieces(k16_ref, to_mod16(k))

    v = acc_v.astype(BF16)
    acc_u = column_tile(4)
    store_heads(v1_ref, v)
    store_head_pieces(v16_ref, to_mod16(v))

    ga_ref[...] = _silu(acc_ga).astype(BF16)
    acc_gf = column_tile(5)

    for g, u16 in enumerate(to_mod16(acc_u.astype(BF16))):
        for r in range(RADIX):
            u16_ref[r, g * RADIX:(g + 1) * RADIX, :] = u16[r * RADIX:(r + 1) * RADIX, :].astype(BF16)

    gf_ref[...] = _silu(acc_gf).astype(BF16)


def _inproj(x2d, batch, seq, gain, w_in, qg, kg, perm):
    t = x2d.shape[0]
    rows = INPROJ_GROUPS * ROW_GROUP
    pieces = INPROJ_GROUPS * RADIX
    groups = seq // rows
    sub = seq // RADIX
    qa, qb, qc = _rope_tables(seq, HEAD_DIM ** -0.5 * math.log2(math.e))
    ka, kb, kc = _rope_tables(seq, 1.0)
    rope_spec = pl.BlockSpec((rows, HEAD_DIM), lambda i: (i % groups, 0))
    tok_spec = pl.BlockSpec((rows, ATTN_WIDTH), lambda i: (i, 0))
    head_nat_spec = pl.BlockSpec((N_HEADS, rows, HEAD_DIM), lambda i: (0, i, 0))
    head_m16_spec = pl.BlockSpec((N_HEADS, None, RADIX, pieces, HEAD_DIM),
                                 lambda i: (0, i // groups, 0, i % groups, 0))
    head_nat = jax.ShapeDtypeStruct((N_HEADS, t, HEAD_DIM), BF16)
    head_m16 = lambda dt: jax.ShapeDtypeStruct((N_HEADS, batch, RADIX, sub, HEAD_DIM), dt)
    return pl.pallas_call(
        _inproj_kernel,
        grid=(t // rows,),
        in_specs=[
            pl.BlockSpec((rows, D_MODEL), lambda i: (i, 0)),
            _resident((1, D_MODEL)),
            _resident(w_in.shape),
            _resident((1, HEAD_DIM)), _resident((1, HEAD_DIM)),
            rope_spec, rope_spec, rope_spec, rope_spec, rope_spec, rope_spec,
            _resident((ROW_GROUP, ROW_GROUP)),
        ],
        out_specs=[
            head_m16_spec, head_nat_spec, head_m16_spec, head_nat_spec, head_m16_spec,
            tok_spec,
            pl.BlockSpec((None, RADIX, pieces, FOURIER_WIDTH), lambda i: (i // groups, 0, i % groups, 0)),
            tok_spec,
        ],
        out_shape=[
            head_m16(F32), head_nat, head_m16(BF16), head_nat, head_m16(BF16),
            jax.ShapeDtypeStruct((t, ATTN_WIDTH), BF16),
            jax.ShapeDtypeStruct((batch, RADIX, sub, FOURIER_WIDTH), BF16),
            jax.ShapeDtypeStruct((t, FOURIER_WIDTH), BF16),
        ],
        compiler_params=_params("arbitrary"),
        name="inproj",
    )(x2d, gain, w_in, qg, kg, qa, qb, qc, ka, kb, kc, perm)


def _window_start(first, half, total, window):
    return min(max(first - half, 0), total - window)


class _Pattern:
    def __init__(self, q_first, k_first, diff0, scale):
        deltas = [scale * (q0 - k0) for q0, k0 in zip(q_first, k_first)]
        uniq = sorted(set(deltas))
        self.q_first, self.k_first = q_first, k_first
        self.table_of = [uniq.index(d) for d in deltas]
        self.bias = jnp.asarray(np.stack(
            [np.where(np.abs(diff0 + d) <= HALF_KEYS, 0.0, MASK_VALUE) for d in uniq]).astype(np.float32)
        ).astype(BF16)


def _attention_patterns(seq):
    sub = seq // RADIX
    qi = np.arange(Q_BLOCK)[:, None]
    win16 = min(Q_BLOCK + 2 * HALF_KEYS, sub)
    q16 = list(range(0, sub, Q_BLOCK))
    p16 = _Pattern(q16, [_window_start(q0, HALF_KEYS, sub, win16) for q0 in q16],
                   qi - np.arange(win16)[None, :], 1)
    q_rows, k_rows = Q_BLOCK // 4, Q_BLOCK // 4 + 2 * HALF_KEYS // 4
    kc = np.arange(4 * k_rows)[None, :]
    q4 = list(range(0, sub, q_rows))
    p4 = _Pattern(q4, [_window_start(q0, HALF_KEYS // 4, sub, k_rows) for q0 in q4],
                  4 * (qi % q_rows - kc % k_rows) + (qi // q_rows - kc // k_rows), 4)
    win1 = Q_BLOCK + 2 * HALF_KEYS
    q1 = list(range(0, seq, Q_BLOCK))
    p1 = _Pattern(q1, [_window_start(q0, HALF_KEYS, seq, win1) for q0 in q1],
                  RADIX * (qi % (Q_BLOCK // RADIX)) + qi // (Q_BLOCK // RADIX) - np.arange(win1)[None, :], 1)
    return p16, p4, p1


def _attn_kernel(q16_ref, k1_ref, v1_ref, k16_ref, v16_ref, b16_ref, b4_ref, b1_ref,
                 o_ref, acc_s, m_s, l_s, p0_s, w0_s, p1_s, w1_s, *, pat16, pat4, pat1):
    def gather(ref, pieces):
        return jnp.concatenate([ref[r, pl.ds(s0, n), :] for r, s0, n in pieces], axis=0)

    def scatter(ref, pieces, value):
        off = 0
        for r, s0, n in pieces:
            ref[r, pl.ds(s0, n), :] = value[off:off + n]
            off += n

    def scores_stage(blocks, first, p_scr, w_scr):
        for i, (q, k, _, bias, pieces, (_, m_h, _)) in enumerate(blocks):
            s = lax.dot_general(q(), k(), (((1,), (1,)), ((), ())), preferred_element_type=F32)
            s = s.astype(BF16) + bias()
            m_new = jnp.broadcast_to(jnp.max(s, axis=-1, keepdims=True), (Q_BLOCK, HEAD_DIM)).astype(F32)
            if not first:
                m_old = gather(m_h, pieces)
                m_new = jnp.maximum(m_old, m_new)
                w_scr[i] = jnp.exp2(m_old - m_new)
            width = s.shape[1]
            m_b = m_new.astype(BF16)
            p_scr[i, :, :width] = jnp.exp2(s - jnp.concatenate([m_b] * (width // HEAD_DIM), axis=1))
            scatter(m_h, pieces, m_new)

    def values_stage(blocks, first, p_scr, w_scr):
        for i, (_, _, v, _, pieces, (acc_h, _, l_h)) in enumerate(blocks):
            vw = v()
            v_aug = jnp.concatenate([vw, jnp.ones(vw.shape, BF16)], axis=1)
            pv = jnp.dot(p_scr[i, :, :vw.shape[0]], v_aug, preferred_element_type=F32)
            acc, l = pv[:, :HEAD_DIM], pv[:, HEAD_DIM:]
            if not first:
                w_old = w_scr[i]
                acc = w_old * gather(acc_h, pieces) + acc
                l = w_old * gather(l_h, pieces) + l
            scatter(l_h, pieces, l)
            scatter(acc_h, pieces, acc)

    win16 = b16_ref.shape[2]
    q_rows4 = Q_BLOCK // 4
    k_rows4 = b4_ref.shape[2] // 4
    q_rows1 = Q_BLOCK // RADIX
    win1 = b1_ref.shape[2]
    blocks16, blocks4, blocks1 = [], [], []
    for hd in range(q16_ref.shape[0]):
        q16, k1, v1, k16, v16 = (ref.at[hd] for ref in (q16_ref, k1_ref, v1_ref, k16_ref, v16_ref))
        state = (acc_s.at[hd], m_s.at[hd], l_s.at[hd])

        for r in range(RADIX):
            for blk, (l0, ws) in enumerate(zip(pat16.q_first, pat16.k_first)):
                blocks16.append((
                    lambda q16=q16, r=r, l0=l0: q16[r, l0:l0 + Q_BLOCK, :].astype(BF16),
                    lambda k16=k16, r=r, ws=ws: k16[r, ws:ws + win16, :],
                    lambda v16=v16, r=r, ws=ws: v16[r, ws:ws + win16, :],
                    lambda blk=blk: b16_ref[pat16.table_of[blk]],
                    [(r, l0, Q_BLOCK)], state))

        for r4 in range(4):
            res = [r4 + 4 * a for a in range(4)]
            for blk, (l0, ws) in enumerate(zip(pat4.q_first, pat4.k_first)):
                q_pieces = [(r, l0, q_rows4) for r in res]
                k_pieces = [(r, ws, k_rows4) for r in res]
                blocks4.append((
                    lambda q16=q16, p=q_pieces: gather(q16, p).astype(BF16),
                    lambda k16=k16, p=k_pieces: gather(k16, p),
                    lambda v16=v16, p=k_pieces: gather(v16, p),
                    lambda blk=blk: b4_ref[pat4.table_of[blk]],
                    q_pieces, state))

        for blk, (n0, ws) in enumerate(zip(pat1.q_first, pat1.k_first)):
            q_pieces = [(r, n0 // RADIX, q_rows1) for r in range(RADIX)]
            blocks1.append((
                lambda q16=q16, p=q_pieces: gather(q16, p).astype(BF16),
                lambda k1=k1, ws=ws: k1[ws:ws + win1, :],
                lambda v1=v1, ws=ws: v1[ws:ws + win1, :],
                lambda blk=blk: b1_ref[pat1.table_of[blk]],
                q_pieces, state))

    groups = []
    for blocks, first in ((blocks16, True), (blocks4, False), (blocks1, False)):
        groups.extend((blocks[i:i + BLOCKS_PER_GROUP], first) for i in range(0, len(blocks), BLOCKS_PER_GROUP))

    slots = ((p0_s, w0_s), (p1_s, w1_s))
    scores_stage(*groups[0], *slots[0])
    for g in range(1, len(groups)):
        scores_stage(*groups[g], *slots[g % 2])
        values_stage(*groups[g - 1], *slots[(g - 1) % 2])
    values_stage(*groups[-1], *slots[(len(groups) - 1) % 2])

    def finish(i, carry):
        hd, r = i // RADIX, i % RADIX
        o_ref[hd, r] = (acc_s[hd, r] * (1.0 / l_s[hd, r])).astype(BF16)
        return carry

    lax.fori_loop(0, q16_ref.shape[0] * RADIX, finish, 0)


def _attention(q16, k1, v1, k16, v16, batch, seq):
    sub = seq // RADIX
    heads = ATTN_HEADS_PER_STEP
    pat16, pat4, pat1 = _attention_patterns(seq)
    m16_spec = pl.BlockSpec((heads, None, RADIX, sub, HEAD_DIM), lambda b, h: (h, b, 0, 0, 0))
    nat_spec = pl.BlockSpec((heads, seq, HEAD_DIM), lambda b, h: (h, b, 0))
    state = pltpu.VMEM((heads, RADIX, sub, HEAD_DIM), F32)
    probs = pltpu.VMEM((BLOCKS_PER_GROUP, Q_BLOCK, Q_BLOCK + 2 * HALF_KEYS), BF16)
    rescale = pltpu.VMEM((BLOCKS_PER_GROUP, Q_BLOCK, HEAD_DIM), F32)
    return pl.pallas_call(
        functools.partial(_attn_kernel, pat16=pat16, pat4=pat4, pat1=pat1),
        grid=(batch, N_HEADS // heads),
        in_specs=[m16_spec, nat_spec, nat_spec, m16_spec, m16_spec,
                  _resident(pat16.bias.shape), _resident(pat4.bias.shape), _resident(pat1.bias.shape)],
        out_specs=m16_spec,
        out_shape=jax.ShapeDtypeStruct((N_HEADS, batch, RADIX, sub, HEAD_DIM), BF16),
        scratch_shapes=[state, state, state, probs, rescale, probs, rescale],
        compiler_params=_params("parallel", "parallel"),
        name="attention",
    )(q16, k1, v1, k16, v16, pat16.bias, pat4.bias, pat1.bias)


def _chanmat_kernel(c_ref, s_ref, w_ref, m_ref):
    for g in range(N_GROUPS):
        w = w_ref[g]
        mc = jnp.dot(c_ref[...], w, preferred_element_type=F32, precision=lax.Precision.HIGHEST)
        ms = jnp.dot(s_ref[...], w, preferred_element_type=F32, precision=lax.Precision.HIGHEST)
        m_ref[g, :, :GROUP_DIM] = mc.astype(BF16)
        m_ref[g, :, GROUP_DIM:] = ms.astype(BF16)


def _channel_matrices(w_fourier):
    idx = np.arange(GROUP_DIM)
    cc, sc = _dft_cos_sin(GROUP_DIM, idx, idx, GROUP_DIM ** -0.5)
    return pl.pallas_call(
        _chanmat_kernel,
        out_shape=jax.ShapeDtypeStruct((N_GROUPS, GROUP_DIM, 2 * GROUP_DIM), BF16),
        name="chanmat",
    )(jnp.asarray(cc, F32), jnp.asarray(sc, F32), w_fourier)


def _fourier_kernel(u_ref, m_ref, c_ref, smc_ref, cps_ref, mc_ref, ms_ref, gf_ref, o_ref, yr_s, yi_s,
                    *, stage1_steps, tiles):
    t = pl.program_id(1)
    dot = functools.partial(jnp.dot, preferred_element_type=F32)
    per_step = u_ref.shape[0]

    @pl.when(t < stage1_steps)
    def _():
        z = []
        for j in range(per_step):
            a_parts, b_parts, ab_parts = [], [], []
            for g in range(N_GROUPS):
                ab = dot(u_ref[j, :, g * GROUP_DIM:(g + 1) * GROUP_DIM], m_ref[g])
                a, b = ab[:, :GROUP_DIM], ab[:, GROUP_DIM:]
                a_parts.append(a.astype(BF16))
                b_parts.append(b.astype(BF16))
                ab_parts.append((a + b).astype(BF16))
            z.append(tuple(jnp.concatenate(p, axis=1) for p in (a_parts, b_parts, ab_parts)))
        c, s_minus_c, c_plus_s = c_ref[...], smc_ref[...], cps_ref[...]
        for j, (a, b, a_plus_b) in enumerate(z):
            n2 = t * per_step + j
            k1 = dot(c, a_plus_b)
            yr_s[n2] = (k1 - dot(c_plus_s, b)).astype(BF16)
            yi_s[n2] = (k1 + dot(s_minus_c, a)).astype(BF16)

    @pl.when(t >= stage1_steps)
    def _():
        first_row = (t - stage1_steps) * (tiles * RADIX)
        for i in range(tiles):
            rows = pl.ds(pl.multiple_of(first_row + i * RADIX, RADIX), RADIX)
            yr = jnp.concatenate([yr_s[n2, rows, :] for n2 in range(RADIX)], axis=0)
            yi = jnp.concatenate([yi_s[n2, rows, :] for n2 in range(RADIX)], axis=0)
            x = dot(mc_ref[i], yr) + dot(ms_ref[i], yi)
            out_rows = slice(i * RADIX, (i + 1) * RADIX)
            for k2 in range(RADIX):
                piece = x[k2 * RADIX:(k2 + 1) * RADIX, :] * gf_ref[k2, out_rows, :].astype(F32)
                o_ref[k2, out_rows, :] = piece.astype(BF16)


def _stage2_matrices(seq):
    s1 = seq // RADIX
    tiles = s1 // RADIX
    c, k2, k1, n2 = np.meshgrid(np.arange(tiles), np.arange(RADIX), np.arange(RADIX), np.arange(RADIX),
                                indexing="ij")
    k = RADIX * c + k1 + s1 * k2
    ang = 2.0 * np.pi * ((n2 * k) % seq).astype(np.float64) / seq
    mc = np.zeros((tiles, ROW_GROUP, ROW_GROUP), np.float32)
    ms = np.zeros((tiles, ROW_GROUP, ROW_GROUP), np.float32)
    mc[c, RADIX * k2 + k1, RADIX * n2 + k1] = np.cos(ang) * seq ** -0.5
    ms[c, RADIX * k2 + k1, RADIX * n2 + k1] = -np.sin(ang) * seq ** -0.5
    return _bf16_const(mc), _bf16_const(ms)


def _fourier(u16, chan_mats, gf, batch, seq, tiles=FOURIER2_TILES):
    s1 = seq // RADIX
    idx = np.arange(s1)
    c1, sn1 = _dft_cos_sin(s1, idx, idx)
    mc, ms = _stage2_matrices(seq)
    per_step = FOURIER1_ROWS // s1
    stage1_steps = RADIX // per_step
    rows = tiles * RADIX
    stage2 = lambda t: jnp.maximum(t - stage1_steps, 0)
    nat_spec = pl.BlockSpec((None, RADIX, rows, FOURIER_WIDTH), lambda b, t: (b, 0, stage2(t), 0))
    m_spec = pl.BlockSpec((tiles, ROW_GROUP, ROW_GROUP), lambda b, t: (stage2(t), 0, 0))
    y_scratch = pltpu.VMEM((RADIX, s1, FOURIER_WIDTH), BF16)
    out = pl.pallas_call(
        functools.partial(_fourier_kernel, stage1_steps=stage1_steps, tiles=tiles),
        grid=(batch, stage1_steps + s1 // rows),
        in_specs=[pl.BlockSpec((None, per_step, s1, FOURIER_WIDTH),
                               lambda b, t: (b, jnp.minimum(t, stage1_steps - 1), 0, 0)),
                  _resident(chan_mats.shape), _resident((s1, s1)), _resident((s1, s1)), _resident((s1, s1)),
                  m_spec, m_spec, nat_spec],
        out_specs=nat_spec,
        out_shape=jax.ShapeDtypeStruct((batch, RADIX, s1, FOURIER_WIDTH), BF16),
        scratch_shapes=[y_scratch, y_scratch],
        compiler_params=_params("parallel", "arbitrary"),
        name="fourier",
    )(u16, chan_mats, _bf16_const(c1), _bf16_const(sn1 - c1), _bf16_const(c1 + sn1), mc, ms,
      gf.reshape(batch, RADIX, s1, FOURIER_WIDTH))
    return out.reshape(batch * seq, FOURIER_WIDTH)


def _outproj_kernel(x_ref, o16_ref, ga_ref, mf_ref, unperm_ref, w_ref, y_ref):
    unperm = unperm_ref[...]
    attn = []
    for g in range(OUTPROJ_GROUPS):
        rows = slice(g * RADIX, (g + 1) * RADIX)
        heads = [jnp.concatenate([o16_ref[hd, r, rows, :] for r in range(RADIX)], axis=0)
                 for hd in range(N_HEADS)]
        attn16 = jnp.concatenate(heads, axis=1)
        attn.append(jnp.dot(unperm, attn16, preferred_element_type=F32))
    y = x_ref[...] + jnp.dot(mf_ref[...], w_ref[ATTN_WIDTH:, :], preferred_element_type=F32)
    mix_a = (jnp.concatenate(attn, axis=0) * ga_ref[...].astype(F32)).astype(BF16)
    y_ref[...] = y + jnp.dot(mix_a, w_ref[:ATTN_WIDTH, :], preferred_element_type=F32)


def _outproj(x2d, o16, ga, mix_f, unperm, w_out, seq):
    t = x2d.shape[0]
    rows = OUTPROJ_GROUPS * ROW_GROUP
    steps_per_seq = seq // rows
    tok_spec = pl.BlockSpec((rows, ATTN_WIDTH), lambda i: (i, 0))
    x_spec = pl.BlockSpec((rows, D_MODEL), lambda i: (i, 0))
    return pl.pallas_call(
        _outproj_kernel,
        grid=(t // rows,),
        in_specs=[x_spec,
                  pl.BlockSpec((N_HEADS, None, RADIX, OUTPROJ_GROUPS * RADIX, HEAD_DIM),
                               lambda i: (0, i // steps_per_seq, 0, i % steps_per_seq, 0)),
                  tok_spec, tok_spec,
                  _resident((ROW_GROUP, ROW_GROUP)),
                  _resident((D_MODEL, D_MODEL))],
        out_specs=x_spec,
        out_shape=jax.ShapeDtypeStruct((t, D_MODEL), F32),
        compiler_params=_params("arbitrary"),
        name="outproj",
    )(x2d, o16, ga, mix_f, unperm, w_out)


def _layer(x, gain, w_in, qg, kg, chan_mats, w_out, perm, unperm):
    batch, seq, _ = x.shape
    x2d = x.reshape(batch * seq, D_MODEL)
    q16, k1, k16, v1, v16, ga, u16, gf = _inproj(x2d, batch, seq, gain, w_in, qg, kg, perm)
    o16 = _attention(q16, k1, v1, k16, v16, batch, seq)
    mix_f = _fourier(u16, chan_mats, gf, batch, seq)
    y = _outproj(x2d, o16, ga, mix_f, unperm, w_out, seq)
    return y.reshape(batch, seq, D_MODEL)


def kernel(x_prompt, x_sample, rms_gain, w_in, q_norm_gain, k_norm_gain, w_fourier, w_out):
    depth = rms_gain.shape[0]
    p = _group_permutation()
    perm, unperm = _bf16_const(p), _bf16_const(p.T)
    for l in range(depth):
        gain = rms_gain[l].reshape(1, D_MODEL)
        w_in_l = w_in[l].astype(BF16)
        w_out_l = w_out[l].astype(BF16)
        qg = q_norm_gain[l].reshape(1, HEAD_DIM)
        kg = k_norm_gain[l].reshape(1, HEAD_DIM)
        chan_mats = _channel_matrices(w_fourier[l])
        x_prompt = _layer(x_prompt, gain, w_in_l, qg, kg, chan_mats, w_out_l, perm, unperm)
        x_sample = _layer(x_sample, gain, w_in_l, qg, kg, chan_mats, w_out_l, perm, unperm)
    return (x_prompt, x_sample)
```

```python
import functools
import math

import jax
import jax.numpy as jnp
import numpy as np
from jax import lax
from jax.experimental import pallas as pl
from jax.experimental.pallas import tpu as pltpu

D_MODEL = 2048
ATTN_WIDTH = 1024
FOURIER_WIDTH = 1024
HEAD_DIM = 128
N_HEADS = ATTN_WIDTH // HEAD_DIM
N_GROUPS = 4
GROUP_DIM = FOURIER_WIDTH // N_GROUPS
ROPE_THETA = 500000.0
ROPE_DIM = HEAD_DIM // 4
ROPE_HALF = ROPE_DIM // 2
HALF_KEYS = 64
RMS_EPS = 1e-6
RADIX = 16
ROW_GROUP = RADIX * RADIX
Q_BLOCK = 128
FOURIER1_ROWS = 1024
FOURIER2_TILES = 8
ATTN_HEADS_PER_STEP = 2
INPROJ_GROUPS = 1
OUTPROJ_GROUPS = 2
BLOCKS_PER_GROUP_LONG = 6
BLOCKS_PER_GROUP_SHORT = 4
MASK_VALUE = -1e30

VMEM_LIMIT_BYTES = 56 * 1024 * 1024

F32 = jnp.float32
BF16 = jnp.bfloat16


def _params(*semantics):
    return pltpu.CompilerParams(dimension_semantics=semantics, vmem_limit_bytes=VMEM_LIMIT_BYTES)


def _resident(shape):
    return pl.BlockSpec(shape, lambda *_: (0,) * len(shape), pipeline_mode=pl.Buffered(1))


def _rope_tables(seq, scale):
    expo = np.arange(ROPE_HALF, dtype=np.float32) / np.float32(ROPE_HALF)
    inv_freq = (np.float32(1.0) / np.power(np.float32(ROPE_THETA), expo)).astype(np.float32)
    ang = (np.arange(seq, dtype=np.float32)[:, None] * inv_freq[None, :]).astype(np.float64)
    cos, sin = np.cos(ang), np.sin(ang)
    a = np.ones((seq, HEAD_DIM))
    b = np.zeros((seq, HEAD_DIM))
    c = np.zeros((seq, HEAD_DIM))
    a[:, :ROPE_HALF] = cos
    a[:, ROPE_HALF:ROPE_DIM] = cos
    b[:, ROPE_HALF:ROPE_DIM] = sin
    c[:, :ROPE_HALF] = -sin
    return tuple(jnp.asarray((t * scale).astype(np.float32)) for t in (a, b, c))


def _dft_cos_sin(n, rows, cols, scale=1.0):
    m = (np.asarray(rows, dtype=np.int64)[:, None] * np.asarray(cols, dtype=np.int64)[None, :]) % n
    ang = 2.0 * np.pi * m.astype(np.float64) / n
    return np.cos(ang) * scale, np.sin(ang) * scale


def _group_permutation():
    p = np.zeros((ROW_GROUP, ROW_GROUP), np.float32)
    j, r = np.meshgrid(np.arange(RADIX), np.arange(RADIX), indexing="ij")
    p[(RADIX * r + j).ravel(), (RADIX * j + r).ravel()] = 1.0
    return p


def _bf16_const(a):
    return jnp.asarray(np.asarray(a, np.float32)).astype(BF16)


def _silu(a):
    return a * (1.0 / (1.0 + jnp.exp(-a)))


def _inproj_kernel(x_ref, gain_ref, w_ref, qg_ref, kg_ref,
                   qa_ref, qb_ref, qc_ref, ka_ref, kb_ref, kc_ref, perm_ref,
                   q16_ref, k1_ref, k16_ref, v1_ref, v16_ref, ga_ref, u16_ref, gf_ref):
    x = x_ref[...]
    ms = jnp.mean(x * x, axis=-1, keepdims=True)
    h = (x * lax.rsqrt(ms + RMS_EPS) * gain_ref[...]).astype(BF16)
    perm = perm_ref[...]

    def column_tile(j):
        cols = slice(j * ATTN_WIDTH, (j + 1) * ATTN_WIDTH)
        return jnp.dot(h, w_ref[:, cols], preferred_element_type=F32)

    def norm_rope(acc, g_ref, a_ref, b_ref, c_ref):
        heads = []
        for hd in range(N_HEADS):
            a = acc[:, hd * HEAD_DIM:(hd + 1) * HEAD_DIM]
            ms_h = jnp.mean(a * a, axis=-1, keepdims=True)
            n = a * lax.rsqrt(ms_h + RMS_EPS) * g_ref[...]
            r = (n * a_ref[...]
                 + pltpu.roll(n, ROPE_HALF, 1) * b_ref[...]
                 + pltpu.roll(n, HEAD_DIM - ROPE_HALF, 1) * c_ref[...])
            heads.append(r.astype(BF16))
        return jnp.concatenate(heads, axis=1)

    def to_mod16(t):
        return [jnp.dot(perm, t[g * ROW_GROUP:(g + 1) * ROW_GROUP], preferred_element_type=F32)
                for g in range(INPROJ_GROUPS)]

    def store_heads(ref, t):
        for hd in range(N_HEADS):
            ref[hd] = t[:, hd * HEAD_DIM:(hd + 1) * HEAD_DIM]

    def store_head_pieces(ref, groups):
        for g, t in enumerate(groups):
            for hd in range(N_HEADS):
                for r in range(RADIX):
                    ref[hd, r, g * RADIX:(g + 1) * RADIX, :] = (
                        t[r * RADIX:(r + 1) * RADIX, hd * HEAD_DIM:(hd + 1) * HEAD_DIM].astype(ref.dtype))

    acc_q = column_tile(0)
    acc_k = column_tile(1)
    q = norm_rope(acc_q, qg_ref, qa_ref, qb_ref, qc_ref)
    acc_v = column_tile(2)
    store_head_pieces(q16_ref, to_mod16(q))

    k = norm_rope(acc_k, kg_ref, ka_ref, kb_ref, kc_ref)
    acc_ga = column_tile(3)
    store_heads(k1_ref, k)
    store_head_pieces(k16_ref, to_mod16(k))

    v = acc_v.astype(BF16)
    acc_u = column_tile(4)
    store_heads(v1_ref, v)
    store_head_pieces(v16_ref, to_mod16(v))

    ga_ref[...] = _silu(acc_ga).astype(BF16)
    acc_gf = column_tile(5)

    for g, u16 in enumerate(to_mod16(acc_u.astype(BF16))):
        for r in range(RADIX):
            u16_ref[r, g * RADIX:(g + 1) * RADIX, :] = u16[r * RADIX:(r + 1) * RADIX, :].astype(BF16)

    gf_ref[...] = _silu(acc_gf).astype(BF16)


def _inproj(x2d, batch, seq, gain, w_in, qg, kg, perm):
    t = x2d.shape[0]
    rows = INPROJ_GROUPS * ROW_GROUP
    pieces = INPROJ_GROUPS * RADIX
    groups = seq // rows
    sub = seq // RADIX
    qa, qb, qc = _rope_tables(seq, HEAD_DIM ** -0.5 * math.log2(math.e))
    ka, kb, kc = _rope_tables(seq, 1.0)
    rope_spec = pl.BlockSpec((rows, HEAD_DIM), lambda i: (i % groups, 0))
    tok_spec = pl.BlockSpec((rows, ATTN_WIDTH), lambda i: (i, 0))
    head_nat_spec = pl.BlockSpec((N_HEADS, rows, HEAD_DIM), lambda i: (0, i, 0))
    head_m16_spec = pl.BlockSpec((N_HEADS, None, RADIX, pieces, HEAD_DIM),
                                 lambda i: (0, i // groups, 0, i % groups, 0))
    head_nat = jax.ShapeDtypeStruct((N_HEADS, t, HEAD_DIM), BF16)
    head_m16 = lambda dt: jax.ShapeDtypeStruct((N_HEADS, batch, RADIX, sub, HEAD_DIM), dt)
    return pl.pallas_call(
        _inproj_kernel,
        grid=(t // rows,),
        in_specs=[
            pl.BlockSpec((rows, D_MODEL), lambda i: (i, 0)),
            _resident((1, D_MODEL)),
            _resident(w_in.shape),
            _resident((1, HEAD_DIM)), _resident((1, HEAD_DIM)),
            rope_spec, rope_spec, rope_spec, rope_spec, rope_spec, rope_spec,
            _resident((ROW_GROUP, ROW_GROUP)),
        ],
        out_specs=[
            head_m16_spec, head_nat_spec, head_m16_spec, head_nat_spec, head_m16_spec,
            tok_spec,
            pl.BlockSpec((None, RADIX, pieces, FOURIER_WIDTH), lambda i: (i // groups, 0, i % groups, 0)),
            tok_spec,
        ],
        out_shape=[
            head_m16(F32), head_nat, head_m16(BF16), head_nat, head_m16(BF16),
            jax.ShapeDtypeStruct((t, ATTN_WIDTH), BF16),
            jax.ShapeDtypeStruct((batch, RADIX, sub, FOURIER_WIDTH), BF16),
            jax.ShapeDtypeStruct((t, FOURIER_WIDTH), BF16),
        ],
        compiler_params=_params("arbitrary"),
        name="inproj",
    )(x2d, gain, w_in, qg, kg, qa, qb, qc, ka, kb, kc, perm)


def _window_start(first, half, total, window):
    return min(max(first - half, 0), total - window)


class _Pattern:
    def __init__(self, q_first, k_first, diff0, scale):
        deltas = [scale * (q0 - k0) for q0, k0 in zip(q_first, k_first)]
        uniq = sorted(set(deltas))
        self.q_first, self.k_first = q_first, k_first
        self.table_of = [uniq.index(d) for d in deltas]
        self.bias = jnp.asarray(np.stack(
            [np.where(np.abs(diff0 + d) <= HALF_KEYS, 0.0, MASK_VALUE) for d in uniq]).astype(np.float32)
        ).astype(BF16)


def _attention_patterns(seq):
    sub = seq // RADIX
    qi = np.arange(Q_BLOCK)[:, None]
    win16 = min(Q_BLOCK + 2 * HALF_KEYS, sub)
    q16 = list(range(0, sub, Q_BLOCK))
    p16 = _Pattern(q16, [_window_start(q0, HALF_KEYS, sub, win16) for q0 in q16],
                   qi - np.arange(win16)[None, :], 1)
    q_rows, k_rows = Q_BLOCK // 4, Q_BLOCK // 4 + 2 * HALF_KEYS // 4
    kc = np.arange(4 * k_rows)[None, :]
    q4 = list(range(0, sub, q_rows))
    p4 = _Pattern(q4, [_window_start(q0, HALF_KEYS // 4, sub, k_rows) for q0 in q4],
                  4 * (qi % q_rows - kc % k_rows) + (qi // q_rows - kc // k_rows), 4)
    win1 = Q_BLOCK + 2 * HALF_KEYS
    q1 = list(range(0, seq, Q_BLOCK))
    p1 = _Pattern(q1, [_window_start(q0, HALF_KEYS, seq, win1) for q0 in q1],
                  RADIX * (qi % (Q_BLOCK // RADIX)) + qi // (Q_BLOCK // RADIX) - np.arange(win1)[None, :], 1)
    return p16, p4, p1


def _attn_kernel(q16_ref, k1_ref, v1_ref, k16_ref, v16_ref, b16_ref, b4_ref, b1_ref,
                 o_ref, acc_s, m_s, l_s, p0_s, w0_s, p1_s, w1_s, *, pat16, pat4, pat1):
    def gather(ref, pieces):
        return jnp.concatenate([ref[r, pl.ds(s0, n), :] for r, s0, n in pieces], axis=0)

    def scatter(ref, pieces, value):
        off = 0
        for r, s0, n in pieces:
            ref[r, pl.ds(s0, n), :] = value[off:off + n]
            off += n

    def scores_stage(blocks, first, p_scr, w_scr):
        for i, (q, k, _, bias, pieces, (_, m_h, _)) in enumerate(blocks):
            s = lax.dot_general(q(), k(), (((1,), (1,)), ((), ())), preferred_element_type=F32)
            s = s.astype(BF16) + bias()
            m_new = jnp.broadcast_to(jnp.max(s, axis=-1, keepdims=True), (Q_BLOCK, HEAD_DIM)).astype(F32)
            if not first:
                m_old = gather(m_h, pieces)
                m_new = jnp.maximum(m_old, m_new)
                w_scr[i] = jnp.exp2(m_old - m_new)
            width = s.shape[1]
            m_b = m_new.astype(BF16)
            p_scr[i, :, :width] = jnp.exp2(s - jnp.concatenate([m_b] * (width // HEAD_DIM), axis=1))
            scatter(m_h, pieces, m_new)

    def values_stage(blocks, first, p_scr, w_scr):
        for i, (_, _, v, _, pieces, (acc_h, _, l_h)) in enumerate(blocks):
            vw = v()
            v_aug = jnp.concatenate([vw, jnp.ones(vw.shape, BF16)], axis=1)
            pv = jnp.dot(p_scr[i, :, :vw.shape[0]], v_aug, preferred_element_type=F32)
            acc, l = pv[:, :HEAD_DIM], pv[:, HEAD_DIM:]
            if not first:
                w_old = w_scr[i]
                acc = w_old * gather(acc_h, pieces) + acc
                l = w_old * gather(l_h, pieces) + l
            scatter(l_h, pieces, l)
            scatter(acc_h, pieces, acc)

    win16 = b16_ref.shape[2]
    q_rows4 = Q_BLOCK // 4
    k_rows4 = b4_ref.shape[2] // 4
    q_rows1 = Q_BLOCK // RADIX
    win1 = b1_ref.shape[2]
    blocks16, blocks4, blocks1 = [], [], []
    for hd in range(q16_ref.shape[0]):
        q16, k1, v1, k16, v16 = (ref.at[hd] for ref in (q16_ref, k1_ref, v1_ref, k16_ref, v16_ref))
        state = (acc_s.at[hd], m_s.at[hd], l_s.at[hd])

        for r in range(RADIX):
            for blk, (l0, ws) in enumerate(zip(pat16.q_first, pat16.k_first)):
                blocks16.append((
                    lambda q16=q16, r=r, l0=l0: q16[r, l0:l0 + Q_BLOCK, :].astype(BF16),
                    lambda k16=k16, r=r, ws=ws: k16[r, ws:ws + win16, :],
                    lambda v16=v16, r=r, ws=ws: v16[r, ws:ws + win16, :],
                    lambda blk=blk: b16_ref[pat16.table_of[blk]],
                    [(r, l0, Q_BLOCK)], state))

        for r4 in range(4):
            res = [r4 + 4 * a for a in range(4)]
            for blk, (l0, ws) in enumerate(zip(pat4.q_first, pat4.k_first)):
                q_pieces = [(r, l0, q_rows4) for r in res]
                k_pieces = [(r, ws, k_rows4) for r in res]
                blocks4.append((
                    lambda q16=q16, p=q_pieces: gather(q16, p).astype(BF16),
                    lambda k16=k16, p=k_pieces: gather(k16, p),
                    lambda v16=v16, p=k_pieces: gather(v16, p),
                    lambda blk=blk: b4_ref[pat4.table_of[blk]],
                    q_pieces, state))

        for blk, (n0, ws) in enumerate(zip(pat1.q_first, pat1.k_first)):
            q_pieces = [(r, n0 // RADIX, q_rows1) for r in range(RADIX)]
            blocks1.append((
                lambda q16=q16, p=q_pieces: gather(q16, p).astype(BF16),
                lambda k1=k1, ws=ws: k1[ws:ws + win1, :],
                lambda v1=v1, ws=ws: v1[ws:ws + win1, :],
                lambda blk=blk: b1_ref[pat1.table_of[blk]],
                q_pieces, state))

    groups = []
    for blocks, first in ((blocks16, True), (blocks4, False), (blocks1, False)):
        groups.extend((blocks[i:i + p0_s.shape[0]], first) for i in range(0, len(blocks), p0_s.shape[0]))

    slots = ((p0_s, w0_s), (p1_s, w1_s))
    scores_stage(*groups[0], *slots[0])
    for g in range(1, len(groups)):
        scores_stage(*groups[g], *slots[g % 2])
        values_stage(*groups[g - 1], *slots[(g - 1) % 2])
    values_stage(*groups[-1], *slots[(len(groups) - 1) % 2])

    def finish(i, carry):
        hd, r = i // RADIX, i % RADIX
        o_ref[hd, r] = (acc_s[hd, r] * (1.0 / l_s[hd, r])).astype(BF16)
        return carry

    lax.fori_loop(0, q16_ref.shape[0] * RADIX, finish, 0)


def _attention(q16, k1, v1, k16, v16, batch, seq):
    sub = seq // RADIX
    heads = ATTN_HEADS_PER_STEP
    pat16, pat4, pat1 = _attention_patterns(seq)
    m16_spec = pl.BlockSpec((heads, None, RADIX, sub, HEAD_DIM), lambda b, h: (h, b, 0, 0, 0))
    nat_spec = pl.BlockSpec((heads, seq, HEAD_DIM), lambda b, h: (h, b, 0))
    state = pltpu.VMEM((heads, RADIX, sub, HEAD_DIM), F32)
    group = BLOCKS_PER_GROUP_LONG if sub > Q_BLOCK else BLOCKS_PER_GROUP_SHORT
    probs = pltpu.VMEM((group, Q_BLOCK, Q_BLOCK + 2 * HALF_KEYS), BF16)
    rescale = pltpu.VMEM((group, Q_BLOCK, HEAD_DIM), F32)
    return pl.pallas_call(
        functools.partial(_attn_kernel, pat16=pat16, pat4=pat4, pat1=pat1),
        grid=(batch, N_HEADS // heads),
        in_specs=[m16_spec, nat_spec, nat_spec, m16_spec, m16_spec,
                  _resident(pat16.bias.shape), _resident(pat4.bias.shape), _resident(pat1.bias.shape)],
        out_specs=m16_spec,
        out_shape=jax.ShapeDtypeStruct((N_HEADS, batch, RADIX, sub, HEAD_DIM), BF16),
        scratch_shapes=[state, state, state, probs, rescale, probs, rescale],
        compiler_params=_params("parallel", "parallel"),
        name="attention",
    )(q16, k1, v1, k16, v16, pat16.bias, pat4.bias, pat1.bias)


def _chanmat_kernel(c_ref, s_ref, w_ref, m_ref):
    for g in range(N_GROUPS):
        w = w_ref[g]
        mc = jnp.dot(c_ref[...], w, preferred_element_type=F32, precision=lax.Precision.HIGHEST)
        ms = jnp.dot(s_ref[...], w, preferred_element_type=F32, precision=lax.Precision.HIGHEST)
        m_ref[g, :, :GROUP_DIM] = mc.astype(BF16)
        m_ref[g, :, GROUP_DIM:] = ms.astype(BF16)


def _channel_matrices(w_fourier):
    idx = np.arange(GROUP_DIM)
    cc, sc = _dft_cos_sin(GROUP_DIM, idx, idx, GROUP_DIM ** -0.5)
    return pl.pallas_call(
        _chanmat_kernel,
        out_shape=jax.ShapeDtypeStruct((N_GROUPS, GROUP_DIM, 2 * GROUP_DIM), BF16),
        name="chanmat",
    )(jnp.asarray(cc, F32), jnp.asarray(sc, F32), w_fourier)


def _fourier_kernel(u_ref, m_ref, c_ref, smc_ref, cps_ref, mc_ref, ms_ref, gf_ref, o_ref, yr_s, yi_s,
                    *, stage1_steps, tiles):
    t = pl.program_id(1)
    dot = functools.partial(jnp.dot, preferred_element_type=F32)
    per_step = u_ref.shape[0]

    @pl.when(t < stage1_steps)
    def _():
        z = []
        for j in range(per_step):
            a_parts, b_parts, ab_parts = [], [], []
            for g in range(N_GROUPS):
                ab = dot(u_ref[j, :, g * GROUP_DIM:(g + 1) * GROUP_DIM], m_ref[g])
                a, b = ab[:, :GROUP_DIM], ab[:, GROUP_DIM:]
                a_parts.append(a.astype(BF16))
                b_parts.append(b.astype(BF16))
                ab_parts.append((a + b).astype(BF16))
            z.append(tuple(jnp.concatenate(p, axis=1) for p in (a_parts, b_parts, ab_parts)))
        c, s_minus_c, c_plus_s = c_ref[...], smc_ref[...], cps_ref[...]
        for j, (a, b, a_plus_b) in enumerate(z):
            n2 = t * per_step + j
            k1 = dot(c, a_plus_b)
            yr_s[n2] = (k1 - dot(c_plus_s, b)).astype(BF16)
            yi_s[n2] = (k1 + dot(s_minus_c, a)).astype(BF16)

    @pl.when(t >= stage1_steps)
    def _():
        first_row = (t - stage1_steps) * (tiles * RADIX)
        for i in range(tiles):
            rows = pl.ds(pl.multiple_of(first_row + i * RADIX, RADIX), RADIX)
            yr = jnp.concatenate([yr_s[n2, rows, :] for n2 in range(RADIX)], axis=0)
            yi = jnp.concatenate([yi_s[n2, rows, :] for n2 in range(RADIX)], axis=0)
            x = dot(mc_ref[i], yr) + dot(ms_ref[i], yi)
            out_rows = slice(i * RADIX, (i + 1) * RADIX)
            for k2 in range(RADIX):
                piece = x[k2 * RADIX:(k2 + 1) * RADIX, :] * gf_ref[k2, out_rows, :].astype(F32)
                o_ref[k2, out_rows, :] = piece.astype(BF16)


def _stage2_matrices(seq):
    s1 = seq // RADIX
    tiles = s1 // RADIX
    c, k2, k1, n2 = np.meshgrid(np.arange(tiles), np.arange(RADIX), np.arange(RADIX), np.arange(RADIX),
                                indexing="ij")
    k = RADIX * c + k1 + s1 * k2
    ang = 2.0 * np.pi * ((n2 * k) % seq).astype(np.float64) / seq
    mc = np.zeros((tiles, ROW_GROUP, ROW_GROUP), np.float32)
    ms = np.zeros((tiles, ROW_GROUP, ROW_GROUP), np.float32)
    mc[c, RADIX * k2 + k1, RADIX * n2 + k1] = np.cos(ang) * seq ** -0.5
    ms[c, RADIX * k2 + k1, RADIX * n2 + k1] = -np.sin(ang) * seq ** -0.5
    return _bf16_const(mc), _bf16_const(ms)


def _fourier(u16, chan_mats, gf, batch, seq, tiles=FOURIER2_TILES):
    s1 = seq // RADIX
    idx = np.arange(s1)
    c1, sn1 = _dft_cos_sin(s1, idx, idx)
    mc, ms = _stage2_matrices(seq)
    per_step = FOURIER1_ROWS // s1
    stage1_steps = RADIX // per_step
    rows = tiles * RADIX
    stage2 = lambda t: jnp.maximum(t - stage1_steps, 0)
    nat_spec = pl.BlockSpec((None, RADIX, rows, FOURIER_WIDTH), lambda b, t: (b, 0, stage2(t), 0))
    m_spec = pl.BlockSpec((tiles, ROW_GROUP, ROW_GROUP), lambda b, t: (stage2(t), 0, 0))
    y_scratch = pltpu.VMEM((RADIX, s1, FOURIER_WIDTH), BF16)
    out = pl.pallas_call(
        functools.partial(_fourier_kernel, stage1_steps=stage1_steps, tiles=tiles),
        grid=(batch, stage1_steps + s1 // rows),
        in_specs=[pl.BlockSpec((None, per_step, s1, FOURIER_WIDTH),
                               lambda b, t: (b, jnp.minimum(t, stage1_steps - 1), 0, 0)),
                  _resident(chan_mats.shape), _resident((s1, s1)), _resident((s1, s1)), _resident((s1, s1)),
                  m_spec, m_spec, nat_spec],
        out_specs=nat_spec,
        out_shape=jax.ShapeDtypeStruct((batch, RADIX, s1, FOURIER_WIDTH), BF16),
        scratch_shapes=[y_scratch, y_scratch],
        compiler_params=_params("parallel", "arbitrary"),
        name="fourier",
    )(u16, chan_mats, _bf16_const(c1), _bf16_const(sn1 - c1), _bf16_const(c1 + sn1), mc, ms,
      gf.reshape(batch, RADIX, s1, FOURIER_WIDTH))
    return out.reshape(batch * seq, FOURIER_WIDTH)


def _outproj_kernel(x_ref, o16_ref, ga_ref, mf_ref, unperm_ref, w_ref, y_ref):
    unperm = unperm_ref[...]
    attn = []
    for g in range(OUTPROJ_GROUPS):
        rows = slice(g * RADIX, (g + 1) * RADIX)
        heads = [jnp.concatenate([o16_ref[hd, r, rows, :] for r in range(RADIX)], axis=0)
                 for hd in range(N_HEADS)]
        attn16 = jnp.concatenate(heads, axis=1)
        attn.append(jnp.dot(unperm, attn16, preferred_element_type=F32))
    y = x_ref[...] + jnp.dot(mf_ref[...], w_ref[ATTN_WIDTH:, :], preferred_element_type=F32)
    mix_a = (jnp.concatenate(attn, axis=0) * ga_ref[...].astype(F32)).astype(BF16)
    y_ref[...] = y + jnp.dot(mix_a, w_ref[:ATTN_WIDTH, :], preferred_element_type=F32)


def _outproj(x2d, o16, ga, mix_f, unperm, w_out, seq):
    t = x2d.shape[0]
    rows = OUTPROJ_GROUPS * ROW_GROUP
    steps_per_seq = seq // rows
    tok_spec = pl.BlockSpec((rows, ATTN_WIDTH), lambda i: (i, 0))
    x_spec = pl.BlockSpec((rows, D_MODEL), lambda i: (i, 0))
    return pl.pallas_call(
        _outproj_kernel,
        grid=(t // rows,),
        in_specs=[x_spec,
                  pl.BlockSpec((N_HEADS, None, RADIX, OUTPROJ_GROUPS * RADIX, HEAD_DIM),
                               lambda i: (0, i // steps_per_seq, 0, i % steps_per_seq, 0)),
                  tok_spec, tok_spec,
                  _resident((ROW_GROUP, ROW_GROUP)),
                  _resident((D_MODEL, D_MODEL))],
        out_specs=x_spec,
        out_shape=jax.ShapeDtypeStruct((t, D_MODEL), F32),
        compiler_params=_params("arbitrary"),
        name="outproj",
    )(x2d, o16, ga, mix_f, unperm, w_out)


def _layer(x, gain, w_in, qg, kg, chan_mats, w_out, perm, unperm):
    batch, seq, _ = x.shape
    x2d = x.reshape(batch * seq, D_MODEL)
    q16, k1, k16, v1, v16, ga, u16, gf = _inproj(x2d, batch, seq, gain, w_in, qg, kg, perm)
    o16 = _attention(q16, k1, v1, k16, v16, batch, seq)
    mix_f = _fourier(u16, chan_mats, gf, batch, seq)
    y = _outproj(x2d, o16, ga, mix_f, unperm, w_out, seq)
    return y.reshape(batch, seq, D_MODEL)


def kernel(x_prompt, x_sample, rms_gain, w_in, q_norm_gain, k_norm_gain, w_fourier, w_out):
    depth = rms_gain.shape[0]
    p = _group_permutation()
    perm, unperm = _bf16_const(p), _bf16_const(p.T)
    for l in range(depth):
        gain = rms_gain[l].reshape(1, D_MODEL)
        w_in_l = w_in[l].astype(BF16)
        w_out_l = w_out[l].astype(BF16)
        qg = q_norm_gain[l].reshape(1, HEAD_DIM)
        kg = k_norm_gain[l].reshape(1, HEAD_DIM)
        chan_mats = _channel_matrices(w_fourier[l])
        x_prompt = _layer(x_prompt, gain, w_in_l, qg, kg, chan_mats, w_out_l, perm, unperm)
        x_sample = _layer(x_sample, gain, w_in_l, qg, kg, chan_mats, w_out_l, perm, unperm)
    return (x_prompt, x_sample)
```

```python
import functools
import math

import jax
import jax.numpy as jnp
import numpy as np
from jax import lax
from jax.experimental import pallas as pl
from jax.experimental.pallas import tpu as pltpu

D_MODEL = 2048
ATTN_WIDTH = 1024
FOURIER_WIDTH = 1024
HEAD_DIM = 128
N_HEADS = ATTN_WIDTH // HEAD_DIM
N_GROUPS = 4
GROUP_DIM = FOURIER_WIDTH // N_GROUPS
ROPE_THETA = 500000.0
ROPE_DIM = HEAD_DIM // 4
ROPE_HALF = ROPE_DIM // 2
HALF_KEYS = 64
RMS_EPS = 1e-6
RADIX = 16
ROW_GROUP = RADIX * RADIX
Q_BLOCK = 128
FOURIER1_ROWS = 1024
FOURIER2_TILES = 8
ATTN_HEADS_PER_STEP = 2
INPROJ_GROUPS = 1
OUTPROJ_GROUPS = 2
BLOCKS_PER_GROUP_LONG = 6
BLOCKS_PER_GROUP_SHORT = 4
MASK_VALUE = -1e30

VMEM_LIMIT_BYTES = 56 * 1024 * 1024

F32 = jnp.float32
BF16 = jnp.bfloat16


def _params(*semantics):
    return pltpu.CompilerParams(dimension_semantics=semantics, vmem_limit_bytes=VMEM_LIMIT_BYTES)


def _resident(shape):
    return pl.BlockSpec(shape, lambda *_: (0,) * len(shape), pipeline_mode=pl.Buffered(1))


def _rope_tables(seq, scale):
    expo = np.arange(ROPE_HALF, dtype=np.float32) / np.float32(ROPE_HALF)
    inv_freq = (np.float32(1.0) / np.power(np.float32(ROPE_THETA), expo)).astype(np.float32)
    ang = (np.arange(seq, dtype=np.float32)[:, None] * inv_freq[None, :]).astype(np.float64)
    cos, sin = np.cos(ang), np.sin(ang)
    a = np.ones((seq, HEAD_DIM))
    b = np.zeros((seq, HEAD_DIM))
    c = np.zeros((seq, HEAD_DIM))
    a[:, :ROPE_HALF] = cos
    a[:, ROPE_HALF:ROPE_DIM] = cos
    b[:, ROPE_HALF:ROPE_DIM] = sin
    c[:, :ROPE_HALF] = -sin
    return tuple(jnp.asarray((t * scale).astype(np.float32)) for t in (a, b, c))


def _dft_cos_sin(n, rows, cols, scale=1.0):
    m = (np.asarray(rows, dtype=np.int64)[:, None] * np.asarray(cols, dtype=np.int64)[None, :]) % n
    ang = 2.0 * np.pi * m.astype(np.float64) / n
    return np.cos(ang) * scale, np.sin(ang) * scale


def _group_permutation():
    p = np.zeros((ROW_GROUP, ROW_GROUP), np.float32)
    j, r = np.meshgrid(np.arange(RADIX), np.arange(RADIX), indexing="ij")
    p[(RADIX * r + j).ravel(), (RADIX * j + r).ravel()] = 1.0
    return p


def _bf16_const(a):
    return jnp.asarray(np.asarray(a, np.float32)).astype(BF16)


def _silu(a):
    return a * (1.0 / (1.0 + jnp.exp(-a)))


def _inproj_kernel(x_ref, gain_ref, w_ref, qg_ref, kg_ref,
                   qa_ref, qb_ref, qc_ref, ka_ref, kb_ref, kc_ref, perm_ref,
                   q16_ref, k1_ref, k16_ref, v1_ref, v16_ref, ga_ref, u16_ref, gf_ref):
    x = x_ref[...]
    ms = jnp.mean(x * x, axis=-1, keepdims=True)
    h = (x * lax.rsqrt(ms + RMS_EPS) * gain_ref[...]).astype(BF16)
    perm = perm_ref[...]

    def column_tile(j):
        cols = slice(j * ATTN_WIDTH, (j + 1) * ATTN_WIDTH)
        return jnp.dot(h, w_ref[:, cols], preferred_element_type=F32)

    def norm_rope(acc, g_ref, a_ref, b_ref, c_ref):
        heads = []
        for hd in range(N_HEADS):
            a = acc[:, hd * HEAD_DIM:(hd + 1) * HEAD_DIM]
            ms_h = jnp.mean(a * a, axis=-1, keepdims=True)
            n = a * lax.rsqrt(ms_h + RMS_EPS) * g_ref[...]
            r = (n * a_ref[...]
                 + pltpu.roll(n, ROPE_HALF, 1) * b_ref[...]
                 + pltpu.roll(n, HEAD_DIM - ROPE_HALF, 1) * c_ref[...])
            heads.append(r.astype(BF16))
        return jnp.concatenate(heads, axis=1)

    def to_mod16(t):
        return [jnp.dot(perm, t[g * ROW_GROUP:(g + 1) * ROW_GROUP], preferred_element_type=F32)
                for g in range(INPROJ_GROUPS)]

    def store_heads(ref, t):
        for hd in range(N_HEADS):
            ref[hd] = t[:, hd * HEAD_DIM:(hd + 1) * HEAD_DIM]

    def store_head_pieces(ref, groups):
        for g, t in enumerate(groups):
            for hd in range(N_HEADS):
                for r in range(RADIX):
                    ref[hd, r, g * RADIX:(g + 1) * RADIX, :] = (
                        t[r * RADIX:(r + 1) * RADIX, hd * HEAD_DIM:(hd + 1) * HEAD_DIM].astype(ref.dtype))

    acc_q = column_tile(0)
    acc_k = column_tile(1)
    q = norm_rope(acc_q, qg_ref, qa_ref, qb_ref, qc_ref)
    acc_v = column_tile(2)
    store_head_pieces(q16_ref, to_mod16(q))

    k = norm_rope(acc_k, kg_ref, ka_ref, kb_ref, kc_ref)
    acc_ga = column_tile(3)
    store_heads(k1_ref, k)
    store_head_pieces(k16_ref, to_mod16(k))

    v = acc_v.astype(BF16)
    acc_u = column_tile(4)
    store_heads(v1_ref, v)
    store_head_pieces(v16_ref, to_mod16(v))

    ga_ref[...] = _silu(acc_ga).astype(BF16)
    acc_gf = column_tile(5)

    for g, u16 in enumerate(to_mod16(acc_u.astype(BF16))):
        for r in range(RADIX):
            u16_ref[r, g * RADIX:(g + 1) * RADIX, :] = u16[r * RADIX:(r + 1) * RADIX, :].astype(BF16)

    gf_ref[...] = _silu(acc_gf).astype(BF16)


def _inproj(x2d, batch, seq, gain, w_in, qg, kg, perm):
    t = x2d.shape[0]
    rows = INPROJ_GROUPS * ROW_GROUP
    pieces = INPROJ_GROUPS * RADIX
    groups = seq // rows
    sub = seq // RADIX
    qa, qb, qc = _rope_tables(seq, HEAD_DIM ** -0.5 * math.log2(math.e))
    ka, kb, kc = _rope_tables(seq, 1.0)
    rope_spec = pl.BlockSpec((rows, HEAD_DIM), lambda i: (i % groups, 0))
    tok_spec = pl.BlockSpec((rows, ATTN_WIDTH), lambda i: (i, 0))
    head_nat_spec = pl.BlockSpec((N_HEADS, rows, HEAD_DIM), lambda i: (0, i, 0))
    head_m16_spec = pl.BlockSpec((N_HEADS, None, RADIX, pieces, HEAD_DIM),
                                 lambda i: (0, i // groups, 0, i % groups, 0))
    head_nat = jax.ShapeDtypeStruct((N_HEADS, t, HEAD_DIM), BF16)
    head_m16 = lambda dt: jax.ShapeDtypeStruct((N_HEADS, batch, RADIX, sub, HEAD_DIM), dt)
    return pl.pallas_call(
        _inproj_kernel,
        grid=(t // rows,),
        in_specs=[
            pl.BlockSpec((rows, D_MODEL), lambda i: (i, 0)),
            _resident((1, D_MODEL)),
            _resident(w_in.shape),
            _resident((1, HEAD_DIM)), _resident((1, HEAD_DIM)),
            rope_spec, rope_spec, rope_spec, rope_spec, rope_spec, rope_spec,
            _resident((ROW_GROUP, ROW_GROUP)),
        ],
        out_specs=[
            head_m16_spec, head_nat_spec, head_m16_spec, head_nat_spec, head_m16_spec,
            tok_spec,
            pl.BlockSpec((None, RADIX, pieces, FOURIER_WIDTH), lambda i: (i // groups, 0, i % groups, 0)),
            tok_spec,
        ],
        out_shape=[
            head_m16(F32), head_nat, head_m16(BF16), head_nat, head_m16(BF16),
            jax.ShapeDtypeStruct((t, ATTN_WIDTH), BF16),
            jax.ShapeDtypeStruct((batch, RADIX, sub, FOURIER_WIDTH), BF16),
            jax.ShapeDtypeStruct((t, FOURIER_WIDTH), BF16),
        ],
        compiler_params=_params("arbitrary"),
        name="inproj",
    )(x2d, gain, w_in, qg, kg, qa, qb, qc, ka, kb, kc, perm)


def _window_start(first, half, total, window):
    return min(max(first - half, 0), total - window)


class _Pattern:
    def __init__(self, q_first, k_first, diff0, scale):
        deltas = [scale * (q0 - k0) for q0, k0 in zip(q_first, k_first)]
        uniq = sorted(set(deltas))
        self.q_first, self.k_first = q_first, k_first
        self.table_of = [uniq.index(d) for d in deltas]
        self.bias = jnp.asarray(np.stack(
            [np.where(np.abs(diff0 + d) <= HALF_KEYS, 0.0, MASK_VALUE) for d in uniq]).astype(np.float32)
        ).astype(BF16)


def _attention_patterns(seq):
    sub = seq // RADIX
    qi = np.arange(Q_BLOCK)[:, None]
    win16 = min(Q_BLOCK + 2 * HALF_KEYS, sub)
    q16 = list(range(0, sub, Q_BLOCK))
    p16 = _Pattern(q16, [_window_start(q0, HALF_KEYS, sub, win16) for q0 in q16],
                   qi - np.arange(win16)[None, :], 1)
    q_rows, k_rows = Q_BLOCK // 4, Q_BLOCK // 4 + 2 * HALF_KEYS // 4
    kc = np.arange(4 * k_rows)[None, :]
    q4 = list(range(0, sub, q_rows))
    p4 = _Pattern(q4, [_window_start(q0, HALF_KEYS // 4, sub, k_rows) for q0 in q4],
                  4 * (qi % q_rows - kc % k_rows) + (qi // q_rows - kc // k_rows), 4)
    win1 = Q_BLOCK + 2 * HALF_KEYS
    q1 = list(range(0, seq, Q_BLOCK))
    p1 = _Pattern(q1, [_window_start(q0, HALF_KEYS, seq, win1) for q0 in q1],
                  RADIX * (qi % (Q_BLOCK // RADIX)) + qi // (Q_BLOCK // RADIX) - np.arange(win1)[None, :], 1)
    return p16, p4, p1


def _attn_kernel(q16_ref, k1_ref, v1_ref, k16_ref, v16_ref, b16_ref, b4_ref, b1_ref,
                 o_ref, acc_s, m_s, l_s, p0_s, w0_s, p1_s, w1_s, *, pat16, pat4, pat1):
    def gather(ref, pieces):
        return jnp.concatenate([ref[r, pl.ds(s0, n), :] for r, s0, n in pieces], axis=0)

    def scatter(ref, pieces, value):
        off = 0
        for r, s0, n in pieces:
            ref[r, pl.ds(s0, n), :] = value[off:off + n]
            off += n

    def scores_stage(blocks, first, p_scr, w_scr):
        for i, (q, k, _, bias, pieces, (_, m_h, _)) in enumerate(blocks):
            s = lax.dot_general(q(), k(), (((1,), (1,)), ((), ())), preferred_element_type=F32)
            s = s.astype(BF16) + bias()
            m_new = jnp.broadcast_to(jnp.max(s, axis=-1, keepdims=True), (Q_BLOCK, HEAD_DIM)).astype(F32)
            if not first:
                m_old = gather(m_h, pieces)
                m_new = jnp.maximum(m_old, m_new)
                w_scr[i] = jnp.exp2(m_old - m_new)
            width = s.shape[1]
            m_b = m_new.astype(BF16)
            p_scr[i, :, :width] = jnp.exp2(s - jnp.concatenate([m_b] * (width // HEAD_DIM), axis=1))
            scatter(m_h, pieces, m_new)

    def values_stage(blocks, first, p_scr, w_scr):
        for i, (_, _, v, _, pieces, (acc_h, _, l_h)) in enumerate(blocks):
            vw = v()
            v_aug = jnp.concatenate([vw, jnp.ones(vw.shape, BF16)], axis=1)
            pv = jnp.dot(p_scr[i, :, :vw.shape[0]], v_aug, preferred_element_type=F32)
            acc, l = pv[:, :HEAD_DIM], pv[:, HEAD_DIM:]
            if not first:
                w_old = w_scr[i]
                acc = w_old * gather(acc_h, pieces) + acc
                l = w_old * gather(l_h, pieces) + l
            scatter(l_h, pieces, l)
            scatter(acc_h, pieces, acc)

    win16 = b16_ref.shape[2]
    q_rows4 = Q_BLOCK // 4
    k_rows4 = b4_ref.shape[2] // 4
    q_rows1 = Q_BLOCK // RADIX
    win1 = b1_ref.shape[2]
    blocks16, blocks4, blocks1 = [], [], []
    for hd in range(q16_ref.shape[0]):
        q16, k1, v1, k16, v16 = (ref.at[hd] for ref in (q16_ref, k1_ref, v1_ref, k16_ref, v16_ref))
        state = (acc_s.at[hd], m_s.at[hd], l_s.at[hd])

        for r in range(RADIX):
            for blk, (l0, ws) in enumerate(zip(pat16.q_first, pat16.k_first)):
                blocks16.append((
                    lambda q16=q16, r=r, l0=l0: q16[r, l0:l0 + Q_BLOCK, :].astype(BF16),
                    lambda k16=k16, r=r, ws=ws: k16[r, ws:ws + win16, :],
                    lambda v16=v16, r=r, ws=ws: v16[r, ws:ws + win16, :],
                    lambda blk=blk: b16_ref[pat16.table_of[blk]],
                    [(r, l0, Q_BLOCK)], state))

        for r4 in range(4):
            res = [r4 + 4 * a for a in range(4)]
            for blk, (l0, ws) in enumerate(zip(pat4.q_first, pat4.k_first)):
                q_pieces = [(r, l0, q_rows4) for r in res]
                k_pieces = [(r, ws, k_rows4) for r in res]
                blocks4.append((
                    lambda q16=q16, p=q_pieces: gather(q16, p).astype(BF16),
                    lambda k16=k16, p=k_pieces: gather(k16, p),
                    lambda v16=v16, p=k_pieces: gather(v16, p),
                    lambda blk=blk: b4_ref[pat4.table_of[blk]],
                    q_pieces, state))

        for blk, (n0, ws) in enumerate(zip(pat1.q_first, pat1.k_first)):
            q_pieces = [(r, n0 // RADIX, q_rows1) for r in range(RADIX)]
            blocks1.append((
                lambda q16=q16, p=q_pieces: gather(q16, p).astype(BF16),
                lambda k1=k1, ws=ws: k1[ws:ws + win1, :],
                lambda v1=v1, ws=ws: v1[ws:ws + win1, :],
                lambda blk=blk: b1_ref[pat1.table_of[blk]],
                q_pieces, state))

    groups = []
    for blocks, first in ((blocks16, True), (blocks4, False), (blocks1, False)):
        groups.extend((blocks[i:i + p0_s.shape[0]], first) for i in range(0, len(blocks), p0_s.shape[0]))

    slots = ((p0_s, w0_s), (p1_s, w1_s))
    scores_stage(*groups[0], *slots[0])
    for g in range(1, len(groups)):
        scores_stage(*groups[g], *slots[g % 2])
        values_stage(*groups[g - 1], *slots[(g - 1) % 2])
    values_stage(*groups[-1], *slots[(len(groups) - 1) % 2])

    def finish(i, carry):
        hd, r = i // RADIX, i % RADIX
        o_ref[hd, r] = (acc_s[hd, r] * (1.0 / l_s[hd, r])).astype(BF16)
        return carry

    lax.fori_loop(0, q16_ref.shape[0] * RADIX, finish, 0)


def _attention(q16, k1, v1, k16, v16, batch, seq):
    sub = seq // RADIX
    heads = ATTN_HEADS_PER_STEP * (2 if sub <= Q_BLOCK else 1)
    pat16, pat4, pat1 = _attention_patterns(seq)
    m16_spec = pl.BlockSpec((heads, None, RADIX, sub, HEAD_DIM), lambda b, h: (h, b, 0, 0, 0))
    nat_spec = pl.BlockSpec((heads, seq, HEAD_DIM), lambda b, h: (h, b, 0))
    state = pltpu.VMEM((heads, RADIX, sub, HEAD_DIM), F32)
    group = BLOCKS_PER_GROUP_LONG if sub > Q_BLOCK else BLOCKS_PER_GROUP_SHORT
    probs = pltpu.VMEM((group, Q_BLOCK, Q_BLOCK + 2 * HALF_KEYS), BF16)
    rescale = pltpu.VMEM((group, Q_BLOCK, HEAD_DIM), F32)
    return pl.pallas_call(
        functools.partial(_attn_kernel, pat16=pat16, pat4=pat4, pat1=pat1),
        grid=(batch, N_HEADS // heads),
        in_specs=[m16_spec, nat_spec, nat_spec, m16_spec, m16_spec,
                  _resident(pat16.bias.shape), _resident(pat4.bias.shape), _resident(pat1.bias.shape)],
        out_specs=m16_spec,
        out_shape=jax.ShapeDtypeStruct((N_HEADS, batch, RADIX, sub, HEAD_DIM), BF16),
        scratch_shapes=[state, state, state, probs, rescale, probs, rescale],
        compiler_params=_params("parallel", "parallel"),
        name="attention",
    )(q16, k1, v1, k16, v16, pat16.bias, pat4.bias, pat1.bias)


def _chanmat_kernel(c_ref, s_ref, w_ref, m_ref):
    for g in range(N_GROUPS):
        w = w_ref[g]
        mc = jnp.dot(c_ref[...], w, preferred_element_type=F32, precision=lax.Precision.HIGHEST)
        ms = jnp.dot(s_ref[...], w, preferred_element_type=F32, precision=lax.Precision.HIGHEST)
        m_ref[g, :, :GROUP_DIM] = mc.astype(BF16)
        m_ref[g, :, GROUP_DIM:] = ms.astype(BF16)


def _channel_matrices(w_fourier):
    idx = np.arange(GROUP_DIM)
    cc, sc = _dft_cos_sin(GROUP_DIM, idx, idx, GROUP_DIM ** -0.5)
    return pl.pallas_call(
        _chanmat_kernel,
        out_shape=jax.ShapeDtypeStruct((N_GROUPS, GROUP_DIM, 2 * GROUP_DIM), BF16),
        name="chanmat",
    )(jnp.asarray(cc, F32), jnp.asarray(sc, F32), w_fourier)


def _fourier_kernel(u_ref, m_ref, c_ref, smc_ref, cps_ref, mc_ref, ms_ref, gf_ref, o_ref, yr_s, yi_s,
                    *, stage1_steps, tiles):
    t = pl.program_id(1)
    dot = functools.partial(jnp.dot, preferred_element_type=F32)
    per_step = u_ref.shape[0]

    @pl.when(t < stage1_steps)
    def _():
        z = []
        for j in range(per_step):
            a_parts, b_parts, ab_parts = [], [], []
            for g in range(N_GROUPS):
                ab = dot(u_ref[j, :, g * GROUP_DIM:(g + 1) * GROUP_DIM], m_ref[g])
                a, b = ab[:, :GROUP_DIM], ab[:, GROUP_DIM:]
                a_parts.append(a.astype(BF16))
                b_parts.append(b.astype(BF16))
                ab_parts.append((a + b).astype(BF16))
            z.append(tuple(jnp.concatenate(p, axis=1) for p in (a_parts, b_parts, ab_parts)))
        c, s_minus_c, c_plus_s = c_ref[...], smc_ref[...], cps_ref[...]
        for j, (a, b, a_plus_b) in enumerate(z):
            n2 = t * per_step + j
            k1 = dot(c, a_plus_b)
            yr_s[n2] = (k1 - dot(c_plus_s, b)).astype(BF16)
            yi_s[n2] = (k1 + dot(s_minus_c, a)).astype(BF16)

    @pl.when(t >= stage1_steps)
    def _():
        first_row = (t - stage1_steps) * (tiles * RADIX)
        for i in range(tiles):
            rows = pl.ds(pl.multiple_of(first_row + i * RADIX, RADIX), RADIX)
            yr = jnp.concatenate([yr_s[n2, rows, :] for n2 in range(RADIX)], axis=0)
            yi = jnp.concatenate([yi_s[n2, rows, :] for n2 in range(RADIX)], axis=0)
            x = dot(mc_ref[i], yr) + dot(ms_ref[i], yi)
            out_rows = slice(i * RADIX, (i + 1) * RADIX)
            for k2 in range(RADIX):
                piece = x[k2 * RADIX:(k2 + 1) * RADIX, :] * gf_ref[k2, out_rows, :].astype(F32)
                o_ref[k2, out_rows, :] = piece.astype(BF16)


def _stage2_matrices(seq):
    s1 = seq // RADIX
    tiles = s1 // RADIX
    c, k2, k1, n2 = np.meshgrid(np.arange(tiles), np.arange(RADIX), np.arange(RADIX), np.arange(RADIX),
                                indexing="ij")
    k = RADIX * c + k1 + s1 * k2
    ang = 2.0 * np.pi * ((n2 * k) % seq).astype(np.float64) / seq
    mc = np.zeros((tiles, ROW_GROUP, ROW_GROUP), np.float32)
    ms = np.zeros((tiles, ROW_GROUP, ROW_GROUP), np.float32)
    mc[c, RADIX * k2 + k1, RADIX * n2 + k1] = np.cos(ang) * seq ** -0.5
    ms[c, RADIX * k2 + k1, RADIX * n2 + k1] = -np.sin(ang) * seq ** -0.5
    return _bf16_const(mc), _bf16_const(ms)


def _fourier(u16, chan_mats, gf, batch, seq, tiles=FOURIER2_TILES):
    s1 = seq // RADIX
    idx = np.arange(s1)
    c1, sn1 = _dft_cos_sin(s1, idx, idx)
    mc, ms = _stage2_matrices(seq)
    per_step = FOURIER1_ROWS // s1
    stage1_steps = RADIX // per_step
    rows = tiles * RADIX
    stage2 = lambda t: jnp.maximum(t - stage1_steps, 0)
    nat_spec = pl.BlockSpec((None, RADIX, rows, FOURIER_WIDTH), lambda b, t: (b, 0, stage2(t), 0))
    m_spec = pl.BlockSpec((tiles, ROW_GROUP, ROW_GROUP), lambda b, t: (stage2(t), 0, 0))
    y_scratch = pltpu.VMEM((RADIX, s1, FOURIER_WIDTH), BF16)
    out = pl.pallas_call(
        functools.partial(_fourier_kernel, stage1_steps=stage1_steps, tiles=tiles),
        grid=(batch, stage1_steps + s1 // rows),
        in_specs=[pl.BlockSpec((None, per_step, s1, FOURIER_WIDTH),
                               lambda b, t: (b, jnp.minimum(t, stage1_steps - 1), 0, 0)),
                  _resident(chan_mats.shape), _resident((s1, s1)), _resident((s1, s1)), _resident((s1, s1)),
                  m_spec, m_spec, nat_spec],
        out_specs=nat_spec,
        out_shape=jax.ShapeDtypeStruct((batch, RADIX, s1, FOURIER_WIDTH), BF16),
        scratch_shapes=[y_scratch, y_scratch],
        compiler_params=_params("parallel", "arbitrary"),
        name="fourier",
    )(u16, chan_mats, _bf16_const(c1), _bf16_const(sn1 - c1), _bf16_const(c1 + sn1), mc, ms,
      gf.reshape(batch, RADIX, s1, FOURIER_WIDTH))
    return out.reshape(batch * seq, FOURIER_WIDTH)


def _outproj_kernel(x_ref, o16_ref, ga_ref, mf_ref, unperm_ref, w_ref, y_ref):
    unperm = unperm_ref[...]
    attn = []
    for g in range(OUTPROJ_GROUPS):
        rows = slice(g * RADIX, (g + 1) * RADIX)
        heads = [jnp.concatenate([o16_ref[hd, r, rows, :] for r in range(RADIX)], axis=0)
                 for hd in range(N_HEADS)]
        attn16 = jnp.concatenate(heads, axis=1)
        attn.append(jnp.dot(unperm, attn16, preferred_element_type=F32))
    y = x_ref[...] + jnp.dot(mf_ref[...], w_ref[ATTN_WIDTH:, :], preferred_element_type=F32)
    mix_a = (jnp.concatenate(attn, axis=0) * ga_ref[...].astype(F32)).astype(BF16)
    y_ref[...] = y + jnp.dot(mix_a, w_ref[:ATTN_WIDTH, :], preferred_element_type=F32)


def _outproj(x2d, o16, ga, mix_f, unperm, w_out, seq):
    t = x2d.shape[0]
    rows = OUTPROJ_GROUPS * ROW_GROUP
    steps_per_seq = seq // rows
    tok_spec = pl.BlockSpec((rows, ATTN_WIDTH), lambda i: (i, 0))
    x_spec = pl.BlockSpec((rows, D_MODEL), lambda i: (i, 0))
    return pl.pallas_call(
        _outproj_kernel,
        grid=(t // rows,),
        in_specs=[x_spec,
                  pl.BlockSpec((N_HEADS, None, RADIX, OUTPROJ_GROUPS * RADIX, HEAD_DIM),
                               lambda i: (0, i // steps_per_seq, 0, i % steps_per_seq, 0)),
                  tok_spec, tok_spec,
                  _resident((ROW_GROUP, ROW_GROUP)),
                  _resident((D_MODEL, D_MODEL))],
        out_specs=x_spec,
        out_shape=jax.ShapeDtypeStruct((t, D_MODEL), F32),
        compiler_params=_params("arbitrary"),
        name="outproj",
    )(x2d, o16, ga, mix_f, unperm, w_out)


def _layer(x, gain, w_in, qg, kg, chan_mats, w_out, perm, unperm):
    batch, seq, _ = x.shape
    x2d = x.reshape(batch * seq, D_MODEL)
    q16, k1, k16, v1, v16, ga, u16, gf = _inproj(x2d, batch, seq, gain, w_in, qg, kg, perm)
    o16 = _attention(q16, k1, v1, k16, v16, batch, seq)
    mix_f = _fourier(u16, chan_mats, gf, batch, seq)
    y = _outproj(x2d, o16, ga, mix_f, unperm, w_out, seq)
    return y.reshape(batch, seq, D_MODEL)


def kernel(x_prompt, x_sample, rms_gain, w_in, q_norm_gain, k_norm_gain, w_fourier, w_out):
    depth = rms_gain.shape[0]
    p = _group_permutation()
    perm, unperm = _bf16_const(p), _bf16_const(p.T)
    for l in range(depth):
        gain = rms_gain[l].reshape(1, D_MODEL)
        w_in_l = w_in[l].astype(BF16)
        w_out_l = w_out[l].astype(BF16)
        qg = q_norm_gain[l].reshape(1, HEAD_DIM)
        kg = k_norm_gain[l].reshape(1, HEAD_DIM)
        chan_mats = _channel_matrices(w_fourier[l])
        x_prompt = _layer(x_prompt, gain, w_in_l, qg, kg, chan_mats, w_out_l, perm, unperm)
        x_sample = _layer(x_sample, gain, w_in_l, qg, kg, chan_mats, w_out_l, perm, unperm)
    return (x_prompt, x_sample)
```

```python
import functools
import math

import jax
import jax.numpy as jnp
import numpy as np
from jax import lax
from jax.experimental import pallas as pl
from jax.experimental.pallas import tpu as pltpu

D_MODEL = 2048
ATTN_WIDTH = 1024
FOURIER_WIDTH = 1024
HEAD_DIM = 128
N_HEADS = ATTN_WIDTH // HEAD_DIM
N_GROUPS = 4
GROUP_DIM = FOURIER_WIDTH // N_GROUPS
ROPE_THETA = 500000.0
ROPE_DIM = HEAD_DIM // 4
ROPE_HALF = ROPE_DIM // 2
HALF_KEYS = 64
RMS_EPS = 1e-6
RADIX = 16
ROW_GROUP = RADIX * RADIX
Q_BLOCK = 128
FOURIER1_ROWS = 1024
FOURIER2_TILES = 8
ATTN_HEADS_PER_STEP = 2
INPROJ_GROUPS = 1
OUTPROJ_GROUPS = 2
BLOCKS_PER_GROUP_LONG = 6
BLOCKS_PER_GROUP_SHORT = 6
MASK_VALUE = -1e30

VMEM_LIMIT_BYTES = 56 * 1024 * 1024

F32 = jnp.float32
BF16 = jnp.bfloat16


def _params(*semantics):
    return pltpu.CompilerParams(dimension_semantics=semantics, vmem_limit_bytes=VMEM_LIMIT_BYTES)


def _resident(shape):
    return pl.BlockSpec(shape, lambda *_: (0,) * len(shape), pipeline_mode=pl.Buffered(1))


def _rope_tables(seq, scale):
    expo = np.arange(ROPE_HALF, dtype=np.float32) / np.float32(ROPE_HALF)
    inv_freq = (np.float32(1.0) / np.power(np.float32(ROPE_THETA), expo)).astype(np.float32)
    ang = (np.arange(seq, dtype=np.float32)[:, None] * inv_freq[None, :]).astype(np.float64)
    cos, sin = np.cos(ang), np.sin(ang)
    a = np.ones((seq, HEAD_DIM))
    b = np.zeros((seq, HEAD_DIM))
    c = np.zeros((seq, HEAD_DIM))
    a[:, :ROPE_HALF] = cos
    a[:, ROPE_HALF:ROPE_DIM] = cos
    b[:, ROPE_HALF:ROPE_DIM] = sin
    c[:, :ROPE_HALF] = -sin
    return tuple(jnp.asarray((t * scale).astype(np.float32)) for t in (a, b, c))


def _dft_cos_sin(n, rows, cols, scale=1.0):
    m = (np.asarray(rows, dtype=np.int64)[:, None] * np.asarray(cols, dtype=np.int64)[None, :]) % n
    ang = 2.0 * np.pi * m.astype(np.float64) / n
    return np.cos(ang) * scale, np.sin(ang) * scale


def _group_permutation():
    p = np.zeros((ROW_GROUP, ROW_GROUP), np.float32)
    j, r = np.meshgrid(np.arange(RADIX), np.arange(RADIX), indexing="ij")
    p[(RADIX * r + j).ravel(), (RADIX * j + r).ravel()] = 1.0
    return p


def _bf16_const(a):
    return jnp.asarray(np.asarray(a, np.float32)).astype(BF16)


def _silu(a):
    return a * (1.0 / (1.0 + jnp.exp(-a)))


def _inproj_kernel(x_ref, gain_ref, w_ref, qg_ref, kg_ref,
                   qa_ref, qb_ref, qc_ref, ka_ref, kb_ref, kc_ref, perm_ref,
                   q16_ref, k1_ref, k16_ref, v1_ref, v16_ref, ga_ref, u16_ref, gf_ref):
    x = x_ref[...]
    ms = jnp.mean(x * x, axis=-1, keepdims=True)
    h = (x * lax.rsqrt(ms + RMS_EPS) * gain_ref[...]).astype(BF16)
    perm = perm_ref[...]

    def column_tile(j):
        cols = slice(j * ATTN_WIDTH, (j + 1) * ATTN_WIDTH)
        return jnp.dot(h, w_ref[:, cols], preferred_element_type=F32)

    def norm_rope(acc, g_ref, a_ref, b_ref, c_ref):
        heads = []
        for hd in range(N_HEADS):
            a = acc[:, hd * HEAD_DIM:(hd + 1) * HEAD_DIM]
            ms_h = jnp.mean(a * a, axis=-1, keepdims=True)
            n = a * lax.rsqrt(ms_h + RMS_EPS) * g_ref[...]
            r = (n * a_ref[...]
                 + pltpu.roll(n, ROPE_HALF, 1) * b_ref[...]
                 + pltpu.roll(n, HEAD_DIM - ROPE_HALF, 1) * c_ref[...])
            heads.append(r.astype(BF16))
        return jnp.concatenate(heads, axis=1)

    def to_mod16(t):
        return [jnp.dot(perm, t[g * ROW_GROUP:(g + 1) * ROW_GROUP], preferred_element_type=F32)
                for g in range(INPROJ_GROUPS)]

    def store_heads(ref, t):
        for hd in range(N_HEADS):
            ref[hd] = t[:, hd * HEAD_DIM:(hd + 1) * HEAD_DIM]

    def store_head_pieces(ref, groups):
        for g, t in enumerate(groups):
            for hd in range(N_HEADS):
                for r in range(RADIX):
                    ref[hd, r, g * RADIX:(g + 1) * RADIX, :] = (
                        t[r * RADIX:(r + 1) * RADIX, hd * HEAD_DIM:(hd + 1) * HEAD_DIM].astype(ref.dtype))

    acc_q = column_tile(0)
    acc_k = column_tile(1)
    q = norm_rope(acc_q, qg_ref, qa_ref, qb_ref, qc_ref)
    acc_v = column_tile(2)
    store_head_pieces(q16_ref, to_mod16(q))

    k = norm_rope(acc_k, kg_ref, ka_ref, kb_ref, kc_ref)
    acc_ga = column_tile(3)
    store_heads(k1_ref, k)
    store_head_pieces(k16_ref, to_mod16(k))

    v = acc_v.astype(BF16)
    acc_u = column_tile(4)
    store_heads(v1_ref, v)
    store_head_pieces(v16_ref, to_mod16(v))

    ga_ref[...] = _silu(acc_ga).astype(BF16)
    acc_gf = column_tile(5)

    for g, u16 in enumerate(to_mod16(acc_u.astype(BF16))):
        for r in range(RADIX):
            u16_ref[r, g * RADIX:(g + 1) * RADIX, :] = u16[r * RADIX:(r + 1) * RADIX, :].astype(BF16)

    gf_ref[...] = _silu(acc_gf).astype(BF16)


def _inproj(x2d, batch, seq, gain, w_in, qg, kg, perm):
    t = x2d.shape[0]
    rows = INPROJ_GROUPS * ROW_GROUP
    pieces = INPROJ_GROUPS * RADIX
    groups = seq // rows
    sub = seq // RADIX
    qa, qb, qc = _rope_tables(seq, HEAD_DIM ** -0.5 * math.log2(math.e))
    ka, kb, kc = _rope_tables(seq, 1.0)
    rope_spec = pl.BlockSpec((rows, HEAD_DIM), lambda i: (i % groups, 0))
    tok_spec = pl.BlockSpec((rows, ATTN_WIDTH), lambda i: (i, 0))
    head_nat_spec = pl.BlockSpec((N_HEADS, rows, HEAD_DIM), lambda i: (0, i, 0))
    head_m16_spec = pl.BlockSpec((N_HEADS, None, RADIX, pieces, HEAD_DIM),
                                 lambda i: (0, i // groups, 0, i % groups, 0))
    head_nat = jax.ShapeDtypeStruct((N_HEADS, t, HEAD_DIM), BF16)
    head_m16 = lambda dt: jax.ShapeDtypeStruct((N_HEADS, batch, RADIX, sub, HEAD_DIM), dt)
    return pl.pallas_call(
        _inproj_kernel,
        grid=(t // rows,),
        in_specs=[
            pl.BlockSpec((rows, D_MODEL), lambda i: (i, 0)),
            _resident((1, D_MODEL)),
            _resident(w_in.shape),
            _resident((1, HEAD_DIM)), _resident((1, HEAD_DIM)),
            rope_spec, rope_spec, rope_spec, rope_spec, rope_spec, rope_spec,
            _resident((ROW_GROUP, ROW_GROUP)),
        ],
        out_specs=[
            head_m16_spec, head_nat_spec, head_m16_spec, head_nat_spec, head_m16_spec,
            tok_spec,
            pl.BlockSpec((None, RADIX, pieces, FOURIER_WIDTH), lambda i: (i // groups, 0, i % groups, 0)),
            tok_spec,
        ],
        out_shape=[
            head_m16(F32), head_nat, head_m16(BF16), head_nat, head_m16(BF16),
            jax.ShapeDtypeStruct((t, ATTN_WIDTH), BF16),
            jax.ShapeDtypeStruct((batch, RADIX, sub, FOURIER_WIDTH), BF16),
            jax.ShapeDtypeStruct((t, FOURIER_WIDTH), BF16),
        ],
        compiler_params=_params("arbitrary"),
        name="inproj",
    )(x2d, gain, w_in, qg, kg, qa, qb, qc, ka, kb, kc, perm)


def _window_start(first, half, total, window):
    return min(max(first - half, 0), total - window)


class _Pattern:
    def __init__(self, q_first, k_first, diff0, scale):
        deltas = [scale * (q0 - k0) for q0, k0 in zip(q_first, k_first)]
        uniq = sorted(set(deltas))
        self.q_first, self.k_first = q_first, k_first
        self.table_of = [uniq.index(d) for d in deltas]
        self.bias = jnp.asarray(np.stack(
            [np.where(np.abs(diff0 + d) <= HALF_KEYS, 0.0, MASK_VALUE) for d in uniq]).astype(np.float32)
        ).astype(BF16)


def _attention_patterns(seq):
    sub = seq // RADIX
    qi = np.arange(Q_BLOCK)[:, None]
    win16 = min(Q_BLOCK + 2 * HALF_KEYS, sub)
    q16 = list(range(0, sub, Q_BLOCK))
    p16 = _Pattern(q16, [_window_start(q0, HALF_KEYS, sub, win16) for q0 in q16],
                   qi - np.arange(win16)[None, :], 1)
    q_rows, k_rows = Q_BLOCK // 4, Q_BLOCK // 4 + 2 * HALF_KEYS // 4
    kc = np.arange(4 * k_rows)[None, :]
    q4 = list(range(0, sub, q_rows))
    p4 = _Pattern(q4, [_window_start(q0, HALF_KEYS // 4, sub, k_rows) for q0 in q4],
                  4 * (qi % q_rows - kc % k_rows) + (qi // q_rows - kc // k_rows), 4)
    win1 = Q_BLOCK + 2 * HALF_KEYS
    q1 = list(range(0, seq, Q_BLOCK))
    p1 = _Pattern(q1, [_window_start(q0, HALF_KEYS, seq, win1) for q0 in q1],
                  RADIX * (qi % (Q_BLOCK // RADIX)) + qi // (Q_BLOCK // RADIX) - np.arange(win1)[None, :], 1)
    return p16, p4, p1


def _attn_kernel(q16_ref, k1_ref, v1_ref, k16_ref, v16_ref, b16_ref, b4_ref, b1_ref,
                 o_ref, acc_s, m_s, l_s, p0_s, w0_s, p1_s, w1_s, *, pat16, pat4, pat1):
    def gather(ref, pieces):
        return jnp.concatenate([ref[r, pl.ds(s0, n), :] for r, s0, n in pieces], axis=0)

    def scatter(ref, pieces, value):
        off = 0
        for r, s0, n in pieces:
            ref[r, pl.ds(s0, n), :] = value[off:off + n]
            off += n

    def scores_stage(blocks, first, p_scr, w_scr):
        for i, (q, k, _, bias, pieces, (_, m_h, _)) in enumerate(blocks):
            s = lax.dot_general(q(), k(), (((1,), (1,)), ((), ())), preferred_element_type=F32)
            s = s.astype(BF16) + bias()
            m_new = jnp.broadcast_to(jnp.max(s, axis=-1, keepdims=True), (Q_BLOCK, HEAD_DIM)).astype(F32)
            if not first:
                m_old = gather(m_h, pieces)
                m_new = jnp.maximum(m_old, m_new)
                w_scr[i] = jnp.exp2(m_old - m_new)
            width = s.shape[1]
            m_b = m_new.astype(BF16)
            p_scr[i, :, :width] = jnp.exp2(s - jnp.concatenate([m_b] * (width // HEAD_DIM), axis=1))
            scatter(m_h, pieces, m_new)

    def values_stage(blocks, first, p_scr, w_scr):
        for i, (_, _, v, _, pieces, (acc_h, _, l_h)) in enumerate(blocks):
            vw = v()
            v_aug = jnp.concatenate([vw, jnp.ones(vw.shape, BF16)], axis=1)
            pv = jnp.dot(p_scr[i, :, :vw.shape[0]], v_aug, preferred_element_type=F32)
            acc, l = pv[:, :HEAD_DIM], pv[:, HEAD_DIM:]
            if not first:
                w_old = w_scr[i]
                acc = w_old * gather(acc_h, pieces) + acc
                l = w_old * gather(l_h, pieces) + l
            scatter(l_h, pieces, l)
            scatter(acc_h, pieces, acc)

    win16 = b16_ref.shape[2]
    q_rows4 = Q_BLOCK // 4
    k_rows4 = b4_ref.shape[2] // 4
    q_rows1 = Q_BLOCK // RADIX
    win1 = b1_ref.shape[2]
    blocks16, blocks4, blocks1 = [], [], []
    for hd in range(q16_ref.shape[0]):
        q16, k1, v1, k16, v16 = (ref.at[hd] for ref in (q16_ref, k1_ref, v1_ref, k16_ref, v16_ref))
        state = (acc_s.at[hd], m_s.at[hd], l_s.at[hd])

        for r in range(RADIX):
            for blk, (l0, ws) in enumerate(zip(pat16.q_first, pat16.k_first)):
                blocks16.append((
                    lambda q16=q16, r=r, l0=l0: q16[r, l0:l0 + Q_BLOCK, :].astype(BF16),
                    lambda k16=k16, r=r, ws=ws: k16[r, ws:ws + win16, :],
                    lambda v16=v16, r=r, ws=ws: v16[r, ws:ws + win16, :],
                    lambda blk=blk: b16_ref[pat16.table_of[blk]],
                    [(r, l0, Q_BLOCK)], state))

        for r4 in range(4):
            res = [r4 + 4 * a for a in range(4)]
            for blk, (l0, ws) in enumerate(zip(pat4.q_first, pat4.k_first)):
                q_pieces = [(r, l0, q_rows4) for r in res]
                k_pieces = [(r, ws, k_rows4) for r in res]
                blocks4.append((
                    lambda q16=q16, p=q_pieces: gather(q16, p).astype(BF16),
                    lambda k16=k16, p=k_pieces: gather(k16, p),
                    lambda v16=v16, p=k_pieces: gather(v16, p),
                    lambda blk=blk: b4_ref[pat4.table_of[blk]],
                    q_pieces, state))

        for blk, (n0, ws) in enumerate(zip(pat1.q_first, pat1.k_first)):
            q_pieces = [(r, n0 // RADIX, q_rows1) for r in range(RADIX)]
            blocks1.append((
                lambda q16=q16, p=q_pieces: gather(q16, p).astype(BF16),
                lambda k1=k1, ws=ws: k1[ws:ws + win1, :],
                lambda v1=v1, ws=ws: v1[ws:ws + win1, :],
                lambda blk=blk: b1_ref[pat1.table_of[blk]],
                q_pieces, state))

    groups = []
    for blocks, first in ((blocks16, True), (blocks4, False), (blocks1, False)):
        groups.extend((blocks[i:i + p0_s.shape[0]], first) for i in range(0, len(blocks), p0_s.shape[0]))

    slots = ((p0_s, w0_s), (p1_s, w1_s))
    scores_stage(*groups[0], *slots[0])
    for g in range(1, len(groups)):
        scores_stage(*groups[g], *slots[g % 2])
        values_stage(*groups[g - 1], *slots[(g - 1) % 2])
    values_stage(*groups[-1], *slots[(len(groups) - 1) % 2])

    def finish(i, carry):
        hd, r = i // RADIX, i % RADIX
        o_ref[hd, r] = (acc_s[hd, r] * (1.0 / l_s[hd, r])).astype(BF16)
        return carry

    lax.fori_loop(0, q16_ref.shape[0] * RADIX, finish, 0)


def _attention(q16, k1, v1, k16, v16, batch, seq):
    sub = seq // RADIX
    heads = ATTN_HEADS_PER_STEP * (2 if sub <= Q_BLOCK else 1)
    pat16, pat4, pat1 = _attention_patterns(seq)
    m16_spec = pl.BlockSpec((heads, None, RADIX, sub, HEAD_DIM), lambda b, h: (h, b, 0, 0, 0))
    nat_spec = pl.BlockSpec((heads, seq, HEAD_DIM), lambda b, h: (h, b, 0))
    state = pltpu.VMEM((heads, RADIX, sub, HEAD_DIM), F32)
    group = BLOCKS_PER_GROUP_LONG if sub > Q_BLOCK else BLOCKS_PER_GROUP_SHORT
    probs = pltpu.VMEM((group, Q_BLOCK, Q_BLOCK + 2 * HALF_KEYS), BF16)
    rescale = pltpu.VMEM((group, Q_BLOCK, HEAD_DIM), F32)
    return pl.pallas_call(
        functools.partial(_attn_kernel, pat16=pat16, pat4=pat4, pat1=pat1),
        grid=(batch, N_HEADS // heads),
        in_specs=[m16_spec, nat_spec, nat_spec, m16_spec, m16_spec,
                  _resident(pat16.bias.shape), _resident(pat4.bias.shape), _resident(pat1.bias.shape)],
        out_specs=m16_spec,
        out_shape=jax.ShapeDtypeStruct((N_HEADS, batch, RADIX, sub, HEAD_DIM), BF16),
        scratch_shapes=[state, state, state, probs, rescale, probs, rescale],
        compiler_params=_params("parallel", "parallel"),
        name="attention",
    )(q16, k1, v1, k16, v16, pat16.bias, pat4.bias, pat1.bias)


def _chanmat_kernel(c_ref, s_ref, w_ref, m_ref):
    for g in range(N_GROUPS):
        w = w_ref[g]
        mc = jnp.dot(c_ref[...], w, preferred_element_type=F32, precision=lax.Precision.HIGHEST)
        ms = jnp.dot(s_ref[...], w, preferred_element_type=F32, precision=lax.Precision.HIGHEST)
        m_ref[g, :, :GROUP_DIM] = mc.astype(BF16)
        m_ref[g, :, GROUP_DIM:] = ms.astype(BF16)


def _channel_matrices(w_fourier):
    idx = np.arange(GROUP_DIM)
    cc, sc = _dft_cos_sin(GROUP_DIM, idx, idx, GROUP_DIM ** -0.5)
    return pl.pallas_call(
        _chanmat_kernel,
        out_shape=jax.ShapeDtypeStruct((N_GROUPS, GROUP_DIM, 2 * GROUP_DIM), BF16),
        name="chanmat",
    )(jnp.asarray(cc, F32), jnp.asarray(sc, F32), w_fourier)


def _fourier_kernel(u_ref, m_ref, c_ref, smc_ref, cps_ref, mc_ref, ms_ref, gf_ref, o_ref, yr_s, yi_s,
                    *, stage1_steps, tiles):
    t = pl.program_id(1)
    dot = functools.partial(jnp.dot, preferred_element_type=F32)
    per_step = u_ref.shape[0]

    @pl.when(t < stage1_steps)
    def _():
        z = []
        for j in range(per_step):
            a_parts, b_parts, ab_parts = [], [], []
            for g in range(N_GROUPS):
                ab = dot(u_ref[j, :, g * GROUP_DIM:(g + 1) * GROUP_DIM], m_ref[g])
                a, b = ab[:, :GROUP_DIM], ab[:, GROUP_DIM:]
                a_parts.append(a.astype(BF16))
                b_parts.append(b.astype(BF16))
                ab_parts.append((a + b).astype(BF16))
            z.append(tuple(jnp.concatenate(p, axis=1) for p in (a_parts, b_parts, ab_parts)))
        c, s_minus_c, c_plus_s = c_ref[...], smc_ref[...], cps_ref[...]
        for j, (a, b, a_plus_b) in enumerate(z):
            n2 = t * per_step + j
            k1 = dot(c, a_plus_b)
            yr_s[n2] = (k1 - dot(c_plus_s, b)).astype(BF16)
            yi_s[n2] = (k1 + dot(s_minus_c, a)).astype(BF16)

    @pl.when(t >= stage1_steps)
    def _():
        first_row = (t - stage1_steps) * (tiles * RADIX)
        for i in range(tiles):
            rows = pl.ds(pl.multiple_of(first_row + i * RADIX, RADIX), RADIX)
            yr = jnp.concatenate([yr_s[n2, rows, :] for n2 in range(RADIX)], axis=0)
            yi = jnp.concatenate([yi_s[n2, rows, :] for n2 in range(RADIX)], axis=0)
            x = dot(mc_ref[i], yr) + dot(ms_ref[i], yi)
            out_rows = slice(i * RADIX, (i + 1) * RADIX)
            for k2 in range(RADIX):
                piece = x[k2 * RADIX:(k2 + 1) * RADIX, :] * gf_ref[k2, out_rows, :].astype(F32)
                o_ref[k2, out_rows, :] = piece.astype(BF16)


def _stage2_matrices(seq):
    s1 = seq // RADIX
    tiles = s1 // RADIX
    c, k2, k1, n2 = np.meshgrid(np.arange(tiles), np.arange(RADIX), np.arange(RADIX), np.arange(RADIX),
                                indexing="ij")
    k = RADIX * c + k1 + s1 * k2
    ang = 2.0 * np.pi * ((n2 * k) % seq).astype(np.float64) / seq
    mc = np.zeros((tiles, ROW_GROUP, ROW_GROUP), np.float32)
    ms = np.zeros((tiles, ROW_GROUP, ROW_GROUP), np.float32)
    mc[c, RADIX * k2 + k1, RADIX * n2 + k1] = np.cos(ang) * seq ** -0.5
    ms[c, RADIX * k2 + k1, RADIX * n2 + k1] = -np.sin(ang) * seq ** -0.5
    return _bf16_const(mc), _bf16_const(ms)


def _fourier(u16, chan_mats, gf, batch, seq, tiles=FOURIER2_TILES):
    s1 = seq // RADIX
    idx = np.arange(s1)
    c1, sn1 = _dft_cos_sin(s1, idx, idx)
    mc, ms = _stage2_matrices(seq)
    per_step = FOURIER1_ROWS // s1
    stage1_steps = RADIX // per_step
    rows = tiles * RADIX
    stage2 = lambda t: jnp.maximum(t - stage1_steps, 0)
    nat_spec = pl.BlockSpec((None, RADIX, rows, FOURIER_WIDTH), lambda b, t: (b, 0, stage2(t), 0))
    m_spec = pl.BlockSpec((tiles, ROW_GROUP, ROW_GROUP), lambda b, t: (stage2(t), 0, 0))
    y_scratch = pltpu.VMEM((RADIX, s1, FOURIER_WIDTH), BF16)
    out = pl.pallas_call(
        functools.partial(_fourier_kernel, stage1_steps=stage1_steps, tiles=tiles),
        grid=(batch, stage1_steps + s1 // rows),
        in_specs=[pl.BlockSpec((None, per_step, s1, FOURIER_WIDTH),
                               lambda b, t: (b, jnp.minimum(t, stage1_steps - 1), 0, 0)),
                  _resident(chan_mats.shape), _resident((s1, s1)), _resident((s1, s1)), _resident((s1, s1)),
                  m_spec, m_spec, nat_spec],
        out_specs=nat_spec,
        out_shape=jax.ShapeDtypeStruct((batch, RADIX, s1, FOURIER_WIDTH), BF16),
        scratch_shapes=[y_scratch, y_scratch],
        compiler_params=_params("parallel", "arbitrary"),
        name="fourier",
    )(u16, chan_mats, _bf16_const(c1), _bf16_const(sn1 - c1), _bf16_const(c1 + sn1), mc, ms,
      gf.reshape(batch, RADIX, s1, FOURIER_WIDTH))
    return out.reshape(batch * seq, FOURIER_WIDTH)


def _outproj_kernel(x_ref, o16_ref, ga_ref, mf_ref, unperm_ref, w_ref, y_ref):
    unperm = unperm_ref[...]
    attn = []
    for g in range(OUTPROJ_GROUPS):
        rows = slice(g * RADIX, (g + 1) * RADIX)
        heads = [jnp.concatenate([o16_ref[hd, r, rows, :] for r in range(RADIX)], axis=0)
                 for hd in range(N_HEADS)]
        attn16 = jnp.concatenate(heads, axis=1)
        attn.append(jnp.dot(unperm, attn16, preferred_element_type=F32))
    y = x_ref[...] + jnp.dot(mf_ref[...], w_ref[ATTN_WIDTH:, :], preferred_element_type=F32)
    mix_a = (jnp.concatenate(attn, axis=0) * ga_ref[...].astype(F32)).astype(BF16)
    y_ref[...] = y + jnp.dot(mix_a, w_ref[:ATTN_WIDTH, :], preferred_element_type=F32)


def _outproj(x2d, o16, ga, mix_f, unperm, w_out, seq):
    t = x2d.shape[0]
    rows = OUTPROJ_GROUPS * ROW_GROUP
    steps_per_seq = seq // rows
    tok_spec = pl.BlockSpec((rows, ATTN_WIDTH), lambda i: (i, 0))
    x_spec = pl.BlockSpec((rows, D_MODEL), lambda i: (i, 0))
    return pl.pallas_call(
        _outproj_kernel,
        grid=(t // rows,),
        in_specs=[x_spec,
                  pl.BlockSpec((N_HEADS, None, RADIX, OUTPROJ_GROUPS * RADIX, HEAD_DIM),
                               lambda i: (0, i // steps_per_seq, 0, i % steps_per_seq, 0)),
                  tok_spec, tok_spec,
                  _resident((ROW_GROUP, ROW_GROUP)),
                  _resident((D_MODEL, D_MODEL))],
        out_specs=x_spec,
        out_shape=jax.ShapeDtypeStruct((t, D_MODEL), F32),
        compiler_params=_params("arbitrary"),
        name="outproj",
    )(x2d, o16, ga, mix_f, unperm, w_out)


def _layer(x, gain, w_in, qg, kg, chan_mats, w_out, perm, unperm):
    batch, seq, _ = x.shape
    x2d = x.reshape(batch * seq, D_MODEL)
    q16, k1, k16, v1, v16, ga, u16, gf = _inproj(x2d, batch, seq, gain, w_in, qg, kg, perm)
    o16 = _attention(q16, k1, v1, k16, v16, batch, seq)
    mix_f = _fourier(u16, chan_mats, gf, batch, seq)
    y = _outproj(x2d, o16, ga, mix_f, unperm, w_out, seq)
    return y.reshape(batch, seq, D_MODEL)


def kernel(x_prompt, x_sample, rms_gain, w_in, q_norm_gain, k_norm_gain, w_fourier, w_out):
    depth = rms_gain.shape[0]
    p = _group_permutation()
    perm, unperm = _bf16_const(p), _bf16_const(p.T)
    for l in range(depth):
        gain = rms_gain[l].reshape(1, D_MODEL)
        w_in_l = w_in[l].astype(BF16)
        w_out_l = w_out[l].astype(BF16)
        qg = q_norm_gain[l].reshape(1, HEAD_DIM)
        kg = k_norm_gain[l].reshape(1, HEAD_DIM)
        chan_mats = _channel_matrices(w_fourier[l])
        x_prompt = _layer(x_prompt, gain, w_in_l, qg, kg, chan_mats, w_out_l, perm, unperm)
        x_sample = _layer(x_sample, gain, w_in_l, qg, kg, chan_mats, w_out_l, perm, unperm)
    return (x_prompt, x_sample)
```

```python
import functools
import math

import jax
import jax.numpy as jnp
import numpy as np
from jax import lax
from jax.experimental import pallas as pl
from jax.experimental.pallas import tpu as pltpu

D_MODEL = 2048
ATTN_WIDTH = 1024
FOURIER_WIDTH = 1024
HEAD_DIM = 128
N_HEADS = ATTN_WIDTH // HEAD_DIM
N_GROUPS = 4
GROUP_DIM = FOURIER_WIDTH // N_GROUPS
ROPE_THETA = 500000.0
ROPE_DIM = HEAD_DIM // 4
ROPE_HALF = ROPE_DIM // 2
HALF_KEYS = 64
RMS_EPS = 1e-6
RADIX = 16
ROW_GROUP = RADIX * RADIX
Q_BLOCK = 128
FOURIER1_ROWS = 1024
FOURIER2_TILES = 8
ATTN_HEADS_PER_STEP = 2
INPROJ_GROUPS = 1
OUTPROJ_GROUPS = 2
BLOCKS_PER_GROUP_LONG = 6
BLOCKS_PER_GROUP_SHORT = 4
MASK_VALUE = -1e30

VMEM_LIMIT_BYTES = 56 * 1024 * 1024

F32 = jnp.float32
BF16 = jnp.bfloat16


def _params(*semantics):
    return pltpu.CompilerParams(dimension_semantics=semantics, vmem_limit_bytes=VMEM_LIMIT_BYTES)


def _resident(shape):
    return pl.BlockSpec(shape, lambda *_: (0,) * len(shape), pipeline_mode=pl.Buffered(1))


def _rope_tables(seq, scale):
    expo = np.arange(ROPE_HALF, dtype=np.float32) / np.float32(ROPE_HALF)
    inv_freq = (np.float32(1.0) / np.power(np.float32(ROPE_THETA), expo)).astype(np.float32)
    ang = (np.arange(seq, dtype=np.float32)[:, None] * inv_freq[None, :]).astype(np.float64)
    cos, sin = np.cos(ang), np.sin(ang)
    a = np.ones((seq, HEAD_DIM))
    b = np.zeros((seq, HEAD_DIM))
    c = np.zeros((seq, HEAD_DIM))
    a[:, :ROPE_HALF] = cos
    a[:, ROPE_HALF:ROPE_DIM] = cos
    b[:, ROPE_HALF:ROPE_DIM] = sin
    c[:, :ROPE_HALF] = -sin
    return tuple(jnp.asarray((t * scale).astype(np.float32)) for t in (a, b, c))


def _dft_cos_sin(n, rows, cols, scale=1.0):
    m = (np.asarray(rows, dtype=np.int64)[:, None] * np.asarray(cols, dtype=np.int64)[None, :]) % n
    ang = 2.0 * np.pi * m.astype(np.float64) / n
    return np.cos(ang) * scale, np.sin(ang) * scale


def _group_permutation():
    p = np.zeros((ROW_GROUP, ROW_GROUP), np.float32)
    j, r = np.meshgrid(np.arange(RADIX), np.arange(RADIX), indexing="ij")
    p[(RADIX * r + j).ravel(), (RADIX * j + r).ravel()] = 1.0
    return p


def _bf16_const(a):
    return jnp.asarray(np.asarray(a, np.float32)).astype(BF16)


def _silu(a):
    return a * (1.0 / (1.0 + jnp.exp(-a)))


def _inproj_kernel(x_ref, gain_ref, w_ref, qg_ref, kg_ref,
                   qa_ref, qb_ref, qc_ref, ka_ref, kb_ref, kc_ref, perm_ref,
                   q16_ref, k1_ref, k16_ref, v1_ref, v16_ref, ga_ref, u16_ref, gf_ref):
    x = x_ref[...]
    ms = jnp.mean(x * x, axis=-1, keepdims=True)
    h = (x * lax.rsqrt(ms + RMS_EPS) * gain_ref[...]).astype(BF16)
    perm = perm_ref[...]

    def column_tile(j):
        cols = slice(j * ATTN_WIDTH, (j + 1) * ATTN_WIDTH)
        return jnp.dot(h, w_ref[:, cols], preferred_element_type=F32)

    def norm_rope(acc, g_ref, a_ref, b_ref, c_ref):
        heads = []
        for hd in range(N_HEADS):
            a = acc[:, hd * HEAD_DIM:(hd + 1) * HEAD_DIM]
            ms_h = jnp.mean(a * a, axis=-1, keepdims=True)
            n = a * lax.rsqrt(ms_h + RMS_EPS) * g_ref[...]
            r = (n * a_ref[...]
                 + pltpu.roll(n, ROPE_HALF, 1) * b_ref[...]
                 + pltpu.roll(n, HEAD_DIM - ROPE_HALF, 1) * c_ref[...])
            heads.append(r.astype(BF16))
        return jnp.concatenate(heads, axis=1)

    def to_mod16(t):
        return [jnp.dot(perm, t[g * ROW_GROUP:(g + 1) * ROW_GROUP], preferred_element_type=F32)
                for g in range(INPROJ_GROUPS)]

    def store_heads(ref, t):
        for hd in range(N_HEADS):
            ref[hd] = t[:, hd * HEAD_DIM:(hd + 1) * HEAD_DIM]

    def store_head_pieces(ref, groups):
        for g, t in enumerate(groups):
            for hd in range(N_HEADS):
                for r in range(RADIX):
                    ref[hd, r, g * RADIX:(g + 1) * RADIX, :] = (
                        t[r * RADIX:(r + 1) * RADIX, hd * HEAD_DIM:(hd + 1) * HEAD_DIM].astype(ref.dtype))

    acc_q = column_tile(0)
    acc_k = column_tile(1)
    q = norm_rope(acc_q, qg_ref, qa_ref, qb_ref, qc_ref)
    acc_v = column_tile(2)
    store_head_pieces(q16_ref, to_mod16(q))

    k = norm_rope(acc_k, kg_ref, ka_ref, kb_ref, kc_ref)
    acc_ga = column_tile(3)
    store_heads(k1_ref, k)
    store_head_pieces(k16_ref, to_mod16(k))

    v = acc_v.astype(BF16)
    acc_u = column_tile(4)
    store_heads(v1_ref, v)
    store_head_pieces(v16_ref, to_mod16(v))

    ga_ref[...] = _silu(acc_ga).astype(BF16)
    acc_gf = column_tile(5)

    for g, u16 in enumerate(to_mod16(acc_u.astype(BF16))):
        for r in range(RADIX):
            u16_ref[r, g * RADIX:(g + 1) * RADIX, :] = u16[r * RADIX:(r + 1) * RADIX, :].astype(BF16)

    gf_ref[...] = _silu(acc_gf).astype(BF16)


def _inproj(x2d, batch, seq, gain, w_in, qg, kg, perm):
    t = x2d.shape[0]
    rows = INPROJ_GROUPS * ROW_GROUP
    pieces = INPROJ_GROUPS * RADIX
    groups = seq // rows
    sub = seq // RADIX
    qa, qb, qc = _rope_tables(seq, HEAD_DIM ** -0.5 * math.log2(math.e))
    ka, kb, kc = _rope_tables(seq, 1.0)
    rope_spec = pl.BlockSpec((rows, HEAD_DIM), lambda i: (i % groups, 0))
    tok_spec = pl.BlockSpec((rows, ATTN_WIDTH), lambda i: (i, 0))
    head_nat_spec = pl.BlockSpec((N_HEADS, rows, HEAD_DIM), lambda i: (0, i, 0))
    head_m16_spec = pl.BlockSpec((N_HEADS, None, RADIX, pieces, HEAD_DIM),
                                 lambda i: (0, i // groups, 0, i % groups, 0))
    head_nat = jax.ShapeDtypeStruct((N_HEADS, t, HEAD_DIM), BF16)
    head_m16 = lambda dt: jax.ShapeDtypeStruct((N_HEADS, batch, RADIX, sub, HEAD_DIM), dt)
    return pl.pallas_call(
        _inproj_kernel,
        grid=(t // rows,),
        in_specs=[
            pl.BlockSpec((rows, D_MODEL), lambda i: (i, 0)),
            _resident((1, D_MODEL)),
            _resident(w_in.shape),
            _resident((1, HEAD_DIM)), _resident((1, HEAD_DIM)),
            rope_spec, rope_spec, rope_spec, rope_spec, rope_spec, rope_spec,
            _resident((ROW_GROUP, ROW_GROUP)),
        ],
        out_specs=[
            head_m16_spec, head_nat_spec, head_m16_spec, head_nat_spec, head_m16_spec,
            tok_spec,
            pl.BlockSpec((None, RADIX, pieces, FOURIER_WIDTH), lambda i: (i // groups, 0, i % groups, 0)),
            tok_spec,
        ],
        out_shape=[
            head_m16(_q16_dtype(seq)), head_nat, head_m16(BF16), head_nat, head_m16(BF16),
            jax.ShapeDtypeStruct((t, ATTN_WIDTH), BF16),
            jax.ShapeDtypeStruct((batch, RADIX, sub, FOURIER_WIDTH), BF16),
            jax.ShapeDtypeStruct((t, FOURIER_WIDTH), BF16),
        ],
        compiler_params=_params("arbitrary"),
        name="inproj",
    )(x2d, gain, w_in, qg, kg, qa, qb, qc, ka, kb, kc, perm)


def _window_start(first, half, total, window):
    return min(max(first - half, 0), total - window)


class _Pattern:
    def __init__(self, q_first, k_first, diff0, scale):
        deltas = [scale * (q0 - k0) for q0, k0 in zip(q_first, k_first)]
        uniq = sorted(set(deltas))
        self.q_first, self.k_first = q_first, k_first
        self.table_of = [uniq.index(d) for d in deltas]
        self.bias = jnp.asarray(np.stack(
            [np.where(np.abs(diff0 + d) <= HALF_KEYS, 0.0, MASK_VALUE) for d in uniq]).astype(np.float32)
        ).astype(BF16)


def _attention_patterns(seq):
    sub = seq // RADIX
    qi = np.arange(Q_BLOCK)[:, None]
    win16 = min(Q_BLOCK + 2 * HALF_KEYS, sub)
    q16 = list(range(0, sub, Q_BLOCK))
    p16 = _Pattern(q16, [_window_start(q0, HALF_KEYS, sub, win16) for q0 in q16],
                   qi - np.arange(win16)[None, :], 1)
    q_rows, k_rows = Q_BLOCK // 4, Q_BLOCK // 4 + 2 * HALF_KEYS // 4
    kc = np.arange(4 * k_rows)[None, :]
    q4 = list(range(0, sub, q_rows))
    p4 = _Pattern(q4, [_window_start(q0, HALF_KEYS // 4, sub, k_rows) for q0 in q4],
                  4 * (qi % q_rows - kc % k_rows) + (qi // q_rows - kc // k_rows), 4)
    win1 = Q_BLOCK + 2 * HALF_KEYS
    q1 = list(range(0, seq, Q_BLOCK))
    p1 = _Pattern(q1, [_window_start(q0, HALF_KEYS, seq, win1) for q0 in q1],
                  RADIX * (qi % (Q_BLOCK // RADIX)) + qi // (Q_BLOCK // RADIX) - np.arange(win1)[None, :], 1)
    return p16, p4, p1


def _q16_dtype(seq):
    return BF16 if seq // RADIX <= Q_BLOCK else F32


def _attn_kernel(q16_in_ref, k1_ref, v1_ref, k16_ref, v16_ref, b16_ref, b4_ref, b1_ref,
                 o_ref, acc_s, m_s, l_s, p0_s, w0_s, p1_s, w1_s, *q_wide, pat16, pat4, pat1):
    if q_wide:
        (q16_ref,) = q_wide
        for hd in range(q16_in_ref.shape[0]):
            for r in range(RADIX):
                q16_ref[hd, r] = q16_in_ref[hd, r].astype(F32)
    else:
        q16_ref = q16_in_ref

    def gather(ref, pieces):
        return jnp.concatenate([ref[r, pl.ds(s0, n), :] for r, s0, n in pieces], axis=0)

    def scatter(ref, pieces, value):
        off = 0
        for r, s0, n in pieces:
            ref[r, pl.ds(s0, n), :] = value[off:off + n]
            off += n

    def scores_stage(blocks, first, p_scr, w_scr):
        for i, (q, k, _, bias, pieces, (_, m_h, _)) in enumerate(blocks):
            s = lax.dot_general(q(), k(), (((1,), (1,)), ((), ())), preferred_element_type=F32)
            s = s.astype(BF16) + bias()
            m_new = jnp.broadcast_to(jnp.max(s, axis=-1, keepdims=True), (Q_BLOCK, HEAD_DIM)).astype(F32)
            if not first:
                m_old = gather(m_h, pieces)
                m_new = jnp.maximum(m_old, m_new)
                w_scr[i] = jnp.exp2(m_old - m_new)
            width = s.shape[1]
            m_b = m_new.astype(BF16)
            p_scr[i, :, :width] = jnp.exp2(s - jnp.concatenate([m_b] * (width // HEAD_DIM), axis=1))
            scatter(m_h, pieces, m_new)

    def values_stage(blocks, first, p_scr, w_scr):
        for i, (_, _, v, _, pieces, (acc_h, _, l_h)) in enumerate(blocks):
            vw = v()
            v_aug = jnp.concatenate([vw, jnp.ones(vw.shape, BF16)], axis=1)
            pv = jnp.dot(p_scr[i, :, :vw.shape[0]], v_aug, preferred_element_type=F32)
            acc, l = pv[:, :HEAD_DIM], pv[:, HEAD_DIM:]
            if not first:
                w_old = w_scr[i]
                acc = w_old * gather(acc_h, pieces) + acc
                l = w_old * gather(l_h, pieces) + l
            scatter(l_h, pieces, l)
            scatter(acc_h, pieces, acc)

    win16 = b16_ref.shape[2]
    q_rows4 = Q_BLOCK // 4
    k_rows4 = b4_ref.shape[2] // 4
    q_rows1 = Q_BLOCK // RADIX
    win1 = b1_ref.shape[2]
    blocks16, blocks4, blocks1 = [], [], []
    for hd in range(q16_ref.shape[0]):
        q16, k1, v1, k16, v16 = (ref.at[hd] for ref in (q16_ref, k1_ref, v1_ref, k16_ref, v16_ref))
        state = (acc_s.at[hd], m_s.at[hd], l_s.at[hd])

        for r in range(RADIX):
            for blk, (l0, ws) in enumerate(zip(pat16.q_first, pat16.k_first)):
                blocks16.append((
                    lambda q16=q16, r=r, l0=l0: q16[r, l0:l0 + Q_BLOCK, :].astype(BF16),
                    lambda k16=k16, r=r, ws=ws: k16[r, ws:ws + win16, :],
                    lambda v16=v16, r=r, ws=ws: v16[r, ws:ws + win16, :],
                    lambda blk=blk: b16_ref[pat16.table_of[blk]],
                    [(r, l0, Q_BLOCK)], state))

        for r4 in range(4):
            res = [r4 + 4 * a for a in range(4)]
            for blk, (l0, ws) in enumerate(zip(pat4.q_first, pat4.k_first)):
                q_pieces = [(r, l0, q_rows4) for r in res]
                k_pieces = [(r, ws, k_rows4) for r in res]
                blocks4.append((
                    lambda q16=q16, p=q_pieces: gather(q16, p).astype(BF16),
                    lambda k16=k16, p=k_pieces: gather(k16, p),
                    lambda v16=v16, p=k_pieces: gather(v16, p),
                    lambda blk=blk: b4_ref[pat4.table_of[blk]],
                    q_pieces, state))

        for blk, (n0, ws) in enumerate(zip(pat1.q_first, pat1.k_first)):
            q_pieces = [(r, n0 // RADIX, q_rows1) for r in range(RADIX)]
            blocks1.append((
                lambda q16=q16, p=q_pieces: gather(q16, p).astype(BF16),
                lambda k1=k1, ws=ws: k1[ws:ws + win1, :],
                lambda v1=v1, ws=ws: v1[ws:ws + win1, :],
                lambda blk=blk: b1_ref[pat1.table_of[blk]],
                q_pieces, state))

    groups = []
    for blocks, first in ((blocks16, True), (blocks4, False), (blocks1, False)):
        groups.extend((blocks[i:i + p0_s.shape[0]], first) for i in range(0, len(blocks), p0_s.shape[0]))

    slots = ((p0_s, w0_s), (p1_s, w1_s))
    scores_stage(*groups[0], *slots[0])
    for g in range(1, len(groups)):
        scores_stage(*groups[g], *slots[g % 2])
        values_stage(*groups[g - 1], *slots[(g - 1) % 2])
    values_stage(*groups[-1], *slots[(len(groups) - 1) % 2])

    def finish(i, carry):
        hd, r = i // RADIX, i % RADIX
        o_ref[hd, r] = (acc_s[hd, r] * (1.0 / l_s[hd, r])).astype(BF16)
        return carry

    lax.fori_loop(0, q16_ref.shape[0] * RADIX, finish, 0)


def _attention(q16, k1, v1, k16, v16, batch, seq):
    sub = seq // RADIX
    heads = ATTN_HEADS_PER_STEP * (2 if sub <= Q_BLOCK else 1)
    pat16, pat4, pat1 = _attention_patterns(seq)
    m16_spec = pl.BlockSpec((heads, None, RADIX, sub, HEAD_DIM), lambda b, h: (h, b, 0, 0, 0))
    nat_spec = pl.BlockSpec((heads, seq, HEAD_DIM), lambda b, h: (h, b, 0))
    state = pltpu.VMEM((heads, RADIX, sub, HEAD_DIM), F32)
    group = BLOCKS_PER_GROUP_LONG if sub > Q_BLOCK else BLOCKS_PER_GROUP_SHORT
    probs = pltpu.VMEM((group, Q_BLOCK, Q_BLOCK + 2 * HALF_KEYS), BF16)
    rescale = pltpu.VMEM((group, Q_BLOCK, HEAD_DIM), F32)
    return pl.pallas_call(
        functools.partial(_attn_kernel, pat16=pat16, pat4=pat4, pat1=pat1),
        grid=(batch, N_HEADS // heads),
        in_specs=[m16_spec, nat_spec, nat_spec, m16_spec, m16_spec,
                  _resident(pat16.bias.shape), _resident(pat4.bias.shape), _resident(pat1.bias.shape)],
        out_specs=m16_spec,
        out_shape=jax.ShapeDtypeStruct((N_HEADS, batch, RADIX, sub, HEAD_DIM), BF16),
        scratch_shapes=[state, state, state, probs, rescale, probs, rescale]
        + ([state] if q16.dtype == BF16 else []),
        compiler_params=_params("parallel", "parallel"),
        name="attention",
    )(q16, k1, v1, k16, v16, pat16.bias, pat4.bias, pat1.bias)


def _chanmat_kernel(c_ref, s_ref, w_ref, m_ref):
    for g in range(N_GROUPS):
        w = w_ref[g]
        mc = jnp.dot(c_ref[...], w, preferred_element_type=F32, precision=lax.Precision.HIGHEST)
        ms = jnp.dot(s_ref[...], w, preferred_element_type=F32, precision=lax.Precision.HIGHEST)
        m_ref[g, :, :GROUP_DIM] = mc.astype(BF16)
        m_ref[g, :, GROUP_DIM:] = ms.astype(BF16)


def _channel_matrices(w_fourier):
    idx = np.arange(GROUP_DIM)
    cc, sc = _dft_cos_sin(GROUP_DIM, idx, idx, GROUP_DIM ** -0.5)
    return pl.pallas_call(
        _chanmat_kernel,
        out_shape=jax.ShapeDtypeStruct((N_GROUPS, GROUP_DIM, 2 * GROUP_DIM), BF16),
        name="chanmat",
    )(jnp.asarray(cc, F32), jnp.asarray(sc, F32), w_fourier)


def _fourier_kernel(u_ref, m_ref, c_ref, smc_ref, cps_ref, mc_ref, ms_ref, gf_ref, o_ref, yr_s, yi_s,
                    *, stage1_steps, tiles):
    t = pl.program_id(1)
    dot = functools.partial(jnp.dot, preferred_element_type=F32)
    per_step = u_ref.shape[0]

    @pl.when(t < stage1_steps)
    def _():
        z = []
        for j in range(per_step):
            a_parts, b_parts, ab_parts = [], [], []
            for g in range(N_GROUPS):
                ab = dot(u_ref[j, :, g * GROUP_DIM:(g + 1) * GROUP_DIM], m_ref[g])
                a, b = ab[:, :GROUP_DIM], ab[:, GROUP_DIM:]
                a_parts.append(a.astype(BF16))
                b_parts.append(b.astype(BF16))
                ab_parts.append((a + b).astype(BF16))
            z.append(tuple(jnp.concatenate(p, axis=1) for p in (a_parts, b_parts, ab_parts)))
        c, s_minus_c, c_plus_s = c_ref[...], smc_ref[...], cps_ref[...]
        for j, (a, b, a_plus_b) in enumerate(z):
            n2 = t * per_step + j
            k1 = dot(c, a_plus_b)
            yr_s[n2] = (k1 - dot(c_plus_s, b)).astype(BF16)
            yi_s[n2] = (k1 + dot(s_minus_c, a)).astype(BF16)

    @pl.when(t >= stage1_steps)
    def _():
        first_row = (t - stage1_steps) * (tiles * RADIX)
        for i in range(tiles):
            rows = pl.ds(pl.multiple_of(first_row + i * RADIX, RADIX), RADIX)
            yr = jnp.concatenate([yr_s[n2, rows, :] for n2 in range(RADIX)], axis=0)
            yi = jnp.concatenate([yi_s[n2, rows, :] for n2 in range(RADIX)], axis=0)
            x = dot(mc_ref[i], yr) + dot(ms_ref[i], yi)
            out_rows = slice(i * RADIX, (i + 1) * RADIX)
            for k2 in range(RADIX):
                piece = x[k2 * RADIX:(k2 + 1) * RADIX, :] * gf_ref[k2, out_rows, :].astype(F32)
                o_ref[k2, out_rows, :] = piece.astype(BF16)


def _stage2_matrices(seq):
    s1 = seq // RADIX
    tiles = s1 // RADIX
    c, k2, k1, n2 = np.meshgrid(np.arange(tiles), np.arange(RADIX), np.arange(RADIX), np.arange(RADIX),
                                indexing="ij")
    k = RADIX * c + k1 + s1 * k2
    ang = 2.0 * np.pi * ((n2 * k) % seq).astype(np.float64) / seq
    mc = np.zeros((tiles, ROW_GROUP, ROW_GROUP), np.float32)
    ms = np.zeros((tiles, ROW_GROUP, ROW_GROUP), np.float32)
    mc[c, RADIX * k2 + k1, RADIX * n2 + k1] = np.cos(ang) * seq ** -0.5
    ms[c, RADIX * k2 + k1, RADIX * n2 + k1] = -np.sin(ang) * seq ** -0.5
    return _bf16_const(mc), _bf16_const(ms)


def _fourier(u16, chan_mats, gf, batch, seq, tiles=FOURIER2_TILES):
    s1 = seq // RADIX
    idx = np.arange(s1)
    c1, sn1 = _dft_cos_sin(s1, idx, idx)
    mc, ms = _stage2_matrices(seq)
    per_step = FOURIER1_ROWS // s1
    stage1_steps = RADIX // per_step
    rows = tiles * RADIX
    stage2 = lambda t: jnp.maximum(t - stage1_steps, 0)
    nat_spec = pl.BlockSpec((None, RADIX, rows, FOURIER_WIDTH), lambda b, t: (b, 0, stage2(t), 0))
    m_spec = pl.BlockSpec((tiles, ROW_GROUP, ROW_GROUP), lambda b, t: (stage2(t), 0, 0))
    y_scratch = pltpu.VMEM((RADIX, s1, FOURIER_WIDTH), BF16)
    out = pl.pallas_call(
        functools.partial(_fourier_kernel, stage1_steps=stage1_steps, tiles=tiles),
        grid=(batch, stage1_steps + s1 // rows),
        in_specs=[pl.BlockSpec((None, per_step, s1, FOURIER_WIDTH),
                               lambda b, t: (b, jnp.minimum(t, stage1_steps - 1), 0, 0)),
                  _resident(chan_mats.shape), _resident((s1, s1)), _resident((s1, s1)), _resident((s1, s1)),
                  m_spec, m_spec, nat_spec],
        out_specs=nat_spec,
        out_shape=jax.ShapeDtypeStruct((batch, RADIX, s1, FOURIER_WIDTH), BF16),
        scratch_shapes=[y_scratch, y_scratch],
        compiler_params=_params("parallel", "arbitrary"),
        name="fourier",
    )(u16, chan_mats, _bf16_const(c1), _bf16_const(sn1 - c1), _bf16_const(c1 + sn1), mc, ms,
      gf.reshape(batch, RADIX, s1, FOURIER_WIDTH))
    return out.reshape(batch * seq, FOURIER_WIDTH)


def _outproj_kernel(x_ref, o16_ref, ga_ref, mf_ref, unperm_ref, w_ref, y_ref):
    unperm = unperm_ref[...]
    attn = []
    for g in range(OUTPROJ_GROUPS):
        rows = slice(g * RADIX, (g + 1) * RADIX)
        heads = [jnp.concatenate([o16_ref[hd, r, rows, :] for r in range(RADIX)], axis=0)
                 for hd in range(N_HEADS)]
        attn16 = jnp.concatenate(heads, axis=1)
        attn.append(jnp.dot(unperm, attn16, preferred_element_type=F32))
    y = x_ref[...] + jnp.dot(mf_ref[...], w_ref[ATTN_WIDTH:, :], preferred_element_type=F32)
    mix_a = (jnp.concatenate(attn, axis=0) * ga_ref[...].astype(F32)).astype(BF16)
    y_ref[...] = y + jnp.dot(mix_a, w_ref[:ATTN_WIDTH, :], preferred_element_type=F32)


def _outproj(x2d, o16, ga, mix_f, unperm, w_out, seq):
    t = x2d.shape[0]
    rows = OUTPROJ_GROUPS * ROW_GROUP
    steps_per_seq = seq // rows
    tok_spec = pl.BlockSpec((rows, ATTN_WIDTH), lambda i: (i, 0))
    x_spec = pl.BlockSpec((rows, D_MODEL), lambda i: (i, 0))
    return pl.pallas_call(
        _outproj_kernel,
        grid=(t // rows,),
        in_specs=[x_spec,
                  pl.BlockSpec((N_HEADS, None, RADIX, OUTPROJ_GROUPS * RADIX, HEAD_DIM),
                               lambda i: (0, i // steps_per_seq, 0, i % steps_per_seq, 0)),
                  tok_spec, tok_spec,
                  _resident((ROW_GROUP, ROW_GROUP)),
                  _resident((D_MODEL, D_MODEL))],
        out_specs=x_spec,
        out_shape=jax.ShapeDtypeStruct((t, D_MODEL), F32),
        compiler_params=_params("arbitrary"),
        name="outproj",
    )(x2d, o16, ga, mix_f, unperm, w_out)


def _layer(x, gain, w_in, qg, kg, chan_mats, w_out, perm, unperm):
    batch, seq, _ = x.shape
    x2d = x.reshape(batch * seq, D_MODEL)
    q16, k1, k16, v1, v16, ga, u16, gf = _inproj(x2d, batch, seq, gain, w_in, qg, kg, perm)
    o16 = _attention(q16, k1, v1, k16, v16, batch, seq)
    mix_f = _fourier(u16, chan_mats, gf, batch, seq)
    y = _outproj(x2d, o16, ga, mix_f, unperm, w_out, seq)
    return y.reshape(batch, seq, D_MODEL)


def kernel(x_prompt, x_sample, rms_gain, w_in, q_norm_gain, k_norm_gain, w_fourier, w_out):
    depth = rms_gain.shape[0]
    p = _group_permutation()
    perm, unperm = _bf16_const(p), _bf16_const(p.T)
    for l in range(depth):
        gain = rms_gain[l].reshape(1, D_MODEL)
        w_in_l = w_in[l].astype(BF16)
        w_out_l = w_out[l].astype(BF16)
        qg = q_norm_gain[l].reshape(1, HEAD_DIM)
        kg = k_norm_gain[l].reshape(1, HEAD_DIM)
        chan_mats = _channel_matrices(w_fourier[l])
        x_prompt = _layer(x_prompt, gain, w_in_l, qg, kg, chan_mats, w_out_l, perm, unperm)
        x_sample = _layer(x_sample, gain, w_in_l, qg, kg, chan_mats, w_out_l, perm, unperm)
    return (x_prompt, x_sample)
```

```python
import functools
import math

import jax
import jax.numpy as jnp
import numpy as np
from jax import lax
from jax.experimental import pallas as pl
from jax.experimental.pallas import tpu as pltpu

D_MODEL = 2048
ATTN_WIDTH = 1024
FOURIER_WIDTH = 1024
HEAD_DIM = 128
N_HEADS = ATTN_WIDTH // HEAD_DIM
N_GROUPS = 4
GROUP_DIM = FOURIER_WIDTH // N_GROUPS
ROPE_THETA = 500000.0
ROPE_DIM = HEAD_DIM // 4
ROPE_HALF = ROPE_DIM // 2
HALF_KEYS = 64
RMS_EPS = 1e-6
RADIX = 16
ROW_GROUP = RADIX * RADIX
Q_BLOCK = 128
FOURIER1_ROWS = 1024
FOURIER2_TILES = 8
ATTN_HEADS_PER_STEP = 2
INPROJ_GROUPS = 1
OUTPROJ_GROUPS = 2
BLOCKS_PER_GROUP_LONG = 6
BLOCKS_PER_GROUP_SHORT = 4
MASK_VALUE = -1e30

VMEM_LIMIT_BYTES = 56 * 1024 * 1024

F32 = jnp.float32
BF16 = jnp.bfloat16


def _params(*semantics):
    return pltpu.CompilerParams(dimension_semantics=semantics, vmem_limit_bytes=VMEM_LIMIT_BYTES)


def _resident(shape):
    return pl.BlockSpec(shape, lambda *_: (0,) * len(shape), pipeline_mode=pl.Buffered(1))


def _rope_tables(seq, scale):
    expo = np.arange(ROPE_HALF, dtype=np.float32) / np.float32(ROPE_HALF)
    inv_freq = (np.float32(1.0) / np.power(np.float32(ROPE_THETA), expo)).astype(np.float32)
    ang = (np.arange(seq, dtype=np.float32)[:, None] * inv_freq[None, :]).astype(np.float64)
    cos, sin = np.cos(ang), np.sin(ang)
    a = np.ones((seq, HEAD_DIM))
    b = np.zeros((seq, HEAD_DIM))
    c = np.zeros((seq, HEAD_DIM))
    a[:, :ROPE_HALF] = cos
    a[:, ROPE_HALF:ROPE_DIM] = cos
    b[:, ROPE_HALF:ROPE_DIM] = sin
    c[:, :ROPE_HALF] = -sin
    return tuple(jnp.asarray((t * scale).astype(np.float32)) for t in (a, b, c))


def _dft_cos_sin(n, rows, cols, scale=1.0):
    m = (np.asarray(rows, dtype=np.int64)[:, None] * np.asarray(cols, dtype=np.int64)[None, :]) % n
    ang = 2.0 * np.pi * m.astype(np.float64) / n
    return np.cos(ang) * scale, np.sin(ang) * scale


def _group_permutation():
    p = np.zeros((ROW_GROUP, ROW_GROUP), np.float32)
    j, r = np.meshgrid(np.arange(RADIX), np.arange(RADIX), indexing="ij")
    p[(RADIX * r + j).ravel(), (RADIX * j + r).ravel()] = 1.0
    return p


def _bf16_const(a):
    return jnp.asarray(np.asarray(a, np.float32)).astype(BF16)


def _silu(a):
    return a * (1.0 / (1.0 + jnp.exp(-a)))


def _inproj_kernel(x_ref, gain_ref, w_ref, qg_ref, kg_ref,
                   qa_ref, qb_ref, qc_ref, ka_ref, kb_ref, kc_ref, perm_ref,
                   q16_ref, k1_ref, k16_ref, v1_ref, v16_ref, ga_ref, u16_ref, gf_ref):
    x = x_ref[...]
    ms = jnp.mean(x * x, axis=-1, keepdims=True)
    h = (x * lax.rsqrt(ms + RMS_EPS) * gain_ref[...]).astype(BF16)
    perm = perm_ref[...]

    half_w = ATTN_WIDTH // 2
    half_heads = N_HEADS // 2

    def column_tile(n):
        cols = slice(n * half_w, (n + 1) * half_w)
        return jnp.dot(h, w_ref[:, cols], preferred_element_type=F32)

    def norm_rope(acc, g_ref, a_ref, b_ref, c_ref):
        heads = []
        for hd in range(half_heads):
            a = acc[:, hd * HEAD_DIM:(hd + 1) * HEAD_DIM]
            ms_h = jnp.mean(a * a, axis=-1, keepdims=True)
            n = a * lax.rsqrt(ms_h + RMS_EPS) * g_ref[...]
            r = (n * a_ref[...]
                 + pltpu.roll(n, ROPE_HALF, 1) * b_ref[...]
                 + pltpu.roll(n, HEAD_DIM - ROPE_HALF, 1) * c_ref[...])
            heads.append(r.astype(BF16))
        return jnp.concatenate(heads, axis=1)

    def to_mod16(t):
        return [jnp.dot(perm, t[g * ROW_GROUP:(g + 1) * ROW_GROUP], preferred_element_type=F32)
                for g in range(INPROJ_GROUPS)]

    def store_heads(ref, t, head0):
        for hd in range(half_heads):
            ref[head0 + hd] = t[:, hd * HEAD_DIM:(hd + 1) * HEAD_DIM]

    def store_head_pieces(ref, groups, head0):
        for g, t in enumerate(groups):
            for hd in range(half_heads):
                for r in range(RADIX):
                    ref[head0 + hd, r, g * RADIX:(g + 1) * RADIX, :] = (
                        t[r * RADIX:(r + 1) * RADIX, hd * HEAD_DIM:(hd + 1) * HEAD_DIM].astype(ref.dtype))

    def finish_q(acc, half):
        q = norm_rope(acc, qg_ref, qa_ref, qb_ref, qc_ref)
        store_head_pieces(q16_ref, to_mod16(q), half * half_heads)

    def finish_k(acc, half):
        k = norm_rope(acc, kg_ref, ka_ref, kb_ref, kc_ref)
        store_heads(k1_ref, k, half * half_heads)
        store_head_pieces(k16_ref, to_mod16(k), half * half_heads)

    def finish_v(acc, half):
        v = acc.astype(BF16)
        store_heads(v1_ref, v, half * half_heads)
        store_head_pieces(v16_ref, to_mod16(v), half * half_heads)

    def finish_gate(ref):
        def finish(acc, half):
            ref[:, half * half_w:(half + 1) * half_w] = _silu(acc).astype(BF16)
        return finish

    def finish_u(acc, half):
        for g, u16 in enumerate(to_mod16(acc.astype(BF16))):
            for r in range(RADIX):
                u16_ref[r, g * RADIX:(g + 1) * RADIX, half * half_w:(half + 1) * half_w] = (
                    u16[r * RADIX:(r + 1) * RADIX, :].astype(BF16))

    finishers = (finish_q, finish_k, finish_v, finish_gate(ga_ref), finish_u, finish_gate(gf_ref))
    n_tiles = 2 * len(finishers)
    accs = {0: column_tile(0), 1: column_tile(1)}
    for n in range(n_tiles):
        if n + 2 < n_tiles:
            accs[n + 2] = column_tile(n + 2)
        finishers[n // 2](accs.pop(n), n % 2)


def _inproj(x2d, batch, seq, gain, w_in, qg, kg, perm):
    t = x2d.shape[0]
    rows = INPROJ_GROUPS * ROW_GROUP
    pieces = INPROJ_GROUPS * RADIX
    groups = seq // rows
    sub = seq // RADIX
    qa, qb, qc = _rope_tables(seq, HEAD_DIM ** -0.5 * math.log2(math.e))
    ka, kb, kc = _rope_tables(seq, 1.0)
    rope_spec = pl.BlockSpec((rows, HEAD_DIM), lambda i: (i % groups, 0))
    tok_spec = pl.BlockSpec((rows, ATTN_WIDTH), lambda i: (i, 0))
    head_nat_spec = pl.BlockSpec((N_HEADS, rows, HEAD_DIM), lambda i: (0, i, 0))
    head_m16_spec = pl.BlockSpec((N_HEADS, None, RADIX, pieces, HEAD_DIM),
                                 lambda i: (0, i // groups, 0, i % groups, 0))
    head_nat = jax.ShapeDtypeStruct((N_HEADS, t, HEAD_DIM), BF16)
    head_m16 = lambda dt: jax.ShapeDtypeStruct((N_HEADS, batch, RADIX, sub, HEAD_DIM), dt)
    return pl.pallas_call(
        _inproj_kernel,
        grid=(t // rows,),
        in_specs=[
            pl.BlockSpec((rows, D_MODEL), lambda i: (i, 0)),
            _resident((1, D_MODEL)),
            _resident(w_in.shape),
            _resident((1, HEAD_DIM)), _resident((1, HEAD_DIM)),
            rope_spec, rope_spec, rope_spec, rope_spec, rope_spec, rope_spec,
            _resident((ROW_GROUP, ROW_GROUP)),
        ],
        out_specs=[
            head_m16_spec, head_nat_spec, head_m16_spec, head_nat_spec, head_m16_spec,
            tok_spec,
            pl.BlockSpec((None, RADIX, pieces, FOURIER_WIDTH), lambda i: (i // groups, 0, i % groups, 0)),
            tok_spec,
        ],
        out_shape=[
            head_m16(F32), head_nat, head_m16(BF16), head_nat, head_m16(BF16),
            jax.ShapeDtypeStruct((t, ATTN_WIDTH), BF16),
            jax.ShapeDtypeStruct((batch, RADIX, sub, FOURIER_WIDTH), BF16),
            jax.ShapeDtypeStruct((t, FOURIER_WIDTH), BF16),
        ],
        compiler_params=_params("arbitrary"),
        name="inproj",
    )(x2d, gain, w_in, qg, kg, qa, qb, qc, ka, kb, kc, perm)


def _window_start(first, half, total, window):
    return min(max(first - half, 0), total - window)


class _Pattern:
    def __init__(self, q_first, k_first, diff0, scale):
        deltas = [scale * (q0 - k0) for q0, k0 in zip(q_first, k_first)]
        uniq = sorted(set(deltas))
        self.q_first, self.k_first = q_first, k_first
        self.table_of = [uniq.index(d) for d in deltas]
        self.bias = jnp.asarray(np.stack(
            [np.where(np.abs(diff0 + d) <= HALF_KEYS, 0.0, MASK_VALUE) for d in uniq]).astype(np.float32)
        ).astype(BF16)


def _attention_patterns(seq):
    sub = seq // RADIX
    qi = np.arange(Q_BLOCK)[:, None]
    win16 = min(Q_BLOCK + 2 * HALF_KEYS, sub)
    q16 = list(range(0, sub, Q_BLOCK))
    p16 = _Pattern(q16, [_window_start(q0, HALF_KEYS, sub, win16) for q0 in q16],
                   qi - np.arange(win16)[None, :], 1)
    q_rows, k_rows = Q_BLOCK // 4, Q_BLOCK // 4 + 2 * HALF_KEYS // 4
    kc = np.arange(4 * k_rows)[None, :]
    q4 = list(range(0, sub, q_rows))
    p4 = _Pattern(q4, [_window_start(q0, HALF_KEYS // 4, sub, k_rows) for q0 in q4],
                  4 * (qi % q_rows - kc % k_rows) + (qi // q_rows - kc // k_rows), 4)
    win1 = Q_BLOCK + 2 * HALF_KEYS
    q1 = list(range(0, seq, Q_BLOCK))
    p1 = _Pattern(q1, [_window_start(q0, HALF_KEYS, seq, win1) for q0 in q1],
                  RADIX * (qi % (Q_BLOCK // RADIX)) + qi // (Q_BLOCK // RADIX) - np.arange(win1)[None, :], 1)
    return p16, p4, p1


def _attn_kernel(q16_ref, k1_ref, v1_ref, k16_ref, v16_ref, b16_ref, b4_ref, b1_ref,
                 o_ref, acc_s, m_s, l_s, p0_s, w0_s, p1_s, w1_s, *, pat16, pat4, pat1):
    def gather(ref, pieces):
        return jnp.concatenate([ref[r, pl.ds(s0, n), :] for r, s0, n in pieces], axis=0)

    def scatter(ref, pieces, value):
        off = 0
        for r, s0, n in pieces:
            ref[r, pl.ds(s0, n), :] = value[off:off + n]
            off += n

    def scores_stage(blocks, first, p_scr, w_scr):
        for i, (q, k, _, bias, pieces, (_, m_h, _)) in enumerate(blocks):
            s = lax.dot_general(q(), k(), (((1,), (1,)), ((), ())), preferred_element_type=F32)
            s = s.astype(BF16) + bias()
            m_new = jnp.broadcast_to(jnp.max(s, axis=-1, keepdims=True), (Q_BLOCK, HEAD_DIM)).astype(F32)
            if not first:
                m_old = gather(m_h, pieces)
                m_new = jnp.maximum(m_old, m_new)
                w_scr[i] = jnp.exp2(m_old - m_new)
            width = s.shape[1]
            m_b = m_new.astype(BF16)
            p_scr[i, :, :width] = jnp.exp2(s - jnp.concatenate([m_b] * (width // HEAD_DIM), axis=1))
            scatter(m_h, pieces, m_new)

    def values_stage(blocks, first, p_scr, w_scr):
        for i, (_, _, v, _, pieces, (acc_h, _, l_h)) in enumerate(blocks):
            vw = v()
            v_aug = jnp.concatenate([vw, jnp.ones(vw.shape, BF16)], axis=1)
            pv = jnp.dot(p_scr[i, :, :vw.shape[0]], v_aug, preferred_element_type=F32)
            acc, l = pv[:, :HEAD_DIM], pv[:, HEAD_DIM:]
            if not first:
                w_old = w_scr[i]
                acc = w_old * gather(acc_h, pieces) + acc
                l = w_old * gather(l_h, pieces) + l
            scatter(l_h, pieces, l)
            scatter(acc_h, pieces, acc)

    win16 = b16_ref.shape[2]
    q_rows4 = Q_BLOCK // 4
    k_rows4 = b4_ref.shape[2] // 4
    q_rows1 = Q_BLOCK // RADIX
    win1 = b1_ref.shape[2]
    blocks16, blocks4, blocks1 = [], [], []
    for hd in range(q16_ref.shape[0]):
        q16, k1, v1, k16, v16 = (ref.at[hd] for ref in (q16_ref, k1_ref, v1_ref, k16_ref, v16_ref))
        state = (acc_s.at[hd], m_s.at[hd], l_s.at[hd])

        for r in range(RADIX):
            for blk, (l0, ws) in enumerate(zip(pat16.q_first, pat16.k_first)):
                blocks16.append((
                    lambda q16=q16, r=r, l0=l0: q16[r, l0:l0 + Q_BLOCK, :].astype(BF16),
                    lambda k16=k16, r=r, ws=ws: k16[r, ws:ws + win16, :],
                    lambda v16=v16, r=r, ws=ws: v16[r, ws:ws + win16, :],
                    lambda blk=blk: b16_ref[pat16.table_of[blk]],
                    [(r, l0, Q_BLOCK)], state))

        for r4 in range(4):
            res = [r4 + 4 * a for a in range(4)]
            for blk, (l0, ws) in enumerate(zip(pat4.q_first, pat4.k_first)):
                q_pieces = [(r, l0, q_rows4) for r in res]
                k_pieces = [(r, ws, k_rows4) for r in res]
                blocks4.append((
                    lambda q16=q16, p=q_pieces: gather(q16, p).astype(BF16),
                    lambda k16=k16, p=k_pieces: gather(k16, p),
                    lambda v16=v16, p=k_pieces: gather(v16, p),
                    lambda blk=blk: b4_ref[pat4.table_of[blk]],
                    q_pieces, state))

        for blk, (n0, ws) in enumerate(zip(pat1.q_first, pat1.k_first)):
            q_pieces = [(r, n0 // RADIX, q_rows1) for r in range(RADIX)]
            blocks1.append((
                lambda q16=q16, p=q_pieces: gather(q16, p).astype(BF16),
                lambda k1=k1, ws=ws: k1[ws:ws + win1, :],
                lambda v1=v1, ws=ws: v1[ws:ws + win1, :],
                lambda blk=blk: b1_ref[pat1.table_of[blk]],
                q_pieces, state))

    groups = []
    for blocks, first in ((blocks16, True), (blocks4, False), (blocks1, False)):
        groups.extend((blocks[i:i + p0_s.shape[0]], first) for i in range(0, len(blocks), p0_s.shape[0]))

    slots = ((p0_s, w0_s), (p1_s, w1_s))
    scores_stage(*groups[0], *slots[0])
    for g in range(1, len(groups)):
        scores_stage(*groups[g], *slots[g % 2])
        values_stage(*groups[g - 1], *slots[(g - 1) % 2])
    values_stage(*groups[-1], *slots[(len(groups) - 1) % 2])

    def finish(i, carry):
        hd, r = i // RADIX, i % RADIX
        o_ref[hd, r] = (acc_s[hd, r] * (1.0 / l_s[hd, r])).astype(BF16)
        return carry

    lax.fori_loop(0, q16_ref.shape[0] * RADIX, finish, 0)


def _attention(q16, k1, v1, k16, v16, batch, seq):
    sub = seq // RADIX
    heads = ATTN_HEADS_PER_STEP * (2 if sub <= Q_BLOCK else 1)
    pat16, pat4, pat1 = _attention_patterns(seq)
    m16_spec = pl.BlockSpec((heads, None, RADIX, sub, HEAD_DIM), lambda b, h: (h, b, 0, 0, 0))
    nat_spec = pl.BlockSpec((heads, seq, HEAD_DIM), lambda b, h: (h, b, 0))
    state = pltpu.VMEM((heads, RADIX, sub, HEAD_DIM), F32)
    group = BLOCKS_PER_GROUP_LONG if sub > Q_BLOCK else BLOCKS_PER_GROUP_SHORT
    probs = pltpu.VMEM((group, Q_BLOCK, Q_BLOCK + 2 * HALF_KEYS), BF16)
    rescale = pltpu.VMEM((group, Q_BLOCK, HEAD_DIM), F32)
    return pl.pallas_call(
        functools.partial(_attn_kernel, pat16=pat16, pat4=pat4, pat1=pat1),
        grid=(batch, N_HEADS // heads),
        in_specs=[m16_spec, nat_spec, nat_spec, m16_spec, m16_spec,
                  _resident(pat16.bias.shape), _resident(pat4.bias.shape), _resident(pat1.bias.shape)],
        out_specs=m16_spec,
        out_shape=jax.ShapeDtypeStruct((N_HEADS, batch, RADIX, sub, HEAD_DIM), BF16),
        scratch_shapes=[state, state, state, probs, rescale, probs, rescale],
        compiler_params=_params("parallel", "parallel"),
        name="attention",
    )(q16, k1, v1, k16, v16, pat16.bias, pat4.bias, pat1.bias)


def _chanmat_kernel(c_ref, s_ref, w_ref, m_ref):
    for g in range(N_GROUPS):
        w = w_ref[g]
        mc = jnp.dot(c_ref[...], w, preferred_element_type=F32, precision=lax.Precision.HIGHEST)
        ms = jnp.dot(s_ref[...], w, preferred_element_type=F32, precision=lax.Precision.HIGHEST)
        m_ref[g, :, :GROUP_DIM] = mc.astype(BF16)
        m_ref[g, :, GROUP_DIM:] = ms.astype(BF16)


def _channel_matrices(w_fourier):
    idx = np.arange(GROUP_DIM)
    cc, sc = _dft_cos_sin(GROUP_DIM, idx, idx, GROUP_DIM ** -0.5)
    return pl.pallas_call(
        _chanmat_kernel,
        out_shape=jax.ShapeDtypeStruct((N_GROUPS, GROUP_DIM, 2 * GROUP_DIM), BF16),
        name="chanmat",
    )(jnp.asarray(cc, F32), jnp.asarray(sc, F32), w_fourier)


def _fourier_kernel(u_ref, m_ref, c_ref, smc_ref, cps_ref, mc_ref, ms_ref, gf_ref, o_ref, yr_s, yi_s,
                    *, stage1_steps, tiles):
    t = pl.program_id(1)
    dot = functools.partial(jnp.dot, preferred_element_type=F32)
    per_step = u_ref.shape[0]

    @pl.when(t < stage1_steps)
    def _():
        z = []
        for j in range(per_step):
            a_parts, b_parts, ab_parts = [], [], []
            for g in range(N_GROUPS):
                ab = dot(u_ref[j, :, g * GROUP_DIM:(g + 1) * GROUP_DIM], m_ref[g])
                a, b = ab[:, :GROUP_DIM], ab[:, GROUP_DIM:]
                a_parts.append(a.astype(BF16))
                b_parts.append(b.astype(BF16))
                ab_parts.append((a + b).astype(BF16))
            z.append(tuple(jnp.concatenate(p, axis=1) for p in (a_parts, b_parts, ab_parts)))
        c, s_minus_c, c_plus_s = c_ref[...], smc_ref[...], cps_ref[...]
        for j, (a, b, a_plus_b) in enumerate(z):
            n2 = t * per_step + j
            k1 = dot(c, a_plus_b)
            yr_s[n2] = (k1 - dot(c_plus_s, b)).astype(BF16)
            yi_s[n2] = (k1 + dot(s_minus_c, a)).astype(BF16)

    @pl.when(t >= stage1_steps)
    def _():
        first_row = (t - stage1_steps) * (tiles * RADIX)
        for i in range(tiles):
            rows = pl.ds(pl.multiple_of(first_row + i * RADIX, RADIX), RADIX)
            yr = jnp.concatenate([yr_s[n2, rows, :] for n2 in range(RADIX)], axis=0)
            yi = jnp.concatenate([yi_s[n2, rows, :] for n2 in range(RADIX)], axis=0)
            x = dot(mc_ref[i], yr) + dot(ms_ref[i], yi)
            out_rows = slice(i * RADIX, (i + 1) * RADIX)
            for k2 in range(RADIX):
                piece = x[k2 * RADIX:(k2 + 1) * RADIX, :] * gf_ref[k2, out_rows, :].astype(F32)
                o_ref[k2, out_rows, :] = piece.astype(BF16)


def _stage2_matrices(seq):
    s1 = seq // RADIX
    tiles = s1 // RADIX
    c, k2, k1, n2 = np.meshgrid(np.arange(tiles), np.arange(RADIX), np.arange(RADIX), np.arange(RADIX),
                                indexing="ij")
    k = RADIX * c + k1 + s1 * k2
    ang = 2.0 * np.pi * ((n2 * k) % seq).astype(np.float64) / seq
    mc = np.zeros((tiles, ROW_GROUP, ROW_GROUP), np.float32)
    ms = np.zeros((tiles, ROW_GROUP, ROW_GROUP), np.float32)
    mc[c, RADIX * k2 + k1, RADIX * n2 + k1] = np.cos(ang) * seq ** -0.5
    ms[c, RADIX * k2 + k1, RADIX * n2 + k1] = -np.sin(ang) * seq ** -0.5
    return _bf16_const(mc), _bf16_const(ms)


def _fourier(u16, chan_mats, gf, batch, seq, tiles=FOURIER2_TILES):
    s1 = seq // RADIX
    idx = np.arange(s1)
    c1, sn1 = _dft_cos_sin(s1, idx, idx)
    mc, ms = _stage2_matrices(seq)
    per_step = FOURIER1_ROWS // s1
    stage1_steps = RADIX // per_step
    rows = tiles * RADIX
    stage2 = lambda t: jnp.maximum(t - stage1_steps, 0)
    nat_spec = pl.BlockSpec((None, RADIX, rows, FOURIER_WIDTH), lambda b, t: (b, 0, stage2(t), 0))
    m_spec = pl.BlockSpec((tiles, ROW_GROUP, ROW_GROUP), lambda b, t: (stage2(t), 0, 0))
    y_scratch = pltpu.VMEM((RADIX, s1, FOURIER_WIDTH), BF16)
    out = pl.pallas_call(
        functools.partial(_fourier_kernel, stage1_steps=stage1_steps, tiles=tiles),
        grid=(batch, stage1_steps + s1 // rows),
        in_specs=[pl.BlockSpec((None, per_step, s1, FOURIER_WIDTH),
                               lambda b, t: (b, jnp.minimum(t, stage1_steps - 1), 0, 0)),
                  _resident(chan_mats.shape), _resident((s1, s1)), _resident((s1, s1)), _resident((s1, s1)),
                  m_spec, m_spec, nat_spec],
        out_specs=nat_spec,
        out_shape=jax.ShapeDtypeStruct((batch, RADIX, s1, FOURIER_WIDTH), BF16),
        scratch_shapes=[y_scratch, y_scratch],
        compiler_params=_params("parallel", "arbitrary"),
        name="fourier",
    )(u16, chan_mats, _bf16_const(c1), _bf16_const(sn1 - c1), _bf16_const(c1 + sn1), mc, ms,
      gf.reshape(batch, RADIX, s1, FOURIER_WIDTH))
    return out.reshape(batch * seq, FOURIER_WIDTH)


def _outproj_kernel(x_ref, o16_ref, ga_ref, mf_ref, unperm_ref, w_ref, y_ref):
    unperm = unperm_ref[...]
    attn = []
    for g in range(OUTPROJ_GROUPS):
        rows = slice(g * RADIX, (g + 1) * RADIX)
        heads = [jnp.concatenate([o16_ref[hd, r, rows, :] for r in range(RADIX)], axis=0)
                 for hd in range(N_HEADS)]
        attn16 = jnp.concatenate(heads, axis=1)
        attn.append(jnp.dot(unperm, attn16, preferred_element_type=F32))
    y = x_ref[...] + jnp.dot(mf_ref[...], w_ref[ATTN_WIDTH:, :], preferred_element_type=F32)
    mix_a = (jnp.concatenate(attn, axis=0) * ga_ref[...].astype(F32)).astype(BF16)
    y_ref[...] = y + jnp.dot(mix_a, w_ref[:ATTN_WIDTH, :], preferred_element_type=F32)


def _outproj(x2d, o16, ga, mix_f, unperm, w_out, seq):
    t = x2d.shape[0]
    rows = OUTPROJ_GROUPS * ROW_GROUP
    steps_per_seq = seq // rows
    tok_spec = pl.BlockSpec((rows, ATTN_WIDTH), lambda i: (i, 0))
    x_spec = pl.BlockSpec((rows, D_MODEL), lambda i: (i, 0))
    return pl.pallas_call(
        _outproj_kernel,
        grid=(t // rows,),
        in_specs=[x_spec,
                  pl.BlockSpec((N_HEADS, None, RADIX, OUTPROJ_GROUPS * RADIX, HEAD_DIM),
                               lambda i: (0, i // steps_per_seq, 0, i % steps_per_seq, 0)),
                  tok_spec, tok_spec,
                  _resident((ROW_GROUP, ROW_GROUP)),
                  _resident((D_MODEL, D_MODEL))],
        out_specs=x_spec,
        out_shape=jax.ShapeDtypeStruct((t, D_MODEL), F32),
        compiler_params=_params("arbitrary"),
        name="outproj",
    )(x2d, o16, ga, mix_f, unperm, w_out)


def _layer(x, gain, w_in, qg, kg, chan_mats, w_out, perm, unperm):
    batch, seq, _ = x.shape
    x2d = x.reshape(batch * seq, D_MODEL)
    q16, k1, k16, v1, v16, ga, u16, gf = _inproj(x2d, batch, seq, gain, w_in, qg, kg, perm)
    o16 = _attention(q16, k1, v1, k16, v16, batch, seq)
    mix_f = _fourier(u16, chan_mats, gf, batch, seq)
    y = _outproj(x2d, o16, ga, mix_f, unperm, w_out, seq)
    return y.reshape(batch, seq, D_MODEL)


def kernel(x_prompt, x_sample, rms_gain, w_in, q_norm_gain, k_norm_gain, w_fourier, w_out):
    depth = rms_gain.shape[0]
    p = _group_permutation()
    perm, unperm = _bf16_const(p), _bf16_const(p.T)
    for l in range(depth):
        gain = rms_gain[l].reshape(1, D_MODEL)
        w_in_l = w_in[l].astype(BF16)
        w_out_l = w_out[l].astype(BF16)
        qg = q_norm_gain[l].reshape(1, HEAD_DIM)
        kg = k_norm_gain[l].reshape(1, HEAD_DIM)
        chan_mats = _channel_matrices(w_fourier[l])
        x_prompt = _layer(x_prompt, gain, w_in_l, qg, kg, chan_mats, w_out_l, perm, unperm)
        x_sample = _layer(x_sample, gain, w_in_l, qg, kg, chan_mats, w_out_l, perm, unperm)
    return (x_prompt, x_sample)
```

```python
import functools
import math

import jax
import jax.numpy as jnp
import numpy as np
from jax import lax
from jax.experimental import pallas as pl
from jax.experimental.pallas import tpu as pltpu

D_MODEL = 2048
ATTN_WIDTH = 1024
FOURIER_WIDTH = 1024
HEAD_DIM = 128
N_HEADS = ATTN_WIDTH // HEAD_DIM
N_GROUPS = 4
GROUP_DIM = FOURIER_WIDTH // N_GROUPS
ROPE_THETA = 500000.0
ROPE_DIM = HEAD_DIM // 4
ROPE_HALF = ROPE_DIM // 2
HALF_KEYS = 64
RMS_EPS = 1e-6
RADIX = 16
ROW_GROUP = RADIX * RADIX
Q_BLOCK = 128
FOURIER1_ROWS = 1024
FOURIER2_TILES = 8
ATTN_HEADS_PER_STEP = 2
INPROJ_GROUPS = 1
OUTPROJ_GROUPS = 2
BLOCKS_PER_GROUP_LONG = 6
BLOCKS_PER_GROUP_SHORT = 4
MASK_VALUE = -1e30

VMEM_LIMIT_BYTES = 56 * 1024 * 1024

F32 = jnp.float32
BF16 = jnp.bfloat16


def _params(*semantics):
    return pltpu.CompilerParams(dimension_semantics=semantics, vmem_limit_bytes=VMEM_LIMIT_BYTES)


def _resident(shape):
    return pl.BlockSpec(shape, lambda *_: (0,) * len(shape), pipeline_mode=pl.Buffered(1))


def _rope_tables(seq, scale):
    expo = np.arange(ROPE_HALF, dtype=np.float32) / np.float32(ROPE_HALF)
    inv_freq = (np.float32(1.0) / np.power(np.float32(ROPE_THETA), expo)).astype(np.float32)
    ang = (np.arange(seq, dtype=np.float32)[:, None] * inv_freq[None, :]).astype(np.float64)
    cos, sin = np.cos(ang), np.sin(ang)
    a = np.ones((seq, HEAD_DIM))
    b = np.zeros((seq, HEAD_DIM))
    c = np.zeros((seq, HEAD_DIM))
    a[:, :ROPE_HALF] = cos
    a[:, ROPE_HALF:ROPE_DIM] = cos
    b[:, ROPE_HALF:ROPE_DIM] = sin
    c[:, :ROPE_HALF] = -sin
    return tuple(jnp.asarray((t * scale).astype(np.float32)) for t in (a, b, c))


def _dft_cos_sin(n, rows, cols, scale=1.0):
    m = (np.asarray(rows, dtype=np.int64)[:, None] * np.asarray(cols, dtype=np.int64)[None, :]) % n
    ang = 2.0 * np.pi * m.astype(np.float64) / n
    return np.cos(ang) * scale, np.sin(ang) * scale


def _group_permutation():
    p = np.zeros((ROW_GROUP, ROW_GROUP), np.float32)
    j, r = np.meshgrid(np.arange(RADIX), np.arange(RADIX), indexing="ij")
    p[(RADIX * r + j).ravel(), (RADIX * j + r).ravel()] = 1.0
    return p


def _bf16_const(a):
    return jnp.asarray(np.asarray(a, np.float32)).astype(BF16)


def _silu(a):
    return a * (1.0 / (1.0 + jnp.exp(-a)))


def _inproj_kernel(x_ref, gain_ref, w_ref, qg_ref, kg_ref,
                   qa_ref, qb_ref, qc_ref, ka_ref, kb_ref, kc_ref, perm_ref,
                   q16_ref, k1_ref, k16_ref, v1_ref, v16_ref, ga_ref, u16_ref, gf_ref):
    x = x_ref[...]
    ms = jnp.mean(x * x, axis=-1, keepdims=True)
    h = (x * lax.rsqrt(ms + RMS_EPS) * gain_ref[...]).astype(BF16)
    perm = perm_ref[...]

    def column_tile(j):
        cols = slice(j * ATTN_WIDTH, (j + 1) * ATTN_WIDTH)
        return jnp.dot(h, w_ref[:, cols], preferred_element_type=F32)

    def norm_rope(acc, g_ref, a_ref, b_ref, c_ref):
        heads = []
        for hd in range(N_HEADS):
            a = acc[:, hd * HEAD_DIM:(hd + 1) * HEAD_DIM]
            ms_h = jnp.mean(a * a, axis=-1, keepdims=True)
            n = a * lax.rsqrt(ms_h + RMS_EPS) * g_ref[...]
            r = (n * a_ref[...]
                 + pltpu.roll(n, ROPE_HALF, 1) * b_ref[...]
                 + pltpu.roll(n, HEAD_DIM - ROPE_HALF, 1) * c_ref[...])
            heads.append(r.astype(BF16))
        return jnp.concatenate(heads, axis=1)

    def to_mod16(t):
        return [jnp.dot(perm, t[g * ROW_GROUP:(g + 1) * ROW_GROUP], preferred_element_type=F32)
                for g in range(INPROJ_GROUPS)]

    def store_heads(ref, t):
        for hd in range(N_HEADS):
            ref[hd] = t[:, hd * HEAD_DIM:(hd + 1) * HEAD_DIM]

    def store_head_pieces(ref, groups):
        for g, t in enumerate(groups):
            for hd in range(N_HEADS):
                for r in range(RADIX):
                    ref[hd, r, g * RADIX:(g + 1) * RADIX, :] = (
                        t[r * RADIX:(r + 1) * RADIX, hd * HEAD_DIM:(hd + 1) * HEAD_DIM].astype(ref.dtype))

    acc_q = column_tile(0)
    acc_k = column_tile(1)
    q = norm_rope(acc_q, qg_ref, qa_ref, qb_ref, qc_ref)
    acc_v = column_tile(2)
    store_head_pieces(q16_ref, to_mod16(q))

    k = norm_rope(acc_k, kg_ref, ka_ref, kb_ref, kc_ref)
    acc_ga = column_tile(3)
    store_heads(k1_ref, k)
    store_head_pieces(k16_ref, to_mod16(k))

    v = acc_v.astype(BF16)
    acc_u = column_tile(4)
    store_heads(v1_ref, v)
    store_head_pieces(v16_ref, to_mod16(v))

    ga_ref[...] = _silu(acc_ga).astype(BF16)
    acc_gf = column_tile(5)

    for g, u16 in enumerate(to_mod16(acc_u.astype(BF16))):
        for r in range(RADIX):
            u16_ref[r, g * RADIX:(g + 1) * RADIX, :] = u16[r * RADIX:(r + 1) * RADIX, :].astype(BF16)

    gf_ref[...] = _silu(acc_gf).astype(BF16)


def _inproj(x2d, batch, seq, gain, w_in, qg, kg, perm):
    t = x2d.shape[0]
    rows = INPROJ_GROUPS * ROW_GROUP
    pieces = INPROJ_GROUPS * RADIX
    groups = seq // rows
    sub = seq // RADIX
    qa, qb, qc = _rope_tables(seq, HEAD_DIM ** -0.5 * math.log2(math.e))
    ka, kb, kc = _rope_tables(seq, 1.0)
    rope_spec = pl.BlockSpec((rows, HEAD_DIM), lambda i: (i % groups, 0))
    tok_spec = pl.BlockSpec((rows, ATTN_WIDTH), lambda i: (i, 0))
    head_nat_spec = pl.BlockSpec((N_HEADS, rows, HEAD_DIM), lambda i: (0, i, 0))
    head_m16_spec = pl.BlockSpec((N_HEADS, None, RADIX, pieces, HEAD_DIM),
                                 lambda i: (0, i // groups, 0, i % groups, 0))
    head_nat = jax.ShapeDtypeStruct((N_HEADS, t, HEAD_DIM), BF16)
    head_m16 = lambda dt: jax.ShapeDtypeStruct((N_HEADS, batch, RADIX, sub, HEAD_DIM), dt)
    return pl.pallas_call(
        _inproj_kernel,
        grid=(t // rows,),
        in_specs=[
            pl.BlockSpec((rows, D_MODEL), lambda i: (i, 0)),
            _resident((1, D_MODEL)),
            _resident(w_in.shape),
            _resident((1, HEAD_DIM)), _resident((1, HEAD_DIM)),
            rope_spec, rope_spec, rope_spec, rope_spec, rope_spec, rope_spec,
            _resident((ROW_GROUP, ROW_GROUP)),
        ],
        out_specs=[
            head_m16_spec, head_nat_spec, head_m16_spec, head_nat_spec, head_m16_spec,
            tok_spec,
            pl.BlockSpec((None, RADIX, pieces, FOURIER_WIDTH), lambda i: (i // groups, 0, i % groups, 0)),
            tok_spec,
        ],
        out_shape=[
            head_m16(F32), head_nat, head_m16(BF16), head_nat, head_m16(BF16),
            jax.ShapeDtypeStruct((t, ATTN_WIDTH), BF16),
            jax.ShapeDtypeStruct((batch, RADIX, sub, FOURIER_WIDTH), BF16),
            jax.ShapeDtypeStruct((t, FOURIER_WIDTH), BF16),
        ],
        compiler_params=_params("arbitrary"),
        name="inproj",
    )(x2d, gain, w_in, qg, kg, qa, qb, qc, ka, kb, kc, perm)


def _window_start(first, half, total, window):
    return min(max(first - half, 0), total - window)


class _Pattern:
    def __init__(self, q_first, k_first, diff0, scale):
        deltas = [scale * (q0 - k0) for q0, k0 in zip(q_first, k_first)]
        uniq = sorted(set(deltas))
        self.q_first, self.k_first = q_first, k_first
        self.table_of = [uniq.index(d) for d in deltas]
        self.bias = jnp.asarray(np.stack(
            [np.where(np.abs(diff0 + d) <= HALF_KEYS, 0.0, MASK_VALUE) for d in uniq]).astype(np.float32)
        ).astype(BF16)


def _attention_patterns(seq):
    sub = seq // RADIX
    qi = np.arange(Q_BLOCK)[:, None]
    win16 = min(Q_BLOCK + 2 * HALF_KEYS, sub)
    q16 = list(range(0, sub, Q_BLOCK))
    p16 = _Pattern(q16, [_window_start(q0, HALF_KEYS, sub, win16) for q0 in q16],
                   qi - np.arange(win16)[None, :], 1)
    q_rows, k_rows = Q_BLOCK // 4, Q_BLOCK // 4 + 2 * HALF_KEYS // 4
    kc = np.arange(4 * k_rows)[None, :]
    q4 = list(range(0, sub, q_rows))
    p4 = _Pattern(q4, [_window_start(q0, HALF_KEYS // 4, sub, k_rows) for q0 in q4],
                  4 * (qi % q_rows - kc % k_rows) + (qi // q_rows - kc // k_rows), 4)
    win1 = Q_BLOCK + 2 * HALF_KEYS
    q1 = list(range(0, seq, Q_BLOCK))
    p1 = _Pattern(q1, [_window_start(q0, HALF_KEYS, seq, win1) for q0 in q1],
                  RADIX * (qi % (Q_BLOCK // RADIX)) + qi // (Q_BLOCK // RADIX) - np.arange(win1)[None, :], 1)
    return p16, p4, p1


def _attn_kernel(q16_ref, k1_ref, v1_ref, k16_ref, v16_ref, b16_ref, b4_ref, b1_ref,
                 o_ref, acc_s, m_s, l_s, p0_s, w0_s, p1_s, w1_s, *, pat16, pat4, pat1):
    def gather(ref, pieces):
        return jnp.concatenate([ref[r, pl.ds(s0, n), :] for r, s0, n in pieces], axis=0)

    def scatter(ref, pieces, value):
        off = 0
        for r, s0, n in pieces:
            ref[r, pl.ds(s0, n), :] = value[off:off + n]
            off += n

    def scores_stage(blocks, first, p_scr, w_scr):
        for i, (q, k, _, bias, pieces, (_, m_h, _)) in enumerate(blocks):
            s = lax.dot_general(q(), k(), (((1,), (1,)), ((), ())), preferred_element_type=F32)
            s = s.astype(BF16) + bias()
            m_new = jnp.broadcast_to(jnp.max(s, axis=-1, keepdims=True), (Q_BLOCK, HEAD_DIM)).astype(F32)
            if not first:
                m_old = gather(m_h, pieces)
                m_new = jnp.maximum(m_old, m_new)
                w_scr[i] = jnp.exp2(m_old - m_new)
            width = s.shape[1]
            m_b = m_new.astype(BF16)
            p_scr[i, :, :width] = jnp.exp2(s - jnp.concatenate([m_b] * (width // HEAD_DIM), axis=1))
            scatter(m_h, pieces, m_new)

    def values_stage(blocks, first, p_scr, w_scr):
        for i, (_, _, v, _, pieces, (acc_h, _, l_h)) in enumerate(blocks):
            vw = v()
            v_aug = jnp.concatenate([vw, jnp.ones(vw.shape, BF16)], axis=1)
            pv = jnp.dot(p_scr[i, :, :vw.shape[0]], v_aug, preferred_element_type=F32)
            acc, l = pv[:, :HEAD_DIM], pv[:, HEAD_DIM:]
            if not first:
                w_old = w_scr[i]
                acc = w_old * gather(acc_h, pieces) + acc
                l = w_old * gather(l_h, pieces) + l
            scatter(l_h, pieces, l)
            scatter(acc_h, pieces, acc)

    win16 = b16_ref.shape[2]
    q_rows4 = Q_BLOCK // 4
    k_rows4 = b4_ref.shape[2] // 4
    q_rows1 = Q_BLOCK // RADIX
    win1 = b1_ref.shape[2]
    blocks16, blocks4, blocks1 = [], [], []
    for hd in range(q16_ref.shape[0]):
        q16, k1, v1, k16, v16 = (ref.at[hd] for ref in (q16_ref, k1_ref, v1_ref, k16_ref, v16_ref))
        state = (acc_s.at[hd], m_s.at[hd], l_s.at[hd])

        for r in range(RADIX):
            for blk, (l0, ws) in enumerate(zip(pat16.q_first, pat16.k_first)):
                blocks16.append((
                    lambda q16=q16, r=r, l0=l0: q16[r, l0:l0 + Q_BLOCK, :].astype(BF16),
                    lambda k16=k16, r=r, ws=ws: k16[r, ws:ws + win16, :],
                    lambda v16=v16, r=r, ws=ws: v16[r, ws:ws + win16, :],
                    lambda blk=blk: b16_ref[pat16.table_of[blk]],
                    [(r, l0, Q_BLOCK)], state))

        for r4 in range(4):
            res = [r4 + 4 * a for a in range(4)]
            for blk, (l0, ws) in enumerate(zip(pat4.q_first, pat4.k_first)):
                q_pieces = [(r, l0, q_rows4) for r in res]
                k_pieces = [(r, ws, k_rows4) for r in res]
                blocks4.append((
                    lambda q16=q16, p=q_pieces: gather(q16, p).astype(BF16),
                    lambda k16=k16, p=k_pieces: gather(k16, p),
                    lambda v16=v16, p=k_pieces: gather(v16, p),
                    lambda blk=blk: b4_ref[pat4.table_of[blk]],
                    q_pieces, state))

        for blk, (n0, ws) in enumerate(zip(pat1.q_first, pat1.k_first)):
            q_pieces = [(r, n0 // RADIX, q_rows1) for r in range(RADIX)]
            blocks1.append((
                lambda q16=q16, p=q_pieces: gather(q16, p).astype(BF16),
                lambda k1=k1, ws=ws: k1[ws:ws + win1, :],
                lambda v1=v1, ws=ws: v1[ws:ws + win1, :],
                lambda blk=blk: b1_ref[pat1.table_of[blk]],
                q_pieces, state))

    groups = []
    for blocks, first in ((blocks16, True), (blocks4, False), (blocks1, False)):
        groups.extend((blocks[i:i + p0_s.shape[0]], first) for i in range(0, len(blocks), p0_s.shape[0]))

    slots = ((p0_s, w0_s), (p1_s, w1_s))
    scores_stage(*groups[0], *slots[0])
    for g in range(1, len(groups)):
        scores_stage(*groups[g], *slots[g % 2])
        values_stage(*groups[g - 1], *slots[(g - 1) % 2])
    values_stage(*groups[-1], *slots[(len(groups) - 1) % 2])

    def finish(i, carry):
        hd, r = i // RADIX, i % RADIX
        o_ref[hd, r] = (acc_s[hd, r] * (1.0 / l_s[hd, r])).astype(BF16)
        return carry

    lax.fori_loop(0, q16_ref.shape[0] * RADIX, finish, 0, unroll=4)


def _attention(q16, k1, v1, k16, v16, batch, seq):
    sub = seq // RADIX
    heads = ATTN_HEADS_PER_STEP * (2 if sub <= Q_BLOCK else 1)
    pat16, pat4, pat1 = _attention_patterns(seq)
    m16_spec = pl.BlockSpec((heads, None, RADIX, sub, HEAD_DIM), lambda b, h: (h, b, 0, 0, 0))
    nat_spec = pl.BlockSpec((heads, seq, HEAD_DIM), lambda b, h: (h, b, 0))
    state = pltpu.VMEM((heads, RADIX, sub, HEAD_DIM), F32)
    group = BLOCKS_PER_GROUP_LONG if sub > Q_BLOCK else BLOCKS_PER_GROUP_SHORT
    probs = pltpu.VMEM((group, Q_BLOCK, Q_BLOCK + 2 * HALF_KEYS), BF16)
    rescale = pltpu.VMEM((group, Q_BLOCK, HEAD_DIM), F32)
    return pl.pallas_call(
        functools.partial(_attn_kernel, pat16=pat16, pat4=pat4, pat1=pat1),
        grid=(batch, N_HEADS // heads),
        in_specs=[m16_spec, nat_spec, nat_spec, m16_spec, m16_spec,
                  _resident(pat16.bias.shape), _resident(pat4.bias.shape), _resident(pat1.bias.shape)],
        out_specs=m16_spec,
        out_shape=jax.ShapeDtypeStruct((N_HEADS, batch, RADIX, sub, HEAD_DIM), BF16),
        scratch_shapes=[state, state, state, probs, rescale, probs, rescale],
        compiler_params=_params("parallel", "parallel"),
        name="attention",
    )(q16, k1, v1, k16, v16, pat16.bias, pat4.bias, pat1.bias)


def _chanmat_kernel(c_ref, s_ref, w_ref, m_ref):
    for g in range(N_GROUPS):
        w = w_ref[g]
        mc = jnp.dot(c_ref[...], w, preferred_element_type=F32, precision=lax.Precision.HIGHEST)
        ms = jnp.dot(s_ref[...], w, preferred_element_type=F32, precision=lax.Precision.HIGHEST)
        m_ref[g, :, :GROUP_DIM] = mc.astype(BF16)
        m_ref[g, :, GROUP_DIM:] = ms.astype(BF16)


def _channel_matrices(w_fourier):
    idx = np.arange(GROUP_DIM)
    cc, sc = _dft_cos_sin(GROUP_DIM, idx, idx, GROUP_DIM ** -0.5)
    return pl.pallas_call(
        _chanmat_kernel,
        out_shape=jax.ShapeDtypeStruct((N_GROUPS, GROUP_DIM, 2 * GROUP_DIM), BF16),
        name="chanmat",
    )(jnp.asarray(cc, F32), jnp.asarray(sc, F32), w_fourier)


def _fourier_kernel(u_ref, m_ref, c_ref, smc_ref, cps_ref, mc_ref, ms_ref, gf_ref, o_ref, yr_s, yi_s,
                    *, stage1_steps, tiles):
    t = pl.program_id(1)
    dot = functools.partial(jnp.dot, preferred_element_type=F32)
    per_step = u_ref.shape[0]

    @pl.when(t < stage1_steps)
    def _():
        z = []
        for j in range(per_step):
            a_parts, b_parts, ab_parts = [], [], []
            for g in range(N_GROUPS):
                ab = dot(u_ref[j, :, g * GROUP_DIM:(g + 1) * GROUP_DIM], m_ref[g])
                a, b = ab[:, :GROUP_DIM], ab[:, GROUP_DIM:]
                a_parts.append(a.astype(BF16))
                b_parts.append(b.astype(BF16))
                ab_parts.append((a + b).astype(BF16))
            z.append(tuple(jnp.concatenate(p, axis=1) for p in (a_parts, b_parts, ab_parts)))
        c, s_minus_c, c_plus_s = c_ref[...], smc_ref[...], cps_ref[...]
        for j, (a, b, a_plus_b) in enumerate(z):
            n2 = t * per_step + j
            k1 = dot(c, a_plus_b)
            yr_s[n2] = (k1 - dot(c_plus_s, b)).astype(BF16)
            yi_s[n2] = (k1 + dot(s_minus_c, a)).astype(BF16)

    @pl.when(t >= stage1_steps)
    def _():
        first_row = (t - stage1_steps) * (tiles * RADIX)
        for i in range(tiles):
            rows = pl.ds(pl.multiple_of(first_row + i * RADIX, RADIX), RADIX)
            yr = jnp.concatenate([yr_s[n2, rows, :] for n2 in range(RADIX)], axis=0)
            yi = jnp.concatenate([yi_s[n2, rows, :] for n2 in range(RADIX)], axis=0)
            x = dot(mc_ref[i], yr) + dot(ms_ref[i], yi)
            out_rows = slice(i * RADIX, (i + 1) * RADIX)
            for k2 in range(RADIX):
                piece = x[k2 * RADIX:(k2 + 1) * RADIX, :] * gf_ref[k2, out_rows, :].astype(F32)
                o_ref[k2, out_rows, :] = piece.astype(BF16)


def _stage2_matrices(seq):
    s1 = seq // RADIX
    tiles = s1 // RADIX
    c, k2, k1, n2 = np.meshgrid(np.arange(tiles), np.arange(RADIX), np.arange(RADIX), np.arange(RADIX),
                                indexing="ij")
    k = RADIX * c + k1 + s1 * k2
    ang = 2.0 * np.pi * ((n2 * k) % seq).astype(np.float64) / seq
    mc = np.zeros((tiles, ROW_GROUP, ROW_GROUP), np.float32)
    ms = np.zeros((tiles, ROW_GROUP, ROW_GROUP), np.float32)
    mc[c, RADIX * k2 + k1, RADIX * n2 + k1] = np.cos(ang) * seq ** -0.5
    ms[c, RADIX * k2 + k1, RADIX * n2 + k1] = -np.sin(ang) * seq ** -0.5
    return _bf16_const(mc), _bf16_const(ms)


def _fourier(u16, chan_mats, gf, batch, seq, tiles=FOURIER2_TILES):
    s1 = seq // RADIX
    idx = np.arange(s1)
    c1, sn1 = _dft_cos_sin(s1, idx, idx)
    mc, ms = _stage2_matrices(seq)
    per_step = FOURIER1_ROWS // s1
    stage1_steps = RADIX // per_step
    rows = tiles * RADIX
    stage2 = lambda t: jnp.maximum(t - stage1_steps, 0)
    nat_spec = pl.BlockSpec((None, RADIX, rows, FOURIER_WIDTH), lambda b, t: (b, 0, stage2(t), 0))
    m_spec = pl.BlockSpec((tiles, ROW_GROUP, ROW_GROUP), lambda b, t: (stage2(t), 0, 0))
    y_scratch = pltpu.VMEM((RADIX, s1, FOURIER_WIDTH), BF16)
    out = pl.pallas_call(
        functools.partial(_fourier_kernel, stage1_steps=stage1_steps, tiles=tiles),
        grid=(batch, stage1_steps + s1 // rows),
        in_specs=[pl.BlockSpec((None, per_step, s1, FOURIER_WIDTH),
                               lambda b, t: (b, jnp.minimum(t, stage1_steps - 1), 0, 0)),
                  _resident(chan_mats.shape), _resident((s1, s1)), _resident((s1, s1)), _resident((s1, s1)),
                  m_spec, m_spec, nat_spec],
        out_specs=nat_spec,
        out_shape=jax.ShapeDtypeStruct((batch, RADIX, s1, FOURIER_WIDTH), BF16),
        scratch_shapes=[y_scratch, y_scratch],
        compiler_params=_params("parallel", "arbitrary"),
        name="fourier",
    )(u16, chan_mats, _bf16_const(c1), _bf16_const(sn1 - c1), _bf16_const(c1 + sn1), mc, ms,
      gf.reshape(batch, RADIX, s1, FOURIER_WIDTH))
    return out.reshape(batch * seq, FOURIER_WIDTH)


def _outproj_kernel(x_ref, o16_ref, ga_ref, mf_ref, unperm_ref, w_ref, y_ref):
    unperm = unperm_ref[...]
    attn = []
    for g in range(OUTPROJ_GROUPS):
        rows = slice(g * RADIX, (g + 1) * RADIX)
        heads = [jnp.concatenate([o16_ref[hd, r, rows, :] for r in range(RADIX)], axis=0)
                 for hd in range(N_HEADS)]
        attn16 = jnp.concatenate(heads, axis=1)
        attn.append(jnp.dot(unperm, attn16, preferred_element_type=F32))
    y = x_ref[...] + jnp.dot(mf_ref[...], w_ref[ATTN_WIDTH:, :], preferred_element_type=F32)
    mix_a = (jnp.concatenate(attn, axis=0) * ga_ref[...].astype(F32)).astype(BF16)
    y_ref[...] = y + jnp.dot(mix_a, w_ref[:ATTN_WIDTH, :], preferred_element_type=F32)


def _outproj(x2d, o16, ga, mix_f, unperm, w_out, seq):
    t = x2d.shape[0]
    rows = OUTPROJ_GROUPS * ROW_GROUP
    steps_per_seq = seq // rows
    tok_spec = pl.BlockSpec((rows, ATTN_WIDTH), lambda i: (i, 0))
    x_spec = pl.BlockSpec((rows, D_MODEL), lambda i: (i, 0))
    return pl.pallas_call(
        _outproj_kernel,
        grid=(t // rows,),
        in_specs=[x_spec,
                  pl.BlockSpec((N_HEADS, None, RADIX, OUTPROJ_GROUPS * RADIX, HEAD_DIM),
                               lambda i: (0, i // steps_per_seq, 0, i % steps_per_seq, 0)),
                  tok_spec, tok_spec,
                  _resident((ROW_GROUP, ROW_GROUP)),
                  _resident((D_MODEL, D_MODEL))],
        out_specs=x_spec,
        out_shape=jax.ShapeDtypeStruct((t, D_MODEL), F32),
        compiler_params=_params("arbitrary"),
        name="outproj",
    )(x2d, o16, ga, mix_f, unperm, w_out)


def _layer(x, gain, w_in, qg, kg, chan_mats, w_out, perm, unperm):
    batch, seq, _ = x.shape
    x2d = x.reshape(batch * seq, D_MODEL)
    q16, k1, k16, v1, v16, ga, u16, gf = _inproj(x2d, batch, seq, gain, w_in, qg, kg, perm)
    o16 = _attention(q16, k1, v1, k16, v16, batch, seq)
    mix_f = _fourier(u16, chan_mats, gf, batch, seq)
    y = _outproj(x2d, o16, ga, mix_f, unperm, w_out, seq)
    return y.reshape(batch, seq, D_MODEL)


def kernel(x_prompt, x_sample, rms_gain, w_in, q_norm_gain, k_norm_gain, w_fourier, w_out):
    depth = rms_gain.shape[0]
    p = _group_permutation()
    perm, unperm = _bf16_const(p), _bf16_const(p.T)
    for l in range(depth):
        gain = rms_gain[l].reshape(1, D_MODEL)
        w_in_l = w_in[l].astype(BF16)
        w_out_l = w_out[l].astype(BF16)
        qg = q_norm_gain[l].reshape(1, HEAD_DIM)
        kg = k_norm_gain[l].reshape(1, HEAD_DIM)
        chan_mats = _channel_matrices(w_fourier[l])
        x_prompt = _layer(x_prompt, gain, w_in_l, qg, kg, chan_mats, w_out_l, perm, unperm)
        x_sample = _layer(x_sample, gain, w_in_l, qg, kg, chan_mats, w_out_l, perm, unperm)
    return (x_prompt, x_sample)
```

```python
import functools
import math

import jax
import jax.numpy as jnp
import numpy as np
from jax import lax
from jax.experimental import pallas as pl
from jax.experimental.pallas import tpu as pltpu

D_MODEL = 2048
ATTN_WIDTH = 1024
FOURIER_WIDTH = 1024
HEAD_DIM = 128
N_HEADS = ATTN_WIDTH // HEAD_DIM
N_GROUPS = 4
GROUP_DIM = FOURIER_WIDTH // N_GROUPS
ROPE_THETA = 500000.0
ROPE_DIM = HEAD_DIM // 4
ROPE_HALF = ROPE_DIM // 2
HALF_KEYS = 64
RMS_EPS = 1e-6
RADIX = 16
ROW_GROUP = RADIX * RADIX
Q_BLOCK = 128
FOURIER1_ROWS = 1024
FOURIER2_TILES = 8
ATTN_HEADS_PER_STEP = 2
INPROJ_GROUPS = 1
OUTPROJ_GROUPS = 2
BLOCKS_PER_GROUP_LONG = 6
BLOCKS_PER_GROUP_SHORT = 4
MASK_VALUE = -1e30

VMEM_LIMIT_BYTES = 56 * 1024 * 1024

F32 = jnp.float32
BF16 = jnp.bfloat16


def _params(*semantics):
    return pltpu.CompilerParams(dimension_semantics=semantics, vmem_limit_bytes=VMEM_LIMIT_BYTES)


def _resident(shape):
    return pl.BlockSpec(shape, lambda *_: (0,) * len(shape), pipeline_mode=pl.Buffered(1))


def _rope_tables(seq, scale):
    expo = np.arange(ROPE_HALF, dtype=np.float32) / np.float32(ROPE_HALF)
    inv_freq = (np.float32(1.0) / np.power(np.float32(ROPE_THETA), expo)).astype(np.float32)
    ang = (np.arange(seq, dtype=np.float32)[:, None] * inv_freq[None, :]).astype(np.float64)
    cos, sin = np.cos(ang), np.sin(ang)
    a = np.ones((seq, HEAD_DIM))
    b = np.zeros((seq, HEAD_DIM))
    c = np.zeros((seq, HEAD_DIM))
    a[:, :ROPE_HALF] = cos
    a[:, ROPE_HALF:ROPE_DIM] = cos
    b[:, ROPE_HALF:ROPE_DIM] = sin
    c[:, :ROPE_HALF] = -sin
    return tuple(jnp.asarray((t * scale).astype(np.float32)) for t in (a, b, c))


def _dft_cos_sin(n, rows, cols, scale=1.0):
    m = (np.asarray(rows, dtype=np.int64)[:, None] * np.asarray(cols, dtype=np.int64)[None, :]) % n
    ang = 2.0 * np.pi * m.astype(np.float64) / n
    return np.cos(ang) * scale, np.sin(ang) * scale


def _group_permutation():
    p = np.zeros((ROW_GROUP, ROW_GROUP), np.float32)
    j, r = np.meshgrid(np.arange(RADIX), np.arange(RADIX), indexing="ij")
    p[(RADIX * r + j).ravel(), (RADIX * j + r).ravel()] = 1.0
    return p


def _bf16_const(a):
    return jnp.asarray(np.asarray(a, np.float32)).astype(BF16)


def _silu(a):
    return a * (1.0 / (1.0 + jnp.exp(-a)))


def _inproj_kernel(x_ref, gain_ref, w_ref, qg_ref, kg_ref,
                   qa_ref, qb_ref, qc_ref, ka_ref, kb_ref, kc_ref, perm_ref,
                   q16_ref, k1_ref, k16_ref, v1_ref, v16_ref, ga_ref, u16_ref, gf_ref):
    x = x_ref[...]
    ms = jnp.mean(x * x, axis=-1, keepdims=True)
    h = (x * lax.rsqrt(ms + RMS_EPS) * gain_ref[...]).astype(BF16)
    perm = perm_ref[...]

    def column_tile(j):
        cols = slice(j * ATTN_WIDTH, (j + 1) * ATTN_WIDTH)
        return jnp.dot(h, w_ref[:, cols], preferred_element_type=F32)

    def norm_rope(acc, g_ref, a_ref, b_ref, c_ref):
        heads = []
        for hd in range(N_HEADS):
            a = acc[:, hd * HEAD_DIM:(hd + 1) * HEAD_DIM]
            ms_h = jnp.mean(a * a, axis=-1, keepdims=True)
            n = a * lax.rsqrt(ms_h + RMS_EPS) * g_ref[...]
            r = (n * a_ref[...]
                 + pltpu.roll(n, ROPE_HALF, 1) * b_ref[...]
                 + pltpu.roll(n, HEAD_DIM - ROPE_HALF, 1) * c_ref[...])
            heads.append(r.astype(BF16))
        return jnp.concatenate(heads, axis=1)

    def to_mod16(t):
        return [jnp.dot(perm, t[g * ROW_GROUP:(g + 1) * ROW_GROUP], preferred_element_type=F32)
                for g in range(INPROJ_GROUPS)]

    def store_heads(ref, t):
        for hd in range(N_HEADS):
            ref[hd] = t[:, hd * HEAD_DIM:(hd + 1) * HEAD_DIM]

    def store_head_pieces(ref, groups):
        for g, t in enumerate(groups):
            for hd in range(N_HEADS):
                for r in range(RADIX):
                    ref[hd, r, g * RADIX:(g + 1) * RADIX, :] = (
                        t[r * RADIX:(r + 1) * RADIX, hd * HEAD_DIM:(hd + 1) * HEAD_DIM].astype(ref.dtype))

    acc_q = column_tile(0)
    acc_k = column_tile(1)
    q = norm_rope(acc_q, qg_ref, qa_ref, qb_ref, qc_ref)
    acc_v = column_tile(2)
    store_head_pieces(q16_ref, to_mod16(q))

    k = norm_rope(acc_k, kg_ref, ka_ref, kb_ref, kc_ref)
    acc_ga = column_tile(3)
    store_heads(k1_ref, k)
    store_head_pieces(k16_ref, to_mod16(k))

    v = acc_v.astype(BF16)
    acc_u = column_tile(4)
    store_heads(v1_ref, v)
    store_head_pieces(v16_ref, to_mod16(v))

    ga_ref[...] = _silu(acc_ga).astype(BF16)
    acc_gf = column_tile(5)

    for g, u16 in enumerate(to_mod16(acc_u.astype(BF16))):
        for r in range(RADIX):
            u16_ref[r, g * RADIX:(g + 1) * RADIX, :] = u16[r * RADIX:(r + 1) * RADIX, :].astype(BF16)

    gf_ref[...] = _silu(acc_gf).astype(BF16)


def _inproj(x2d, batch, seq, gain, w_in, qg, kg, perm):
    t = x2d.shape[0]
    rows = INPROJ_GROUPS * ROW_GROUP
    pieces = INPROJ_GROUPS * RADIX
    groups = seq // rows
    sub = seq // RADIX
    qa, qb, qc = _rope_tables(seq, HEAD_DIM ** -0.5 * math.log2(math.e))
    ka, kb, kc = _rope_tables(seq, 1.0)
    rope_spec = pl.BlockSpec((rows, HEAD_DIM), lambda i: (i % groups, 0))
    tok_spec = pl.BlockSpec((rows, ATTN_WIDTH), lambda i: (i, 0))
    head_nat_spec = pl.BlockSpec((N_HEADS, rows, HEAD_DIM), lambda i: (0, i, 0))
    head_m16_spec = pl.BlockSpec((N_HEADS, None, RADIX, pieces, HEAD_DIM),
                                 lambda i: (0, i // groups, 0, i % groups, 0))
    head_nat = jax.ShapeDtypeStruct((N_HEADS, t, HEAD_DIM), BF16)
    head_m16 = lambda dt: jax.ShapeDtypeStruct((N_HEADS, batch, RADIX, sub, HEAD_DIM), dt)
    return pl.pallas_call(
        _inproj_kernel,
        grid=(t // rows,),
        in_specs=[
            pl.BlockSpec((rows, D_MODEL), lambda i: (i, 0)),
            _resident((1, D_MODEL)),
            _resident(w_in.shape),
            _resident((1, HEAD_DIM)), _resident((1, HEAD_DIM)),
            rope_spec, rope_spec, rope_spec, rope_spec, rope_spec, rope_spec,
            _resident((ROW_GROUP, ROW_GROUP)),
        ],
        out_specs=[
            head_m16_spec, head_nat_spec, head_m16_spec, head_nat_spec, head_m16_spec,
            tok_spec,
            pl.BlockSpec((None, RADIX, pieces, FOURIER_WIDTH), lambda i: (i // groups, 0, i % groups, 0)),
            tok_spec,
        ],
        out_shape=[
            head_m16(F32), head_nat, head_m16(BF16), head_nat, head_m16(BF16),
            jax.ShapeDtypeStruct((t, ATTN_WIDTH), BF16),
            jax.ShapeDtypeStruct((batch, RADIX, sub, FOURIER_WIDTH), BF16),
            jax.ShapeDtypeStruct((t, FOURIER_WIDTH), BF16),
        ],
        compiler_params=_params("arbitrary"),
        name="inproj",
    )(x2d, gain, w_in, qg, kg, qa, qb, qc, ka, kb, kc, perm)


def _window_start(first, half, total, window):
    return min(max(first - half, 0), total - window)


class _Pattern:
    def __init__(self, q_first, k_first, diff0, scale):
        deltas = [scale * (q0 - k0) for q0, k0 in zip(q_first, k_first)]
        uniq = sorted(set(deltas))
        self.q_first, self.k_first = q_first, k_first
        self.table_of = [uniq.index(d) for d in deltas]
        self.bias = jnp.asarray(np.stack(
            [np.where(np.abs(diff0 + d) <= HALF_KEYS, 0.0, MASK_VALUE) for d in uniq]).astype(np.float32)
        ).astype(BF16)


def _attention_patterns(seq):
    sub = seq // RADIX
    qi = np.arange(Q_BLOCK)[:, None]
    win16 = min(Q_BLOCK + 2 * HALF_KEYS, sub)
    q16 = list(range(0, sub, Q_BLOCK))
    p16 = _Pattern(q16, [_window_start(q0, HALF_KEYS, sub, win16) for q0 in q16],
                   qi - np.arange(win16)[None, :], 1)
    q_rows, k_rows = Q_BLOCK // 4, Q_BLOCK // 4 + 2 * HALF_KEYS // 4
    kc = np.arange(4 * k_rows)[None, :]
    q4 = list(range(0, sub, q_rows))
    p4 = _Pattern(q4, [_window_start(q0, HALF_KEYS // 4, sub, k_rows) for q0 in q4],
                  4 * (qi % q_rows - kc % k_rows) + (qi // q_rows - kc // k_rows), 4)
    win1 = Q_BLOCK + 2 * HALF_KEYS
    q1 = list(range(0, seq, Q_BLOCK))
    p1 = _Pattern(q1, [_window_start(q0, HALF_KEYS, seq, win1) for q0 in q1],
                  RADIX * (qi % (Q_BLOCK // RADIX)) + qi // (Q_BLOCK // RADIX) - np.arange(win1)[None, :], 1)
    return p16, p4, p1


def _attn_kernel(q16_ref, k1_ref, v1_ref, k16_ref, v16_ref, b16_ref, b4_ref, b1_ref,
                 o_ref, acc_s, m_s, l_s, p0_s, w0_s, p1_s, w1_s, *, pat16, pat4, pat1):
    def gather(ref, pieces):
        return jnp.concatenate([ref[r, pl.ds(s0, n), :] for r, s0, n in pieces], axis=0)

    def scatter(ref, pieces, value):
        off = 0
        for r, s0, n in pieces:
            ref[r, pl.ds(s0, n), :] = value[off:off + n]
            off += n

    def scores_stage(blocks, first, p_scr, w_scr):
        for i, (q, k, _, bias, pieces, (_, m_h, _)) in enumerate(blocks):
            s = lax.dot_general(q(), k(), (((1,), (1,)), ((), ())), preferred_element_type=F32)
            s = s.astype(BF16) + bias()
            m_new = jnp.broadcast_to(jnp.max(s, axis=-1, keepdims=True), (Q_BLOCK, HEAD_DIM)).astype(F32)
            if not first:
                m_old = gather(m_h, pieces)
                m_new = jnp.maximum(m_old, m_new)
                w_scr[i] = jnp.exp2(m_old - m_new)
            width = s.shape[1]
            m_b = m_new.astype(BF16)
            p_scr[i, :, :width] = jnp.exp2(s - jnp.concatenate([m_b] * (width // HEAD_DIM), axis=1))
            scatter(m_h, pieces, m_new)

    def values_stage(blocks, first, p_scr, w_scr):
        for i, (_, _, v, _, pieces, (acc_h, _, l_h)) in enumerate(blocks):
            vw = v()
            v_aug = jnp.concatenate([vw, jnp.ones(vw.shape, BF16)], axis=1)
            pv = jnp.dot(p_scr[i, :, :vw.shape[0]], v_aug, preferred_element_type=F32)
            acc, l = pv[:, :HEAD_DIM], pv[:, HEAD_DIM:]
            if not first:
                w_old = w_scr[i]
                acc = w_old * gather(acc_h, pieces) + acc
                l = w_old * gather(l_h, pieces) + l
            scatter(l_h, pieces, l)
            scatter(acc_h, pieces, acc)

    win16 = b16_ref.shape[2]
    q_rows4 = Q_BLOCK // 4
    k_rows4 = b4_ref.shape[2] // 4
    q_rows1 = Q_BLOCK // RADIX
    win1 = b1_ref.shape[2]
    blocks16, blocks4, blocks1 = [], [], []
    for hd in range(q16_ref.shape[0]):
        q16, k1, v1, k16, v16 = (ref.at[hd] for ref in (q16_ref, k1_ref, v1_ref, k16_ref, v16_ref))
        state = (acc_s.at[hd], m_s.at[hd], l_s.at[hd])

        for r in range(RADIX):
            for blk, (l0, ws) in enumerate(zip(pat16.q_first, pat16.k_first)):
                blocks16.append((
                    lambda q16=q16, r=r, l0=l0: q16[r, l0:l0 + Q_BLOCK, :].astype(BF16),
                    lambda k16=k16, r=r, ws=ws: k16[r, ws:ws + win16, :],
                    lambda v16=v16, r=r, ws=ws: v16[r, ws:ws + win16, :],
                    lambda blk=blk: b16_ref[pat16.table_of[blk]],
                    [(r, l0, Q_BLOCK)], state))

        for r4 in range(4):
            res = [r4 + 4 * a for a in range(4)]
            for blk, (l0, ws) in enumerate(zip(pat4.q_first, pat4.k_first)):
                q_pieces = [(r, l0, q_rows4) for r in res]
                k_pieces = [(r, ws, k_rows4) for r in res]
                blocks4.append((
                    lambda q16=q16, p=q_pieces: gather(q16, p).astype(BF16),
                    lambda k16=k16, p=k_pieces: gather(k16, p),
                    lambda v16=v16, p=k_pieces: gather(v16, p),
                    lambda blk=blk: b4_ref[pat4.table_of[blk]],
                    q_pieces, state))

        for blk, (n0, ws) in enumerate(zip(pat1.q_first, pat1.k_first)):
            q_pieces = [(r, n0 // RADIX, q_rows1) for r in range(RADIX)]
            blocks1.append((
                lambda q16=q16, p=q_pieces: gather(q16, p).astype(BF16),
                lambda k1=k1, ws=ws: k1[ws:ws + win1, :],
                lambda v1=v1, ws=ws: v1[ws:ws + win1, :],
                lambda blk=blk: b1_ref[pat1.table_of[blk]],
                q_pieces, state))

    groups = []
    for blocks, first in ((blocks16, True), (blocks4, False), (blocks1, False)):
        groups.extend((blocks[i:i + p0_s.shape[0]], first) for i in range(0, len(blocks), p0_s.shape[0]))

    slots = ((p0_s, w0_s), (p1_s, w1_s))
    scores_stage(*groups[0], *slots[0])
    for g in range(1, len(groups)):
        scores_stage(*groups[g], *slots[g % 2])
        values_stage(*groups[g - 1], *slots[(g - 1) % 2])
    values_stage(*groups[-1], *slots[(len(groups) - 1) % 2])

    for hd in range(q16_ref.shape[0]):
        for r in range(RADIX):
            o_ref[hd, r] = (acc_s[hd, r] * (1.0 / l_s[hd, r])).astype(BF16)


def _attention(q16, k1, v1, k16, v16, batch, seq):
    sub = seq // RADIX
    heads = ATTN_HEADS_PER_STEP * (2 if sub <= Q_BLOCK else 1)
    pat16, pat4, pat1 = _attention_patterns(seq)
    m16_spec = pl.BlockSpec((heads, None, RADIX, sub, HEAD_DIM), lambda b, h: (h, b, 0, 0, 0))
    nat_spec = pl.BlockSpec((heads, seq, HEAD_DIM), lambda b, h: (h, b, 0))
    state = pltpu.VMEM((heads, RADIX, sub, HEAD_DIM), F32)
    group = BLOCKS_PER_GROUP_LONG if sub > Q_BLOCK else BLOCKS_PER_GROUP_SHORT
    probs = pltpu.VMEM((group, Q_BLOCK, Q_BLOCK + 2 * HALF_KEYS), BF16)
    rescale = pltpu.VMEM((group, Q_BLOCK, HEAD_DIM), F32)
    return pl.pallas_call(
        functools.partial(_attn_kernel, pat16=pat16, pat4=pat4, pat1=pat1),
        grid=(batch, N_HEADS // heads),
        in_specs=[m16_spec, nat_spec, nat_spec, m16_spec, m16_spec,
                  _resident(pat16.bias.shape), _resident(pat4.bias.shape), _resident(pat1.bias.shape)],
        out_specs=m16_spec,
        out_shape=jax.ShapeDtypeStruct((N_HEADS, batch, RADIX, sub, HEAD_DIM), BF16),
        scratch_shapes=[state, state, state, probs, rescale, probs, rescale],
        compiler_params=_params("parallel", "parallel"),
        name="attention",
    )(q16, k1, v1, k16, v16, pat16.bias, pat4.bias, pat1.bias)


def _chanmat_kernel(c_ref, s_ref, w_ref, m_ref):
    for g in range(N_GROUPS):
        w = w_ref[g]
        mc = jnp.dot(c_ref[...], w, preferred_element_type=F32, precision=lax.Precision.HIGHEST)
        ms = jnp.dot(s_ref[...], w, preferred_element_type=F32, precision=lax.Precision.HIGHEST)
        m_ref[g, :, :GROUP_DIM] = mc.astype(BF16)
        m_ref[g, :, GROUP_DIM:] = ms.astype(BF16)


def _channel_matrices(w_fourier):
    idx = np.arange(GROUP_DIM)
    cc, sc = _dft_cos_sin(GROUP_DIM, idx, idx, GROUP_DIM ** -0.5)
    return pl.pallas_call(
        _chanmat_kernel,
        out_shape=jax.ShapeDtypeStruct((N_GROUPS, GROUP_DIM, 2 * GROUP_DIM), BF16),
        name="chanmat",
    )(jnp.asarray(cc, F32), jnp.asarray(sc, F32), w_fourier)


def _fourier_kernel(u_ref, m_ref, c_ref, smc_ref, cps_ref, mc_ref, ms_ref, gf_ref, o_ref, yr_s, yi_s,
                    *, stage1_steps, tiles):
    t = pl.program_id(1)
    dot = functools.partial(jnp.dot, preferred_element_type=F32)
    per_step = u_ref.shape[0]

    @pl.when(t < stage1_steps)
    def _():
        z = []
        for j in range(per_step):
            a_parts, b_parts, ab_parts = [], [], []
            for g in range(N_GROUPS):
                ab = dot(u_ref[j, :, g * GROUP_DIM:(g + 1) * GROUP_DIM], m_ref[g])
                a, b = ab[:, :GROUP_DIM], ab[:, GROUP_DIM:]
                a_parts.append(a.astype(BF16))
                b_parts.append(b.astype(BF16))
                ab_parts.append((a + b).astype(BF16))
            z.append(tuple(jnp.concatenate(p, axis=1) for p in (a_parts, b_parts, ab_parts)))
        c, s_minus_c, c_plus_s = c_ref[...], smc_ref[...], cps_ref[...]
        for j, (a, b, a_plus_b) in enumerate(z):
            n2 = t * per_step + j
            k1 = dot(c, a_plus_b)
            yr_s[n2] = (k1 - dot(c_plus_s, b)).astype(BF16)
            yi_s[n2] = (k1 + dot(s_minus_c, a)).astype(BF16)

    @pl.when(t >= stage1_steps)
    def _():
        first_row = (t - stage1_steps) * (tiles * RADIX)
        for i in range(tiles):
            rows = pl.ds(pl.multiple_of(first_row + i * RADIX, RADIX), RADIX)
            yr = jnp.concatenate([yr_s[n2, rows, :] for n2 in range(RADIX)], axis=0)
            yi = jnp.concatenate([yi_s[n2, rows, :] for n2 in range(RADIX)], axis=0)
            x = dot(mc_ref[i], yr) + dot(ms_ref[i], yi)
            out_rows = slice(i * RADIX, (i + 1) * RADIX)
            for k2 in range(RADIX):
                piece = x[k2 * RADIX:(k2 + 1) * RADIX, :] * gf_ref[k2, out_rows, :].astype(F32)
                o_ref[k2, out_rows, :] = piece.astype(BF16)


def _stage2_matrices(seq):
    s1 = seq // RADIX
    tiles = s1 // RADIX
    c, k2, k1, n2 = np.meshgrid(np.arange(tiles), np.arange(RADIX), np.arange(RADIX), np.arange(RADIX),
                                indexing="ij")
    k = RADIX * c + k1 + s1 * k2
    ang = 2.0 * np.pi * ((n2 * k) % seq).astype(np.float64) / seq
    mc = np.zeros((tiles, ROW_GROUP, ROW_GROUP), np.float32)
    ms = np.zeros((tiles, ROW_GROUP, ROW_GROUP), np.float32)
    mc[c, RADIX * k2 + k1, RADIX * n2 + k1] = np.cos(ang) * seq ** -0.5
    ms[c, RADIX * k2 + k1, RADIX * n2 + k1] = -np.sin(ang) * seq ** -0.5
    return _bf16_const(mc), _bf16_const(ms)


def _fourier(u16, chan_mats, gf, batch, seq, tiles=FOURIER2_TILES):
    s1 = seq // RADIX
    idx = np.arange(s1)
    c1, sn1 = _dft_cos_sin(s1, idx, idx)
    mc, ms = _stage2_matrices(seq)
    per_step = FOURIER1_ROWS // s1
    stage1_steps = RADIX // per_step
    rows = tiles * RADIX
    stage2 = lambda t: jnp.maximum(t - stage1_steps, 0)
    nat_spec = pl.BlockSpec((None, RADIX, rows, FOURIER_WIDTH), lambda b, t: (b, 0, stage2(t), 0))
    m_spec = pl.BlockSpec((tiles, ROW_GROUP, ROW_GROUP), lambda b, t: (stage2(t), 0, 0))
    y_scratch = pltpu.VMEM((RADIX, s1, FOURIER_WIDTH), BF16)
    out = pl.pallas_call(
        functools.partial(_fourier_kernel, stage1_steps=stage1_steps, tiles=tiles),
        grid=(batch, stage1_steps + s1 // rows),
        in_specs=[pl.BlockSpec((None, per_step, s1, FOURIER_WIDTH),
                               lambda b, t: (b, jnp.minimum(t, stage1_steps - 1), 0, 0)),
                  _resident(chan_mats.shape), _resident((s1, s1)), _resident((s1, s1)), _resident((s1, s1)),
                  m_spec, m_spec, nat_spec],
        out_specs=nat_spec,
        out_shape=jax.ShapeDtypeStruct((batch, RADIX, s1, FOURIER_WIDTH), BF16),
        scratch_shapes=[y_scratch, y_scratch],
        compiler_params=_params("parallel", "arbitrary"),
        name="fourier",
    )(u16, chan_mats, _bf16_const(c1), _bf16_const(sn1 - c1), _bf16_const(c1 + sn1), mc, ms,
      gf.reshape(batch, RADIX, s1, FOURIER_WIDTH))
    return out.reshape(batch * seq, FOURIER_WIDTH)


def _outproj_kernel(x_ref, o16_ref, ga_ref, mf_ref, unperm_ref, w_ref, y_ref):
    unperm = unperm_ref[...]
    attn = []
    for g in range(OUTPROJ_GROUPS):
        rows = slice(g * RADIX, (g + 1) * RADIX)
        heads = [jnp.concatenate([o16_ref[hd, r, rows, :] for r in range(RADIX)], axis=0)
                 for hd in range(N_HEADS)]
        attn16 = jnp.concatenate(heads, axis=1)
        attn.append(jnp.dot(unperm, attn16, preferred_element_type=F32))
    y = x_ref[...] + jnp.dot(mf_ref[...], w_ref[ATTN_WIDTH:, :], preferred_element_type=F32)
    mix_a = (jnp.concatenate(attn, axis=0) * ga_ref[...].astype(F32)).astype(BF16)
    y_ref[...] = y + jnp.dot(mix_a, w_ref[:ATTN_WIDTH, :], preferred_element_type=F32)


def _outproj(x2d, o16, ga, mix_f, unperm, w_out, seq):
    t = x2d.shape[0]
    rows = OUTPROJ_GROUPS * ROW_GROUP
    steps_per_seq = seq // rows
    tok_spec = pl.BlockSpec((rows, ATTN_WIDTH), lambda i: (i, 0))
    x_spec = pl.BlockSpec((rows, D_MODEL), lambda i: (i, 0))
    return pl.pallas_call(
        _outproj_kernel,
        grid=(t // rows,),
        in_specs=[x_spec,
                  pl.BlockSpec((N_HEADS, None, RADIX, OUTPROJ_GROUPS * RADIX, HEAD_DIM),
                               lambda i: (0, i // steps_per_seq, 0, i % steps_per_seq, 0)),
                  tok_spec, tok_spec,
                  _resident((ROW_GROUP, ROW_GROUP)),
                  _resident((D_MODEL, D_MODEL))],
        out_specs=x_spec,
        out_shape=jax.ShapeDtypeStruct((t, D_MODEL), F32),
        compiler_params=_params("arbitrary"),
        name="outproj",
    )(x2d, o16, ga, mix_f, unperm, w_out)


def _layer(x, gain, w_in, qg, kg, chan_mats, w_out, perm, unperm):
    batch, seq, _ = x.shape
    x2d = x.reshape(batch * seq, D_MODEL)
    q16, k1, k16, v1, v16, ga, u16, gf = _inproj(x2d, batch, seq, gain, w_in, qg, kg, perm)
    o16 = _attention(q16, k1, v1, k16, v16, batch, seq)
    mix_f = _fourier(u16, chan_mats, gf, batch, seq)
    y = _outproj(x2d, o16, ga, mix_f, unperm, w_out, seq)
    return y.reshape(batch, seq, D_MODEL)


def kernel(x_prompt, x_sample, rms_gain, w_in, q_norm_gain, k_norm_gain, w_fourier, w_out):
    depth = rms_gain.shape[0]
    p = _group_permutation()
    perm, unperm = _bf16_const(p), _bf16_const(p.T)
    for l in range(depth):
        gain = rms_gain[l].reshape(1, D_MODEL)
        w_in_l = w_in[l].astype(BF16)
        w_out_l = w_out[l].astype(BF16)
        qg = q_norm_gain[l].reshape(1, HEAD_DIM)
        kg = k_norm_gain[l].reshape(1, HEAD_DIM)
        chan_mats = _channel_matrices(w_fourier[l])
        x_prompt = _layer(x_prompt, gain, w_in_l, qg, kg, chan_mats, w_out_l, perm, unperm)
        x_sample = _layer(x_sample, gain, w_in_l, qg, kg, chan_mats, w_out_l, perm, unperm)
    return (x_prompt, x_sample)
```
